```python
import math
import jax
import jax.numpy as jnp
from jax import lax
import numpy as np

D_MODEL = 1024
BATCH = 16
SEQ = 2048
DEPTH = 1

CHUNK = 64
EPS = 1e-6
NEG_INF = -1e30

A_HEADS = 8
A_HEAD_DIM = 64
A_WIDTH = A_HEADS * A_HEAD_DIM
A_LEFT_CHUNKS = 8
A_BAND = A_LEFT_CHUNKS + 1
REL_CLIP = 128

B_HEADS = 8
B_HEAD_DIM = 128
B_WIDTH = B_HEADS * B_HEAD_DIM
CONV_WIDTH = 4
DT_MIN = 0.001
DT_MAX = 0.1

N_BRANCHES = 2

IN_SPLITS = (A_WIDTH, A_WIDTH, A_WIDTH, 3 * B_WIDTH, B_HEADS, B_HEADS, B_WIDTH, N_BRANCHES * D_MODEL)
IN_WIDTH = sum(IN_SPLITS)
IN_OFFSETS = tuple(int(o) for o in np.cumsum(IN_SPLITS)[:-1])

PEER_HEADS = 8
PEER_QDIM = 256
PEER_HALF = PEER_QDIM // 2
PEER_NKEYS = 128
PEER_N_EXPERTS = PEER_NKEYS * PEER_NKEYS
PEER_TOPK = 16
PEER_BLOCK = 128

kernel_name = 'hybrid_chunkattn_gdn_peer_block'


def rms_norm(x, w):
    xf = x.astype(jnp.float32)
    y = xf * lax.rsqrt(jnp.mean(xf * xf, axis=-1, keepdims=True) + EPS)
    return (y * w.astype(jnp.float32)).astype(x.dtype)


def l2_normalize(t):
    return t * lax.rsqrt(jnp.sum(t * t, axis=-1, keepdims=True) + EPS)


def chunk_band_attention(q, k, v, q_norm_w, k_norm_w, rel_bias):
    b, s, _ = q.shape
    n = s // CHUNK
    shp = (b, n, CHUNK, A_HEADS, A_HEAD_DIM)
    q = rms_norm(q.reshape(shp), q_norm_w)
    k = rms_norm(k.reshape(shp), k_norm_w)
    v = v.reshape(shp)
    pad = ((0, 0), (A_LEFT_CHUNKS, 0), (0, 0), (0, 0), (0, 0))
    band = jnp.arange(n)[:, None] + jnp.arange(A_BAND)[None, :]
    kb = jnp.pad(k, pad)[:, band].reshape(b, n, A_BAND * CHUNK, A_HEADS, A_HEAD_DIM)
    vb = jnp.pad(v, pad)[:, band].reshape(b, n, A_BAND * CHUNK, A_HEADS, A_HEAD_DIM)
    scores = jnp.einsum('bnihd,bnjhd->bnhij', q, kb).astype(jnp.float32) * (A_HEAD_DIM ** -0.5)
    qpos = A_LEFT_CHUNKS * CHUNK + jnp.arange(CHUNK)
    kpos = jnp.arange(A_BAND * CHUNK)
    rel = jnp.clip(qpos[:, None] - kpos[None, :], -REL_CLIP, REL_CLIP) + REL_CLIP
    bias = rel_bias.astype(jnp.float32)[:, rel]
    valid = jnp.repeat(band >= A_LEFT_CHUNKS, CHUNK, axis=1)
    scores = jnp.where(valid[None, :, None, None, :], scores + bias[None, None], NEG_INF)
    probs = jax.nn.softmax(scores, axis=-1).astype(v.dtype)
    out = jnp.einsum('bnhij,bnjhd->bnihd', probs, vb)
    return out.reshape(b, s, A_WIDTH)


def causal_short_conv(x, w):
    s = x.shape[1]
    xp = jnp.pad(x, ((0, 0), (CONV_WIDTH - 1, 0), (0, 0)))
    return sum(xp[:, i:i + s] * w[i] for i in range(CONV_WIDTH))


def chunk_gated_delta_rule(q, k, v, g, beta):
    c = q.shape[3]
    tri_incl = jnp.tril(jnp.ones((c, c), dtype=bool))
    tri_strict = jnp.tril(jnp.ones((c, c), dtype=bool), -1)
    eye = jnp.eye(c, dtype=jnp.float32)
    g_cum = jnp.cumsum(g, axis=-1)
    decay = jnp.exp(jnp.where(tri_incl, g_cum[..., :, None] - g_cum[..., None, :], -jnp.inf))
    k_beta = k * beta[..., None]
    v_beta = v * beta[..., None]
    a_mat = jnp.where(tri_strict, jnp.einsum('bhnid,bhnjd->bhnij', k_beta, k) * decay, 0.0)
    t_mat = lax.linalg.triangular_solve(eye + a_mat, jnp.broadcast_to(eye, a_mat.shape),
                                        left_side=True, lower=True)
    u = jnp.einsum('bhnij,bhnje->bhnie', t_mat, v_beta)
    w = jnp.einsum('bhnij,bhnjd->bhnid', t_mat, k_beta * jnp.exp(g_cum)[..., None])
    attn_intra = jnp.where(tri_incl, jnp.einsum('bhnid,bhnjd->bhnij', q, k) * decay, 0.0)
    g_last = g_cum[..., -1]
    k_dec = k * jnp.exp(g_last[..., None] - g_cum)[..., None]
    q_dec = q * jnp.exp(g_cum)[..., None]

    def step(state, inp):
        qd, kd, w_c, u_c, attn_c, gl = inp
        v_new = u_c - jnp.einsum('bhcd,bhde->bhce', w_c, state)
        o = jnp.einsum('bhcd,bhde->bhce', qd, state) + jnp.einsum('bhij,bhje->bhie', attn_c, v_new)
        state = state * jnp.exp(gl)[..., None, None] + jnp.einsum('bhcd,bhce->bhde', kd, v_new)
        return state, o

    xs = tuple(jnp.moveaxis(t, 2, 0) for t in (q_dec, k_dec, w, u, attn_intra, g_last))
    b, h = q.shape[0], q.shape[1]
    s0 = jnp.zeros((b, h, q.shape[-1], v.shape[-1]), jnp.float32)
    _, o = lax.scan(step, s0, xs)
    return jnp.moveaxis(o, 0, 2)


def gated_deltanet(qkv_raw, a_raw, beta_raw, gate_raw, conv_w, a_log, dt_bias, norm_w):
    b, s, _ = qkv_raw.shape
    n = s // CHUNK
    f32 = jnp.float32
    qkv = jax.nn.silu(causal_short_conv(qkv_raw, conv_w)).astype(f32)
    q, k, v = jnp.split(qkv, 3, axis=-1)

    def to_chunks(t):
        return t.reshape(b, n, CHUNK, B_HEADS, B_HEAD_DIM).transpose(0, 3, 1, 2, 4)

    def heads_to_chunks(t):
        return t.reshape(b, n, CHUNK, B_HEADS).transpose(0, 3, 1, 2)

    q = l2_normalize(to_chunks(q)) * (B_HEAD_DIM ** -0.5)
    k = l2_normalize(to_chunks(k))
    v = to_chunks(v)
    beta = heads_to_chunks(jax.nn.sigmoid(beta_raw.astype(f32)))
    g = heads_to_chunks(-jnp.exp(a_log.astype(f32)) * jax.nn.softplus(a_raw.astype(f32) + dt_bias.astype(f32)))
    o = chunk_gated_delta_rule(q, k, v, g, beta)
    o = o.transpose(0, 2, 3, 1, 4).reshape(b, s, B_HEADS, B_HEAD_DIM)
    o = rms_norm(o, norm_w) * jax.nn.silu(gate_raw.astype(f32).reshape(b, s, B_HEADS, B_HEAD_DIM))
    return o.reshape(b, s, B_WIDTH).astype(qkv_raw.dtype)


def peer_ffn(h, w_query, keys_1, keys_2, expert_down, expert_up):
    b, s, d = h.shape
    hb = h.reshape((b * s) // PEER_BLOCK, PEER_BLOCK, d)

    def block(xt):
        q = (xt @ w_query).reshape(PEER_BLOCK, PEER_HEADS, PEER_QDIM)
        s1 = jnp.einsum('phd,hkd->phk', q[..., :PEER_HALF], keys_1).astype(jnp.float32)
        s2 = jnp.einsum('phd,hkd->phk', q[..., PEER_HALF:], keys_2).astype(jnp.float32)
        v1, i1 = lax.top_k(s1, PEER_TOPK)
        v2, i2 = lax.top_k(s2, PEER_TOPK)
        cand = (v1[..., :, None] + v2[..., None, :]).reshape(PEER_BLOCK, PEER_HEADS, PEER_TOPK * PEER_TOPK)
        cand_idx = (i1[..., :, None] * PEER_NKEYS + i2[..., None, :]).reshape(PEER_BLOCK, PEER_HEADS, PEER_TOPK * PEER_TOPK)
        top_s, pos = lax.top_k(cand, PEER_TOPK)
        expert_idx = jnp.take_along_axis(cand_idx, pos, axis=-1)
        gate = jax.nn.softmax(top_s, axis=-1)
        u = expert_down[expert_idx]
        act = jax.nn.gelu(jnp.einsum('pd,phkd->phk', xt, u).astype(jnp.float32), approximate=False)
        vv = expert_up[expert_idx]
        return jnp.einsum('phk,phkd->pd', (gate * act).astype(xt.dtype), vv)

    return lax.map(block, hb).reshape(b, s, d)


def hybrid_layer(x, norm_mix_w, w_in, a_q_norm_w, a_k_norm_w, a_rel_bias, b_conv_w, b_a_log,
                 b_dt_bias, b_norm_w, b_merge, w_proj_a, w_proj_b, w_out, norm_ffn_w,
                 peer_w_query, peer_keys_1, peer_keys_2, peer_down, peer_up):
    b, s, _ = x.shape
    h = rms_norm(x, norm_mix_w)
    z = h @ w_in
    qa, ka, va, qkv_b, a_b, beta_b, gate_b, merge_raw = jnp.split(z, IN_OFFSETS, axis=-1)
    o_a = chunk_band_attention(qa, ka, va, a_q_norm_w, a_k_norm_w, a_rel_bias)
    o_b = gated_deltanet(qkv_b, a_b, beta_b, gate_b, b_conv_w, b_a_log, b_dt_bias, b_norm_w)
    gates = jax.nn.sigmoid(merge_raw.astype(jnp.float32) + b_merge.astype(jnp.float32))
    gates = gates.reshape(b, s, N_BRANCHES, D_MODEL).astype(x.dtype)
    mixed = gates[..., 0, :] * (o_a @ w_proj_a) + gates[..., 1, :] * (o_b @ w_proj_b)
    x = x + mixed @ w_out
    x = x + peer_ffn(rms_norm(x, norm_ffn_w), peer_w_query, peer_keys_1, peer_keys_2, peer_down, peer_up)
    return x


def setup_inputs(seed: int = 0) -> dict:
    key = jax.random.key(seed)
    ks = jax.random.split(key, 20)
    f32 = jnp.float32
    L = DEPTH

    def nrm(k, shape, scale):
        return jax.random.normal(k, shape, f32) * scale

    def gain(k, n):
        return 1.0 + 0.02 * jax.random.normal(k, (L, n), f32)

    dt = jnp.exp(jax.random.uniform(ks[7], (L, B_HEADS), f32, math.log(DT_MIN), math.log(DT_MAX)))
    return {
        'x': nrm(ks[0], (BATCH, SEQ, D_MODEL), 1.0),
        'norm_mix_w': gain(ks[1], D_MODEL),
        'w_in': nrm(ks[2], (L, D_MODEL, IN_WIDTH), D_MODEL ** -0.5),
        'a_q_norm_w': gain(ks[3], A_HEAD_DIM),
        'a_k_norm_w': gain(ks[4], A_HEAD_DIM),
        'a_rel_bias': nrm(ks[5], (L, A_HEADS, 2 * REL_CLIP + 1), 0.5),
        'b_conv_w': nrm(ks[6], (L, CONV_WIDTH, 3 * B_WIDTH), CONV_WIDTH ** -0.5),
        'b_a_log': jnp.log(jax.random.uniform(ks[8], (L, B_HEADS), f32, 1.0, 16.0)),
        'b_dt_bias': dt + jnp.log(-jnp.expm1(-dt)),
        'b_norm_w': gain(ks[9], B_HEAD_DIM),
        'b_merge': nrm(ks[10], (L, N_BRANCHES * D_MODEL), 0.02),
        'w_proj_a': nrm(ks[11], (L, A_WIDTH, D_MODEL), A_WIDTH ** -0.5),
        'w_proj_b': nrm(ks[12], (L, B_WIDTH, D_MODEL), B_WIDTH ** -0.5),
        'w_out': nrm(ks[13], (L, D_MODEL, D_MODEL), D_MODEL ** -0.5),
        'norm_ffn_w': gain(ks[14], D_MODEL),
        'peer_w_query': nrm(ks[15], (L, D_MODEL, PEER_HEADS * PEER_QDIM), D_MODEL ** -0.5),
        'peer_keys_1': nrm(ks[16], (L, PEER_HEADS, PEER_NKEYS, PEER_HALF), PEER_HALF ** -0.5),
        'peer_keys_2': nrm(ks[17], (L, PEER_HEADS, PEER_NKEYS, PEER_HALF), PEER_HALF ** -0.5),
        'peer_down': nrm(ks[18], (L, PEER_N_EXPERTS, D_MODEL), D_MODEL ** -0.5),
        'peer_up': nrm(ks[19], (L, PEER_N_EXPERTS, D_MODEL), PEER_HEADS ** -0.5),
    }


def reference(x, norm_mix_w, w_in, a_q_norm_w, a_k_norm_w, a_rel_bias, b_conv_w, b_a_log,
              b_dt_bias, b_norm_w, b_merge, w_proj_a, w_proj_b, w_out, norm_ffn_w,
              peer_w_query, peer_keys_1, peer_keys_2, peer_down, peer_up):
    for l in range(DEPTH):
        x = hybrid_layer(x, norm_mix_w[l], w_in[l], a_q_norm_w[l], a_k_norm_w[l], a_rel_bias[l],
                         b_conv_w[l], b_a_log[l], b_dt_bias[l], b_norm_w[l], b_merge[l],
                         w_proj_a[l], w_proj_b[l], w_out[l], norm_ffn_w[l], peer_w_query[l],
                         peer_keys_1[l], peer_keys_2[l], peer_down[l], peer_up[l])
    return x
```

```python
import functools

import jax
import jax.numpy as jnp
from jax import lax
from jax.experimental import pallas as pl
from jax.experimental.pallas import tpu as pltpu

F32 = jnp.float32
BF16 = jnp.bfloat16

EPS = 1e-6
NEG = -1e30

CHUNK = 64
A_HEADS = 8
A_HEAD_DIM = 64
A_WIDTH = A_HEADS * A_HEAD_DIM
A_LEFT_CHUNKS = 8
REL_CLIP = 128
B_HEADS = 8
B_HEAD_DIM = 128
B_WIDTH = B_HEADS * B_HEAD_DIM
CONV_WIDTH = 4
PEER_HEADS = 8
PEER_HALF = 128
PEER_NKEYS = 128
PEER_TOPK = 16
PEER_SLOTS = PEER_HEADS * PEER_TOPK

LANES = 128
SUBLANES = 8
VMEM_LIMIT = 56 * 1024 * 1024

Z_QKV_B = 0
Z_GATE_B = 3 * B_WIDTH
Z_MERGE = Z_GATE_B + B_WIDTH
Z_SMALL_A = 0
Z_SMALL_BETA = B_HEADS


def _dot(a, b):
    return jnp.dot(a, b, preferred_element_type=F32)


def _dot_nt(a, b):
    return lax.dot_general(a, b, (((1,), (1,)), ((), ())), preferred_element_type=F32)


def _dot_tn(a, b):
    return lax.dot_general(a, b, (((0,), (0,)), ((), ())), preferred_element_type=F32)


def _split_bf16(a):
    hi = a.astype(BF16)
    lo = (a - hi.astype(F32)).astype(BF16)
    return hi, lo


def _sigmoid(x):
    return 1.0 / (1.0 + jnp.exp(-x))


def _inproj_body(x_ref, nw_ref, w_ref, ws_ref, z_ref, zs_ref, h_scr):
    @pl.when(pl.program_id(1) == 0)
    def _():
        x = x_ref[...]
        ms = jnp.mean(x * x, axis=-1, keepdims=True)
        hb = (x * lax.rsqrt(ms + EPS) * nw_ref[...]).astype(BF16)
        h_scr[...] = hb
        zs_ref[...] = _dot(hb, ws_ref[...])

    z_ref[...] = _dot(h_scr[...], w_ref[...]).astype(z_ref.dtype)


def _in_proj(x2, norm_w, w_main, w_small, tm, tn):
    t, d = x2.shape
    n = w_main.shape[1]
    return pl.pallas_call(
        _inproj_body,
        grid=(t // tm, n // tn),
        in_specs=[
            pl.BlockSpec((tm, d), lambda i, j: (i, 0)),
            pl.BlockSpec((1, d), lambda i, j: (0, 0)),
            pl.BlockSpec((d, tn), lambda i, j: (0, j)),
            pl.BlockSpec((d, LANES), lambda i, j: (0, 0)),
        ],
        out_specs=[
            pl.BlockSpec((tm, tn), lambda i, j: (i, j)),
            pl.BlockSpec((tm, LANES), lambda i, j: (i, 0)),
        ],
        out_shape=[
            jax.ShapeDtypeStruct((t, n), BF16),
            jax.ShapeDtypeStruct((t, LANES), F32),
        ],
        scratch_shapes=[pltpu.VMEM((tm, d), BF16)],
        compiler_params=pltpu.CompilerParams(
            dimension_semantics=("parallel", "arbitrary"), vmem_limit_bytes=VMEM_LIMIT),
        name="in_proj",
    )(x2, norm_w, w_main, w_small)


ATT_TQ = 256
ATT_WIN = ATT_TQ + A_LEFT_CHUNKS * CHUNK
ATT_NKB = ATT_WIN // ATT_TQ


def _attn_body(q_ref, k0_ref, k1_ref, k2_ref, v0_ref, v1_ref, v2_ref, bias_ref, qw_ref, kw_ref, seg_ref,
               o_ref):
    t = pl.program_id(1)
    seg = seg_ref[...]

    def head_rms(a, w):
        hi, lo = _split_bf16(a * a)
        ms = _dot(hi, seg) + _dot(lo, seg)
        return a * lax.rsqrt(ms + EPS) * w

    q = head_rms(q_ref[0].astype(F32), qw_ref[...]) * (A_HEAD_DIM ** -0.5)
    k = jnp.concatenate([k0_ref[0], k1_ref[0], k2_ref[0]], axis=0).astype(F32)
    k = head_rms(k, kw_ref[...]).astype(BF16)
    v = jnp.concatenate([v0_ref[0], v1_ref[0], v2_ref[0]], axis=0)

    kpos = t * ATT_TQ - A_LEFT_CHUNKS * CHUNK + lax.broadcasted_iota(jnp.int32, (1, ATT_WIN), 1)
    valid = kpos >= 0
    lane = lax.broadcasted_iota(jnp.int32, (1, LANES), 1)
    first = lane < A_HEAD_DIM

    for pair in range(A_HEADS // 2):
        cols = slice(pair * LANES, (pair + 1) * LANES)
        qp = q[:, cols]
        kp = k[:, cols]
        vp = v[:, cols]
        outs = []
        for half in range(2):
            h = 2 * pair + half
            qm = jnp.where(first if half == 0 else jnp.logical_not(first), qp, 0.0).astype(BF16)
            s = _dot_nt(qm, kp) + bias_ref[h]
            s = jnp.where(valid, s, NEG)
            mx = jnp.max(s, axis=-1, keepdims=True)
            p = jnp.exp(s - mx)
            den = jnp.sum(p, axis=-1, keepdims=True)
            outs.append(_dot(p.astype(BF16), vp) / den)
        o_ref[0, :, cols] = jnp.where(first, outs[0], outs[1]).astype(o_ref.dtype)


def _band_attention(z3, bias_tab, qw, kw, seg, col_q):
    b, s, _ = z3.shape
    nt = s // ATT_TQ

    def kv_spec(back, col):
        return pl.BlockSpec((1, ATT_TQ, A_WIDTH), lambda bi, ti: (bi, jnp.maximum(ti - back, 0), col))

    return pl.pallas_call(
        _attn_body,
        grid=(b, nt),
        in_specs=[
            pl.BlockSpec((1, ATT_TQ, A_WIDTH), lambda bi, ti: (bi, ti, col_q)),
            kv_spec(2, col_q + 1), kv_spec(1, col_q + 1), kv_spec(0, col_q + 1),
            kv_spec(2, col_q + 2), kv_spec(1, col_q + 2), kv_spec(0, col_q + 2),
            pl.BlockSpec((A_HEADS, ATT_TQ, ATT_WIN), lambda bi, ti: (0, 0, 0)),
            pl.BlockSpec((1, A_WIDTH), lambda bi, ti: (0, 0)),
            pl.BlockSpec((1, A_WIDTH), lambda bi, ti: (0, 0)),
            pl.BlockSpec((A_WIDTH, A_WIDTH), lambda bi, ti: (0, 0)),
        ],
        out_specs=pl.BlockSpec((1, ATT_TQ, A_WIDTH), lambda bi, ti: (bi, ti, 0)),
        out_shape=jax.ShapeDtypeStruct((b, s, A_WIDTH), BF16),
        compiler_params=pltpu.CompilerParams(
            dimension_semantics=("parallel", "parallel"), vmem_limit_bytes=VMEM_LIMIT),
        name="band_attn",
    )(z3, z3, z3, z3, z3, z3, z3, bias_tab, qw, kw, seg)


def _attn_bias_table(rel_bias):
    qi = jnp.arange(ATT_TQ)[:, None]
    kj = jnp.arange(ATT_WIN)[None, :]
    rel = jnp.clip(A_LEFT_CHUNKS * CHUNK + qi - kj, -REL_CLIP, REL_CLIP) + REL_CLIP
    qc = qi // CHUNK
    kc = kj // CHUNK
    in_band = (kc >= qc) & (kc <= qc + A_LEFT_CHUNKS)
    return jnp.where(in_band[None], rel_bias.astype(F32)[:, rel], NEG)


GDN_TAIL = SUBLANES


def _gdn_body(qkv_ref, zs_ref, gate_ref, convw_ref, hp_ref, nw_ref, o_ref, xbuf, s_scr):
    c = CHUNK
    hd = B_HEAD_DIM

    @pl.when(pl.program_id(1) == 0)
    def _():
        xbuf[0:GDN_TAIL, :] = jnp.zeros((GDN_TAIL, 3 * B_WIDTH), F32)
        s_scr[...] = jnp.zeros_like(s_scr)

    xbuf[GDN_TAIL:GDN_TAIL + c, :] = qkv_ref[0].astype(F32)
    y = None
    for j in range(CONV_WIDTH):
        tap = convw_ref[j:j + 1, :] * xbuf[pl.ds(GDN_TAIL - (CONV_WIDTH - 1) + j, c), :]
        y = tap if y is None else y + tap
    xbuf[0:GDN_TAIL, :] = xbuf[c:c + GDN_TAIL, :]
    y = y * _sigmoid(y)

    zs = zs_ref[0]
    a_neg = -(jnp.exp(hp_ref[0:1, :]) * hp_ref[2:3, :])
    xs = zs + hp_ref[1:2, :]
    g_all = a_neg * (jnp.maximum(xs, 0.0) + jnp.log(1.0 + jnp.exp(-jnp.abs(xs))))
    beta_all = _sigmoid(zs)

    row = lax.broadcasted_iota(jnp.int32, (c, c), 0)
    col = lax.broadcasted_iota(jnp.int32, (c, c), 1)
    tril = row >= col
    tril_strict = row > col
    eye = (row == col).astype(F32)
    lower = tril.astype(BF16)
    upper = (row <= col).astype(BF16)

    gh, gl = _split_bf16(g_all)
    gc_all = _dot(lower, gh) + _dot(lower, gl)
    gth, gtl = _split_bf16(g_all.T)
    gc_t = _dot(gth, upper) + _dot(gtl, upper)

    for h in range(B_HEADS):
        cols = slice(h * hd, (h + 1) * hd)
        qh = y[:, h * hd:(h + 1) * hd]
        kh = y[:, B_WIDTH + h * hd:B_WIDTH + (h + 1) * hd]
        vh = y[:, 2 * B_WIDTH + h * hd:2 * B_WIDTH + (h + 1) * hd]
        qh = qh * lax.rsqrt(jnp.sum(qh * qh, axis=-1, keepdims=True) + EPS) * (hd ** -0.5)
        kh = kh * lax.rsqrt(jnp.sum(kh * kh, axis=-1, keepdims=True) + EPS)
        beta = beta_all[:, Z_SMALL_BETA + h:Z_SMALL_BETA + h + 1]
        gc = gc_all[:, Z_SMALL_A + h:Z_SMALL_A + h + 1]
        gc_row = gc_t[Z_SMALL_A + h:Z_SMALL_A + h + 1, :]
        g_last = gc_row[:, c - 1:c]
        decay = jnp.exp(jnp.where(tril, gc - gc_row, NEG))
        kb = kh * beta
        vb = vh * beta
        kf = kh.astype(BF16)
        a_mat = jnp.where(tril_strict, _dot_nt(kb.astype(BF16), kf) * decay, 0.0)
        attn = _dot_nt(qh.astype(BF16), kf) * decay

        t_mat = eye - a_mat
        a_pow = a_mat
        for _ in range(5):
            ab = a_pow.astype(BF16)
            a_pow = _dot(ab, ab)
            t_mat = t_mat + _dot(t_mat.astype(BF16), a_pow.astype(BF16))
        tb = t_mat.astype(BF16)
        e_gc = jnp.exp(gc)
        u = _dot(tb, vb.astype(BF16))
        w = _dot(tb, (kb * e_gc).astype(BF16))

        state = s_scr[h]
        sb = state.astype(BF16)
        v_new = u - _dot(w.astype(BF16), sb)
        vnb = v_new.astype(BF16)
        o = _dot((qh * e_gc).astype(BF16), sb) + _dot(attn.astype(BF16), vnb)
        k_dec = kh * jnp.exp(g_last - gc)
        s_scr[h] = state * jnp.exp(g_last) + _dot_tn(k_dec.astype(BF16), vnb)

        gate = gate_ref[0, :, cols].astype(F32)
        on = o * lax.rsqrt(jnp.mean(o * o, axis=-1, keepdims=True) + EPS) * nw_ref[...]
        o_ref[0, :, cols] = (on * (gate * _sigmoid(gate))).astype(o_ref.dtype)


def _gated_deltanet(z3, zs3, conv_w, head_params, norm_w):
    b, s, _ = z3.shape
    n = s // CHUNK
    return pl.pallas_call(
        _gdn_body,
        grid=(b, n),
        in_specs=[
            pl.BlockSpec((1, CHUNK, 3 * B_WIDTH), lambda bi, ni: (bi, ni, Z_QKV_B // (3 * B_WIDTH))),
            pl.BlockSpec((1, CHUNK, LANES), lambda bi, ni: (bi, ni, 0)),
            pl.BlockSpec((1, CHUNK, B_WIDTH), lambda bi, ni: (bi, ni, Z_GATE_B // B_WIDTH)),
            pl.BlockSpec((CONV_WIDTH, 3 * B_WIDTH), lambda bi, ni: (0, 0)),
            pl.BlockSpec((SUBLANES, LANES), lambda bi, ni: (0, 0)),
            pl.BlockSpec((1, B_HEAD_DIM), lambda bi, ni: (0, 0)),
        ],
        out_specs=pl.BlockSpec((1, CHUNK, B_WIDTH), lambda bi, ni: (bi, ni, 0)),
        out_shape=jax.ShapeDtypeStruct((b, s, B_WIDTH), BF16),
        scratch_shapes=[
            pltpu.VMEM((GDN_TAIL + CHUNK, 3 * B_WIDTH), F32),
            pltpu.VMEM((B_HEADS, B_HEAD_DIM, B_HEAD_DIM), F32),
        ],
        compiler_params=pltpu.CompilerParams(
            dimension_semantics=("parallel", "arbitrary"), vmem_limit_bytes=VMEM_LIMIT),
        name="gdn",
    )(z3, zs3, z3, conv_w, head_params, norm_w)


MERGE_TM = 256


def _top_k_rows(s, k, payload=None):
    n = s.shape[0]
    rows = lax.broadcasted_iota(jnp.int32, s.shape, 0)
    vals, picks = [], []
    for _ in range(k):
        m = jnp.max(s, axis=0, keepdims=True)
        i = jnp.min(jnp.where(s == m, rows, n), axis=0, keepdims=True)
        hit = rows == i
        vals.append(m)
        picks.append(i if payload is None else jnp.max(jnp.where(hit, payload, -1), axis=0, keepdims=True))
        s = jnp.where(hit, -jnp.inf, s)
    return jnp.concatenate(vals, axis=0), jnp.concatenate(picks, axis=0)


def _merge_body(oa_ref, ob_ref, mr_ref, x_ref, bm_ref, pa_ref, pb_ref, wo_ref, nw_ref, wq_ref, k1_ref, k2_ref,
                x1_ref, xn_ref, idx_ref, gate_ref):
    d = x_ref.shape[1]
    mr = mr_ref[...].astype(F32) + bm_ref[...]
    mixed = (_sigmoid(mr[:, :d]) * _dot(oa_ref[...], pa_ref[...])
             + _sigmoid(mr[:, d:]) * _dot(ob_ref[...], pb_ref[...]))
    x1 = x_ref[...] + _dot(mixed.astype(BF16), wo_ref[...])
    x1_ref[...] = x1
    xn = x1 * lax.rsqrt(jnp.mean(x1 * x1, axis=-1, keepdims=True) + EPS) * nw_ref[...]
    xn_ref[...] = xn
    q = _dot(xn.astype(BF16), wq_ref[...]).astype(BF16)

    idx_rows, gate_rows = [], []
    for h in range(PEER_HEADS):
        q1 = q[:, (2 * h) * PEER_HALF:(2 * h + 1) * PEER_HALF]
        q2 = q[:, (2 * h + 1) * PEER_HALF:(2 * h + 2) * PEER_HALF]
        s1 = _dot_nt(k1_ref[h], q1)
        s2 = _dot_nt(k2_ref[h], q2)
        v1, i1 = _top_k_rows(s1, PEER_TOPK)
        v2, i2 = _top_k_rows(s2, PEER_TOPK)
        cand = jnp.concatenate([v1[a:a + 1] + v2 for a in range(PEER_TOPK)], axis=0)
        cand_idx = jnp.concatenate([i1[a:a + 1] * PEER_NKEYS + i2 for a in range(PEER_TOPK)], axis=0)
        top_s, expert = _top_k_rows(cand, PEER_TOPK, payload=cand_idx)
        e = jnp.exp(top_s - top_s[0:1])
        gate_rows.append(e / jnp.sum(e, axis=0, keepdims=True))
        idx_rows.append(expert)
    idx_ref[...] = jnp.concatenate(idx_rows, axis=0).T
    gate_ref[...] = jnp.concatenate(gate_rows, axis=0)


def _merge_route(o_a, o_b, z2, x2, b_merge, pa, pb, wo, nw, wq, k1, k2):
    t, d = x2.shape
    tm = MERGE_TM
    full = lambda a: pl.BlockSpec(a.shape, lambda i: (0,) * a.ndim)
    return pl.pallas_call(
        _merge_body,
        grid=(t // tm,),
        in_specs=[
            pl.BlockSpec((tm, A_WIDTH), lambda i: (i, 0)),
            pl.BlockSpec((tm, B_WIDTH), lambda i: (i, 0)),
            pl.BlockSpec((tm, 2 * d), lambda i: (i, Z_MERGE // (2 * d))),
            pl.BlockSpec((tm, d), lambda i: (i, 0)),
            full(b_merge), full(pa), full(pb), full(wo), full(nw), full(wq), full(k1), full(k2),
        ],
        out_specs=[
            pl.BlockSpec((tm, d), lambda i: (i, 0)),
            pl.BlockSpec((tm, d), lambda i: (i, 0)),
            pl.BlockSpec((tm, PEER_SLOTS), lambda i: (i, 0)),
            pl.BlockSpec((PEER_SLOTS, tm), lambda i: (0, i)),
        ],
        out_shape=[
            jax.ShapeDtypeStruct((t, d), F32),
            jax.ShapeDtypeStruct((t, d), F32),
            jax.ShapeDtypeStruct((t, PEER_SLOTS), jnp.int32),
            jax.ShapeDtypeStruct((PEER_SLOTS, t), F32),
        ],
        compiler_params=pltpu.CompilerParams(
            dimension_semantics=("parallel",), vmem_limit_bytes=VMEM_LIMIT),
        name="merge_route",
    )(o_a, o_b, z2, x2, b_merge, pa, pb, wo, nw, wq, k1, k2)


PEER_TP = 128
PEER_NBUF = 4
PEER_LOOK = PEER_NBUF - 1
PEER_UNROLL = 8


def _erf(x):
    return lax.erf(x)


def _peer_body(idx_ref, xn_ref, gate_ref, x1_ref, tab_ref, o_ref, buf, sem):
    d = xn_ref.shape[1]

    def issue(tok, slot):
        def body(j, carry):
            e = idx_ref[0, tok, j]
            pltpu.make_async_copy(tab_ref.at[pl.ds(e, 1), :], buf.at[slot, pl.ds(j, 1), :], sem.at[slot]).start()
            return carry
        lax.fori_loop(0, PEER_SLOTS, body, 0, unroll=PEER_UNROLL)

    def wait(slot):
        pltpu.make_async_copy(tab_ref.at[pl.ds(0, PEER_SLOTS), :], buf.at[slot], sem.at[slot]).wait()

    for s in range(PEER_LOOK):
        issue(s, s)

    lane = lax.broadcasted_iota(jnp.int32, (PEER_SLOTS, PEER_TP), 1)

    def token(tok, carry):
        slot = tok % PEER_NBUF

        @pl.when(tok + PEER_LOOK < PEER_TP)
        def _():
            issue(tok + PEER_LOOK, (tok + PEER_LOOK) % PEER_NBUF)

        wait(slot)
        x = xn_ref[pl.ds(tok, 1), :]
        act = jnp.sum(buf[slot, :, 0:d] * x, axis=-1, keepdims=True)
        gate = jnp.sum(jnp.where(lane == tok, gate_ref[...], 0.0), axis=-1, keepdims=True)
        w = gate * (0.5 * act * (1.0 + _erf(act * (2.0 ** -0.5))))
        out = jnp.sum(buf[slot, :, d:2 * d] * w, axis=0, keepdims=True)
        o_ref[pl.ds(tok, 1), :] = x1_ref[pl.ds(tok, 1), :] + out
        return carry

    lax.fori_loop(0, PEER_TP, token, 0)


def _peer(idx3, xn, gate_t, x1, table):
    t, d = xn.shape
    tp = PEER_TP
    return pl.pallas_call(
        _peer_body,
        grid=(t // tp,),
        in_specs=[
            pl.BlockSpec((1, tp, PEER_SLOTS), lambda i: (i, 0, 0), memory_space=pltpu.SMEM),
            pl.BlockSpec((tp, d), lambda i: (i, 0)),
            pl.BlockSpec((PEER_SLOTS, tp), lambda i: (0, i)),
            pl.BlockSpec((tp, d), lambda i: (i, 0)),
            pl.BlockSpec(memory_space=pl.ANY),
        ],
        out_specs=pl.BlockSpec((tp, d), lambda i: (i, 0)),
        out_shape=jax.ShapeDtypeStruct((t, d), F32),
        scratch_shapes=[
            pltpu.VMEM((PEER_NBUF, PEER_SLOTS, 2 * d), F32),
            pltpu.SemaphoreType.DMA((PEER_NBUF,)),
        ],
        compiler_params=pltpu.CompilerParams(
            dimension_semantics=("arbitrary",), vmem_limit_bytes=VMEM_LIMIT),
        name="peer",
    )(idx3, xn, gate_t, x1, table)


def _pad_lanes(a):
    return jnp.pad(a, ((0, 0), (0, LANES - a.shape[1])))


def kernel(x, norm_mix_w, w_in, a_q_norm_w, a_k_norm_w, a_rel_bias, b_conv_w, b_a_log, b_dt_bias, b_norm_w, b_merge, w_proj_a, w_proj_b, w_out, norm_ffn_w, peer_w_query, peer_keys_1, peer_keys_2, peer_down, peer_up):
    b, s, d = x.shape
    t = b * s
    depth = w_in.shape[0]
    for l in range(depth):
        x2 = x.reshape(t, d)

        wi = w_in[l]
        o_qkvb = 3 * A_WIDTH
        o_a = o_qkvb + 3 * B_WIDTH
        o_beta = o_a + B_HEADS
        o_gate = o_beta + B_HEADS
        o_merge = o_gate + B_WIDTH
        w_main = jnp.concatenate(
            [wi[:, o_qkvb:o_a], wi[:, o_gate:o_merge], wi[:, o_merge:], wi[:, :o_qkvb]], axis=1).astype(BF16)
        w_small = _pad_lanes(wi[:, o_a:o_gate]).astype(BF16)
        z_attn = Z_MERGE + 2 * d

        z, zs = _in_proj(x2, norm_mix_w[l][None, :], w_main, w_small, tm=1024, tn=1920)
        z3 = z.reshape(b, s, z.shape[1])

        seg = jnp.kron(jnp.eye(A_HEADS, dtype=F32), jnp.full((A_HEAD_DIM, A_HEAD_DIM), 1.0 / A_HEAD_DIM, F32))
        o_att = _band_attention(
            z3, _attn_bias_table(a_rel_bias[l]),
            jnp.tile(a_q_norm_w[l], A_HEADS)[None, :], jnp.tile(a_k_norm_w[l], A_HEADS)[None, :],
            seg.astype(BF16), z_attn // A_WIDTH)

        head_params = jnp.zeros((SUBLANES, LANES), F32)
        head_params = head_params.at[0, :B_HEADS].set(b_a_log[l]).at[1, :B_HEADS].set(b_dt_bias[l])
        head_params = head_params.at[2, :B_HEADS].set(1.0)
        o_gdn = _gated_deltanet(z3, zs.reshape(b, s, LANES), b_conv_w[l], head_params, b_norm_w[l][None, :])

        x1, xn, idx, gate_t = _merge_route(
            o_att.reshape(t, A_WIDTH), o_gdn.reshape(t, B_WIDTH), z, x2, b_merge[l][None, :],
            w_proj_a[l].astype(BF16), w_proj_b[l].astype(BF16), w_out[l].astype(BF16), norm_ffn_w[l][None, :],
            peer_w_query[l].astype(BF16), peer_keys_1[l].astype(BF16), peer_keys_2[l].astype(BF16))

        table = jnp.concatenate([peer_down[l], peer_up[l]], axis=1)
        out = _peer(idx.reshape(t // PEER_TP, PEER_TP, PEER_SLOTS), xn, gate_t, x1, table)
        x = out.reshape(b, s, d)
    return x
```

```python
import functools

import jax
import jax.numpy as jnp
from jax import lax
from jax.experimental import pallas as pl
from jax.experimental.pallas import tpu as pltpu

F32 = jnp.float32
BF16 = jnp.bfloat16

EPS = 1e-6
NEG = -1e30

CHUNK = 64
A_HEADS = 8
A_HEAD_DIM = 64
A_WIDTH = A_HEADS * A_HEAD_DIM
A_LEFT_CHUNKS = 8
REL_CLIP = 128
B_HEADS = 8
B_HEAD_DIM = 128
B_WIDTH = B_HEADS * B_HEAD_DIM
CONV_WIDTH = 4
PEER_HEADS = 8
PEER_HALF = 128
PEER_NKEYS = 128
PEER_TOPK = 16
PEER_SLOTS = PEER_HEADS * PEER_TOPK

LANES = 128
SUBLANES = 8
VMEM_LIMIT = 56 * 1024 * 1024

Z_QKV_B = 0
Z_GATE_B = 3 * B_WIDTH
Z_MERGE = Z_GATE_B + B_WIDTH
Z_SMALL_A = 0
Z_SMALL_BETA = B_HEADS


def _dot(a, b):
    return jnp.dot(a, b, preferred_element_type=F32)


def _dot_nt(a, b):
    return lax.dot_general(a, b, (((1,), (1,)), ((), ())), preferred_element_type=F32)


def _dot_tn(a, b):
    return lax.dot_general(a, b, (((0,), (0,)), ((), ())), preferred_element_type=F32)


def _split_bf16(a):
    hi = a.astype(BF16)
    lo = (a - hi.astype(F32)).astype(BF16)
    return hi, lo


def _sigmoid(x):
    return 1.0 / (1.0 + jnp.exp(-x))


def _inproj_body(x_ref, nw_ref, w_ref, ws_ref, z_ref, zs_ref, h_scr):
    @pl.when(pl.program_id(1) == 0)
    def _():
        x = x_ref[...]
        ms = jnp.mean(x * x, axis=-1, keepdims=True)
        hb = (x * lax.rsqrt(ms + EPS) * nw_ref[...]).astype(BF16)
        h_scr[...] = hb
        zs_ref[...] = _dot(hb, ws_ref[...])

    z_ref[...] = _dot(h_scr[...], w_ref[...]).astype(z_ref.dtype)


def _in_proj(x2, norm_w, w_main, w_small, tm, tn):
    t, d = x2.shape
    n = w_main.shape[1]
    return pl.pallas_call(
        _inproj_body,
        grid=(t // tm, n // tn),
        in_specs=[
            pl.BlockSpec((tm, d), lambda i, j: (i, 0)),
            pl.BlockSpec((1, d), lambda i, j: (0, 0)),
            pl.BlockSpec((d, tn), lambda i, j: (0, j)),
            pl.BlockSpec((d, LANES), lambda i, j: (0, 0)),
        ],
        out_specs=[
            pl.BlockSpec((tm, tn), lambda i, j: (i, j)),
            pl.BlockSpec((tm, LANES), lambda i, j: (i, 0)),
        ],
        out_shape=[
            jax.ShapeDtypeStruct((t, n), BF16),
            jax.ShapeDtypeStruct((t, LANES), F32),
        ],
        scratch_shapes=[pltpu.VMEM((tm, d), BF16)],
        compiler_params=pltpu.CompilerParams(
            dimension_semantics=("parallel", "arbitrary"), vmem_limit_bytes=VMEM_LIMIT),
        name="in_proj",
    )(x2, norm_w, w_main, w_small)


ATT_TQ = 256
ATT_WIN = ATT_TQ + A_LEFT_CHUNKS * CHUNK
ATT_NKB = ATT_WIN // ATT_TQ


def _attn_body(q_ref, k0_ref, k1_ref, k2_ref, v0_ref, v1_ref, v2_ref, bias_ref, qw_ref, kw_ref, seg_ref,
               o_ref):
    t = pl.program_id(1)
    seg = seg_ref[...]

    def head_rms(a, w):
        hi, lo = _split_bf16(a * a)
        ms = _dot(hi, seg) + _dot(lo, seg)
        return a * lax.rsqrt(ms + EPS) * w

    q = head_rms(q_ref[0].astype(F32), qw_ref[...]) * (A_HEAD_DIM ** -0.5)
    k = jnp.concatenate([k0_ref[0], k1_ref[0], k2_ref[0]], axis=0).astype(F32)
    k = head_rms(k, kw_ref[...]).astype(BF16)
    v = jnp.concatenate([v0_ref[0], v1_ref[0], v2_ref[0]], axis=0)

    kpos = t * ATT_TQ - A_LEFT_CHUNKS * CHUNK + lax.broadcasted_iota(jnp.int32, (1, ATT_WIN), 1)
    valid = kpos >= 0
    lane = lax.broadcasted_iota(jnp.int32, (1, LANES), 1)
    first = lane < A_HEAD_DIM

    for pair in range(A_HEADS // 2):
        cols = slice(pair * LANES, (pair + 1) * LANES)
        qp = q[:, cols]
        kp = k[:, cols]
        vp = v[:, cols]
        outs = []
        for half in range(2):
            h = 2 * pair + half
            qm = jnp.where(first if half == 0 else jnp.logical_not(first), qp, 0.0).astype(BF16)
            s = _dot_nt(qm, kp) + bias_ref[h]
            s = jnp.where(valid, s, NEG)
            mx = jnp.max(s, axis=-1, keepdims=True)
            p = jnp.exp(s - mx)
            den = jnp.sum(p, axis=-1, keepdims=True)
            outs.append(_dot(p.astype(BF16), vp) / den)
        o_ref[0, :, cols] = jnp.where(first, outs[0], outs[1]).astype(o_ref.dtype)


def _band_attention(z3, bias_tab, qw, kw, seg, col_q):
    b, s, _ = z3.shape
    nt = s // ATT_TQ

    def kv_spec(back, col):
        return pl.BlockSpec((1, ATT_TQ, A_WIDTH), lambda bi, ti: (bi, jnp.maximum(ti - back, 0), col))

    return pl.pallas_call(
        _attn_body,
        grid=(b, nt),
        in_specs=[
            pl.BlockSpec((1, ATT_TQ, A_WIDTH), lambda bi, ti: (bi, ti, col_q)),
            kv_spec(2, col_q + 1), kv_spec(1, col_q + 1), kv_spec(0, col_q + 1),
            kv_spec(2, col_q + 2), kv_spec(1, col_q + 2), kv_spec(0, col_q + 2),
            pl.BlockSpec((A_HEADS, ATT_TQ, ATT_WIN), lambda bi, ti: (0, 0, 0)),
            pl.BlockSpec((1, A_WIDTH), lambda bi, ti: (0, 0)),
            pl.BlockSpec((1, A_WIDTH), lambda bi, ti: (0, 0)),
            pl.BlockSpec((A_WIDTH, A_WIDTH), lambda bi, ti: (0, 0)),
        ],
        out_specs=pl.BlockSpec((1, ATT_TQ, A_WIDTH), lambda bi, ti: (bi, ti, 0)),
        out_shape=jax.ShapeDtypeStruct((b, s, A_WIDTH), BF16),
        compiler_params=pltpu.CompilerParams(
            dimension_semantics=("parallel", "parallel"), vmem_limit_bytes=VMEM_LIMIT),
        name="band_attn",
    )(z3, z3, z3, z3, z3, z3, z3, bias_tab, qw, kw, seg)


def _attn_bias_table(rel_bias):
    qi = jnp.arange(ATT_TQ)[:, None]
    kj = jnp.arange(ATT_WIN)[None, :]
    rel = jnp.clip(A_LEFT_CHUNKS * CHUNK + qi - kj, -REL_CLIP, REL_CLIP) + REL_CLIP
    qc = qi // CHUNK
    kc = kj // CHUNK
    in_band = (kc >= qc) & (kc <= qc + A_LEFT_CHUNKS)
    return jnp.where(in_band[None], rel_bias.astype(F32)[:, rel], NEG)


GDN_TAIL = SUBLANES


def _gdn_body(qkv_ref, zs_ref, gate_ref, convw_ref, hp_ref, nw_ref, o_ref, xbuf, s_scr):
    c = CHUNK
    hd = B_HEAD_DIM

    @pl.when(pl.program_id(1) == 0)
    def _():
        xbuf[0:GDN_TAIL, :] = jnp.zeros((GDN_TAIL, 3 * B_WIDTH), F32)
        s_scr[...] = jnp.zeros_like(s_scr)

    xbuf[GDN_TAIL:GDN_TAIL + c, :] = qkv_ref[0].astype(F32)
    y = None
    for j in range(CONV_WIDTH):
        tap = convw_ref[j:j + 1, :] * xbuf[pl.ds(GDN_TAIL - (CONV_WIDTH - 1) + j, c), :]
        y = tap if y is None else y + tap
    xbuf[0:GDN_TAIL, :] = xbuf[c:c + GDN_TAIL, :]
    y = y * _sigmoid(y)

    zs = zs_ref[0]
    a_neg = -(jnp.exp(hp_ref[0:1, :]) * hp_ref[2:3, :])
    xs = zs + hp_ref[1:2, :]
    g_all = a_neg * (jnp.maximum(xs, 0.0) + jnp.log(1.0 + jnp.exp(-jnp.abs(xs))))
    beta_all = _sigmoid(zs)

    row = lax.broadcasted_iota(jnp.int32, (c, c), 0)
    col = lax.broadcasted_iota(jnp.int32, (c, c), 1)
    tril = row >= col
    tril_strict = row > col
    eye = (row == col).astype(F32)
    lower = tril.astype(BF16)
    upper = (row <= col).astype(BF16)

    gh, gl = _split_bf16(g_all)
    gc_all = _dot(lower, gh) + _dot(lower, gl)
    gth, gtl = _split_bf16(g_all.T)
    gc_t = _dot(gth, upper) + _dot(gtl, upper)

    for h in range(B_HEADS):
        cols = slice(h * hd, (h + 1) * hd)
        qh = y[:, h * hd:(h + 1) * hd]
        kh = y[:, B_WIDTH + h * hd:B_WIDTH + (h + 1) * hd]
        vh = y[:, 2 * B_WIDTH + h * hd:2 * B_WIDTH + (h + 1) * hd]
        qh = qh * lax.rsqrt(jnp.sum(qh * qh, axis=-1, keepdims=True) + EPS) * (hd ** -0.5)
        kh = kh * lax.rsqrt(jnp.sum(kh * kh, axis=-1, keepdims=True) + EPS)
        beta = beta_all[:, Z_SMALL_BETA + h:Z_SMALL_BETA + h + 1]
        gc = gc_all[:, Z_SMALL_A + h:Z_SMALL_A + h + 1]
        gc_row = gc_t[Z_SMALL_A + h:Z_SMALL_A + h + 1, :]
        g_last = gc_row[:, c - 1:c]
        decay = jnp.exp(jnp.where(tril, gc - gc_row, NEG))
        kb = kh * beta
        vb = vh * beta
        kf = kh.astype(BF16)
        a_mat = jnp.where(tril_strict, _dot_nt(kb.astype(BF16), kf) * decay, 0.0)
        attn = _dot_nt(qh.astype(BF16), kf) * decay

        t_mat = eye - a_mat
        a_pow = a_mat
        for _ in range(5):
            ab = a_pow.astype(BF16)
            a_pow = _dot(ab, ab)
            t_mat = t_mat + _dot(t_mat.astype(BF16), a_pow.astype(BF16))
        tb = t_mat.astype(BF16)
        e_gc = jnp.exp(gc)
        u = _dot(tb, vb.astype(BF16))
        w = _dot(tb, (kb * e_gc).astype(BF16))

        state = s_scr[h]
        sb = state.astype(BF16)
        v_new = u - _dot(w.astype(BF16), sb)
        vnb = v_new.astype(BF16)
        o = _dot((qh * e_gc).astype(BF16), sb) + _dot(attn.astype(BF16), vnb)
        k_dec = kh * jnp.exp(g_last - gc)
        s_scr[h] = state * jnp.exp(g_last) + _dot_tn(k_dec.astype(BF16), vnb)

        gate = gate_ref[0, :, cols].astype(F32)
        on = o * lax.rsqrt(jnp.mean(o * o, axis=-1, keepdims=True) + EPS) * nw_ref[...]
        o_ref[0, :, cols] = (on * (gate * _sigmoid(gate))).astype(o_ref.dtype)


def _gated_deltanet(z3, zs3, conv_w, head_params, norm_w):
    b, s, _ = z3.shape
    n = s // CHUNK
    return pl.pallas_call(
        _gdn_body,
        grid=(b, n),
        in_specs=[
            pl.BlockSpec((1, CHUNK, 3 * B_WIDTH), lambda bi, ni: (bi, ni, Z_QKV_B // (3 * B_WIDTH))),
            pl.BlockSpec((1, CHUNK, LANES), lambda bi, ni: (bi, ni, 0)),
            pl.BlockSpec((1, CHUNK, B_WIDTH), lambda bi, ni: (bi, ni, Z_GATE_B // B_WIDTH)),
            pl.BlockSpec((CONV_WIDTH, 3 * B_WIDTH), lambda bi, ni: (0, 0)),
            pl.BlockSpec((SUBLANES, LANES), lambda bi, ni: (0, 0)),
            pl.BlockSpec((1, B_HEAD_DIM), lambda bi, ni: (0, 0)),
        ],
        out_specs=pl.BlockSpec((1, CHUNK, B_WIDTH), lambda bi, ni: (bi, ni, 0)),
        out_shape=jax.ShapeDtypeStruct((b, s, B_WIDTH), BF16),
        scratch_shapes=[
            pltpu.VMEM((GDN_TAIL + CHUNK, 3 * B_WIDTH), F32),
            pltpu.VMEM((B_HEADS, B_HEAD_DIM, B_HEAD_DIM), F32),
        ],
        compiler_params=pltpu.CompilerParams(
            dimension_semantics=("parallel", "arbitrary"), vmem_limit_bytes=VMEM_LIMIT),
        name="gdn",
    )(z3, zs3, z3, conv_w, head_params, norm_w)


MERGE_TM = 256


def _top_k_rows(s, k, payload=None):
    n = s.shape[0]
    rows = lax.broadcasted_iota(jnp.int32, s.shape, 0)
    vals, picks = [], []
    for _ in range(k):
        m = jnp.max(s, axis=0, keepdims=True)
        i = jnp.min(jnp.where(s == m, rows, n), axis=0, keepdims=True)
        hit = rows == i
        vals.append(m)
        picks.append(i if payload is None else jnp.max(jnp.where(hit, payload, -1), axis=0, keepdims=True))
        s = jnp.where(hit, -jnp.inf, s)
    return jnp.concatenate(vals, axis=0), jnp.concatenate(picks, axis=0)


def _merge_body(oa_ref, ob_ref, mr_ref, x_ref, bm_ref, pa_ref, pb_ref, wo_ref, nw_ref, wq_ref, k1_ref, k2_ref,
                x1_ref, xn_ref, idx_ref, gate_ref):
    d = x_ref.shape[1]
    mr = mr_ref[...].astype(F32) + bm_ref[...]
    mixed = (_sigmoid(mr[:, :d]) * _dot(oa_ref[...], pa_ref[...])
             + _sigmoid(mr[:, d:]) * _dot(ob_ref[...], pb_ref[...]))
    x1 = x_ref[...] + _dot(mixed.astype(BF16), wo_ref[...])
    x1_ref[...] = x1
    xn = x1 * lax.rsqrt(jnp.mean(x1 * x1, axis=-1, keepdims=True) + EPS) * nw_ref[...]
    xn_ref[...] = xn
    q = _dot(xn.astype(BF16), wq_ref[...]).astype(BF16)

    idx_rows, gate_rows = [], []
    for h in range(PEER_HEADS):
        q1 = q[:, (2 * h) * PEER_HALF:(2 * h + 1) * PEER_HALF]
        q2 = q[:, (2 * h + 1) * PEER_HALF:(2 * h + 2) * PEER_HALF]
        s1 = _dot_nt(k1_ref[h], q1)
        s2 = _dot_nt(k2_ref[h], q2)
        v1, i1 = _top_k_rows(s1, PEER_TOPK)
        v2, i2 = _top_k_rows(s2, PEER_TOPK)
        cand = jnp.concatenate([v1[a:a + 1] + v2 for a in range(PEER_TOPK)], axis=0)
        cand_idx = jnp.concatenate([i1[a:a + 1] * PEER_NKEYS + i2 for a in range(PEER_TOPK)], axis=0)
        top_s, expert = _top_k_rows(cand, PEER_TOPK, payload=cand_idx)
        e = jnp.exp(top_s - top_s[0:1])
        gate_rows.append(e / jnp.sum(e, axis=0, keepdims=True))
        idx_rows.append(expert)
    idx_ref[...] = jnp.concatenate(idx_rows, axis=0).T
    gate_ref[...] = jnp.concatenate(gate_rows, axis=0)


def _merge_route(o_a, o_b, z2, x2, b_merge, pa, pb, wo, nw, wq, k1, k2):
    t, d = x2.shape
    tm = MERGE_TM
    full = lambda a: pl.BlockSpec(a.shape, lambda i: (0,) * a.ndim)
    return pl.pallas_call(
        _merge_body,
        grid=(t // tm,),
        in_specs=[
            pl.BlockSpec((tm, A_WIDTH), lambda i: (i, 0)),
            pl.BlockSpec((tm, B_WIDTH), lambda i: (i, 0)),
            pl.BlockSpec((tm, 2 * d), lambda i: (i, Z_MERGE // (2 * d))),
            pl.BlockSpec((tm, d), lambda i: (i, 0)),
            full(b_merge), full(pa), full(pb), full(wo), full(nw), full(wq), full(k1), full(k2),
        ],
        out_specs=[
            pl.BlockSpec((tm, d), lambda i: (i, 0)),
            pl.BlockSpec((tm, d), lambda i: (i, 0)),
            pl.BlockSpec((tm, PEER_SLOTS), lambda i: (i, 0)),
            pl.BlockSpec((PEER_SLOTS, tm), lambda i: (0, i)),
        ],
        out_shape=[
            jax.ShapeDtypeStruct((t, d), F32),
            jax.ShapeDtypeStruct((t, d), F32),
            jax.ShapeDtypeStruct((t, PEER_SLOTS), jnp.int32),
            jax.ShapeDtypeStruct((PEER_SLOTS, t), F32),
        ],
        compiler_params=pltpu.CompilerParams(
            dimension_semantics=("parallel",), vmem_limit_bytes=VMEM_LIMIT),
        name="merge_route",
    )(o_a, o_b, z2, x2, b_merge, pa, pb, wo, nw, wq, k1, k2)


PEER_TP = 128
PEER_NBUF = 4
PEER_LOOK = PEER_NBUF - 1


def _erf(x):
    return lax.erf(x)


def _peer_body(idx_ref, idx_next_ref, xn_ref, gate_ref, x1_ref, tab_ref, o_ref, *scratch):
    bufs, sem = scratch[:PEER_NBUF], scratch[PEER_NBUF]
    planes = xn_ref.shape[1] // LANES
    step = pl.program_id(0)

    def issue(ref, tok, slot):
        for j in range(PEER_SLOTS):
            pltpu.make_async_copy(tab_ref.at[ref[0, tok, j]], bufs[slot].at[:, j, :], sem.at[slot]).start()

    def wait(slot):
        pltpu.make_async_copy(bufs[slot], bufs[slot], sem.at[slot]).wait()

    @pl.when(step == 0)
    def _():
        for s in range(PEER_LOOK):
            issue(idx_ref, s, s)

    lane = lax.broadcasted_iota(jnp.int32, (PEER_SLOTS, PEER_TP), 1)

    def compute(tok, slot):
        buf = bufs[slot]
        x = xn_ref[pl.ds(tok, 1), :]
        acc = None
        for s in range(planes):
            term = buf[s] * x[:, s * LANES:(s + 1) * LANES]
            acc = term if acc is None else acc + term
        act = jnp.sum(acc, axis=-1, keepdims=True)
        gate = jnp.sum(jnp.where(lane == tok, gate_ref[...], 0.0), axis=-1, keepdims=True)
        w = gate * (0.5 * act * (1.0 + _erf(act * (2.0 ** -0.5))))
        out = jnp.concatenate(
            [jnp.sum(buf[planes + s] * w, axis=0, keepdims=True) for s in range(planes)], axis=1)
        o_ref[pl.ds(tok, 1), :] = x1_ref[pl.ds(tok, 1), :] + out

    def group(g, carry):
        for u in range(PEER_NBUF):
            tok = g * PEER_NBUF + u
            issue(idx_ref, tok + PEER_LOOK, (u + PEER_LOOK) % PEER_NBUF)
            wait(u)
            compute(tok, u)
        return carry

    n_groups = PEER_TP // PEER_NBUF
    lax.fori_loop(0, n_groups - 1, group, 0)

    for u in range(PEER_NBUF):
        tok = (n_groups - 1) * PEER_NBUF + u
        nxt = tok + PEER_LOOK
        if nxt < PEER_TP:
            issue(idx_ref, nxt, (u + PEER_LOOK) % PEER_NBUF)
        else:
            issue(idx_next_ref, nxt - PEER_TP, (u + PEER_LOOK) % PEER_NBUF)
        wait(u)
        compute(tok, u)

    @pl.when(step == pl.num_programs(0) - 1)
    def _():
        for s in range(PEER_LOOK):
            wait(s)


def _peer(idx3, xn, gate_t, x1, table):
    t, d = xn.shape
    tp = PEER_TP
    nb = t // tp
    planes = 2 * d // LANES
    return pl.pallas_call(
        _peer_body,
        grid=(nb,),
        in_specs=[
            pl.BlockSpec((1, tp, PEER_SLOTS), lambda i: (i, 0, 0), memory_space=pltpu.SMEM),
            pl.BlockSpec((1, tp, PEER_SLOTS), lambda i: (jnp.minimum(i + 1, nb - 1), 0, 0),
                         memory_space=pltpu.SMEM),
            pl.BlockSpec((tp, d), lambda i: (i, 0)),
            pl.BlockSpec((PEER_SLOTS, tp), lambda i: (0, i)),
            pl.BlockSpec((tp, d), lambda i: (i, 0)),
            pl.BlockSpec(memory_space=pl.ANY),
        ],
        out_specs=pl.BlockSpec((tp, d), lambda i: (i, 0)),
        out_shape=jax.ShapeDtypeStruct((t, d), F32),
        scratch_shapes=[pltpu.VMEM((planes, PEER_SLOTS, LANES), F32) for _ in range(PEER_NBUF)]
        + [pltpu.SemaphoreType.DMA((PEER_NBUF,))],
        compiler_params=pltpu.CompilerParams(
            dimension_semantics=("arbitrary",), vmem_limit_bytes=VMEM_LIMIT),
        name="peer",
    )(idx3, idx3, xn, gate_t, x1, table)


def _pad_lanes(a):
    return jnp.pad(a, ((0, 0), (0, LANES - a.shape[1])))


def kernel(x, norm_mix_w, w_in, a_q_norm_w, a_k_norm_w, a_rel_bias, b_conv_w, b_a_log, b_dt_bias, b_norm_w, b_merge, w_proj_a, w_proj_b, w_out, norm_ffn_w, peer_w_query, peer_keys_1, peer_keys_2, peer_down, peer_up):
    b, s, d = x.shape
    t = b * s
    depth = w_in.shape[0]
    for l in range(depth):
        x2 = x.reshape(t, d)

        wi = w_in[l]
        o_qkvb = 3 * A_WIDTH
        o_a = o_qkvb + 3 * B_WIDTH
        o_beta = o_a + B_HEADS
        o_gate = o_beta + B_HEADS
        o_merge = o_gate + B_WIDTH
        w_main = jnp.concatenate(
            [wi[:, o_qkvb:o_a], wi[:, o_gate:o_merge], wi[:, o_merge:], wi[:, :o_qkvb]], axis=1).astype(BF16)
        w_small = _pad_lanes(wi[:, o_a:o_gate]).astype(BF16)
        z_attn = Z_MERGE + 2 * d

        z, zs = _in_proj(x2, norm_mix_w[l][None, :], w_main, w_small, tm=1024, tn=1920)
        z3 = z.reshape(b, s, z.shape[1])

        seg = jnp.kron(jnp.eye(A_HEADS, dtype=F32), jnp.full((A_HEAD_DIM, A_HEAD_DIM), 1.0 / A_HEAD_DIM, F32))
        o_att = _band_attention(
            z3, _attn_bias_table(a_rel_bias[l]),
            jnp.tile(a_q_norm_w[l], A_HEADS)[None, :], jnp.tile(a_k_norm_w[l], A_HEADS)[None, :],
            seg.astype(BF16), z_attn // A_WIDTH)

        head_params = jnp.zeros((SUBLANES, LANES), F32)
        head_params = head_params.at[0, :B_HEADS].set(b_a_log[l]).at[1, :B_HEADS].set(b_dt_bias[l])
        head_params = head_params.at[2, :B_HEADS].set(1.0)
        o_gdn = _gated_deltanet(z3, zs.reshape(b, s, LANES), b_conv_w[l], head_params, b_norm_w[l][None, :])

        x1, xn, idx, gate_t = _merge_route(
            o_att.reshape(t, A_WIDTH), o_gdn.reshape(t, B_WIDTH), z, x2, b_merge[l][None, :],
            w_proj_a[l].astype(BF16), w_proj_b[l].astype(BF16), w_out[l].astype(BF16), norm_ffn_w[l][None, :],
            peer_w_query[l].astype(BF16), peer_keys_1[l].astype(BF16), peer_keys_2[l].astype(BF16))

        table = jnp.concatenate([peer_down[l], peer_up[l]], axis=1).reshape(-1, 2 * d // LANES, LANES)
        out = _peer(idx.reshape(t // PEER_TP, PEER_TP, PEER_SLOTS), xn, gate_t, x1, table)
        x = out.reshape(b, s, d)
    return x
```

```python
import functools

import jax
import jax.numpy as jnp
from jax import lax
from jax.experimental import pallas as pl
from jax.experimental.pallas import tpu as pltpu

F32 = jnp.float32
BF16 = jnp.bfloat16

EPS = 1e-6
NEG = -1e30

CHUNK = 64
A_HEADS = 8
A_HEAD_DIM = 64
A_WIDTH = A_HEADS * A_HEAD_DIM
A_LEFT_CHUNKS = 8
REL_CLIP = 128
B_HEADS = 8
B_HEAD_DIM = 128
B_WIDTH = B_HEADS * B_HEAD_DIM
CONV_WIDTH = 4
PEER_HEADS = 8
PEER_HALF = 128
PEER_NKEYS = 128
PEER_TOPK = 16
PEER_SLOTS = PEER_HEADS * PEER_TOPK

LANES = 128
SUBLANES = 8
VMEM_LIMIT = 56 * 1024 * 1024

Z_QKV_B = 0
Z_GATE_B = 3 * B_WIDTH
Z_MERGE = Z_GATE_B + B_WIDTH
Z_SMALL_A = 0
Z_SMALL_BETA = B_HEADS


def _dot(a, b):
    return jnp.dot(a, b, preferred_element_type=F32)


def _dot_nt(a, b):
    return lax.dot_general(a, b, (((1,), (1,)), ((), ())), preferred_element_type=F32)


def _dot_tn(a, b):
    return lax.dot_general(a, b, (((0,), (0,)), ((), ())), preferred_element_type=F32)


def _split_bf16(a):
    hi = a.astype(BF16)
    lo = (a - hi.astype(F32)).astype(BF16)
    return hi, lo


def _sigmoid(x):
    return 1.0 / (1.0 + jnp.exp(-x))


def _inproj_body(x_ref, nw_ref, w_ref, ws_ref, z_ref, zs_ref, h_scr):
    @pl.when(pl.program_id(1) == 0)
    def _():
        x = x_ref[...]
        ms = jnp.mean(x * x, axis=-1, keepdims=True)
        hb = (x * lax.rsqrt(ms + EPS) * nw_ref[...]).astype(BF16)
        h_scr[...] = hb
        zs_ref[...] = _dot(hb, ws_ref[...])

    z_ref[...] = _dot(h_scr[...], w_ref[...]).astype(z_ref.dtype)


def _in_proj(x2, norm_w, w_main, w_small, tm, tn):
    t, d = x2.shape
    n = w_main.shape[1]
    return pl.pallas_call(
        _inproj_body,
        grid=(t // tm, n // tn),
        in_specs=[
            pl.BlockSpec((tm, d), lambda i, j: (i, 0)),
            pl.BlockSpec((1, d), lambda i, j: (0, 0)),
            pl.BlockSpec((d, tn), lambda i, j: (0, j)),
            pl.BlockSpec((d, LANES), lambda i, j: (0, 0)),
        ],
        out_specs=[
            pl.BlockSpec((tm, tn), lambda i, j: (i, j)),
            pl.BlockSpec((tm, LANES), lambda i, j: (i, 0)),
        ],
        out_shape=[
            jax.ShapeDtypeStruct((t, n), BF16),
            jax.ShapeDtypeStruct((t, LANES), F32),
        ],
        scratch_shapes=[pltpu.VMEM((tm, d), BF16)],
        compiler_params=pltpu.CompilerParams(
            dimension_semantics=("parallel", "arbitrary"), vmem_limit_bytes=VMEM_LIMIT),
        name="in_proj",
    )(x2, norm_w, w_main, w_small)


ATT_TQ = 256
ATT_WIN = ATT_TQ + A_LEFT_CHUNKS * CHUNK
ATT_NKB = ATT_WIN // ATT_TQ


def _attn_body(q_ref, k0_ref, k1_ref, k2_ref, v0_ref, v1_ref, v2_ref, bias_ref, qw_ref, kw_ref, seg_ref,
               o_ref):
    t = pl.program_id(1)
    seg = seg_ref[...]

    def head_rms(a, w):
        hi, lo = _split_bf16(a * a)
        ms = _dot(hi, seg) + _dot(lo, seg)
        return a * lax.rsqrt(ms + EPS) * w

    q = head_rms(q_ref[0].astype(F32), qw_ref[...]) * (A_HEAD_DIM ** -0.5)
    k = jnp.concatenate([k0_ref[0], k1_ref[0], k2_ref[0]], axis=0).astype(F32)
    k = head_rms(k, kw_ref[...]).astype(BF16)
    v = jnp.concatenate([v0_ref[0], v1_ref[0], v2_ref[0]], axis=0)

    kpos = t * ATT_TQ - A_LEFT_CHUNKS * CHUNK + lax.broadcasted_iota(jnp.int32, (1, ATT_WIN), 1)
    valid = kpos >= 0
    lane = lax.broadcasted_iota(jnp.int32, (1, LANES), 1)
    first = lane < A_HEAD_DIM

    for pair in range(A_HEADS // 2):
        cols = slice(pair * LANES, (pair + 1) * LANES)
        qp = q[:, cols]
        kp = k[:, cols]
        vp = v[:, cols]
        outs = []
        for half in range(2):
            h = 2 * pair + half
            qm = jnp.where(first if half == 0 else jnp.logical_not(first), qp, 0.0).astype(BF16)
            s = _dot_nt(qm, kp) + bias_ref[h]
            s = jnp.where(valid, s, NEG)
            mx = jnp.max(s, axis=-1, keepdims=True)
            p = jnp.exp(s - mx)
            den = jnp.sum(p, axis=-1, keepdims=True)
            outs.append(_dot(p.astype(BF16), vp) / den)
        o_ref[0, :, cols] = jnp.where(first, outs[0], outs[1]).astype(o_ref.dtype)


def _band_attention(z3, bias_tab, qw, kw, seg, col_q):
    b, s, _ = z3.shape
    nt = s // ATT_TQ

    def kv_spec(back, col):
        return pl.BlockSpec((1, ATT_TQ, A_WIDTH), lambda bi, ti: (bi, jnp.maximum(ti - back, 0), col))

    return pl.pallas_call(
        _attn_body,
        grid=(b, nt),
        in_specs=[
            pl.BlockSpec((1, ATT_TQ, A_WIDTH), lambda bi, ti: (bi, ti, col_q)),
            kv_spec(2, col_q + 1), kv_spec(1, col_q + 1), kv_spec(0, col_q + 1),
            kv_spec(2, col_q + 2), kv_spec(1, col_q + 2), kv_spec(0, col_q + 2),
            pl.BlockSpec((A_HEADS, ATT_TQ, ATT_WIN), lambda bi, ti: (0, 0, 0)),
            pl.BlockSpec((1, A_WIDTH), lambda bi, ti: (0, 0)),
            pl.BlockSpec((1, A_WIDTH), lambda bi, ti: (0, 0)),
            pl.BlockSpec((A_WIDTH, A_WIDTH), lambda bi, ti: (0, 0)),
        ],
        out_specs=pl.BlockSpec((1, ATT_TQ, A_WIDTH), lambda bi, ti: (bi, ti, 0)),
        out_shape=jax.ShapeDtypeStruct((b, s, A_WIDTH), BF16),
        compiler_params=pltpu.CompilerParams(
            dimension_semantics=("parallel", "parallel"), vmem_limit_bytes=VMEM_LIMIT),
        name="band_attn",
    )(z3, z3, z3, z3, z3, z3, z3, bias_tab, qw, kw, seg)


def _attn_bias_table(rel_bias):
    rb = rel_bias.astype(F32)
    heads = rb.shape[0]
    lo = A_LEFT_CHUNKS * CHUNK - (ATT_WIN - 1)
    hi = A_LEFT_CHUNKS * CHUNK + ATT_TQ - 1
    strip = jnp.concatenate([
        jnp.broadcast_to(rb[:, :1], (heads, -REL_CLIP - lo)), rb,
        jnp.broadcast_to(rb[:, -1:], (heads, hi - REL_CLIP))], axis=1)
    rev = strip[:, ::-1]
    n = rev.shape[1]
    flat = jnp.broadcast_to(rev[:, None, :], (heads, ATT_TQ, n)).reshape(heads, ATT_TQ * n)
    bias = flat[:, ATT_TQ - 1:ATT_TQ - 1 + ATT_TQ * (n - 1)].reshape(heads, ATT_TQ, n - 1)[:, :, :ATT_WIN]
    qc = jnp.arange(ATT_TQ)[:, None] // CHUNK
    kc = jnp.arange(ATT_WIN)[None, :] // CHUNK
    in_band = (kc >= qc) & (kc <= qc + A_LEFT_CHUNKS)
    return jnp.where(in_band[None], bias, NEG)


GDN_TAIL = SUBLANES


def _gdn_body(qkv_ref, zs_ref, gate_ref, convw_ref, hp_ref, nw_ref, o_ref, xbuf, s_scr):
    c = CHUNK
    hd = B_HEAD_DIM

    @pl.when(pl.program_id(1) == 0)
    def _():
        xbuf[0:GDN_TAIL, :] = jnp.zeros((GDN_TAIL, 3 * B_WIDTH), F32)
        s_scr[...] = jnp.zeros_like(s_scr)

    xbuf[GDN_TAIL:GDN_TAIL + c, :] = qkv_ref[0].astype(F32)
    y = None
    for j in range(CONV_WIDTH):
        tap = convw_ref[j:j + 1, :] * xbuf[pl.ds(GDN_TAIL - (CONV_WIDTH - 1) + j, c), :]
        y = tap if y is None else y + tap
    xbuf[0:GDN_TAIL, :] = xbuf[c:c + GDN_TAIL, :]
    y = y * _sigmoid(y)

    zs = zs_ref[0]
    a_neg = -(jnp.exp(hp_ref[0:1, :]) * hp_ref[2:3, :])
    xs = zs + hp_ref[1:2, :]
    g_all = a_neg * (jnp.maximum(xs, 0.0) + jnp.log(1.0 + jnp.exp(-jnp.abs(xs))))
    beta_all = _sigmoid(zs)

    row = lax.broadcasted_iota(jnp.int32, (c, c), 0)
    col = lax.broadcasted_iota(jnp.int32, (c, c), 1)
    tril = row >= col
    tril_strict = row > col
    eye = (row == col).astype(F32)
    lower = tril.astype(BF16)
    upper = (row <= col).astype(BF16)

    gh, gl = _split_bf16(g_all)
    gc_all = _dot(lower, gh) + _dot(lower, gl)
    gth, gtl = _split_bf16(g_all.T)
    gc_t = _dot(gth, upper) + _dot(gtl, upper)

    heads = range(B_HEADS)
    q, k, v, beta, gc, gc_row, g_last = [], [], [], [], [], [], []
    for h in heads:
        qh = y[:, h * hd:(h + 1) * hd]
        kh = y[:, B_WIDTH + h * hd:B_WIDTH + (h + 1) * hd]
        q.append(qh * lax.rsqrt(jnp.sum(qh * qh, axis=-1, keepdims=True) + EPS) * (hd ** -0.5))
        k.append(kh * lax.rsqrt(jnp.sum(kh * kh, axis=-1, keepdims=True) + EPS))
        v.append(y[:, 2 * B_WIDTH + h * hd:2 * B_WIDTH + (h + 1) * hd])
        beta.append(beta_all[:, Z_SMALL_BETA + h:Z_SMALL_BETA + h + 1])
        gc.append(gc_all[:, Z_SMALL_A + h:Z_SMALL_A + h + 1])
        gc_row.append(gc_t[Z_SMALL_A + h:Z_SMALL_A + h + 1, :])
        g_last.append(gc_row[h][:, c - 1:c])
    decay = [jnp.exp(jnp.where(tril, gc[h] - gc_row[h], NEG)) for h in heads]
    e_gc = [jnp.exp(gc[h]) for h in heads]
    kb = [k[h] * beta[h] for h in heads]
    kf = [k[h].astype(BF16) for h in heads]
    a_pow = [jnp.where(tril_strict, _dot_nt(kb[h].astype(BF16), kf[h]) * decay[h], 0.0) for h in heads]
    attn = [(_dot_nt(q[h].astype(BF16), kf[h]) * decay[h]).astype(BF16) for h in heads]

    t_mat = [eye - a_pow[h] for h in heads]
    for _ in range(5):
        ab = [a_pow[h].astype(BF16) for h in heads]
        a_pow = [_dot(ab[h], ab[h]) for h in heads]
        t_mat = [t_mat[h] + _dot(t_mat[h].astype(BF16), a_pow[h].astype(BF16)) for h in heads]
    tb = [t_mat[h].astype(BF16) for h in heads]
    u = [_dot(tb[h], (v[h] * beta[h]).astype(BF16)) for h in heads]
    w = [_dot(tb[h], (kb[h] * e_gc[h]).astype(BF16)).astype(BF16) for h in heads]

    state = [s_scr[h] for h in heads]
    sb = [state[h].astype(BF16) for h in heads]
    vnb = [(u[h] - _dot(w[h], sb[h])).astype(BF16) for h in heads]
    o = [_dot((q[h] * e_gc[h]).astype(BF16), sb[h]) + _dot(attn[h], vnb[h]) for h in heads]
    k_dec = [(k[h] * jnp.exp(g_last[h] - gc[h])).astype(BF16) for h in heads]
    for h in heads:
        s_scr[h] = state[h] * jnp.exp(g_last[h]) + _dot_tn(k_dec[h], vnb[h])
    for h in heads:
        cols = slice(h * hd, (h + 1) * hd)
        gate = gate_ref[0, :, cols].astype(F32)
        on = o[h] * lax.rsqrt(jnp.mean(o[h] * o[h], axis=-1, keepdims=True) + EPS) * nw_ref[...]
        o_ref[0, :, cols] = (on * (gate * _sigmoid(gate))).astype(o_ref.dtype)


def _gated_deltanet(z3, zs3, conv_w, head_params, norm_w):
    b, s, _ = z3.shape
    n = s // CHUNK
    return pl.pallas_call(
        _gdn_body,
        grid=(b, n),
        in_specs=[
            pl.BlockSpec((1, CHUNK, 3 * B_WIDTH), lambda bi, ni: (bi, ni, Z_QKV_B // (3 * B_WIDTH))),
            pl.BlockSpec((1, CHUNK, LANES), lambda bi, ni: (bi, ni, 0)),
            pl.BlockSpec((1, CHUNK, B_WIDTH), lambda bi, ni: (bi, ni, Z_GATE_B // B_WIDTH)),
            pl.BlockSpec((CONV_WIDTH, 3 * B_WIDTH), lambda bi, ni: (0, 0)),
            pl.BlockSpec((SUBLANES, LANES), lambda bi, ni: (0, 0)),
            pl.BlockSpec((1, B_HEAD_DIM), lambda bi, ni: (0, 0)),
        ],
        out_specs=pl.BlockSpec((1, CHUNK, B_WIDTH), lambda bi, ni: (bi, ni, 0)),
        out_shape=jax.ShapeDtypeStruct((b, s, B_WIDTH), BF16),
        scratch_shapes=[
            pltpu.VMEM((GDN_TAIL + CHUNK, 3 * B_WIDTH), F32),
            pltpu.VMEM((B_HEADS, B_HEAD_DIM, B_HEAD_DIM), F32),
        ],
        compiler_params=pltpu.CompilerParams(
            dimension_semantics=("parallel", "arbitrary"), vmem_limit_bytes=VMEM_LIMIT),
        name="gdn",
    )(z3, zs3, z3, conv_w, head_params, norm_w)


MERGE_TM = 256


def _top_k_rows(s, k, payload=None):
    n = s.shape[0]
    rows = lax.broadcasted_iota(jnp.int32, s.shape, 0)
    vals, picks = [], []
    for _ in range(k):
        m = jnp.max(s, axis=0, keepdims=True)
        i = jnp.min(jnp.where(s == m, rows, n), axis=0, keepdims=True)
        hit = rows == i
        vals.append(m)
        picks.append(i if payload is None else jnp.max(jnp.where(hit, payload, -1), axis=0, keepdims=True))
        s = jnp.where(hit, -jnp.inf, s)
    return jnp.concatenate(vals, axis=0), jnp.concatenate(picks, axis=0)


def _merge_body(oa_ref, ob_ref, mr_ref, x_ref, bm_ref, pa_ref, pb_ref, wo_ref, nw_ref, wq_ref, k1_ref, k2_ref,
                x1_ref, xn_ref, idx_ref, gate_ref):
    d = x_ref.shape[1]
    mr = mr_ref[...].astype(F32) + bm_ref[...]
    mixed = (_sigmoid(mr[:, :d]) * _dot(oa_ref[...], pa_ref[...])
             + _sigmoid(mr[:, d:]) * _dot(ob_ref[...], pb_ref[...]))
    x1 = x_ref[...] + _dot(mixed.astype(BF16), wo_ref[...])
    x1_ref[...] = x1
    xn = x1 * lax.rsqrt(jnp.mean(x1 * x1, axis=-1, keepdims=True) + EPS) * nw_ref[...]
    xn_ref[...] = xn
    q = _dot(xn.astype(BF16), wq_ref[...]).astype(BF16)

    idx_rows, gate_rows = [], []
    for h in range(PEER_HEADS):
        q1 = q[:, (2 * h) * PEER_HALF:(2 * h + 1) * PEER_HALF]
        q2 = q[:, (2 * h + 1) * PEER_HALF:(2 * h + 2) * PEER_HALF]
        s1 = _dot_nt(k1_ref[h], q1)
        s2 = _dot_nt(k2_ref[h], q2)
        v1, i1 = _top_k_rows(s1, PEER_TOPK)
        v2, i2 = _top_k_rows(s2, PEER_TOPK)
        nb = [PEER_TOPK // (a + 1) for a in range(PEER_TOPK)]
        pad = -sum(nb) % SUBLANES
        tm = v1.shape[1]
        cand = jnp.concatenate([v1[a:a + 1] + v2[:nb[a]] for a in range(PEER_TOPK)]
                               + [jnp.full((pad, tm), -jnp.inf, F32)], axis=0)
        cand_idx = jnp.concatenate([i1[a:a + 1] * PEER_NKEYS + i2[:nb[a]] for a in range(PEER_TOPK)]
                                   + [jnp.zeros((pad, tm), jnp.int32)], axis=0)
        top_s, expert = _top_k_rows(cand, PEER_TOPK, payload=cand_idx)
        e = jnp.exp(top_s - top_s[0:1])
        gate_rows.append(e / jnp.sum(e, axis=0, keepdims=True))
        idx_rows.append(expert)
    idx_ref[...] = jnp.concatenate(idx_rows, axis=0).T
    gate_ref[...] = jnp.concatenate(gate_rows, axis=0)


def _merge_route(o_a, o_b, z2, x2, b_merge, pa, pb, wo, nw, wq, k1, k2):
    t, d = x2.shape
    tm = MERGE_TM
    full = lambda a: pl.BlockSpec(a.shape, lambda i: (0,) * a.ndim)
    return pl.pallas_call(
        _merge_body,
        grid=(t // tm,),
        in_specs=[
            pl.BlockSpec((tm, A_WIDTH), lambda i: (i, 0)),
            pl.BlockSpec((tm, B_WIDTH), lambda i: (i, 0)),
            pl.BlockSpec((tm, 2 * d), lambda i: (i, Z_MERGE // (2 * d))),
            pl.BlockSpec((tm, d), lambda i: (i, 0)),
            full(b_merge), full(pa), full(pb), full(wo), full(nw), full(wq), full(k1), full(k2),
        ],
        out_specs=[
            pl.BlockSpec((tm, d), lambda i: (i, 0)),
            pl.BlockSpec((tm, d), lambda i: (i, 0)),
            pl.BlockSpec((tm, PEER_SLOTS), lambda i: (i, 0)),
            pl.BlockSpec((PEER_SLOTS, tm), lambda i: (0, i)),
        ],
        out_shape=[
            jax.ShapeDtypeStruct((t, d), F32),
            jax.ShapeDtypeStruct((t, d), F32),
            jax.ShapeDtypeStruct((t, PEER_SLOTS), jnp.int32),
            jax.ShapeDtypeStruct((PEER_SLOTS, t), F32),
        ],
        compiler_params=pltpu.CompilerParams(
            dimension_semantics=("parallel",), vmem_limit_bytes=VMEM_LIMIT),
        name="merge_route",
    )(o_a, o_b, z2, x2, b_merge, pa, pb, wo, nw, wq, k1, k2)


PEER_TP = 128
PEER_NBUF = 4
PEER_LOOK = PEER_NBUF - 1


def _erf(x):
    return lax.erf(x)


def _peer_body(idx_ref, idx_next_ref, xn_ref, gate_ref, x1_ref, tab_ref, o_ref, *scratch):
    bufs, sem = scratch[:PEER_NBUF], scratch[PEER_NBUF]
    planes = xn_ref.shape[1] // LANES
    step = pl.program_id(0)

    def issue(ref, tok, slot):
        for j in range(PEER_SLOTS):
            pltpu.make_async_copy(tab_ref.at[ref[0, tok, j]], bufs[slot].at[:, j, :], sem.at[slot]).start()

    def wait(slot):
        pltpu.make_async_copy(bufs[slot], bufs[slot], sem.at[slot]).wait()

    @pl.when(step == 0)
    def _():
        for s in range(PEER_LOOK):
            issue(idx_ref, s, s)

    lane = lax.broadcasted_iota(jnp.int32, (PEER_SLOTS, PEER_TP), 1)

    def compute(tok, slot):
        buf = bufs[slot]
        x = xn_ref[pl.ds(tok, 1), :]
        acc = None
        for s in range(planes):
            term = buf[s] * x[:, s * LANES:(s + 1) * LANES]
            acc = term if acc is None else acc + term
        act = jnp.sum(acc, axis=-1, keepdims=True)
        gate = jnp.sum(jnp.where(lane == tok, gate_ref[...], 0.0), axis=-1, keepdims=True)
        w = gate * (0.5 * act * (1.0 + _erf(act * (2.0 ** -0.5))))
        out = jnp.concatenate(
            [jnp.sum(buf[planes + s] * w, axis=0, keepdims=True) for s in range(planes)], axis=1)
        o_ref[pl.ds(tok, 1), :] = x1_ref[pl.ds(tok, 1), :] + out

    def group(g, carry):
        for u in range(PEER_NBUF):
            tok = g * PEER_NBUF + u
            issue(idx_ref, tok + PEER_LOOK, (u + PEER_LOOK) % PEER_NBUF)
            wait(u)
            compute(tok, u)
        return carry

    n_groups = PEER_TP // PEER_NBUF
    lax.fori_loop(0, n_groups - 1, group, 0)

    for u in range(PEER_NBUF):
        tok = (n_groups - 1) * PEER_NBUF + u
        nxt = tok + PEER_LOOK
        if nxt < PEER_TP:
            issue(idx_ref, nxt, (u + PEER_LOOK) % PEER_NBUF)
        else:
            issue(idx_next_ref, nxt - PEER_TP, (u + PEER_LOOK) % PEER_NBUF)
        wait(u)
        compute(tok, u)

    @pl.when(step == pl.num_programs(0) - 1)
    def _():
        for s in range(PEER_LOOK):
            wait(s)


def _peer(idx3, xn, gate_t, x1, table):
    t, d = xn.shape
    tp = PEER_TP
    nb = t // tp
    planes = 2 * d // LANES
    return pl.pallas_call(
        _peer_body,
        grid=(nb,),
        in_specs=[
            pl.BlockSpec((1, tp, PEER_SLOTS), lambda i: (i, 0, 0), memory_space=pltpu.SMEM),
            pl.BlockSpec((1, tp, PEER_SLOTS), lambda i: (jnp.minimum(i + 1, nb - 1), 0, 0),
                         memory_space=pltpu.SMEM),
            pl.BlockSpec((tp, d), lambda i: (i, 0)),
            pl.BlockSpec((PEER_SLOTS, tp), lambda i: (0, i)),
            pl.BlockSpec((tp, d), lambda i: (i, 0)),
            pl.BlockSpec(memory_space=pl.ANY),
        ],
        out_specs=pl.BlockSpec((tp, d), lambda i: (i, 0)),
        out_shape=jax.ShapeDtypeStruct((t, d), F32),
        scratch_shapes=[pltpu.VMEM((planes, PEER_SLOTS, LANES), F32) for _ in range(PEER_NBUF)]
        + [pltpu.SemaphoreType.DMA((PEER_NBUF,))],
        compiler_params=pltpu.CompilerParams(
            dimension_semantics=("arbitrary",), vmem_limit_bytes=VMEM_LIMIT),
        name="peer",
    )(idx3, idx3, xn, gate_t, x1, table)


def _pad_lanes(a):
    return jnp.pad(a, ((0, 0), (0, LANES - a.shape[1])))


def kernel(x, norm_mix_w, w_in, a_q_norm_w, a_k_norm_w, a_rel_bias, b_conv_w, b_a_log, b_dt_bias, b_norm_w, b_merge, w_proj_a, w_proj_b, w_out, norm_ffn_w, peer_w_query, peer_keys_1, peer_keys_2, peer_down, peer_up):
    b, s, d = x.shape
    t = b * s
    depth = w_in.shape[0]
    for l in range(depth):
        x2 = x.reshape(t, d)

        wi = w_in[l]
        o_qkvb = 3 * A_WIDTH
        o_a = o_qkvb + 3 * B_WIDTH
        o_beta = o_a + B_HEADS
        o_gate = o_beta + B_HEADS
        o_merge = o_gate + B_WIDTH
        w_main = jnp.concatenate(
            [wi[:, o_qkvb:o_a], wi[:, o_gate:o_merge], wi[:, o_merge:], wi[:, :o_qkvb]], axis=1).astype(BF16)
        w_small = _pad_lanes(wi[:, o_a:o_gate]).astype(BF16)
        z_attn = Z_MERGE + 2 * d

        z, zs = _in_proj(x2, norm_mix_w[l][None, :], w_main, w_small, tm=1024, tn=1920)
        z3 = z.reshape(b, s, z.shape[1])

        seg = jnp.kron(jnp.eye(A_HEADS, dtype=F32), jnp.full((A_HEAD_DIM, A_HEAD_DIM), 1.0 / A_HEAD_DIM, F32))
        o_att = _band_attention(
            z3, _attn_bias_table(a_rel_bias[l]),
            jnp.tile(a_q_norm_w[l], A_HEADS)[None, :], jnp.tile(a_k_norm_w[l], A_HEADS)[None, :],
            seg.astype(BF16), z_attn // A_WIDTH)

        head_params = jnp.zeros((SUBLANES, LANES), F32)
        head_params = head_params.at[0, :B_HEADS].set(b_a_log[l]).at[1, :B_HEADS].set(b_dt_bias[l])
        head_params = head_params.at[2, :B_HEADS].set(1.0)
        o_gdn = _gated_deltanet(z3, zs.reshape(b, s, LANES), b_conv_w[l], head_params, b_norm_w[l][None, :])

        x1, xn, idx, gate_t = _merge_route(
            o_att.reshape(t, A_WIDTH), o_gdn.reshape(t, B_WIDTH), z, x2, b_merge[l][None, :],
            w_proj_a[l].astype(BF16), w_proj_b[l].astype(BF16), w_out[l].astype(BF16), norm_ffn_w[l][None, :],
            peer_w_query[l].astype(BF16), peer_keys_1[l].astype(BF16), peer_keys_2[l].astype(BF16))

        table = jnp.concatenate([peer_down[l], peer_up[l]], axis=1).reshape(-1, 2 * d // LANES, LANES)
        out = _peer(idx.reshape(t // PEER_TP, PEER_TP, PEER_SLOTS), xn, gate_t, x1, table)
        x = out.reshape(b, s, d)
    return x
```

```python
import functools

import jax
import jax.numpy as jnp
from jax import lax
from jax.experimental import pallas as pl
from jax.experimental.pallas import tpu as pltpu
from jax.experimental.pallas import tpu_sc as plsc

F32 = jnp.float32
BF16 = jnp.bfloat16

EPS = 1e-6
NEG = -1e30

CHUNK = 64
A_HEADS = 8
A_HEAD_DIM = 64
A_WIDTH = A_HEADS * A_HEAD_DIM
A_LEFT_CHUNKS = 8
REL_CLIP = 128
B_HEADS = 8
B_HEAD_DIM = 128
B_WIDTH = B_HEADS * B_HEAD_DIM
CONV_WIDTH = 4
PEER_HEADS = 8
PEER_HALF = 128
PEER_NKEYS = 128
PEER_TOPK = 16
PEER_SLOTS = PEER_HEADS * PEER_TOPK

LANES = 128
SUBLANES = 8
VMEM_LIMIT = 56 * 1024 * 1024

Z_QKV_B = 0
Z_GATE_B = 3 * B_WIDTH
Z_MERGE = Z_GATE_B + B_WIDTH
Z_SMALL_A = 0
Z_SMALL_BETA = B_HEADS


def _dot(a, b):
    return jnp.dot(a, b, preferred_element_type=F32)


def _dot_nt(a, b):
    return lax.dot_general(a, b, (((1,), (1,)), ((), ())), preferred_element_type=F32)


def _dot_tn(a, b):
    return lax.dot_general(a, b, (((0,), (0,)), ((), ())), preferred_element_type=F32)


def _split_bf16(a):
    hi = a.astype(BF16)
    lo = (a - hi.astype(F32)).astype(BF16)
    return hi, lo


def _sigmoid(x):
    return 1.0 / (1.0 + jnp.exp(-x))


def _inproj_body(x_ref, nw_ref, w_ref, ws_ref, z_ref, zs_ref, h_scr):
    @pl.when(pl.program_id(1) == 0)
    def _():
        x = x_ref[...]
        ms = jnp.mean(x * x, axis=-1, keepdims=True)
        hb = (x * lax.rsqrt(ms + EPS) * nw_ref[...]).astype(BF16)
        h_scr[...] = hb
        zs_ref[...] = _dot(hb, ws_ref[...])

    z_ref[...] = _dot(h_scr[...], w_ref[...]).astype(z_ref.dtype)


def _in_proj(x2, norm_w, w_main, w_small, tm, tn):
    t, d = x2.shape
    n = w_main.shape[1]
    return pl.pallas_call(
        _inproj_body,
        grid=(t // tm, n // tn),
        in_specs=[
            pl.BlockSpec((tm, d), lambda i, j: (i, 0)),
            pl.BlockSpec((1, d), lambda i, j: (0, 0)),
            pl.BlockSpec((d, tn), lambda i, j: (0, j)),
            pl.BlockSpec((d, LANES), lambda i, j: (0, 0)),
        ],
        out_specs=[
            pl.BlockSpec((tm, tn), lambda i, j: (i, j)),
            pl.BlockSpec((tm, LANES), lambda i, j: (i, 0)),
        ],
        out_shape=[
            jax.ShapeDtypeStruct((t, n), BF16),
            jax.ShapeDtypeStruct((t, LANES), F32),
        ],
        scratch_shapes=[pltpu.VMEM((tm, d), BF16)],
        compiler_params=pltpu.CompilerParams(
            dimension_semantics=("parallel", "arbitrary"), vmem_limit_bytes=VMEM_LIMIT),
        name="in_proj",
    )(x2, norm_w, w_main, w_small)


ATT_TQ = 256
ATT_WIN = ATT_TQ + A_LEFT_CHUNKS * CHUNK
ATT_NKB = ATT_WIN // ATT_TQ


def _attn_body(q_ref, k0_ref, k1_ref, k2_ref, v0_ref, v1_ref, v2_ref, bias_ref, qw_ref, kw_ref, seg_ref,
               o_ref):
    t = pl.program_id(1)
    seg = seg_ref[...]

    def head_rms(a, w):
        hi, lo = _split_bf16(a * a)
        ms = _dot(hi, seg) + _dot(lo, seg)
        return a * lax.rsqrt(ms + EPS) * w

    q = head_rms(q_ref[0].astype(F32), qw_ref[...]) * (A_HEAD_DIM ** -0.5)
    k = jnp.concatenate([k0_ref[0], k1_ref[0], k2_ref[0]], axis=0).astype(F32)
    k = head_rms(k, kw_ref[...]).astype(BF16)
    v = jnp.concatenate([v0_ref[0], v1_ref[0], v2_ref[0]], axis=0)

    kpos = t * ATT_TQ - A_LEFT_CHUNKS * CHUNK + lax.broadcasted_iota(jnp.int32, (1, ATT_WIN), 1)
    valid = kpos >= 0
    lane = lax.broadcasted_iota(jnp.int32, (1, LANES), 1)
    first = lane < A_HEAD_DIM

    for pair in range(A_HEADS // 2):
        cols = slice(pair * LANES, (pair + 1) * LANES)
        qp = q[:, cols]
        kp = k[:, cols]
        vp = v[:, cols]
        outs = []
        for half in range(2):
            h = 2 * pair + half
            qm = jnp.where(first if half == 0 else jnp.logical_not(first), qp, 0.0).astype(BF16)
            s = _dot_nt(qm, kp) + bias_ref[h]
            s = jnp.where(valid, s, NEG)
            mx = jnp.max(s, axis=-1, keepdims=True)
            p = jnp.exp(s - mx)
            den = jnp.sum(p, axis=-1, keepdims=True)
            outs.append(_dot(p.astype(BF16), vp) / den)
        o_ref[0, :, cols] = jnp.where(first, outs[0], outs[1]).astype(o_ref.dtype)


def _band_attention(z3, bias_tab, qw, kw, seg, col_q):
    b, s, _ = z3.shape
    nt = s // ATT_TQ

    def kv_spec(back, col):
        return pl.BlockSpec((1, ATT_TQ, A_WIDTH), lambda bi, ti: (bi, jnp.maximum(ti - back, 0), col))

    return pl.pallas_call(
        _attn_body,
        grid=(b, nt),
        in_specs=[
            pl.BlockSpec((1, ATT_TQ, A_WIDTH), lambda bi, ti: (bi, ti, col_q)),
            kv_spec(2, col_q + 1), kv_spec(1, col_q + 1), kv_spec(0, col_q + 1),
            kv_spec(2, col_q + 2), kv_spec(1, col_q + 2), kv_spec(0, col_q + 2),
            pl.BlockSpec((A_HEADS, ATT_TQ, ATT_WIN), lambda bi, ti: (0, 0, 0)),
            pl.BlockSpec((1, A_WIDTH), lambda bi, ti: (0, 0)),
            pl.BlockSpec((1, A_WIDTH), lambda bi, ti: (0, 0)),
            pl.BlockSpec((A_WIDTH, A_WIDTH), lambda bi, ti: (0, 0)),
        ],
        out_specs=pl.BlockSpec((1, ATT_TQ, A_WIDTH), lambda bi, ti: (bi, ti, 0)),
        out_shape=jax.ShapeDtypeStruct((b, s, A_WIDTH), BF16),
        compiler_params=pltpu.CompilerParams(
            dimension_semantics=("parallel", "parallel"), vmem_limit_bytes=VMEM_LIMIT),
        name="band_attn",
    )(z3, z3, z3, z3, z3, z3, z3, bias_tab, qw, kw, seg)


def _attn_bias_table(rel_bias):
    rb = rel_bias.astype(F32)
    heads = rb.shape[0]
    lo = A_LEFT_CHUNKS * CHUNK - (ATT_WIN - 1)
    hi = A_LEFT_CHUNKS * CHUNK + ATT_TQ - 1
    strip = jnp.concatenate([
        jnp.broadcast_to(rb[:, :1], (heads, -REL_CLIP - lo)), rb,
        jnp.broadcast_to(rb[:, -1:], (heads, hi - REL_CLIP))], axis=1)
    rev = strip[:, ::-1]
    n = rev.shape[1]
    flat = jnp.broadcast_to(rev[:, None, :], (heads, ATT_TQ, n)).reshape(heads, ATT_TQ * n)
    bias = flat[:, ATT_TQ - 1:ATT_TQ - 1 + ATT_TQ * (n - 1)].reshape(heads, ATT_TQ, n - 1)[:, :, :ATT_WIN]
    qc = jnp.arange(ATT_TQ)[:, None] // CHUNK
    kc = jnp.arange(ATT_WIN)[None, :] // CHUNK
    in_band = (kc >= qc) & (kc <= qc + A_LEFT_CHUNKS)
    return jnp.where(in_band[None], bias, NEG)


GDN_TAIL = SUBLANES


def _gdn_body(qkv_ref, zs_ref, gate_ref, convw_ref, hp_ref, nw_ref, o_ref, xbuf, s_scr):
    c = CHUNK
    hd = B_HEAD_DIM

    @pl.when(pl.program_id(1) == 0)
    def _():
        xbuf[0:GDN_TAIL, :] = jnp.zeros((GDN_TAIL, 3 * B_WIDTH), F32)
        s_scr[...] = jnp.zeros_like(s_scr)

    xbuf[GDN_TAIL:GDN_TAIL + c, :] = qkv_ref[0].astype(F32)
    y = None
    for j in range(CONV_WIDTH):
        tap = convw_ref[j:j + 1, :] * xbuf[pl.ds(GDN_TAIL - (CONV_WIDTH - 1) + j, c), :]
        y = tap if y is None else y + tap
    xbuf[0:GDN_TAIL, :] = xbuf[c:c + GDN_TAIL, :]
    y = y * _sigmoid(y)

    zs = zs_ref[0]
    a_neg = -(jnp.exp(hp_ref[0:1, :]) * hp_ref[2:3, :])
    xs = zs + hp_ref[1:2, :]
    g_all = a_neg * (jnp.maximum(xs, 0.0) + jnp.log(1.0 + jnp.exp(-jnp.abs(xs))))
    beta_all = _sigmoid(zs)

    row = lax.broadcasted_iota(jnp.int32, (c, c), 0)
    col = lax.broadcasted_iota(jnp.int32, (c, c), 1)
    tril = row >= col
    tril_strict = row > col
    eye = (row == col).astype(F32)
    lower = tril.astype(BF16)
    upper = (row <= col).astype(BF16)

    gh, gl = _split_bf16(g_all)
    gc_all = _dot(lower, gh) + _dot(lower, gl)
    gth, gtl = _split_bf16(g_all.T)
    gc_t = _dot(gth, upper) + _dot(gtl, upper)

    heads = range(B_HEADS)
    q, k, v, beta, gc, gc_row, g_last = [], [], [], [], [], [], []
    for h in heads:
        qh = y[:, h * hd:(h + 1) * hd]
        kh = y[:, B_WIDTH + h * hd:B_WIDTH + (h + 1) * hd]
        q.append(qh * lax.rsqrt(jnp.sum(qh * qh, axis=-1, keepdims=True) + EPS) * (hd ** -0.5))
        k.append(kh * lax.rsqrt(jnp.sum(kh * kh, axis=-1, keepdims=True) + EPS))
        v.append(y[:, 2 * B_WIDTH + h * hd:2 * B_WIDTH + (h + 1) * hd])
        beta.append(beta_all[:, Z_SMALL_BETA + h:Z_SMALL_BETA + h + 1])
        gc.append(gc_all[:, Z_SMALL_A + h:Z_SMALL_A + h + 1])
        gc_row.append(gc_t[Z_SMALL_A + h:Z_SMALL_A + h + 1, :])
        g_last.append(gc_row[h][:, c - 1:c])
    decay = [jnp.exp(jnp.where(tril, gc[h] - gc_row[h], NEG)) for h in heads]
    e_gc = [jnp.exp(gc[h]) for h in heads]
    kb = [k[h] * beta[h] for h in heads]
    kf = [k[h].astype(BF16) for h in heads]
    a_pow = [jnp.where(tril_strict, _dot_nt(kb[h].astype(BF16), kf[h]) * decay[h], 0.0) for h in heads]
    attn = [(_dot_nt(q[h].astype(BF16), kf[h]) * decay[h]).astype(BF16) for h in heads]

    t_mat = [eye - a_pow[h] for h in heads]
    for _ in range(5):
        ab = [a_pow[h].astype(BF16) for h in heads]
        a_pow = [_dot(ab[h], ab[h]) for h in heads]
        t_mat = [t_mat[h] + _dot(t_mat[h].astype(BF16), a_pow[h].astype(BF16)) for h in heads]
    tb = [t_mat[h].astype(BF16) for h in heads]
    u = [_dot(tb[h], (v[h] * beta[h]).astype(BF16)) for h in heads]
    w = [_dot(tb[h], (kb[h] * e_gc[h]).astype(BF16)).astype(BF16) for h in heads]

    state = [s_scr[h] for h in heads]
    sb = [state[h].astype(BF16) for h in heads]
    vnb = [(u[h] - _dot(w[h], sb[h])).astype(BF16) for h in heads]
    o = [_dot((q[h] * e_gc[h]).astype(BF16), sb[h]) + _dot(attn[h], vnb[h]) for h in heads]
    k_dec = [(k[h] * jnp.exp(g_last[h] - gc[h])).astype(BF16) for h in heads]
    for h in heads:
        s_scr[h] = state[h] * jnp.exp(g_last[h]) + _dot_tn(k_dec[h], vnb[h])
    for h in heads:
        cols = slice(h * hd, (h + 1) * hd)
        gate = gate_ref[0, :, cols].astype(F32)
        on = o[h] * lax.rsqrt(jnp.mean(o[h] * o[h], axis=-1, keepdims=True) + EPS) * nw_ref[...]
        o_ref[0, :, cols] = (on * (gate * _sigmoid(gate))).astype(o_ref.dtype)


def _gated_deltanet(z3, zs3, conv_w, head_params, norm_w):
    b, s, _ = z3.shape
    n = s // CHUNK
    return pl.pallas_call(
        _gdn_body,
        grid=(b, n),
        in_specs=[
            pl.BlockSpec((1, CHUNK, 3 * B_WIDTH), lambda bi, ni: (bi, ni, Z_QKV_B // (3 * B_WIDTH))),
            pl.BlockSpec((1, CHUNK, LANES), lambda bi, ni: (bi, ni, 0)),
            pl.BlockSpec((1, CHUNK, B_WIDTH), lambda bi, ni: (bi, ni, Z_GATE_B // B_WIDTH)),
            pl.BlockSpec((CONV_WIDTH, 3 * B_WIDTH), lambda bi, ni: (0, 0)),
            pl.BlockSpec((SUBLANES, LANES), lambda bi, ni: (0, 0)),
            pl.BlockSpec((1, B_HEAD_DIM), lambda bi, ni: (0, 0)),
        ],
        out_specs=pl.BlockSpec((1, CHUNK, B_WIDTH), lambda bi, ni: (bi, ni, 0)),
        out_shape=jax.ShapeDtypeStruct((b, s, B_WIDTH), BF16),
        scratch_shapes=[
            pltpu.VMEM((GDN_TAIL + CHUNK, 3 * B_WIDTH), F32),
            pltpu.VMEM((B_HEADS, B_HEAD_DIM, B_HEAD_DIM), F32),
        ],
        compiler_params=pltpu.CompilerParams(
            dimension_semantics=("parallel", "arbitrary"), vmem_limit_bytes=VMEM_LIMIT),
        name="gdn",
    )(z3, zs3, z3, conv_w, head_params, norm_w)


MERGE_TM = 256


def _top_k_rows(s, k, payload=None):
    n = s.shape[0]
    rows = lax.broadcasted_iota(jnp.int32, s.shape, 0)
    vals, picks = [], []
    for _ in range(k):
        m = jnp.max(s, axis=0, keepdims=True)
        i = jnp.min(jnp.where(s == m, rows, n), axis=0, keepdims=True)
        hit = rows == i
        vals.append(m)
        picks.append(i if payload is None else jnp.max(jnp.where(hit, payload, -1), axis=0, keepdims=True))
        s = jnp.where(hit, -jnp.inf, s)
    return jnp.concatenate(vals, axis=0), jnp.concatenate(picks, axis=0)


def _merge_body(oa_ref, ob_ref, mr_ref, x_ref, bm_ref, pa_ref, pb_ref, wo_ref, nw_ref, wq_ref, k1_ref, k2_ref,
                x1_ref, xn_ref, idx_ref, gate_ref, gate_tok_ref):
    d = x_ref.shape[1]
    mr = mr_ref[...].astype(F32) + bm_ref[...]
    mixed = (_sigmoid(mr[:, :d]) * _dot(oa_ref[...], pa_ref[...])
             + _sigmoid(mr[:, d:]) * _dot(ob_ref[...], pb_ref[...]))
    x1 = x_ref[...] + _dot(mixed.astype(BF16), wo_ref[...])
    x1_ref[...] = x1
    xn = x1 * lax.rsqrt(jnp.mean(x1 * x1, axis=-1, keepdims=True) + EPS) * nw_ref[...]
    xn_ref[...] = xn
    q = _dot(xn.astype(BF16), wq_ref[...]).astype(BF16)

    idx_rows, gate_rows = [], []
    for h in range(PEER_HEADS):
        q1 = q[:, (2 * h) * PEER_HALF:(2 * h + 1) * PEER_HALF]
        q2 = q[:, (2 * h + 1) * PEER_HALF:(2 * h + 2) * PEER_HALF]
        s1 = _dot_nt(k1_ref[h], q1)
        s2 = _dot_nt(k2_ref[h], q2)
        v1, i1 = _top_k_rows(s1, PEER_TOPK)
        v2, i2 = _top_k_rows(s2, PEER_TOPK)
        nb = [PEER_TOPK // (a + 1) for a in range(PEER_TOPK)]
        pad = -sum(nb) % SUBLANES
        tm = v1.shape[1]
        cand = jnp.concatenate([v1[a:a + 1] + v2[:nb[a]] for a in range(PEER_TOPK)]
                               + [jnp.full((pad, tm), -jnp.inf, F32)], axis=0)
        cand_idx = jnp.concatenate([i1[a:a + 1] * PEER_NKEYS + i2[:nb[a]] for a in range(PEER_TOPK)]
                                   + [jnp.zeros((pad, tm), jnp.int32)], axis=0)
        top_s, expert = _top_k_rows(cand, PEER_TOPK, payload=cand_idx)
        e = jnp.exp(top_s - top_s[0:1])
        gate_rows.append(e / jnp.sum(e, axis=0, keepdims=True))
        idx_rows.append(expert)
    idx_ref[...] = jnp.concatenate(idx_rows, axis=0).T
    gates = jnp.concatenate(gate_rows, axis=0)
    gate_ref[...] = gates
    gate_tok_ref[...] = gates.T


def _merge_route(o_a, o_b, z2, x2, b_merge, pa, pb, wo, nw, wq, k1, k2):
    t, d = x2.shape
    tm = MERGE_TM
    full = lambda a: pl.BlockSpec(a.shape, lambda i: (0,) * a.ndim)
    return pl.pallas_call(
        _merge_body,
        grid=(t // tm,),
        in_specs=[
            pl.BlockSpec((tm, A_WIDTH), lambda i: (i, 0)),
            pl.BlockSpec((tm, B_WIDTH), lambda i: (i, 0)),
            pl.BlockSpec((tm, 2 * d), lambda i: (i, Z_MERGE // (2 * d))),
            pl.BlockSpec((tm, d), lambda i: (i, 0)),
            full(b_merge), full(pa), full(pb), full(wo), full(nw), full(wq), full(k1), full(k2),
        ],
        out_specs=[
            pl.BlockSpec((tm, d), lambda i: (i, 0)),
            pl.BlockSpec((tm, d), lambda i: (i, 0)),
            pl.BlockSpec((tm, PEER_SLOTS), lambda i: (i, 0)),
            pl.BlockSpec((PEER_SLOTS, tm), lambda i: (0, i)),
            pl.BlockSpec((tm, PEER_SLOTS), lambda i: (i, 0)),
        ],
        out_shape=[
            jax.ShapeDtypeStruct((t, d), F32),
            jax.ShapeDtypeStruct((t, d), F32),
            jax.ShapeDtypeStruct((t, PEER_SLOTS), jnp.int32),
            jax.ShapeDtypeStruct((PEER_SLOTS, t), F32),
            jax.ShapeDtypeStruct((t, PEER_SLOTS), F32),
        ],
        compiler_params=pltpu.CompilerParams(
            dimension_semantics=("parallel",), vmem_limit_bytes=VMEM_LIMIT),
        name="merge_route",
    )(o_a, o_b, z2, x2, b_merge, pa, pb, wo, nw, wq, k1, k2)


PEER_TP = 128
PEER_NBUF = 4
PEER_LOOK = PEER_NBUF - 1
PEER_SC_SHARE_NUM, PEER_SC_SHARE_DEN = 1, 4


def _erf(x):
    return lax.erf(x)


def _peer_body(idx_ref, idx_next_ref, xn_ref, gate_ref, x1_ref, tab_ref, o_ref, *scratch):
    bufs, sem = scratch[:PEER_NBUF], scratch[PEER_NBUF]
    planes = xn_ref.shape[1] // LANES
    step = pl.program_id(0)

    def issue(ref, tok, slot):
        for j in range(PEER_SLOTS):
            pltpu.make_async_copy(tab_ref.at[ref[0, tok, j]], bufs[slot].at[:, j, :], sem.at[slot]).start()

    def wait(slot):
        pltpu.make_async_copy(bufs[slot], bufs[slot], sem.at[slot]).wait()

    @pl.when(step == 0)
    def _():
        for s in range(PEER_LOOK):
            issue(idx_ref, s, s)

    lane = lax.broadcasted_iota(jnp.int32, (PEER_SLOTS, PEER_TP), 1)

    def compute(tok, slot):
        buf = bufs[slot]
        x = xn_ref[pl.ds(tok, 1), :]
        acc = None
        for s in range(planes):
            term = buf[s] * x[:, s * LANES:(s + 1) * LANES]
            acc = term if acc is None else acc + term
        act = jnp.sum(acc, axis=-1, keepdims=True)
        gate = jnp.sum(jnp.where(lane == tok, gate_ref[...], 0.0), axis=-1, keepdims=True)
        w = gate * (0.5 * act * (1.0 + _erf(act * (2.0 ** -0.5))))
        out = jnp.concatenate(
            [jnp.sum(buf[planes + s] * w, axis=0, keepdims=True) for s in range(planes)], axis=1)
        o_ref[pl.ds(tok, 1), :] = x1_ref[pl.ds(tok, 1), :] + out

    def group(g, carry):
        for u in range(PEER_NBUF):
            tok = g * PEER_NBUF + u
            issue(idx_ref, tok + PEER_LOOK, (u + PEER_LOOK) % PEER_NBUF)
            wait(u)
            compute(tok, u)
        return carry

    n_groups = PEER_TP // PEER_NBUF
    lax.fori_loop(0, n_groups - 1, group, 0)

    for u in range(PEER_NBUF):
        tok = (n_groups - 1) * PEER_NBUF + u
        nxt = tok + PEER_LOOK
        if nxt < PEER_TP:
            issue(idx_ref, nxt, (u + PEER_LOOK) % PEER_NBUF)
        else:
            issue(idx_next_ref, nxt - PEER_TP, (u + PEER_LOOK) % PEER_NBUF)
        wait(u)
        compute(tok, u)

    @pl.when(step == pl.num_programs(0) - 1)
    def _():
        for s in range(PEER_LOOK):
            wait(s)


def _peer_alias_body(prev_ref, after_ref, *rest):
    del prev_ref, after_ref
    _peer_body(*rest)


def _peer(idx3, xn, gate_t, x1, table, blk0, nb, prev=None, after=None):
    t, d = xn.shape
    tp = PEER_TP
    planes = 2 * d // LANES
    in_specs = [
        pl.BlockSpec((1, tp, PEER_SLOTS), lambda i: (i + blk0, 0, 0), memory_space=pltpu.SMEM),
        pl.BlockSpec((1, tp, PEER_SLOTS), lambda i: (jnp.minimum(i + 1, nb - 1) + blk0, 0, 0),
                     memory_space=pltpu.SMEM),
        pl.BlockSpec((tp, d), lambda i: (i + blk0, 0)),
        pl.BlockSpec((PEER_SLOTS, tp), lambda i: (0, i + blk0)),
        pl.BlockSpec((tp, d), lambda i: (i + blk0, 0)),
        pl.BlockSpec(memory_space=pl.ANY),
    ]
    args = (idx3, idx3, xn, gate_t, x1, table)
    body, aliases = _peer_body, {}
    if prev is not None:
        body, aliases = _peer_alias_body, {0: 0}
        in_specs = [pl.BlockSpec(memory_space=pl.ANY), pl.BlockSpec(memory_space=pl.ANY)] + in_specs
        args = (prev, after) + args
    return pl.pallas_call(
        body,
        grid=(nb,),
        in_specs=in_specs,
        out_specs=pl.BlockSpec((tp, d), lambda i: (i + blk0, 0)),
        out_shape=jax.ShapeDtypeStruct((t, d), F32),
        scratch_shapes=[pltpu.VMEM((planes, PEER_SLOTS, LANES), F32) for _ in range(PEER_NBUF)]
        + [pltpu.SemaphoreType.DMA((PEER_NBUF,))],
        input_output_aliases=aliases,
        compiler_params=pltpu.CompilerParams(
            dimension_semantics=("arbitrary",), vmem_limit_bytes=VMEM_LIMIT),
        name="peer",
    )(*args)


SC_CORES = 2
SC_SUBCORES = 16
SC_LANES = 16
SC_WORKERS = SC_CORES * SC_SUBCORES
SC_ROWS = 32
SC_RB = 4


def _sc_mesh():
    return plsc.VectorSubcoreMesh(core_axis_name="c", subcore_axis_name="s")


def _sc_worker():
    return lax.axis_index("s") * SC_CORES + lax.axis_index("c")


def _sc_down(idx_flat, xn3, down3, tok0, n_tok):
    planes = xn3.shape[1]
    tw = n_tok // SC_WORKERS
    nchunk = PEER_SLOTS // SC_ROWS
    per_plane = LANES // SC_LANES

    @functools.partial(
        pl.kernel, mesh=_sc_mesh(),
        out_type=jax.ShapeDtypeStruct((n_tok, PEER_SLOTS, SC_LANES), F32),
        scratch_types=[
            pltpu.VMEM((PEER_SLOTS,), jnp.int32),
            pltpu.VMEM((planes, LANES), F32),
            pltpu.VMEM((2, SC_ROWS, planes, LANES), F32),
            pltpu.VMEM((PEER_SLOTS, SC_LANES), F32),
            pltpu.SemaphoreType.DMA((2,)),
        ],
        name="sc_down",
    )
    def run(idx_hbm, xn_hbm, down_hbm, out_hbm, idx_v, x_v, buf, act_v, sem):
        base = tok0 + _sc_worker() * tw

        def gather(kc):
            return pltpu.make_async_copy(
                down_hbm.at[idx_v.at[pl.ds(kc * SC_ROWS, SC_ROWS)]], buf.at[kc % 2], sem.at[kc % 2])

        def token(ti, carry):
            tok = base + ti
            pltpu.sync_copy(idx_hbm.at[pl.ds(tok * PEER_SLOTS, PEER_SLOTS)], idx_v)
            pltpu.sync_copy(xn_hbm.at[tok], x_v)
            gather(0).start()
            for kc in range(nchunk):
                if kc + 1 < nchunk:
                    gather(kc + 1).start()
                gather(kc).wait()

                def rows(g, c2, kc=kc):
                    accs = [jnp.zeros((SC_LANES,), F32) for _ in range(SC_RB)]
                    for p in range(planes):
                        for q in range(per_plane):
                            sl = pl.ds(q * SC_LANES, SC_LANES)
                            xv = x_v[p, sl]
                            for r in range(SC_RB):
                                accs[r] = accs[r] + buf[kc % 2, g * SC_RB + r, p, sl] * xv
                    for r in range(SC_RB):
                        act_v[kc * SC_ROWS + g * SC_RB + r, :] = accs[r]
                    return c2

                lax.fori_loop(0, SC_ROWS // SC_RB, rows, 0)
            pltpu.sync_copy(act_v, out_hbm.at[tok - tok0])
            return carry

        lax.fori_loop(0, tw, token, 0)

    return run(idx_flat, xn3, down3)


def _sc_up(idx_flat, w16, x13, up3, tok0, n_tok):
    planes = x13.shape[1]
    tw = n_tok // SC_WORKERS
    nchunk = PEER_SLOTS // SC_ROWS
    per_plane = LANES // SC_LANES

    @functools.partial(
        pl.kernel, mesh=_sc_mesh(),
        out_type=jax.ShapeDtypeStruct((n_tok, planes, LANES), F32),
        scratch_types=[
            pltpu.VMEM((PEER_SLOTS,), jnp.int32),
            pltpu.VMEM((PEER_SLOTS, SC_LANES), F32),
            pltpu.VMEM((planes, LANES), F32),
            pltpu.VMEM((2, SC_ROWS, planes, LANES), F32),
            pltpu.SemaphoreType.DMA((2,)),
        ],
        name="sc_up",
    )
    def run(idx_hbm, w_hbm, x1_hbm, up_hbm, out_hbm, idx_v, w_v, out_v, buf, sem):
        base = tok0 + _sc_worker() * tw

        def gather(kc):
            return pltpu.make_async_copy(
                up_hbm.at[idx_v.at[pl.ds(kc * SC_ROWS, SC_ROWS)]], buf.at[kc % 2], sem.at[kc % 2])

        def token(ti, carry):
            tok = base + ti
            pltpu.sync_copy(idx_hbm.at[pl.ds(tok * PEER_SLOTS, PEER_SLOTS)], idx_v)
            pltpu.sync_copy(w_hbm.at[tok - tok0], w_v)
            pltpu.sync_copy(x1_hbm.at[tok], out_v)
            gather(0).start()
            for kc in range(nchunk):
                if kc + 1 < nchunk:
                    gather(kc + 1).start()
                gather(kc).wait()

                def rows(j, c2, kc=kc):
                    wj = w_v[kc * SC_ROWS + j, :]
                    for p in range(planes):
                        for q in range(per_plane):
                            sl = pl.ds(q * SC_LANES, SC_LANES)
                            plsc.addupdate(out_v.at[p, sl], wj * buf[kc % 2, j, p, sl])
                    return c2

                lax.fori_loop(0, SC_ROWS, rows, 0)
            pltpu.sync_copy(out_v, out_hbm.at[tok - tok0])
            return carry

        lax.fori_loop(0, tw, token, 0)

    return run(idx_flat, w16, x13, up3)


ACT_TM = 256


def _split3_bf16(a):
    hi = a.astype(BF16)
    r = a - hi.astype(F32)
    mid = r.astype(BF16)
    lo = (r - mid.astype(F32)).astype(BF16)
    return hi, mid, lo


def _peer_act_body(after_ref, part_ref, gate_ref, w_ref):
    del after_ref
    n = PEER_SLOTS * SC_LANES
    grp = lax.broadcasted_iota(jnp.int32, (n, PEER_SLOTS), 0) // SC_LANES
    slot = lax.broadcasted_iota(jnp.int32, (n, PEER_SLOTS), 1)
    fold = (grp == slot).astype(BF16)
    act = sum(_dot(p, fold) for p in _split3_bf16(part_ref[...]))
    w = gate_ref[...] * (0.5 * act * (1.0 + _erf(act * (2.0 ** -0.5))))
    grp_t = lax.broadcasted_iota(jnp.int32, (PEER_SLOTS, n), 1) // SC_LANES
    slot_t = lax.broadcasted_iota(jnp.int32, (PEER_SLOTS, n), 0)
    spread = (grp_t == slot_t).astype(BF16)
    w_ref[...] = sum(_dot(p, spread) for p in _split3_bf16(w))


def _peer_act(after, partial2, gate_tok, blk0):
    n_tok, n = partial2.shape
    tm = ACT_TM
    return pl.pallas_call(
        _peer_act_body,
        grid=(n_tok // tm,),
        in_specs=[
            pl.BlockSpec(memory_space=pl.ANY),
            pl.BlockSpec((tm, n), lambda i: (i, 0)),
            pl.BlockSpec((tm, PEER_SLOTS), lambda i: (i + blk0, 0)),
        ],
        out_specs=pl.BlockSpec((tm, n), lambda i: (i, 0)),
        out_shape=jax.ShapeDtypeStruct((n_tok, n), F32),
        compiler_params=pltpu.CompilerParams(
            dimension_semantics=("parallel",), vmem_limit_bytes=VMEM_LIMIT),
        name="peer_act",
    )(after, partial2, gate_tok)


def _pad_lanes(a):
    return jnp.pad(a, ((0, 0), (0, LANES - a.shape[1])))


def kernel(x, norm_mix_w, w_in, a_q_norm_w, a_k_norm_w, a_rel_bias, b_conv_w, b_a_log, b_dt_bias, b_norm_w, b_merge, w_proj_a, w_proj_b, w_out, norm_ffn_w, peer_w_query, peer_keys_1, peer_keys_2, peer_down, peer_up):
    b, s, d = x.shape
    t = b * s
    depth = w_in.shape[0]
    for l in range(depth):
        x2 = x.reshape(t, d)

        wi = w_in[l]
        o_qkvb = 3 * A_WIDTH
        o_a = o_qkvb + 3 * B_WIDTH
        o_beta = o_a + B_HEADS
        o_gate = o_beta + B_HEADS
        o_merge = o_gate + B_WIDTH
        w_main = jnp.concatenate(
            [wi[:, o_qkvb:o_a], wi[:, o_gate:o_merge], wi[:, o_merge:], wi[:, :o_qkvb]], axis=1).astype(BF16)
        w_small = _pad_lanes(wi[:, o_a:o_gate]).astype(BF16)
        z_attn = Z_MERGE + 2 * d

        z, zs = _in_proj(x2, norm_mix_w[l][None, :], w_main, w_small, tm=1024, tn=1920)
        z3 = z.reshape(b, s, z.shape[1])

        seg = jnp.kron(jnp.eye(A_HEADS, dtype=F32), jnp.full((A_HEAD_DIM, A_HEAD_DIM), 1.0 / A_HEAD_DIM, F32))
        o_att = _band_attention(
            z3, _attn_bias_table(a_rel_bias[l]),
            jnp.tile(a_q_norm_w[l], A_HEADS)[None, :], jnp.tile(a_k_norm_w[l], A_HEADS)[None, :],
            seg.astype(BF16), z_attn // A_WIDTH)

        head_params = jnp.zeros((SUBLANES, LANES), F32)
        head_params = head_params.at[0, :B_HEADS].set(b_a_log[l]).at[1, :B_HEADS].set(b_dt_bias[l])
        head_params = head_params.at[2, :B_HEADS].set(1.0)
        o_gdn = _gated_deltanet(z3, zs.reshape(b, s, LANES), b_conv_w[l], head_params, b_norm_w[l][None, :])

        x1, xn, idx, gate_t, gate_tok = _merge_route(
            o_att.reshape(t, A_WIDTH), o_gdn.reshape(t, B_WIDTH), z, x2, b_merge[l][None, :],
            w_proj_a[l].astype(BF16), w_proj_b[l].astype(BF16), w_out[l].astype(BF16), norm_ffn_w[l][None, :],
            peer_w_query[l].astype(BF16), peer_keys_1[l].astype(BF16), peer_keys_2[l].astype(BF16))

        table = jnp.concatenate([peer_down[l], peer_up[l]], axis=1).reshape(-1, 2 * d // LANES, LANES)
        idx3 = idx.reshape(t // PEER_TP, PEER_TP, PEER_SLOTS)

        n_sc = t * PEER_SC_SHARE_NUM // PEER_SC_SHARE_DEN
        t0 = t - n_sc
        nblk = t0 // PEER_TP
        nblk_a = nblk // 2
        planes = d // LANES
        idx_flat = idx.reshape(-1)
        part = _sc_down(idx_flat, xn.reshape(t, planes, LANES), peer_down[l].reshape(-1, planes, LANES), t0, n_sc)
        out = _peer(idx3, xn, gate_t, x1, table, 0, nblk_a)
        w16 = _peer_act(out, part.reshape(n_sc, PEER_SLOTS * SC_LANES), gate_tok, t0 // ACT_TM)
        out = _peer(idx3, xn, gate_t, x1, table, nblk_a, nblk - nblk_a, prev=out, after=w16)
        sc_out = _sc_up(idx_flat, w16.reshape(n_sc, PEER_SLOTS, SC_LANES), x1.reshape(t, planes, LANES),
                        peer_up[l].reshape(-1, planes, LANES), t0, n_sc)
        out = lax.dynamic_update_slice(out, sc_out.reshape(n_sc, d), (t0, 0))
        x = out.reshape(b, s, d)
    return x
```

```python
import functools

import jax
import jax.numpy as jnp
from jax import lax
from jax.experimental import pallas as pl
from jax.experimental.pallas import tpu as pltpu
from jax.experimental.pallas import tpu_sc as plsc

F32 = jnp.float32
BF16 = jnp.bfloat16

EPS = 1e-6
NEG = -1e30

CHUNK = 64
A_HEADS = 8
A_HEAD_DIM = 64
A_WIDTH = A_HEADS * A_HEAD_DIM
A_LEFT_CHUNKS = 8
REL_CLIP = 128
B_HEADS = 8
B_HEAD_DIM = 128
B_WIDTH = B_HEADS * B_HEAD_DIM
CONV_WIDTH = 4
PEER_HEADS = 8
PEER_HALF = 128
PEER_NKEYS = 128
PEER_TOPK = 16
PEER_SLOTS = PEER_HEADS * PEER_TOPK

LANES = 128
SUBLANES = 8
VMEM_LIMIT = 56 * 1024 * 1024

Z_QKV_B = 0
Z_GATE_B = 3 * B_WIDTH
Z_MERGE = Z_GATE_B + B_WIDTH
Z_SMALL_A = 0
Z_SMALL_BETA = B_HEADS


def _dot(a, b):
    return jnp.dot(a, b, preferred_element_type=F32)


def _dot_nt(a, b):
    return lax.dot_general(a, b, (((1,), (1,)), ((), ())), preferred_element_type=F32)


def _dot_tn(a, b):
    return lax.dot_general(a, b, (((0,), (0,)), ((), ())), preferred_element_type=F32)


def _split_bf16(a):
    hi = a.astype(BF16)
    lo = (a - hi.astype(F32)).astype(BF16)
    return hi, lo


def _sigmoid(x):
    return 1.0 / (1.0 + jnp.exp(-x))


def _inproj_body(x_ref, nw_ref, w_ref, ws_ref, z_ref, zs_ref, h_scr):
    @pl.when(pl.program_id(1) == 0)
    def _():
        x = x_ref[...]
        ms = jnp.mean(x * x, axis=-1, keepdims=True)
        hb = (x * lax.rsqrt(ms + EPS) * nw_ref[...]).astype(BF16)
        h_scr[...] = hb
        zs_ref[...] = _dot(hb, ws_ref[...])

    z_ref[...] = _dot(h_scr[...], w_ref[...]).astype(z_ref.dtype)


def _in_proj(x2, norm_w, w_main, w_small, tm, tn):
    t, d = x2.shape
    n = w_main.shape[1]
    return pl.pallas_call(
        _inproj_body,
        grid=(t // tm, n // tn),
        in_specs=[
            pl.BlockSpec((tm, d), lambda i, j: (i, 0)),
            pl.BlockSpec((1, d), lambda i, j: (0, 0)),
            pl.BlockSpec((d, tn), lambda i, j: (0, j)),
            pl.BlockSpec((d, LANES), lambda i, j: (0, 0)),
        ],
        out_specs=[
            pl.BlockSpec((tm, tn), lambda i, j: (i, j)),
            pl.BlockSpec((tm, LANES), lambda i, j: (i, 0)),
        ],
        out_shape=[
            jax.ShapeDtypeStruct((t, n), BF16),
            jax.ShapeDtypeStruct((t, LANES), F32),
        ],
        scratch_shapes=[pltpu.VMEM((tm, d), BF16)],
        compiler_params=pltpu.CompilerParams(
            dimension_semantics=("parallel", "arbitrary"), vmem_limit_bytes=VMEM_LIMIT),
        name="in_proj",
    )(x2, norm_w, w_main, w_small)


ATT_TQ = 256
ATT_WIN = ATT_TQ + A_LEFT_CHUNKS * CHUNK
ATT_NKB = ATT_WIN // ATT_TQ


def _attn_body(q_ref, k0_ref, k1_ref, k2_ref, v0_ref, v1_ref, v2_ref, bias_ref, qw_ref, kw_ref, seg_ref,
               o_ref):
    t = pl.program_id(1)
    seg = seg_ref[...]

    def head_rms(a, w):
        hi, lo = _split_bf16(a * a)
        ms = _dot(hi, seg) + _dot(lo, seg)
        return a * lax.rsqrt(ms + EPS) * w

    q = head_rms(q_ref[0].astype(F32), qw_ref[...]) * (A_HEAD_DIM ** -0.5)
    k = jnp.concatenate([k0_ref[0], k1_ref[0], k2_ref[0]], axis=0).astype(F32)
    k = head_rms(k, kw_ref[...]).astype(BF16)
    v = jnp.concatenate([v0_ref[0], v1_ref[0], v2_ref[0]], axis=0)

    kpos = t * ATT_TQ - A_LEFT_CHUNKS * CHUNK + lax.broadcasted_iota(jnp.int32, (1, ATT_WIN), 1)
    valid = kpos >= 0
    lane = lax.broadcasted_iota(jnp.int32, (1, LANES), 1)
    first = lane < A_HEAD_DIM

    for pair in range(A_HEADS // 2):
        cols = slice(pair * LANES, (pair + 1) * LANES)
        qp = q[:, cols]
        kp = k[:, cols]
        vp = v[:, cols]
        outs = []
        for half in range(2):
            h = 2 * pair + half
            qm = jnp.where(first if half == 0 else jnp.logical_not(first), qp, 0.0).astype(BF16)
            s = _dot_nt(qm, kp) + bias_ref[h]
            s = jnp.where(valid, s, NEG)
            mx = jnp.max(s, axis=-1, keepdims=True)
            p = jnp.exp(s - mx)
            den = jnp.sum(p, axis=-1, keepdims=True)
            outs.append(_dot(p.astype(BF16), vp) / den)
        o_ref[0, :, cols] = jnp.where(first, outs[0], outs[1]).astype(o_ref.dtype)


def _band_attention(z3, bias_tab, qw, kw, seg, col_q):
    b, s, _ = z3.shape
    nt = s // ATT_TQ

    def kv_spec(back, col):
        return pl.BlockSpec((1, ATT_TQ, A_WIDTH), lambda bi, ti: (bi, jnp.maximum(ti - back, 0), col))

    return pl.pallas_call(
        _attn_body,
        grid=(b, nt),
        in_specs=[
            pl.BlockSpec((1, ATT_TQ, A_WIDTH), lambda bi, ti: (bi, ti, col_q)),
            kv_spec(2, col_q + 1), kv_spec(1, col_q + 1), kv_spec(0, col_q + 1),
            kv_spec(2, col_q + 2), kv_spec(1, col_q + 2), kv_spec(0, col_q + 2),
            pl.BlockSpec((A_HEADS, ATT_TQ, ATT_WIN), lambda bi, ti: (0, 0, 0)),
            pl.BlockSpec((1, A_WIDTH), lambda bi, ti: (0, 0)),
            pl.BlockSpec((1, A_WIDTH), lambda bi, ti: (0, 0)),
            pl.BlockSpec((A_WIDTH, A_WIDTH), lambda bi, ti: (0, 0)),
        ],
        out_specs=pl.BlockSpec((1, ATT_TQ, A_WIDTH), lambda bi, ti: (bi, ti, 0)),
        out_shape=jax.ShapeDtypeStruct((b, s, A_WIDTH), BF16),
        compiler_params=pltpu.CompilerParams(
            dimension_semantics=("parallel", "parallel"), vmem_limit_bytes=VMEM_LIMIT),
        name="band_attn",
    )(z3, z3, z3, z3, z3, z3, z3, bias_tab, qw, kw, seg)


def _attn_bias_table(rel_bias):
    rb = rel_bias.astype(F32)
    heads = rb.shape[0]
    lo = A_LEFT_CHUNKS * CHUNK - (ATT_WIN - 1)
    hi = A_LEFT_CHUNKS * CHUNK + ATT_TQ - 1
    strip = jnp.concatenate([
        jnp.broadcast_to(rb[:, :1], (heads, -REL_CLIP - lo)), rb,
        jnp.broadcast_to(rb[:, -1:], (heads, hi - REL_CLIP))], axis=1)
    rev = strip[:, ::-1]
    n = rev.shape[1]
    flat = jnp.broadcast_to(rev[:, None, :], (heads, ATT_TQ, n)).reshape(heads, ATT_TQ * n)
    bias = flat[:, ATT_TQ - 1:ATT_TQ - 1 + ATT_TQ * (n - 1)].reshape(heads, ATT_TQ, n - 1)[:, :, :ATT_WIN]
    qc = jnp.arange(ATT_TQ)[:, None] // CHUNK
    kc = jnp.arange(ATT_WIN)[None, :] // CHUNK
    in_band = (kc >= qc) & (kc <= qc + A_LEFT_CHUNKS)
    return jnp.where(in_band[None], bias, NEG)


GDN_TAIL = SUBLANES


def _gdn_body(qkv_ref, zs_ref, gate_ref, convw_ref, hp_ref, nw_ref, o_ref, xbuf, s_scr):
    c = CHUNK
    hd = B_HEAD_DIM

    @pl.when(pl.program_id(1) == 0)
    def _():
        xbuf[0:GDN_TAIL, :] = jnp.zeros((GDN_TAIL, 3 * B_WIDTH), F32)
        s_scr[...] = jnp.zeros_like(s_scr)

    xbuf[GDN_TAIL:GDN_TAIL + c, :] = qkv_ref[0].astype(F32)
    y = None
    for j in range(CONV_WIDTH):
        tap = convw_ref[j:j + 1, :] * xbuf[pl.ds(GDN_TAIL - (CONV_WIDTH - 1) + j, c), :]
        y = tap if y is None else y + tap
    xbuf[0:GDN_TAIL, :] = xbuf[c:c + GDN_TAIL, :]
    y = y * _sigmoid(y)

    zs = zs_ref[0]
    a_neg = -(jnp.exp(hp_ref[0:1, :]) * hp_ref[2:3, :])
    xs = zs + hp_ref[1:2, :]
    g_all = a_neg * (jnp.maximum(xs, 0.0) + jnp.log(1.0 + jnp.exp(-jnp.abs(xs))))
    beta_all = _sigmoid(zs)

    row = lax.broadcasted_iota(jnp.int32, (c, c), 0)
    col = lax.broadcasted_iota(jnp.int32, (c, c), 1)
    tril = row >= col
    tril_strict = row > col
    eye = (row == col).astype(F32)
    lower = tril.astype(BF16)
    upper = (row <= col).astype(BF16)

    gh, gl = _split_bf16(g_all)
    gc_all = _dot(lower, gh) + _dot(lower, gl)
    gth, gtl = _split_bf16(g_all.T)
    gc_t = _dot(gth, upper) + _dot(gtl, upper)

    heads = range(B_HEADS)
    q, k, v, beta, gc, gc_row, g_last = [], [], [], [], [], [], []
    for h in heads:
        qh = y[:, h * hd:(h + 1) * hd]
        kh = y[:, B_WIDTH + h * hd:B_WIDTH + (h + 1) * hd]
        q.append(qh * lax.rsqrt(jnp.sum(qh * qh, axis=-1, keepdims=True) + EPS) * (hd ** -0.5))
        k.append(kh * lax.rsqrt(jnp.sum(kh * kh, axis=-1, keepdims=True) + EPS))
        v.append(y[:, 2 * B_WIDTH + h * hd:2 * B_WIDTH + (h + 1) * hd])
        beta.append(beta_all[:, Z_SMALL_BETA + h:Z_SMALL_BETA + h + 1])
        gc.append(gc_all[:, Z_SMALL_A + h:Z_SMALL_A + h + 1])
        gc_row.append(gc_t[Z_SMALL_A + h:Z_SMALL_A + h + 1, :])
        g_last.append(gc_row[h][:, c - 1:c])
    decay = [jnp.exp(jnp.where(tril, gc[h] - gc_row[h], NEG)) for h in heads]
    e_gc = [jnp.exp(gc[h]) for h in heads]
    kb = [k[h] * beta[h] for h in heads]
    kf = [k[h].astype(BF16) for h in heads]
    a_pow = [jnp.where(tril_strict, _dot_nt(kb[h].astype(BF16), kf[h]) * decay[h], 0.0) for h in heads]
    attn = [(_dot_nt(q[h].astype(BF16), kf[h]) * decay[h]).astype(BF16) for h in heads]

    t_mat = [eye - a_pow[h] for h in heads]
    for _ in range(5):
        ab = [a_pow[h].astype(BF16) for h in heads]
        a_pow = [_dot(ab[h], ab[h]) for h in heads]
        t_mat = [t_mat[h] + _dot(t_mat[h].astype(BF16), a_pow[h].astype(BF16)) for h in heads]
    tb = [t_mat[h].astype(BF16) for h in heads]
    u = [_dot(tb[h], (v[h] * beta[h]).astype(BF16)) for h in heads]
    w = [_dot(tb[h], (kb[h] * e_gc[h]).astype(BF16)).astype(BF16) for h in heads]

    state = [s_scr[h] for h in heads]
    sb = [state[h].astype(BF16) for h in heads]
    vnb = [(u[h] - _dot(w[h], sb[h])).astype(BF16) for h in heads]
    o = [_dot((q[h] * e_gc[h]).astype(BF16), sb[h]) + _dot(attn[h], vnb[h]) for h in heads]
    k_dec = [(k[h] * jnp.exp(g_last[h] - gc[h])).astype(BF16) for h in heads]
    for h in heads:
        s_scr[h] = state[h] * jnp.exp(g_last[h]) + _dot_tn(k_dec[h], vnb[h])
    for h in heads:
        cols = slice(h * hd, (h + 1) * hd)
        gate = gate_ref[0, :, cols].astype(F32)
        on = o[h] * lax.rsqrt(jnp.mean(o[h] * o[h], axis=-1, keepdims=True) + EPS) * nw_ref[...]
        o_ref[0, :, cols] = (on * (gate * _sigmoid(gate))).astype(o_ref.dtype)


def _gated_deltanet(z3, zs3, conv_w, head_params, norm_w):
    b, s, _ = z3.shape
    n = s // CHUNK
    return pl.pallas_call(
        _gdn_body,
        grid=(b, n),
        in_specs=[
            pl.BlockSpec((1, CHUNK, 3 * B_WIDTH), lambda bi, ni: (bi, ni, Z_QKV_B // (3 * B_WIDTH))),
            pl.BlockSpec((1, CHUNK, LANES), lambda bi, ni: (bi, ni, 0)),
            pl.BlockSpec((1, CHUNK, B_WIDTH), lambda bi, ni: (bi, ni, Z_GATE_B // B_WIDTH)),
            pl.BlockSpec((CONV_WIDTH, 3 * B_WIDTH), lambda bi, ni: (0, 0)),
            pl.BlockSpec((SUBLANES, LANES), lambda bi, ni: (0, 0)),
            pl.BlockSpec((1, B_HEAD_DIM), lambda bi, ni: (0, 0)),
        ],
        out_specs=pl.BlockSpec((1, CHUNK, B_WIDTH), lambda bi, ni: (bi, ni, 0)),
        out_shape=jax.ShapeDtypeStruct((b, s, B_WIDTH), BF16),
        scratch_shapes=[
            pltpu.VMEM((GDN_TAIL + CHUNK, 3 * B_WIDTH), F32),
            pltpu.VMEM((B_HEADS, B_HEAD_DIM, B_HEAD_DIM), F32),
        ],
        compiler_params=pltpu.CompilerParams(
            dimension_semantics=("parallel", "arbitrary"), vmem_limit_bytes=VMEM_LIMIT),
        name="gdn",
    )(z3, zs3, z3, conv_w, head_params, norm_w)


MERGE_TM = 256


def _top_k_rows(s, k, payload=None):
    n = s.shape[0]
    rows = lax.broadcasted_iota(jnp.int32, s.shape, 0)
    vals, picks = [], []
    for _ in range(k):
        m = jnp.max(s, axis=0, keepdims=True)
        i = jnp.min(jnp.where(s == m, rows, n), axis=0, keepdims=True)
        hit = rows == i
        vals.append(m)
        picks.append(i if payload is None else jnp.max(jnp.where(hit, payload, -1), axis=0, keepdims=True))
        s = jnp.where(hit, -jnp.inf, s)
    return jnp.concatenate(vals, axis=0), jnp.concatenate(picks, axis=0)


def _merge_body(oa_ref, ob_ref, mr_ref, x_ref, bm_ref, pa_ref, pb_ref, wo_ref, nw_ref, wq_ref, k1_ref, k2_ref,
                x1_ref, x1p_ref, xnp_ref, idx_ref, gate_ref, gate_tok_ref):
    tm, d = x_ref.shape
    planes = d // LANES

    def store_planes(ref, a):
        for p in range(planes):
            ref[pl.ds(p, tm, stride=planes), :] = a[:, p * LANES:(p + 1) * LANES]

    mr = mr_ref[...].astype(F32) + bm_ref[...]
    mixed = (_sigmoid(mr[:, :d]) * _dot(oa_ref[...], pa_ref[...])
             + _sigmoid(mr[:, d:]) * _dot(ob_ref[...], pb_ref[...]))
    x1 = x_ref[...] + _dot(mixed.astype(BF16), wo_ref[...])
    x1_ref[...] = x1
    store_planes(x1p_ref, x1)
    xn = x1 * lax.rsqrt(jnp.mean(x1 * x1, axis=-1, keepdims=True) + EPS) * nw_ref[...]
    store_planes(xnp_ref, xn)
    q = _dot(xn.astype(BF16), wq_ref[...]).astype(BF16)

    idx_rows, gate_rows = [], []
    for h in range(PEER_HEADS):
        q1 = q[:, (2 * h) * PEER_HALF:(2 * h + 1) * PEER_HALF]
        q2 = q[:, (2 * h + 1) * PEER_HALF:(2 * h + 2) * PEER_HALF]
        s1 = _dot_nt(k1_ref[h], q1)
        s2 = _dot_nt(k2_ref[h], q2)
        v1, i1 = _top_k_rows(s1, PEER_TOPK)
        v2, i2 = _top_k_rows(s2, PEER_TOPK)
        nb = [PEER_TOPK // (a + 1) for a in range(PEER_TOPK)]
        pad = -sum(nb) % SUBLANES
        tm = v1.shape[1]
        cand = jnp.concatenate([v1[a:a + 1] + v2[:nb[a]] for a in range(PEER_TOPK)]
                               + [jnp.full((pad, tm), -jnp.inf, F32)], axis=0)
        cand_idx = jnp.concatenate([i1[a:a + 1] * PEER_NKEYS + i2[:nb[a]] for a in range(PEER_TOPK)]
                                   + [jnp.zeros((pad, tm), jnp.int32)], axis=0)
        top_s, expert = _top_k_rows(cand, PEER_TOPK, payload=cand_idx)
        e = jnp.exp(top_s - top_s[0:1])
        gate_rows.append(e / jnp.sum(e, axis=0, keepdims=True))
        idx_rows.append(expert)
    idx_ref[...] = jnp.concatenate(idx_rows, axis=0).T
    gates = jnp.concatenate(gate_rows, axis=0)
    gate_ref[...] = gates
    gate_tok_ref[...] = gates.T


def _merge_route(o_a, o_b, z2, x2, b_merge, pa, pb, wo, nw, wq, k1, k2):
    t, d = x2.shape
    tm = MERGE_TM
    full = lambda a: pl.BlockSpec(a.shape, lambda i: (0,) * a.ndim)
    return pl.pallas_call(
        _merge_body,
        grid=(t // tm,),
        in_specs=[
            pl.BlockSpec((tm, A_WIDTH), lambda i: (i, 0)),
            pl.BlockSpec((tm, B_WIDTH), lambda i: (i, 0)),
            pl.BlockSpec((tm, 2 * d), lambda i: (i, Z_MERGE // (2 * d))),
            pl.BlockSpec((tm, d), lambda i: (i, 0)),
            full(b_merge), full(pa), full(pb), full(wo), full(nw), full(wq), full(k1), full(k2),
        ],
        out_specs=[
            pl.BlockSpec((tm, d), lambda i: (i, 0)),
            pl.BlockSpec((tm * d // LANES, LANES), lambda i: (i, 0)),
            pl.BlockSpec((tm * d // LANES, LANES), lambda i: (i, 0)),
            pl.BlockSpec((tm, PEER_SLOTS), lambda i: (i, 0)),
            pl.BlockSpec((PEER_SLOTS, tm), lambda i: (0, i)),
            pl.BlockSpec((tm, PEER_SLOTS), lambda i: (i, 0)),
        ],
        out_shape=[
            jax.ShapeDtypeStruct((t, d), F32),
            jax.ShapeDtypeStruct((t * d // LANES, LANES), F32),
            jax.ShapeDtypeStruct((t * d // LANES, LANES), F32),
            jax.ShapeDtypeStruct((t, PEER_SLOTS), jnp.int32),
            jax.ShapeDtypeStruct((PEER_SLOTS, t), F32),
            jax.ShapeDtypeStruct((t, PEER_SLOTS), F32),
        ],
        compiler_params=pltpu.CompilerParams(
            dimension_semantics=("parallel",), vmem_limit_bytes=VMEM_LIMIT),
        name="merge_route",
    )(o_a, o_b, z2, x2, b_merge, pa, pb, wo, nw, wq, k1, k2)


PEER_TP = 128
PEER_NBUF = 4
PEER_LOOK = PEER_NBUF - 1
PEER_SC_SHARE_NUM, PEER_SC_SHARE_DEN = 1, 4


def _erf(x):
    return lax.erf(x)


def _peer_body(idx_ref, idx_next_ref, xn_ref, gate_ref, x1_ref, tab_ref, o_ref, *scratch):
    bufs, sem = scratch[:PEER_NBUF], scratch[PEER_NBUF]
    planes = x1_ref.shape[1] // LANES
    step = pl.program_id(0)

    def issue(ref, tok, slot):
        for j in range(PEER_SLOTS):
            pltpu.make_async_copy(
                tab_ref.at[pl.ds(2 * ref[0, tok, j], 2)], bufs[slot].at[:, :, j, :], sem.at[slot]).start()

    def wait(slot):
        pltpu.make_async_copy(bufs[slot], bufs[slot], sem.at[slot]).wait()

    @pl.when(step == 0)
    def _():
        for s in range(PEER_LOOK):
            issue(idx_ref, s, s)

    lane = lax.broadcasted_iota(jnp.int32, (PEER_SLOTS, PEER_TP), 1)

    def compute(tok, slot):
        buf = bufs[slot]
        acc = None
        for s in range(planes):
            term = buf[0, s] * xn_ref[pl.ds(tok * planes + s, 1), :]
            acc = term if acc is None else acc + term
        act = jnp.sum(acc, axis=-1, keepdims=True)
        gate = jnp.sum(jnp.where(lane == tok, gate_ref[...], 0.0), axis=-1, keepdims=True)
        w = gate * (0.5 * act * (1.0 + _erf(act * (2.0 ** -0.5))))
        out = jnp.concatenate(
            [jnp.sum(buf[1, s] * w, axis=0, keepdims=True) for s in range(planes)], axis=1)
        o_ref[pl.ds(tok, 1), :] = x1_ref[pl.ds(tok, 1), :] + out

    def group(g, carry):
        for u in range(PEER_NBUF):
            tok = g * PEER_NBUF + u
            issue(idx_ref, tok + PEER_LOOK, (u + PEER_LOOK) % PEER_NBUF)
            wait(u)
            compute(tok, u)
        return carry

    n_groups = PEER_TP // PEER_NBUF
    lax.fori_loop(0, n_groups - 1, group, 0)

    for u in range(PEER_NBUF):
        tok = (n_groups - 1) * PEER_NBUF + u
        nxt = tok + PEER_LOOK
        if nxt < PEER_TP:
            issue(idx_ref, nxt, (u + PEER_LOOK) % PEER_NBUF)
        else:
            issue(idx_next_ref, nxt - PEER_TP, (u + PEER_LOOK) % PEER_NBUF)
        wait(u)
        compute(tok, u)

    @pl.when(step == pl.num_programs(0) - 1)
    def _():
        for s in range(PEER_LOOK):
            wait(s)


def _peer_alias_body(prev_ref, after_ref, *rest):
    del prev_ref, after_ref
    _peer_body(*rest)


def _peer(idx3, xnp, gate_t, x1, table, blk0, nb, prev=None, after=None):
    t, d = x1.shape
    tp = PEER_TP
    planes = d // LANES
    in_specs = [
        pl.BlockSpec((1, tp, PEER_SLOTS), lambda i: (i + blk0, 0, 0), memory_space=pltpu.SMEM),
        pl.BlockSpec((1, tp, PEER_SLOTS), lambda i: (jnp.minimum(i + 1, nb - 1) + blk0, 0, 0),
                     memory_space=pltpu.SMEM),
        pl.BlockSpec((tp * planes, LANES), lambda i: (i + blk0, 0)),
        pl.BlockSpec((PEER_SLOTS, tp), lambda i: (0, i + blk0)),
        pl.BlockSpec((tp, d), lambda i: (i + blk0, 0)),
        pl.BlockSpec(memory_space=pl.ANY),
    ]
    args = (idx3, idx3, xnp, gate_t, x1, table)
    body, aliases = _peer_body, {}
    if prev is not None:
        body, aliases = _peer_alias_body, {0: 0}
        in_specs = [pl.BlockSpec(memory_space=pl.ANY), pl.BlockSpec(memory_space=pl.ANY)] + in_specs
        args = (prev, after) + args
    return pl.pallas_call(
        body,
        grid=(nb,),
        in_specs=in_specs,
        out_specs=pl.BlockSpec((tp, d), lambda i: (i + blk0, 0)),
        out_shape=jax.ShapeDtypeStruct((t, d), F32),
        scratch_shapes=[pltpu.VMEM((2, planes, PEER_SLOTS, LANES), F32) for _ in range(PEER_NBUF)]
        + [pltpu.SemaphoreType.DMA((PEER_NBUF,))],
        input_output_aliases=aliases,
        compiler_params=pltpu.CompilerParams(
            dimension_semantics=("arbitrary",), vmem_limit_bytes=VMEM_LIMIT),
        name="peer",
    )(*args)


SC_CORES = 2
SC_SUBCORES = 16
SC_LANES = 16
SC_WORKERS = SC_CORES * SC_SUBCORES
SC_ROWS = 32
SC_RB = 4


SC_VPR = LANES // SC_LANES
SC_VROWS = PEER_SLOTS // SC_VPR


def _sc_vec(j):
    return (j // SC_VPR, pl.ds((j % SC_VPR) * SC_LANES, SC_LANES))


def _sc_mesh():
    return plsc.VectorSubcoreMesh(core_axis_name="c", subcore_axis_name="s")


def _sc_worker():
    return lax.axis_index("s") * SC_CORES + lax.axis_index("c")


def _sc_down(idx_flat, xn3, down3, tok0, n_tok):
    planes = xn3.shape[1]
    tw = n_tok // SC_WORKERS
    nchunk = PEER_SLOTS // SC_ROWS
    per_plane = LANES // SC_LANES

    @functools.partial(
        pl.kernel, mesh=_sc_mesh(),
        out_type=jax.ShapeDtypeStruct((n_tok, SC_VROWS, LANES), F32),
        scratch_types=[
            pltpu.VMEM((PEER_SLOTS,), jnp.int32),
            pltpu.VMEM((planes, LANES), F32),
            pltpu.VMEM((2, SC_ROWS, planes, LANES), F32),
            pltpu.VMEM((SC_VROWS, LANES), F32),
            pltpu.VMEM((planes, SC_VROWS, LANES), F32),
            pltpu.SemaphoreType.DMA((2,)),
        ],
        name="sc_down",
    )
    def run(idx_hbm, xn_hbm, down_hbm, out_hbm, idx_v, x_v, buf, act_v, part_v, sem):
        base = tok0 + _sc_worker() * tw

        def gather(kc):
            return pltpu.make_async_copy(
                down_hbm.at[idx_v.at[pl.ds(kc * SC_ROWS, SC_ROWS)]], buf.at[kc % 2], sem.at[kc % 2])

        def token(ti, carry):
            tok = base + ti
            pltpu.sync_copy(idx_hbm.at[pl.ds(tok * PEER_SLOTS, PEER_SLOTS)], idx_v)
            pltpu.sync_copy(xn_hbm.at[tok], x_v)
            gather(0).start()
            for kc in range(nchunk):
                if kc + 1 < nchunk:
                    gather(kc + 1).start()
                gather(kc).wait()

                @plsc.parallel_loop(0, planes)
                def _(p, kc=kc):
                    sls = [pl.ds(q * SC_LANES, SC_LANES) for q in range(per_plane)]
                    xs = [x_v[p, sl] for sl in sls]
                    for j in range(SC_ROWS):
                        acc = buf[kc % 2, j, p, sls[0]] * xs[0]
                        for q in range(1, per_plane):
                            acc = acc + buf[kc % 2, j, p, sls[q]] * xs[q]
                        part_v[(p,) + _sc_vec(kc * SC_ROWS + j)] = acc

            @plsc.parallel_loop(0, SC_VROWS)
            def _(r):
                for u in range(SC_VPR):
                    sl = pl.ds(u * SC_LANES, SC_LANES)
                    acc = part_v[0, r, sl]
                    for p in range(1, planes):
                        acc = acc + part_v[p, r, sl]
                    act_v[r, sl] = acc
            pltpu.sync_copy(act_v, out_hbm.at[tok - tok0])
            return carry

        lax.fori_loop(0, tw, token, 0)

    return run(idx_flat, xn3, down3)


def _sc_up(idx_flat, w16, x13, up3, tok0, n_tok):
    planes = x13.shape[1]
    tw = n_tok // SC_WORKERS
    nchunk = PEER_SLOTS // SC_ROWS
    per_plane = LANES // SC_LANES

    @functools.partial(
        pl.kernel, mesh=_sc_mesh(),
        out_type=jax.ShapeDtypeStruct((n_tok, planes, LANES), F32),
        scratch_types=[
            pltpu.VMEM((PEER_SLOTS,), jnp.int32),
            pltpu.VMEM((SC_VROWS, LANES), F32),
            pltpu.VMEM((planes, LANES), F32),
            pltpu.VMEM((2, SC_ROWS, planes, LANES), F32),
            pltpu.SemaphoreType.DMA((2,)),
        ],
        name="sc_up",
    )
    def run(idx_hbm, w_hbm, x1_hbm, up_hbm, out_hbm, idx_v, w_v, out_v, buf, sem):
        base = tok0 + _sc_worker() * tw

        def gather(kc):
            return pltpu.make_async_copy(
                up_hbm.at[idx_v.at[pl.ds(kc * SC_ROWS, SC_ROWS)]], buf.at[kc % 2], sem.at[kc % 2])

        def token(ti, carry):
            tok = base + ti
            pltpu.sync_copy(idx_hbm.at[pl.ds(tok * PEER_SLOTS, PEER_SLOTS)], idx_v)
            pltpu.sync_copy(w_hbm.at[tok - tok0], w_v)
            pltpu.sync_copy(x1_hbm.at[tok], out_v)
            gather(0).start()
            for kc in range(nchunk):
                if kc + 1 < nchunk:
                    gather(kc + 1).start()
                gather(kc).wait()

                @plsc.parallel_loop(0, planes)
                def _(p, kc=kc):
                    sls = [pl.ds(q * SC_LANES, SC_LANES) for q in range(per_plane)]
                    accs = [out_v[p, sl] for sl in sls]
                    for j in range(SC_ROWS):
                        wj = w_v[_sc_vec(kc * SC_ROWS + j)]
                        accs = [a + wj * buf[kc % 2, j, p, sl] for a, sl in zip(accs, sls)]
                    for a, sl in zip(accs, sls):
                        out_v[p, sl] = a
            pltpu.sync_copy(out_v, out_hbm.at[tok - tok0])
            return carry

        lax.fori_loop(0, tw, token, 0)

    return run(idx_flat, w16, x13, up3)


ACT_TM = 256


def _split3_bf16(a):
    hi = a.astype(BF16)
    r = a - hi.astype(F32)
    mid = r.astype(BF16)
    lo = (r - mid.astype(F32)).astype(BF16)
    return hi, mid, lo


def _peer_act_body(after_ref, part_ref, gate_ref, w_ref):
    del after_ref
    li = lax.broadcasted_iota(jnp.int32, (LANES, LANES), 0)
    lj = lax.broadcasted_iota(jnp.int32, (LANES, LANES), 1)
    group_sum = (li // SC_LANES == lj // SC_LANES).astype(BF16)
    gate = _split3_bf16(gate_ref[...])
    for r in range(SC_VROWS):
        act = sum(_dot(p, group_sum) for p in _split3_bf16(part_ref[:, r, :]))
        spread = (li == r * SC_VPR + lj // SC_LANES).astype(BF16)
        g = sum(_dot(p, spread) for p in gate)
        w_ref[:, r, :] = g * (0.5 * act * (1.0 + _erf(act * (2.0 ** -0.5))))


def _peer_act(after, part3, gate_tok, blk0):
    n_tok = part3.shape[0]
    tm = ACT_TM
    return pl.pallas_call(
        _peer_act_body,
        grid=(n_tok // tm,),
        in_specs=[
            pl.BlockSpec(memory_space=pl.ANY),
            pl.BlockSpec((tm, SC_VROWS, LANES), lambda i: (i, 0, 0)),
            pl.BlockSpec((tm, PEER_SLOTS), lambda i: (i + blk0, 0)),
        ],
        out_specs=pl.BlockSpec((tm, SC_VROWS, LANES), lambda i: (i, 0, 0)),
        out_shape=jax.ShapeDtypeStruct(part3.shape, F32),
        compiler_params=pltpu.CompilerParams(
            dimension_semantics=("parallel",), vmem_limit_bytes=VMEM_LIMIT),
        name="peer_act",
    )(after, part3, gate_tok)


def _table_pack_body(down_ref, up_ref, tab_ref):
    te, d = down_ref.shape
    planes = d // LANES
    for p in range(planes):
        cols = slice(p * LANES, (p + 1) * LANES)
        tab_ref[pl.ds(p, te, stride=2 * planes), :] = down_ref[:, cols]
        tab_ref[pl.ds(planes + p, te, stride=2 * planes), :] = up_ref[:, cols]


def _table_pack(down, up, te=512):
    n_exp, d = down.shape
    rows = 2 * d // LANES
    return pl.pallas_call(
        _table_pack_body,
        grid=(n_exp // te,),
        in_specs=[pl.BlockSpec((te, d), lambda i: (i, 0)), pl.BlockSpec((te, d), lambda i: (i, 0))],
        out_specs=pl.BlockSpec((te * rows, LANES), lambda i: (i, 0)),
        out_shape=jax.ShapeDtypeStruct((n_exp * rows, LANES), F32),
        compiler_params=pltpu.CompilerParams(
            dimension_semantics=("parallel",), vmem_limit_bytes=VMEM_LIMIT),
        name="table_pack",
    )(down, up)


def _pad_lanes(a):
    return jnp.pad(a, ((0, 0), (0, LANES - a.shape[1])))


def kernel(x, norm_mix_w, w_in, a_q_norm_w, a_k_norm_w, a_rel_bias, b_conv_w, b_a_log, b_dt_bias, b_norm_w, b_merge, w_proj_a, w_proj_b, w_out, norm_ffn_w, peer_w_query, peer_keys_1, peer_keys_2, peer_down, peer_up):
    b, s, d = x.shape
    t = b * s
    depth = w_in.shape[0]
    for l in range(depth):
        x2 = x.reshape(t, d)

        wi = w_in[l]
        o_qkvb = 3 * A_WIDTH
        o_a = o_qkvb + 3 * B_WIDTH
        o_beta = o_a + B_HEADS
        o_gate = o_beta + B_HEADS
        o_merge = o_gate + B_WIDTH
        w_main = jnp.concatenate(
            [wi[:, o_qkvb:o_a], wi[:, o_gate:o_merge], wi[:, o_merge:], wi[:, :o_qkvb]], axis=1).astype(BF16)
        w_small = _pad_lanes(wi[:, o_a:o_gate]).astype(BF16)
        z_attn = Z_MERGE + 2 * d

        z, zs = _in_proj(x2, norm_mix_w[l][None, :], w_main, w_small, tm=1024, tn=1920)
        z3 = z.reshape(b, s, z.shape[1])

        seg = jnp.kron(jnp.eye(A_HEADS, dtype=F32), jnp.full((A_HEAD_DIM, A_HEAD_DIM), 1.0 / A_HEAD_DIM, F32))
        o_att = _band_attention(
            z3, _attn_bias_table(a_rel_bias[l]),
            jnp.tile(a_q_norm_w[l], A_HEADS)[None, :], jnp.tile(a_k_norm_w[l], A_HEADS)[None, :],
            seg.astype(BF16), z_attn // A_WIDTH)

        head_params = jnp.zeros((SUBLANES, LANES), F32)
        head_params = head_params.at[0, :B_HEADS].set(b_a_log[l]).at[1, :B_HEADS].set(b_dt_bias[l])
        head_params = head_params.at[2, :B_HEADS].set(1.0)
        o_gdn = _gated_deltanet(z3, zs.reshape(b, s, LANES), b_conv_w[l], head_params, b_norm_w[l][None, :])

        x1, x1p, xnp, idx, gate_t, gate_tok = _merge_route(
            o_att.reshape(t, A_WIDTH), o_gdn.reshape(t, B_WIDTH), z, x2, b_merge[l][None, :],
            w_proj_a[l].astype(BF16), w_proj_b[l].astype(BF16), w_out[l].astype(BF16), norm_ffn_w[l][None, :],
            peer_w_query[l].astype(BF16), peer_keys_1[l].astype(BF16), peer_keys_2[l].astype(BF16))

        planes = d // LANES
        n_exp = peer_down.shape[1]
        table = _table_pack(peer_down[l], peer_up[l]).reshape(2 * n_exp, planes, LANES)
        idx3 = idx.reshape(t // PEER_TP, PEER_TP, PEER_SLOTS)

        n_sc = t * PEER_SC_SHARE_NUM // PEER_SC_SHARE_DEN
        t0 = t - n_sc
        nblk = t0 // PEER_TP
        nblk_a = nblk // 2
        idx_down = (2 * idx).reshape(-1)
        part = _sc_down(idx_down, xnp.reshape(t, planes, LANES), table, t0, n_sc)
        out = _peer(idx3, xnp, gate_t, x1, table, 0, nblk_a)
        w16 = _peer_act(out, part, gate_tok, t0 // ACT_TM)
        out = _peer(idx3, xnp, gate_t, x1, table, nblk_a, nblk - nblk_a, prev=out, after=w16)
        sc_out = _sc_up(idx_down + 1, w16, x1p.reshape(t, planes, LANES), table, t0, n_sc)
        out = lax.dynamic_update_slice(out, sc_out.reshape(n_sc, d), (t0, 0))
        x = out.reshape(b, s, d)
    return x
```

```python
import functools

import jax
import jax.numpy as jnp
from jax import lax
from jax.experimental import pallas as pl
from jax.experimental.pallas import tpu as pltpu
from jax.experimental.pallas import tpu_sc as plsc

F32 = jnp.float32
BF16 = jnp.bfloat16

EPS = 1e-6
NEG = -1e30

CHUNK = 64
A_HEADS = 8
A_HEAD_DIM = 64
A_WIDTH = A_HEADS * A_HEAD_DIM
A_LEFT_CHUNKS = 8
REL_CLIP = 128
B_HEADS = 8
B_HEAD_DIM = 128
B_WIDTH = B_HEADS * B_HEAD_DIM
CONV_WIDTH = 4
PEER_HEADS = 8
PEER_HALF = 128
PEER_NKEYS = 128
PEER_TOPK = 16
PEER_SLOTS = PEER_HEADS * PEER_TOPK

LANES = 128
SUBLANES = 8
VMEM_LIMIT = 56 * 1024 * 1024

Z_QKV_B = 0
Z_GATE_B = 3 * B_WIDTH
Z_MERGE = Z_GATE_B + B_WIDTH
Z_SMALL_A = 0
Z_SMALL_BETA = B_HEADS


def _dot(a, b):
    return jnp.dot(a, b, preferred_element_type=F32)


def _dot_nt(a, b):
    return lax.dot_general(a, b, (((1,), (1,)), ((), ())), preferred_element_type=F32)


def _dot_tn(a, b):
    return lax.dot_general(a, b, (((0,), (0,)), ((), ())), preferred_element_type=F32)


def _split_bf16(a):
    hi = a.astype(BF16)
    lo = (a - hi.astype(F32)).astype(BF16)
    return hi, lo


def _sigmoid(x):
    return 1.0 / (1.0 + jnp.exp(-x))


def _inproj_body(x_ref, nw_ref, w_ref, ws_ref, z_ref, zs_ref, h_scr):
    @pl.when(pl.program_id(1) == 0)
    def _():
        x = x_ref[...]
        ms = jnp.mean(x * x, axis=-1, keepdims=True)
        hb = (x * lax.rsqrt(ms + EPS) * nw_ref[...]).astype(BF16)
        h_scr[...] = hb
        zs_ref[...] = _dot(hb, ws_ref[...])

    z_ref[...] = _dot(h_scr[...], w_ref[...]).astype(z_ref.dtype)


def _in_proj(x2, norm_w, w_main, w_small, tm, tn):
    t, d = x2.shape
    n = w_main.shape[1]
    return pl.pallas_call(
        _inproj_body,
        grid=(t // tm, n // tn),
        in_specs=[
            pl.BlockSpec((tm, d), lambda i, j: (i, 0)),
            pl.BlockSpec((1, d), lambda i, j: (0, 0)),
            pl.BlockSpec((d, tn), lambda i, j: (0, j)),
            pl.BlockSpec((d, LANES), lambda i, j: (0, 0)),
        ],
        out_specs=[
            pl.BlockSpec((tm, tn), lambda i, j: (i, j)),
            pl.BlockSpec((tm, LANES), lambda i, j: (i, 0)),
        ],
        out_shape=[
            jax.ShapeDtypeStruct((t, n), BF16),
            jax.ShapeDtypeStruct((t, LANES), F32),
        ],
        scratch_shapes=[pltpu.VMEM((tm, d), BF16)],
        compiler_params=pltpu.CompilerParams(
            dimension_semantics=("parallel", "arbitrary"), vmem_limit_bytes=VMEM_LIMIT),
        name="in_proj",
    )(x2, norm_w, w_main, w_small)


ATT_TQ = 256
ATT_WIN = ATT_TQ + A_LEFT_CHUNKS * CHUNK
ATT_NKB = ATT_WIN // ATT_TQ


def _attn_body(q_ref, k0_ref, k1_ref, k2_ref, v0_ref, v1_ref, v2_ref, bias_ref, qw_ref, kw_ref, seg_ref,
               o_ref):
    t = pl.program_id(1)
    seg = seg_ref[...]

    def head_rms(a, w):
        hi, lo = _split_bf16(a * a)
        ms = _dot(hi, seg) + _dot(lo, seg)
        return a * lax.rsqrt(ms + EPS) * w

    q = head_rms(q_ref[0].astype(F32), qw_ref[...]) * (A_HEAD_DIM ** -0.5)
    k = jnp.concatenate([k0_ref[0], k1_ref[0], k2_ref[0]], axis=0).astype(F32)
    k = head_rms(k, kw_ref[...]).astype(BF16)
    v = jnp.concatenate([v0_ref[0], v1_ref[0], v2_ref[0]], axis=0)

    kpos = t * ATT_TQ - A_LEFT_CHUNKS * CHUNK + lax.broadcasted_iota(jnp.int32, (1, ATT_WIN), 1)
    valid = kpos >= 0
    lane = lax.broadcasted_iota(jnp.int32, (1, LANES), 1)
    first = lane < A_HEAD_DIM

    for pair in range(A_HEADS // 2):
        cols = slice(pair * LANES, (pair + 1) * LANES)
        qp = q[:, cols]
        kp = k[:, cols]
        vp = v[:, cols]
        outs = []
        for half in range(2):
            h = 2 * pair + half
            qm = jnp.where(first if half == 0 else jnp.logical_not(first), qp, 0.0).astype(BF16)
            s = _dot_nt(qm, kp) + bias_ref[h]
            s = jnp.where(valid, s, NEG)
            mx = jnp.max(s, axis=-1, keepdims=True)
            p = jnp.exp(s - mx)
            den = jnp.sum(p, axis=-1, keepdims=True)
            outs.append(_dot(p.astype(BF16), vp) / den)
        o_ref[0, :, cols] = jnp.where(first, outs[0], outs[1]).astype(o_ref.dtype)


def _band_attention(z3, bias_tab, qw, kw, seg, col_q):
    b, s, _ = z3.shape
    nt = s // ATT_TQ

    def kv_spec(back, col):
        return pl.BlockSpec((1, ATT_TQ, A_WIDTH), lambda bi, ti: (bi, jnp.maximum(ti - back, 0), col))

    return pl.pallas_call(
        _attn_body,
        grid=(b, nt),
        in_specs=[
            pl.BlockSpec((1, ATT_TQ, A_WIDTH), lambda bi, ti: (bi, ti, col_q)),
            kv_spec(2, col_q + 1), kv_spec(1, col_q + 1), kv_spec(0, col_q + 1),
            kv_spec(2, col_q + 2), kv_spec(1, col_q + 2), kv_spec(0, col_q + 2),
            pl.BlockSpec((A_HEADS, ATT_TQ, ATT_WIN), lambda bi, ti: (0, 0, 0)),
            pl.BlockSpec((1, A_WIDTH), lambda bi, ti: (0, 0)),
            pl.BlockSpec((1, A_WIDTH), lambda bi, ti: (0, 0)),
            pl.BlockSpec((A_WIDTH, A_WIDTH), lambda bi, ti: (0, 0)),
        ],
        out_specs=pl.BlockSpec((1, ATT_TQ, A_WIDTH), lambda bi, ti: (bi, ti, 0)),
        out_shape=jax.ShapeDtypeStruct((b, s, A_WIDTH), BF16),
        compiler_params=pltpu.CompilerParams(
            dimension_semantics=("parallel", "parallel"), vmem_limit_bytes=VMEM_LIMIT),
        name="band_attn",
    )(z3, z3, z3, z3, z3, z3, z3, bias_tab, qw, kw, seg)


def _attn_bias_table(rel_bias):
    rb = rel_bias.astype(F32)
    heads = rb.shape[0]
    lo = A_LEFT_CHUNKS * CHUNK - (ATT_WIN - 1)
    hi = A_LEFT_CHUNKS * CHUNK + ATT_TQ - 1
    strip = jnp.concatenate([
        jnp.broadcast_to(rb[:, :1], (heads, -REL_CLIP - lo)), rb,
        jnp.broadcast_to(rb[:, -1:], (heads, hi - REL_CLIP))], axis=1)
    rev = strip[:, ::-1]
    n = rev.shape[1]
    flat = jnp.broadcast_to(rev[:, None, :], (heads, ATT_TQ, n)).reshape(heads, ATT_TQ * n)
    bias = flat[:, ATT_TQ - 1:ATT_TQ - 1 + ATT_TQ * (n - 1)].reshape(heads, ATT_TQ, n - 1)[:, :, :ATT_WIN]
    qc = jnp.arange(ATT_TQ)[:, None] // CHUNK
    kc = jnp.arange(ATT_WIN)[None, :] // CHUNK
    in_band = (kc >= qc) & (kc <= qc + A_LEFT_CHUNKS)
    return jnp.where(in_band[None], bias, NEG)


GDN_TAIL = SUBLANES


def _gdn_body(qkv_ref, zs_ref, gate_ref, convw_ref, hp_ref, nw_ref, o_ref, xbuf, s_scr):
    c = CHUNK
    hd = B_HEAD_DIM

    @pl.when(pl.program_id(1) == 0)
    def _():
        xbuf[0:GDN_TAIL, :] = jnp.zeros((GDN_TAIL, 3 * B_WIDTH), F32)
        s_scr[...] = jnp.zeros_like(s_scr)

    xbuf[GDN_TAIL:GDN_TAIL + c, :] = qkv_ref[0].astype(F32)
    y = None
    for j in range(CONV_WIDTH):
        tap = convw_ref[j:j + 1, :] * xbuf[pl.ds(GDN_TAIL - (CONV_WIDTH - 1) + j, c), :]
        y = tap if y is None else y + tap
    xbuf[0:GDN_TAIL, :] = xbuf[c:c + GDN_TAIL, :]
    y = y * _sigmoid(y)

    zs = zs_ref[0]
    a_neg = -(jnp.exp(hp_ref[0:1, :]) * hp_ref[2:3, :])
    xs = zs + hp_ref[1:2, :]
    g_all = a_neg * (jnp.maximum(xs, 0.0) + jnp.log(1.0 + jnp.exp(-jnp.abs(xs))))
    beta_all = _sigmoid(zs)

    row = lax.broadcasted_iota(jnp.int32, (c, c), 0)
    col = lax.broadcasted_iota(jnp.int32, (c, c), 1)
    tril = row >= col
    tril_strict = row > col
    eye = (row == col).astype(F32)
    lower = tril.astype(BF16)
    upper = (row <= col).astype(BF16)

    gh, gl = _split_bf16(g_all)
    gc_all = _dot(lower, gh) + _dot(lower, gl)
    gth, gtl = _split_bf16(g_all.T)
    gc_t = _dot(gth, upper) + _dot(gtl, upper)

    heads = range(B_HEADS)
    q, k, v, beta, gc, gc_row, g_last = [], [], [], [], [], [], []
    for h in heads:
        qh = y[:, h * hd:(h + 1) * hd]
        kh = y[:, B_WIDTH + h * hd:B_WIDTH + (h + 1) * hd]
        q.append(qh * lax.rsqrt(jnp.sum(qh * qh, axis=-1, keepdims=True) + EPS) * (hd ** -0.5))
        k.append(kh * lax.rsqrt(jnp.sum(kh * kh, axis=-1, keepdims=True) + EPS))
        v.append(y[:, 2 * B_WIDTH + h * hd:2 * B_WIDTH + (h + 1) * hd])
        beta.append(beta_all[:, Z_SMALL_BETA + h:Z_SMALL_BETA + h + 1])
        gc.append(gc_all[:, Z_SMALL_A + h:Z_SMALL_A + h + 1])
        gc_row.append(gc_t[Z_SMALL_A + h:Z_SMALL_A + h + 1, :])
        g_last.append(gc_row[h][:, c - 1:c])
    decay = [jnp.exp(jnp.where(tril, gc[h] - gc_row[h], NEG)) for h in heads]
    e_gc = [jnp.exp(gc[h]) for h in heads]
    kb = [k[h] * beta[h] for h in heads]
    kf = [k[h].astype(BF16) for h in heads]
    a_pow = [jnp.where(tril_strict, _dot_nt(kb[h].astype(BF16), kf[h]) * decay[h], 0.0) for h in heads]
    attn = [(_dot_nt(q[h].astype(BF16), kf[h]) * decay[h]).astype(BF16) for h in heads]

    t_mat = [eye - a_pow[h] for h in heads]
    for _ in range(5):
        ab = [a_pow[h].astype(BF16) for h in heads]
        a_pow = [_dot(ab[h], ab[h]) for h in heads]
        t_mat = [t_mat[h] + _dot(t_mat[h].astype(BF16), a_pow[h].astype(BF16)) for h in heads]
    tb = [t_mat[h].astype(BF16) for h in heads]
    u = [_dot(tb[h], (v[h] * beta[h]).astype(BF16)) for h in heads]
    w = [_dot(tb[h], (kb[h] * e_gc[h]).astype(BF16)).astype(BF16) for h in heads]

    state = [s_scr[h] for h in heads]
    sb = [state[h].astype(BF16) for h in heads]
    vnb = [(u[h] - _dot(w[h], sb[h])).astype(BF16) for h in heads]
    o = [_dot((q[h] * e_gc[h]).astype(BF16), sb[h]) + _dot(attn[h], vnb[h]) for h in heads]
    k_dec = [(k[h] * jnp.exp(g_last[h] - gc[h])).astype(BF16) for h in heads]
    for h in heads:
        s_scr[h] = state[h] * jnp.exp(g_last[h]) + _dot_tn(k_dec[h], vnb[h])
    for h in heads:
        cols = slice(h * hd, (h + 1) * hd)
        gate = gate_ref[0, :, cols].astype(F32)
        on = o[h] * lax.rsqrt(jnp.mean(o[h] * o[h], axis=-1, keepdims=True) + EPS) * nw_ref[...]
        o_ref[0, :, cols] = (on * (gate * _sigmoid(gate))).astype(o_ref.dtype)


def _gated_deltanet(z3, zs3, conv_w, head_params, norm_w):
    b, s, _ = z3.shape
    n = s // CHUNK
    return pl.pallas_call(
        _gdn_body,
        grid=(b, n),
        in_specs=[
            pl.BlockSpec((1, CHUNK, 3 * B_WIDTH), lambda bi, ni: (bi, ni, Z_QKV_B // (3 * B_WIDTH))),
            pl.BlockSpec((1, CHUNK, LANES), lambda bi, ni: (bi, ni, 0)),
            pl.BlockSpec((1, CHUNK, B_WIDTH), lambda bi, ni: (bi, ni, Z_GATE_B // B_WIDTH)),
            pl.BlockSpec((CONV_WIDTH, 3 * B_WIDTH), lambda bi, ni: (0, 0)),
            pl.BlockSpec((SUBLANES, LANES), lambda bi, ni: (0, 0)),
            pl.BlockSpec((1, B_HEAD_DIM), lambda bi, ni: (0, 0)),
        ],
        out_specs=pl.BlockSpec((1, CHUNK, B_WIDTH), lambda bi, ni: (bi, ni, 0)),
        out_shape=jax.ShapeDtypeStruct((b, s, B_WIDTH), BF16),
        scratch_shapes=[
            pltpu.VMEM((GDN_TAIL + CHUNK, 3 * B_WIDTH), F32),
            pltpu.VMEM((B_HEADS, B_HEAD_DIM, B_HEAD_DIM), F32),
        ],
        compiler_params=pltpu.CompilerParams(
            dimension_semantics=("parallel", "arbitrary"), vmem_limit_bytes=VMEM_LIMIT),
        name="gdn",
    )(z3, zs3, z3, conv_w, head_params, norm_w)


MERGE_TM = 256


def _top_k_rows(s, k, payload=None):
    n = s.shape[0]
    rows = lax.broadcasted_iota(jnp.int32, s.shape, 0)
    vals, picks = [], []
    for _ in range(k):
        m = jnp.max(s, axis=0, keepdims=True)
        i = jnp.min(jnp.where(s == m, rows, n), axis=0, keepdims=True)
        hit = rows == i
        vals.append(m)
        picks.append(i if payload is None else jnp.max(jnp.where(hit, payload, -1), axis=0, keepdims=True))
        s = jnp.where(hit, -jnp.inf, s)
    return jnp.concatenate(vals, axis=0), jnp.concatenate(picks, axis=0)


def _merge_body(oa_ref, ob_ref, mr_ref, x_ref, bm_ref, pa_ref, pb_ref, wo_ref, nw_ref, wq_ref, k1_ref, k2_ref,
                x1_ref, x1p_ref, xnp_ref, idx_ref, gate_ref, gate_tok_ref):
    tm, d = x_ref.shape
    planes = d // LANES

    def store_planes(ref, a):
        for p in range(planes):
            ref[pl.ds(p, tm, stride=planes), :] = a[:, p * LANES:(p + 1) * LANES]

    mr = mr_ref[...].astype(F32) + bm_ref[...]
    mixed = (_sigmoid(mr[:, :d]) * _dot(oa_ref[...], pa_ref[...])
             + _sigmoid(mr[:, d:]) * _dot(ob_ref[...], pb_ref[...]))
    x1 = x_ref[...] + _dot(mixed.astype(BF16), wo_ref[...])
    x1_ref[...] = x1
    store_planes(x1p_ref, x1)
    xn = x1 * lax.rsqrt(jnp.mean(x1 * x1, axis=-1, keepdims=True) + EPS) * nw_ref[...]
    store_planes(xnp_ref, xn)
    q = _dot(xn.astype(BF16), wq_ref[...]).astype(BF16)

    idx_rows, gate_rows = [], []
    for h in range(PEER_HEADS):
        q1 = q[:, (2 * h) * PEER_HALF:(2 * h + 1) * PEER_HALF]
        q2 = q[:, (2 * h + 1) * PEER_HALF:(2 * h + 2) * PEER_HALF]
        s1 = _dot_nt(k1_ref[h], q1)
        s2 = _dot_nt(k2_ref[h], q2)
        v1, i1 = _top_k_rows(s1, PEER_TOPK)
        v2, i2 = _top_k_rows(s2, PEER_TOPK)
        nb = [PEER_TOPK // (a + 1) for a in range(PEER_TOPK)]
        pad = -sum(nb) % SUBLANES
        tm = v1.shape[1]
        cand = jnp.concatenate([v1[a:a + 1] + v2[:nb[a]] for a in range(PEER_TOPK)]
                               + [jnp.full((pad, tm), -jnp.inf, F32)], axis=0)
        cand_idx = jnp.concatenate([i1[a:a + 1] * PEER_NKEYS + i2[:nb[a]] for a in range(PEER_TOPK)]
                                   + [jnp.zeros((pad, tm), jnp.int32)], axis=0)
        top_s, expert = _top_k_rows(cand, PEER_TOPK, payload=cand_idx)
        e = jnp.exp(top_s - top_s[0:1])
        gate_rows.append(e / jnp.sum(e, axis=0, keepdims=True))
        idx_rows.append(expert)
    idx_ref[...] = jnp.concatenate(idx_rows, axis=0).T
    gates = jnp.concatenate(gate_rows, axis=0)
    gate_ref[...] = gates
    gate_tok_ref[...] = gates.T


def _merge_route(o_a, o_b, z2, x2, b_merge, pa, pb, wo, nw, wq, k1, k2):
    t, d = x2.shape
    tm = MERGE_TM
    full = lambda a: pl.BlockSpec(a.shape, lambda i: (0,) * a.ndim)
    return pl.pallas_call(
        _merge_body,
        grid=(t // tm,),
        in_specs=[
            pl.BlockSpec((tm, A_WIDTH), lambda i: (i, 0)),
            pl.BlockSpec((tm, B_WIDTH), lambda i: (i, 0)),
            pl.BlockSpec((tm, 2 * d), lambda i: (i, Z_MERGE // (2 * d))),
            pl.BlockSpec((tm, d), lambda i: (i, 0)),
            full(b_merge), full(pa), full(pb), full(wo), full(nw), full(wq), full(k1), full(k2),
        ],
        out_specs=[
            pl.BlockSpec((tm, d), lambda i: (i, 0)),
            pl.BlockSpec((tm * d // LANES, LANES), lambda i: (i, 0)),
            pl.BlockSpec((tm * d // LANES, LANES), lambda i: (i, 0)),
            pl.BlockSpec((tm, PEER_SLOTS), lambda i: (i, 0)),
            pl.BlockSpec((PEER_SLOTS, tm), lambda i: (0, i)),
            pl.BlockSpec((tm, PEER_SLOTS), lambda i: (i, 0)),
        ],
        out_shape=[
            jax.ShapeDtypeStruct((t, d), F32),
            jax.ShapeDtypeStruct((t * d // LANES, LANES), F32),
            jax.ShapeDtypeStruct((t * d // LANES, LANES), F32),
            jax.ShapeDtypeStruct((t, PEER_SLOTS), jnp.int32),
            jax.ShapeDtypeStruct((PEER_SLOTS, t), F32),
            jax.ShapeDtypeStruct((t, PEER_SLOTS), F32),
        ],
        compiler_params=pltpu.CompilerParams(
            dimension_semantics=("parallel",), vmem_limit_bytes=VMEM_LIMIT),
        name="merge_route",
    )(o_a, o_b, z2, x2, b_merge, pa, pb, wo, nw, wq, k1, k2)


PEER_TP = 128
PEER_NBUF = 4
PEER_LOOK = PEER_NBUF - 1
PEER_SC_SHARE_NUM, PEER_SC_SHARE_DEN = 11, 32


def _erf(x):
    return lax.erf(x)


PACK_HI = -65536


def _packed_down(bitcast, word):
    return bitcast(word & PACK_HI, F32)


def _packed_up(bitcast, word):
    return bitcast(word << 16, F32)


def _peer_body(idx_ref, idx_next_ref, xn_ref, gate_ref, x1_ref, tab_ref, o_ref, *scratch):
    bufs, sem = scratch[:PEER_NBUF], scratch[PEER_NBUF]
    planes = x1_ref.shape[1] // LANES
    step = pl.program_id(0)

    def issue(ref, tok, slot):
        for j in range(PEER_SLOTS):
            pltpu.make_async_copy(tab_ref.at[ref[0, tok, j]], bufs[slot].at[:, j, :], sem.at[slot]).start()

    def wait(slot):
        pltpu.make_async_copy(bufs[slot], bufs[slot], sem.at[slot]).wait()

    @pl.when(step == 0)
    def _():
        for s in range(PEER_LOOK):
            issue(idx_ref, s, s)

    lane = lax.broadcasted_iota(jnp.int32, (PEER_SLOTS, PEER_TP), 1)

    def compute(tok, slot):
        buf = bufs[slot]
        acc = None
        for s in range(planes):
            term = _packed_down(lax.bitcast_convert_type, buf[s]) * xn_ref[pl.ds(tok * planes + s, 1), :]
            acc = term if acc is None else acc + term
        act = jnp.sum(acc, axis=-1, keepdims=True)
        gate = jnp.sum(jnp.where(lane == tok, gate_ref[...], 0.0), axis=-1, keepdims=True)
        w = gate * (0.5 * act * (1.0 + _erf(act * (2.0 ** -0.5))))
        out = jnp.concatenate(
            [jnp.sum(_packed_up(lax.bitcast_convert_type, buf[s]) * w, axis=0, keepdims=True)
             for s in range(planes)], axis=1)
        o_ref[pl.ds(tok, 1), :] = x1_ref[pl.ds(tok, 1), :] + out

    def group(g, carry):
        for u in range(PEER_NBUF):
            tok = g * PEER_NBUF + u
            issue(idx_ref, tok + PEER_LOOK, (u + PEER_LOOK) % PEER_NBUF)
            wait(u)
            compute(tok, u)
        return carry

    n_groups = PEER_TP // PEER_NBUF
    lax.fori_loop(0, n_groups - 1, group, 0)

    for u in range(PEER_NBUF):
        tok = (n_groups - 1) * PEER_NBUF + u
        nxt = tok + PEER_LOOK
        if nxt < PEER_TP:
            issue(idx_ref, nxt, (u + PEER_LOOK) % PEER_NBUF)
        else:
            issue(idx_next_ref, nxt - PEER_TP, (u + PEER_LOOK) % PEER_NBUF)
        wait(u)
        compute(tok, u)

    @pl.when(step == pl.num_programs(0) - 1)
    def _():
        for s in range(PEER_LOOK):
            wait(s)


def _peer_alias_body(prev_ref, after_ref, *rest):
    del prev_ref, after_ref
    _peer_body(*rest)


def _peer(idx3, xnp, gate_t, x1, table, blk0, nb, prev=None, after=None):
    t, d = x1.shape
    tp = PEER_TP
    planes = d // LANES
    in_specs = [
        pl.BlockSpec((1, tp, PEER_SLOTS), lambda i: (i + blk0, 0, 0), memory_space=pltpu.SMEM),
        pl.BlockSpec((1, tp, PEER_SLOTS), lambda i: (jnp.minimum(i + 1, nb - 1) + blk0, 0, 0),
                     memory_space=pltpu.SMEM),
        pl.BlockSpec((tp * planes, LANES), lambda i: (i + blk0, 0)),
        pl.BlockSpec((PEER_SLOTS, tp), lambda i: (0, i + blk0)),
        pl.BlockSpec((tp, d), lambda i: (i + blk0, 0)),
        pl.BlockSpec(memory_space=pl.ANY),
    ]
    args = (idx3, idx3, xnp, gate_t, x1, table)
    body, aliases = _peer_body, {}
    if prev is not None:
        body, aliases = _peer_alias_body, {0: 0}
        in_specs = [pl.BlockSpec(memory_space=pl.ANY), pl.BlockSpec(memory_space=pl.ANY)] + in_specs
        args = (prev, after) + args
    return pl.pallas_call(
        body,
        grid=(nb,),
        in_specs=in_specs,
        out_specs=pl.BlockSpec((tp, d), lambda i: (i + blk0, 0)),
        out_shape=jax.ShapeDtypeStruct((t, d), F32),
        scratch_shapes=[pltpu.VMEM((planes, PEER_SLOTS, LANES), jnp.int32) for _ in range(PEER_NBUF)]
        + [pltpu.SemaphoreType.DMA((PEER_NBUF,))],
        input_output_aliases=aliases,
        compiler_params=pltpu.CompilerParams(
            dimension_semantics=("arbitrary",), vmem_limit_bytes=VMEM_LIMIT),
        name="peer",
    )(*args)


SC_CORES = 2
SC_SUBCORES = 16
SC_LANES = 16
SC_WORKERS = SC_CORES * SC_SUBCORES
SC_ROWS = 32
SC_RB = 4


SC_VPR = LANES // SC_LANES
SC_VROWS = PEER_SLOTS // SC_VPR


def _sc_vec(j):
    return (j // SC_VPR, pl.ds((j % SC_VPR) * SC_LANES, SC_LANES))


def _sc_mesh():
    return plsc.VectorSubcoreMesh(core_axis_name="c", subcore_axis_name="s")


def _sc_worker():
    return lax.axis_index("s") * SC_CORES + lax.axis_index("c")


def _sc_down(idx_flat, xn3, down3, tok0, n_tok):
    planes = xn3.shape[1]
    tw = n_tok // SC_WORKERS
    nchunk = PEER_SLOTS // SC_ROWS
    per_plane = LANES // SC_LANES

    @functools.partial(
        pl.kernel, mesh=_sc_mesh(),
        out_type=jax.ShapeDtypeStruct((n_tok, SC_VROWS, LANES), F32),
        scratch_types=[
            pltpu.VMEM((PEER_SLOTS,), jnp.int32),
            pltpu.VMEM((planes, LANES), F32),
            pltpu.VMEM((2, SC_ROWS, planes, LANES), jnp.int32),
            pltpu.VMEM((SC_VROWS, LANES), F32),
            pltpu.VMEM((planes, SC_VROWS, LANES), F32),
            pltpu.SemaphoreType.DMA((2,)),
        ],
        compiler_params=pltpu.CompilerParams(needs_layout_passes=False),
        name="sc_down",
    )
    def run(idx_hbm, xn_hbm, down_hbm, out_hbm, idx_v, x_v, buf, act_v, part_v, sem):
        base = tok0 + _sc_worker() * tw

        def gather(kc):
            return pltpu.make_async_copy(
                down_hbm.at[idx_v.at[pl.ds(kc * SC_ROWS, SC_ROWS)]], buf.at[kc % 2], sem.at[kc % 2])

        def token(ti, carry):
            tok = base + ti
            pltpu.sync_copy(idx_hbm.at[pl.ds(tok * PEER_SLOTS, PEER_SLOTS)], idx_v)
            pltpu.sync_copy(xn_hbm.at[tok], x_v)
            gather(0).start()
            for kc in range(nchunk):
                if kc + 1 < nchunk:
                    gather(kc + 1).start()
                gather(kc).wait()

                @plsc.parallel_loop(0, planes)
                def _(p, kc=kc):
                    sls = [pl.ds(q * SC_LANES, SC_LANES) for q in range(per_plane)]
                    xs = [x_v[p, sl] for sl in sls]
                    for j in range(SC_ROWS):
                        acc = _packed_down(plsc.bitcast, buf[kc % 2, j, p, sls[0]]) * xs[0]
                        for q in range(1, per_plane):
                            acc = acc + _packed_down(plsc.bitcast, buf[kc % 2, j, p, sls[q]]) * xs[q]
                        part_v[(p,) + _sc_vec(kc * SC_ROWS + j)] = acc

            @plsc.parallel_loop(0, SC_VROWS)
            def _(r):
                for u in range(SC_VPR):
                    sl = pl.ds(u * SC_LANES, SC_LANES)
                    acc = part_v[0, r, sl]
                    for p in range(1, planes):
                        acc = acc + part_v[p, r, sl]
                    act_v[r, sl] = acc
            pltpu.sync_copy(act_v, out_hbm.at[tok - tok0])
            return carry

        lax.fori_loop(0, tw, token, 0)

    return run(idx_flat, xn3, down3)


def _sc_up(idx_flat, w16, x13, up3, tok0, n_tok):
    planes = x13.shape[1]
    tw = n_tok // SC_WORKERS
    nchunk = PEER_SLOTS // SC_ROWS
    per_plane = LANES // SC_LANES

    @functools.partial(
        pl.kernel, mesh=_sc_mesh(),
        out_type=jax.ShapeDtypeStruct((n_tok, planes, LANES), F32),
        scratch_types=[
            pltpu.VMEM((PEER_SLOTS,), jnp.int32),
            pltpu.VMEM((SC_VROWS, LANES), F32),
            pltpu.VMEM((planes, LANES), F32),
            pltpu.VMEM((2, SC_ROWS, planes, LANES), jnp.int32),
            pltpu.SemaphoreType.DMA((2,)),
        ],
        compiler_params=pltpu.CompilerParams(needs_layout_passes=False),
        name="sc_up",
    )
    def run(idx_hbm, w_hbm, x1_hbm, up_hbm, out_hbm, idx_v, w_v, out_v, buf, sem):
        base = tok0 + _sc_worker() * tw

        def gather(kc):
            return pltpu.make_async_copy(
                up_hbm.at[idx_v.at[pl.ds(kc * SC_ROWS, SC_ROWS)]], buf.at[kc % 2], sem.at[kc % 2])

        def token(ti, carry):
            tok = base + ti
            pltpu.sync_copy(idx_hbm.at[pl.ds(tok * PEER_SLOTS, PEER_SLOTS)], idx_v)
            pltpu.sync_copy(w_hbm.at[tok - tok0], w_v)
            pltpu.sync_copy(x1_hbm.at[tok], out_v)
            gather(0).start()
            for kc in range(nchunk):
                if kc + 1 < nchunk:
                    gather(kc + 1).start()
                gather(kc).wait()

                @plsc.parallel_loop(0, planes)
                def _(p, kc=kc):
                    sls = [pl.ds(q * SC_LANES, SC_LANES) for q in range(per_plane)]
                    accs = [out_v[p, sl] for sl in sls]
                    for j in range(SC_ROWS):
                        wj = w_v[_sc_vec(kc * SC_ROWS + j)]
                        accs = [a + wj * _packed_up(plsc.bitcast, buf[kc % 2, j, p, sl]) for a, sl in zip(accs, sls)]
                    for a, sl in zip(accs, sls):
                        out_v[p, sl] = a
            pltpu.sync_copy(out_v, out_hbm.at[tok - tok0])
            return carry

        lax.fori_loop(0, tw, token, 0)

    return run(idx_flat, w16, x13, up3)


ACT_TM = 256


def _split3_bf16(a):
    hi = a.astype(BF16)
    r = a - hi.astype(F32)
    mid = r.astype(BF16)
    lo = (r - mid.astype(F32)).astype(BF16)
    return hi, mid, lo


def _peer_act_body(after_ref, part_ref, gate_ref, w_ref):
    del after_ref
    li = lax.broadcasted_iota(jnp.int32, (LANES, LANES), 0)
    lj = lax.broadcasted_iota(jnp.int32, (LANES, LANES), 1)
    group_sum = (li // SC_LANES == lj // SC_LANES).astype(BF16)
    gate = _split3_bf16(gate_ref[...])
    for r in range(SC_VROWS):
        act = sum(_dot(p, group_sum) for p in _split3_bf16(part_ref[:, r, :]))
        spread = (li == r * SC_VPR + lj // SC_LANES).astype(BF16)
        g = sum(_dot(p, spread) for p in gate)
        w_ref[:, r, :] = g * (0.5 * act * (1.0 + _erf(act * (2.0 ** -0.5))))


def _peer_act(after, part3, gate_tok, blk0):
    n_tok = part3.shape[0]
    tm = ACT_TM
    return pl.pallas_call(
        _peer_act_body,
        grid=(n_tok // tm,),
        in_specs=[
            pl.BlockSpec(memory_space=pl.ANY),
            pl.BlockSpec((tm, SC_VROWS, LANES), lambda i: (i, 0, 0)),
            pl.BlockSpec((tm, PEER_SLOTS), lambda i: (i + blk0, 0)),
        ],
        out_specs=pl.BlockSpec((tm, SC_VROWS, LANES), lambda i: (i, 0, 0)),
        out_shape=jax.ShapeDtypeStruct(part3.shape, F32),
        compiler_params=pltpu.CompilerParams(
            dimension_semantics=("parallel",), vmem_limit_bytes=VMEM_LIMIT),
        name="peer_act",
    )(after, part3, gate_tok)


def _table_pack_body(down_ref, up_ref, tab_ref):
    te, d = down_ref.shape
    planes = d // LANES
    for p in range(planes):
        cols = slice(p * LANES, (p + 1) * LANES)
        hi = lax.bitcast_convert_type(down_ref[:, cols].astype(BF16).astype(F32), jnp.int32)
        lo = lax.bitcast_convert_type(up_ref[:, cols].astype(BF16).astype(F32), jnp.int32)
        tab_ref[pl.ds(p, te, stride=planes), :] = hi | lax.shift_right_logical(lo, 16)


def _table_pack(down, up, te=512):
    n_exp, d = down.shape
    rows = d // LANES
    return pl.pallas_call(
        _table_pack_body,
        grid=(n_exp // te,),
        in_specs=[pl.BlockSpec((te, d), lambda i: (i, 0)), pl.BlockSpec((te, d), lambda i: (i, 0))],
        out_specs=pl.BlockSpec((te * rows, LANES), lambda i: (i, 0)),
        out_shape=jax.ShapeDtypeStruct((n_exp * rows, LANES), jnp.int32),
        compiler_params=pltpu.CompilerParams(
            dimension_semantics=("parallel",), vmem_limit_bytes=VMEM_LIMIT),
        name="table_pack",
    )(down, up)


def _pad_lanes(a):
    return jnp.pad(a, ((0, 0), (0, LANES - a.shape[1])))


def kernel(x, norm_mix_w, w_in, a_q_norm_w, a_k_norm_w, a_rel_bias, b_conv_w, b_a_log, b_dt_bias, b_norm_w, b_merge, w_proj_a, w_proj_b, w_out, norm_ffn_w, peer_w_query, peer_keys_1, peer_keys_2, peer_down, peer_up):
    b, s, d = x.shape
    t = b * s
    depth = w_in.shape[0]
    for l in range(depth):
        x2 = x.reshape(t, d)

        wi = w_in[l]
        o_qkvb = 3 * A_WIDTH
        o_a = o_qkvb + 3 * B_WIDTH
        o_beta = o_a + B_HEADS
        o_gate = o_beta + B_HEADS
        o_merge = o_gate + B_WIDTH
        w_main = jnp.concatenate(
            [wi[:, o_qkvb:o_a], wi[:, o_gate:o_merge], wi[:, o_merge:], wi[:, :o_qkvb]], axis=1).astype(BF16)
        w_small = _pad_lanes(wi[:, o_a:o_gate]).astype(BF16)
        z_attn = Z_MERGE + 2 * d

        z, zs = _in_proj(x2, norm_mix_w[l][None, :], w_main, w_small, tm=1024, tn=1920)
        z3 = z.reshape(b, s, z.shape[1])

        seg = jnp.kron(jnp.eye(A_HEADS, dtype=F32), jnp.full((A_HEAD_DIM, A_HEAD_DIM), 1.0 / A_HEAD_DIM, F32))
        o_att = _band_attention(
            z3, _attn_bias_table(a_rel_bias[l]),
            jnp.tile(a_q_norm_w[l], A_HEADS)[None, :], jnp.tile(a_k_norm_w[l], A_HEADS)[None, :],
            seg.astype(BF16), z_attn // A_WIDTH)

        head_params = jnp.zeros((SUBLANES, LANES), F32)
        head_params = head_params.at[0, :B_HEADS].set(b_a_log[l]).at[1, :B_HEADS].set(b_dt_bias[l])
        head_params = head_params.at[2, :B_HEADS].set(1.0)
        o_gdn = _gated_deltanet(z3, zs.reshape(b, s, LANES), b_conv_w[l], head_params, b_norm_w[l][None, :])

        x1, x1p, xnp, idx, gate_t, gate_tok = _merge_route(
            o_att.reshape(t, A_WIDTH), o_gdn.reshape(t, B_WIDTH), z, x2, b_merge[l][None, :],
            w_proj_a[l].astype(BF16), w_proj_b[l].astype(BF16), w_out[l].astype(BF16), norm_ffn_w[l][None, :],
            peer_w_query[l].astype(BF16), peer_keys_1[l].astype(BF16), peer_keys_2[l].astype(BF16))

        planes = d // LANES
        n_exp = peer_down.shape[1]
        table = _table_pack(peer_down[l], peer_up[l]).reshape(n_exp, planes, LANES)
        idx3 = idx.reshape(t // PEER_TP, PEER_TP, PEER_SLOTS)

        n_sc = t * PEER_SC_SHARE_NUM // PEER_SC_SHARE_DEN
        t0 = t - n_sc
        nblk = t0 // PEER_TP
        nblk_a = nblk // 2
        idx_flat = idx.reshape(-1)
        part = _sc_down(idx_flat, xnp.reshape(t, planes, LANES), table, t0, n_sc)
        out = _peer(idx3, xnp, gate_t, x1, table, 0, nblk_a)
        w16 = _peer_act(out, part, gate_tok, t0 // ACT_TM)
        out = _peer(idx3, xnp, gate_t, x1, table, nblk_a, nblk - nblk_a, prev=out, after=w16)
        sc_out = _sc_up(idx_flat, w16, x1p.reshape(t, planes, LANES), table, t0, n_sc)
        out = lax.dynamic_update_slice(out, sc_out.reshape(n_sc, d), (t0, 0))
        x = out.reshape(b, s, d)
    return x
```

```python
import functools

import jax
import jax.numpy as jnp
from jax import lax
from jax.experimental import pallas as pl
from jax.experimental.pallas import tpu as pltpu
from jax.experimental.pallas import tpu_sc as plsc

F32 = jnp.float32
BF16 = jnp.bfloat16

EPS = 1e-6
NEG = -1e30

CHUNK = 64
A_HEADS = 8
A_HEAD_DIM = 64
A_WIDTH = A_HEADS * A_HEAD_DIM
A_LEFT_CHUNKS = 8
REL_CLIP = 128
B_HEADS = 8
B_HEAD_DIM = 128
B_WIDTH = B_HEADS * B_HEAD_DIM
CONV_WIDTH = 4
PEER_HEADS = 8
PEER_HALF = 128
PEER_NKEYS = 128
PEER_TOPK = 16
PEER_SLOTS = PEER_HEADS * PEER_TOPK

LANES = 128
SUBLANES = 8
VMEM_LIMIT = 56 * 1024 * 1024

Z_QKV_B = 0
Z_GATE_B = 3 * B_WIDTH
Z_MERGE = Z_GATE_B + B_WIDTH
Z_SMALL_A = 0
Z_SMALL_BETA = B_HEADS


def _dot(a, b):
    return jnp.dot(a, b, preferred_element_type=F32)


def _dot_nt(a, b):
    return lax.dot_general(a, b, (((1,), (1,)), ((), ())), preferred_element_type=F32)


def _dot_tn(a, b):
    return lax.dot_general(a, b, (((0,), (0,)), ((), ())), preferred_element_type=F32)


def _split_bf16(a):
    hi = a.astype(BF16)
    lo = (a - hi.astype(F32)).astype(BF16)
    return hi, lo


def _sigmoid(x):
    return 1.0 / (1.0 + jnp.exp(-x))


def _inproj_body(x_ref, nw_ref, w_ref, ws_ref, z_ref, zs_ref, h_scr):
    @pl.when(pl.program_id(1) == 0)
    def _():
        x = x_ref[...]
        ms = jnp.mean(x * x, axis=-1, keepdims=True)
        hb = (x * lax.rsqrt(ms + EPS) * nw_ref[...]).astype(BF16)
        h_scr[...] = hb
        zs_ref[...] = _dot(hb, ws_ref[...])

    z_ref[...] = _dot(h_scr[...], w_ref[...]).astype(z_ref.dtype)


def _in_proj(x2, norm_w, w_main, w_small, tm, tn):
    t, d = x2.shape
    n = w_main.shape[1]
    return pl.pallas_call(
        _inproj_body,
        grid=(t // tm, n // tn),
        in_specs=[
            pl.BlockSpec((tm, d), lambda i, j: (i, 0)),
            pl.BlockSpec((1, d), lambda i, j: (0, 0)),
            pl.BlockSpec((d, tn), lambda i, j: (0, j)),
            pl.BlockSpec((d, LANES), lambda i, j: (0, 0)),
        ],
        out_specs=[
            pl.BlockSpec((tm, tn), lambda i, j: (i, j)),
            pl.BlockSpec((tm, LANES), lambda i, j: (i, 0)),
        ],
        out_shape=[
            jax.ShapeDtypeStruct((t, n), BF16),
            jax.ShapeDtypeStruct((t, LANES), F32),
        ],
        scratch_shapes=[pltpu.VMEM((tm, d), BF16)],
        compiler_params=pltpu.CompilerParams(
            dimension_semantics=("parallel", "arbitrary"), vmem_limit_bytes=VMEM_LIMIT),
        name="in_proj",
    )(x2, norm_w, w_main, w_small)


ATT_TQ = 256
ATT_WIN = ATT_TQ + A_LEFT_CHUNKS * CHUNK
ATT_NKB = ATT_WIN // ATT_TQ


def _attn_body(q_ref, k0_ref, k1_ref, k2_ref, v0_ref, v1_ref, v2_ref, bias_ref, qw_ref, kw_ref, seg_ref,
               o_ref):
    t = pl.program_id(1)
    seg = seg_ref[...]

    def head_rms(a, w):
        hi, lo = _split_bf16(a * a)
        ms = _dot(hi, seg) + _dot(lo, seg)
        return a * lax.rsqrt(ms + EPS) * w

    q = head_rms(q_ref[0].astype(F32), qw_ref[...]) * (A_HEAD_DIM ** -0.5)
    k = jnp.concatenate([k0_ref[0], k1_ref[0], k2_ref[0]], axis=0).astype(F32)
    k = head_rms(k, kw_ref[...]).astype(BF16)
    v = jnp.concatenate([v0_ref[0], v1_ref[0], v2_ref[0]], axis=0)

    kpos = t * ATT_TQ - A_LEFT_CHUNKS * CHUNK + lax.broadcasted_iota(jnp.int32, (1, ATT_WIN), 1)
    valid = kpos >= 0
    lane = lax.broadcasted_iota(jnp.int32, (1, LANES), 1)
    first = lane < A_HEAD_DIM

    for pair in range(A_HEADS // 2):
        cols = slice(pair * LANES, (pair + 1) * LANES)
        qp = q[:, cols]
        kp = k[:, cols]
        vp = v[:, cols]
        outs = []
        for half in range(2):
            h = 2 * pair + half
            qm = jnp.where(first if half == 0 else jnp.logical_not(first), qp, 0.0).astype(BF16)
            s = _dot_nt(qm, kp) + bias_ref[h]
            s = jnp.where(valid, s, NEG)
            mx = jnp.max(s, axis=-1, keepdims=True)
            p = jnp.exp(s - mx)
            den = jnp.sum(p, axis=-1, keepdims=True)
            outs.append(_dot(p.astype(BF16), vp) / den)
        o_ref[0, :, cols] = jnp.where(first, outs[0], outs[1]).astype(o_ref.dtype)


def _band_attention(z3, bias_tab, qw, kw, seg, col_q):
    b, s, _ = z3.shape
    nt = s // ATT_TQ

    def kv_spec(back, col):
        return pl.BlockSpec((1, ATT_TQ, A_WIDTH), lambda bi, ti: (bi, jnp.maximum(ti - back, 0), col))

    return pl.pallas_call(
        _attn_body,
        grid=(b, nt),
        in_specs=[
            pl.BlockSpec((1, ATT_TQ, A_WIDTH), lambda bi, ti: (bi, ti, col_q)),
            kv_spec(2, col_q + 1), kv_spec(1, col_q + 1), kv_spec(0, col_q + 1),
            kv_spec(2, col_q + 2), kv_spec(1, col_q + 2), kv_spec(0, col_q + 2),
            pl.BlockSpec((A_HEADS, ATT_TQ, ATT_WIN), lambda bi, ti: (0, 0, 0)),
            pl.BlockSpec((1, A_WIDTH), lambda bi, ti: (0, 0)),
            pl.BlockSpec((1, A_WIDTH), lambda bi, ti: (0, 0)),
            pl.BlockSpec((A_WIDTH, A_WIDTH), lambda bi, ti: (0, 0)),
        ],
        out_specs=pl.BlockSpec((1, ATT_TQ, A_WIDTH), lambda bi, ti: (bi, ti, 0)),
        out_shape=jax.ShapeDtypeStruct((b, s, A_WIDTH), BF16),
        compiler_params=pltpu.CompilerParams(
            dimension_semantics=("parallel", "parallel"), vmem_limit_bytes=VMEM_LIMIT),
        name="band_attn",
    )(z3, z3, z3, z3, z3, z3, z3, bias_tab, qw, kw, seg)


def _attn_bias_table(rel_bias):
    rb = rel_bias.astype(F32)
    heads = rb.shape[0]
    lo = A_LEFT_CHUNKS * CHUNK - (ATT_WIN - 1)
    hi = A_LEFT_CHUNKS * CHUNK + ATT_TQ - 1
    strip = jnp.concatenate([
        jnp.broadcast_to(rb[:, :1], (heads, -REL_CLIP - lo)), rb,
        jnp.broadcast_to(rb[:, -1:], (heads, hi - REL_CLIP))], axis=1)
    rev = strip[:, ::-1]
    n = rev.shape[1]
    flat = jnp.broadcast_to(rev[:, None, :], (heads, ATT_TQ, n)).reshape(heads, ATT_TQ * n)
    bias = flat[:, ATT_TQ - 1:ATT_TQ - 1 + ATT_TQ * (n - 1)].reshape(heads, ATT_TQ, n - 1)[:, :, :ATT_WIN]
    qc = jnp.arange(ATT_TQ)[:, None] // CHUNK
    kc = jnp.arange(ATT_WIN)[None, :] // CHUNK
    in_band = (kc >= qc) & (kc <= qc + A_LEFT_CHUNKS)
    return jnp.where(in_band[None], bias, NEG)


GDN_TAIL = SUBLANES


def _gdn_body(qkv_ref, zs_ref, gate_ref, convw_ref, hp_ref, nw_ref, o_ref, xbuf, s_scr):
    c = CHUNK
    hd = B_HEAD_DIM

    @pl.when(pl.program_id(1) == 0)
    def _():
        xbuf[0:GDN_TAIL, :] = jnp.zeros((GDN_TAIL, 3 * B_WIDTH), F32)
        s_scr[...] = jnp.zeros_like(s_scr)

    xbuf[GDN_TAIL:GDN_TAIL + c, :] = qkv_ref[0].astype(F32)
    y = None
    for j in range(CONV_WIDTH):
        tap = convw_ref[j:j + 1, :] * xbuf[pl.ds(GDN_TAIL - (CONV_WIDTH - 1) + j, c), :]
        y = tap if y is None else y + tap
    xbuf[0:GDN_TAIL, :] = xbuf[c:c + GDN_TAIL, :]
    y = y * _sigmoid(y)

    zs = zs_ref[0]
    a_neg = -(jnp.exp(hp_ref[0:1, :]) * hp_ref[2:3, :])
    xs = zs + hp_ref[1:2, :]
    g_all = a_neg * (jnp.maximum(xs, 0.0) + jnp.log(1.0 + jnp.exp(-jnp.abs(xs))))
    beta_all = _sigmoid(zs)

    row = lax.broadcasted_iota(jnp.int32, (c, c), 0)
    col = lax.broadcasted_iota(jnp.int32, (c, c), 1)
    tril = row >= col
    tril_strict = row > col
    eye = (row == col).astype(F32)
    lower = tril.astype(BF16)
    upper = (row <= col).astype(BF16)

    gh, gl = _split_bf16(g_all)
    gc_all = _dot(lower, gh) + _dot(lower, gl)
    gth, gtl = _split_bf16(g_all.T)
    gc_t = _dot(gth, upper) + _dot(gtl, upper)

    heads = range(B_HEADS)
    q, k, v, beta, gc, gc_row, g_last = [], [], [], [], [], [], []
    for h in heads:
        qh = y[:, h * hd:(h + 1) * hd]
        kh = y[:, B_WIDTH + h * hd:B_WIDTH + (h + 1) * hd]
        q.append(qh * lax.rsqrt(jnp.sum(qh * qh, axis=-1, keepdims=True) + EPS) * (hd ** -0.5))
        k.append(kh * lax.rsqrt(jnp.sum(kh * kh, axis=-1, keepdims=True) + EPS))
        v.append(y[:, 2 * B_WIDTH + h * hd:2 * B_WIDTH + (h + 1) * hd])
        beta.append(beta_all[:, Z_SMALL_BETA + h:Z_SMALL_BETA + h + 1])
        gc.append(gc_all[:, Z_SMALL_A + h:Z_SMALL_A + h + 1])
        gc_row.append(gc_t[Z_SMALL_A + h:Z_SMALL_A + h + 1, :])
        g_last.append(gc_row[h][:, c - 1:c])
    decay = [jnp.exp(jnp.where(tril, gc[h] - gc_row[h], NEG)) for h in heads]
    e_gc = [jnp.exp(gc[h]) for h in heads]
    kb = [k[h] * beta[h] for h in heads]
    kf = [k[h].astype(BF16) for h in heads]
    a_pow = [jnp.where(tril_strict, _dot_nt(kb[h].astype(BF16), kf[h]) * decay[h], 0.0) for h in heads]
    attn = [(_dot_nt(q[h].astype(BF16), kf[h]) * decay[h]).astype(BF16) for h in heads]

    t_mat = [eye - a_pow[h] for h in heads]
    for _ in range(5):
        ab = [a_pow[h].astype(BF16) for h in heads]
        a_pow = [_dot(ab[h], ab[h]) for h in heads]
        t_mat = [t_mat[h] + _dot(t_mat[h].astype(BF16), a_pow[h].astype(BF16)) for h in heads]
    tb = [t_mat[h].astype(BF16) for h in heads]
    u = [_dot(tb[h], (v[h] * beta[h]).astype(BF16)) for h in heads]
    w = [_dot(tb[h], (kb[h] * e_gc[h]).astype(BF16)).astype(BF16) for h in heads]

    state = [s_scr[h] for h in heads]
    sb = [state[h].astype(BF16) for h in heads]
    vnb = [(u[h] - _dot(w[h], sb[h])).astype(BF16) for h in heads]
    o = [_dot((q[h] * e_gc[h]).astype(BF16), sb[h]) + _dot(attn[h], vnb[h]) for h in heads]
    k_dec = [(k[h] * jnp.exp(g_last[h] - gc[h])).astype(BF16) for h in heads]
    for h in heads:
        s_scr[h] = state[h] * jnp.exp(g_last[h]) + _dot_tn(k_dec[h], vnb[h])
    for h in heads:
        cols = slice(h * hd, (h + 1) * hd)
        gate = gate_ref[0, :, cols].astype(F32)
        on = o[h] * lax.rsqrt(jnp.mean(o[h] * o[h], axis=-1, keepdims=True) + EPS) * nw_ref[...]
        o_ref[0, :, cols] = (on * (gate * _sigmoid(gate))).astype(o_ref.dtype)


def _gated_deltanet(z3, zs3, conv_w, head_params, norm_w):
    b, s, _ = z3.shape
    n = s // CHUNK
    return pl.pallas_call(
        _gdn_body,
        grid=(b, n),
        in_specs=[
            pl.BlockSpec((1, CHUNK, 3 * B_WIDTH), lambda bi, ni: (bi, ni, Z_QKV_B // (3 * B_WIDTH))),
            pl.BlockSpec((1, CHUNK, LANES), lambda bi, ni: (bi, ni, 0)),
            pl.BlockSpec((1, CHUNK, B_WIDTH), lambda bi, ni: (bi, ni, Z_GATE_B // B_WIDTH)),
            pl.BlockSpec((CONV_WIDTH, 3 * B_WIDTH), lambda bi, ni: (0, 0)),
            pl.BlockSpec((SUBLANES, LANES), lambda bi, ni: (0, 0)),
            pl.BlockSpec((1, B_HEAD_DIM), lambda bi, ni: (0, 0)),
        ],
        out_specs=pl.BlockSpec((1, CHUNK, B_WIDTH), lambda bi, ni: (bi, ni, 0)),
        out_shape=jax.ShapeDtypeStruct((b, s, B_WIDTH), BF16),
        scratch_shapes=[
            pltpu.VMEM((GDN_TAIL + CHUNK, 3 * B_WIDTH), F32),
            pltpu.VMEM((B_HEADS, B_HEAD_DIM, B_HEAD_DIM), F32),
        ],
        compiler_params=pltpu.CompilerParams(
            dimension_semantics=("parallel", "arbitrary"), vmem_limit_bytes=VMEM_LIMIT),
        name="gdn",
    )(z3, zs3, z3, conv_w, head_params, norm_w)


MERGE_TM = 256


def _top_k_rows(s, k, payload=None):
    n = s.shape[0]
    rows = lax.broadcasted_iota(jnp.int32, s.shape, 0)
    vals, picks = [], []
    for _ in range(k):
        m = jnp.max(s, axis=0, keepdims=True)
        i = jnp.min(jnp.where(s == m, rows, n), axis=0, keepdims=True)
        hit = rows == i
        vals.append(m)
        picks.append(i if payload is None else jnp.max(jnp.where(hit, payload, -1), axis=0, keepdims=True))
        s = jnp.where(hit, -jnp.inf, s)
    return jnp.concatenate(vals, axis=0), jnp.concatenate(picks, axis=0)


def _merge_body(oa_ref, ob_ref, mr_ref, x_ref, bm_ref, pa_ref, pb_ref, wo_ref, nw_ref, wq_ref, k1_ref, k2_ref,
                x1_ref, x1p_ref, xnp_ref, idx_ref, gate_ref, gate_tok_ref):
    tm, d = x_ref.shape
    planes = d // LANES

    def store_planes(ref, a):
        for p in range(planes):
            ref[pl.ds(p, tm, stride=planes), :] = a[:, p * LANES:(p + 1) * LANES]

    mr = mr_ref[...].astype(F32) + bm_ref[...]
    mixed = (_sigmoid(mr[:, :d]) * _dot(oa_ref[...], pa_ref[...])
             + _sigmoid(mr[:, d:]) * _dot(ob_ref[...], pb_ref[...]))
    x1 = x_ref[...] + _dot(mixed.astype(BF16), wo_ref[...])
    x1_ref[...] = x1
    store_planes(x1p_ref, x1)
    xn = x1 * lax.rsqrt(jnp.mean(x1 * x1, axis=-1, keepdims=True) + EPS) * nw_ref[...]
    store_planes(xnp_ref, xn)
    q = _dot(xn.astype(BF16), wq_ref[...]).astype(BF16)

    idx_rows, gate_rows = [], []
    for h in range(PEER_HEADS):
        q1 = q[:, (2 * h) * PEER_HALF:(2 * h + 1) * PEER_HALF]
        q2 = q[:, (2 * h + 1) * PEER_HALF:(2 * h + 2) * PEER_HALF]
        s1 = _dot_nt(k1_ref[h], q1)
        s2 = _dot_nt(k2_ref[h], q2)
        v1, i1 = _top_k_rows(s1, PEER_TOPK)
        v2, i2 = _top_k_rows(s2, PEER_TOPK)
        nb = [PEER_TOPK // (a + 1) for a in range(PEER_TOPK)]
        pad = -sum(nb) % SUBLANES
        tm = v1.shape[1]
        cand = jnp.concatenate([v1[a:a + 1] + v2[:nb[a]] for a in range(PEER_TOPK)]
                               + [jnp.full((pad, tm), -jnp.inf, F32)], axis=0)
        cand_idx = jnp.concatenate([i1[a:a + 1] * PEER_NKEYS + i2[:nb[a]] for a in range(PEER_TOPK)]
                                   + [jnp.zeros((pad, tm), jnp.int32)], axis=0)
        top_s, expert = _top_k_rows(cand, PEER_TOPK, payload=cand_idx)
        e = jnp.exp(top_s - top_s[0:1])
        gate_rows.append(e / jnp.sum(e, axis=0, keepdims=True))
        idx_rows.append(expert)
    idx_ref[...] = jnp.concatenate(idx_rows, axis=0).T
    gates = jnp.concatenate(gate_rows, axis=0)
    gate_ref[...] = gates
    gate_tok_ref[...] = gates.T


def _merge_route(o_a, o_b, z2, x2, b_merge, pa, pb, wo, nw, wq, k1, k2):
    t, d = x2.shape
    tm = MERGE_TM
    full = lambda a: pl.BlockSpec(a.shape, lambda i: (0,) * a.ndim)
    return pl.pallas_call(
        _merge_body,
        grid=(t // tm,),
        in_specs=[
            pl.BlockSpec((tm, A_WIDTH), lambda i: (i, 0)),
            pl.BlockSpec((tm, B_WIDTH), lambda i: (i, 0)),
            pl.BlockSpec((tm, 2 * d), lambda i: (i, Z_MERGE // (2 * d))),
            pl.BlockSpec((tm, d), lambda i: (i, 0)),
            full(b_merge), full(pa), full(pb), full(wo), full(nw), full(wq), full(k1), full(k2),
        ],
        out_specs=[
            pl.BlockSpec((tm, d), lambda i: (i, 0)),
            pl.BlockSpec((tm * d // LANES, LANES), lambda i: (i, 0)),
            pl.BlockSpec((tm * d // LANES, LANES), lambda i: (i, 0)),
            pl.BlockSpec((tm, PEER_SLOTS), lambda i: (i, 0)),
            pl.BlockSpec((PEER_SLOTS, tm), lambda i: (0, i)),
            pl.BlockSpec((tm, PEER_SLOTS), lambda i: (i, 0)),
        ],
        out_shape=[
            jax.ShapeDtypeStruct((t, d), F32),
            jax.ShapeDtypeStruct((t * d // LANES, LANES), F32),
            jax.ShapeDtypeStruct((t * d // LANES, LANES), F32),
            jax.ShapeDtypeStruct((t, PEER_SLOTS), jnp.int32),
            jax.ShapeDtypeStruct((PEER_SLOTS, t), F32),
            jax.ShapeDtypeStruct((t, PEER_SLOTS), F32),
        ],
        compiler_params=pltpu.CompilerParams(
            dimension_semantics=("parallel",), vmem_limit_bytes=VMEM_LIMIT),
        name="merge_route",
    )(o_a, o_b, z2, x2, b_merge, pa, pb, wo, nw, wq, k1, k2)


PEER_TP = 128
PEER_NBUF = 4
PEER_LOOK = PEER_NBUF - 1
PEER_SC_SHARE_NUM, PEER_SC_SHARE_DEN = 14, 32


def _erf(x):
    return lax.erf(x)


PACK_HI = -65536


def _packed_down(bitcast, word):
    return bitcast(word & PACK_HI, F32)


def _packed_up(bitcast, word):
    return bitcast(word << 16, F32)


def _peer_body(idx_ref, idx_next_ref, xn_ref, gate_ref, x1_ref, tab_ref, o_ref, *scratch):
    bufs, sem = scratch[:PEER_NBUF], scratch[PEER_NBUF]
    planes = x1_ref.shape[1] // LANES
    step = pl.program_id(0)

    def issue(ref, tok, slot):
        for j in range(PEER_SLOTS):
            pltpu.make_async_copy(tab_ref.at[ref[0, tok, j]], bufs[slot].at[:, j, :], sem.at[slot]).start()

    def wait(slot):
        pltpu.make_async_copy(bufs[slot], bufs[slot], sem.at[slot]).wait()

    @pl.when(step == 0)
    def _():
        for s in range(PEER_LOOK):
            issue(idx_ref, s, s)

    lane = lax.broadcasted_iota(jnp.int32, (PEER_SLOTS, PEER_TP), 1)

    def compute(tok, slot):
        buf = bufs[slot]
        acc = None
        for s in range(planes):
            term = _packed_down(lax.bitcast_convert_type, buf[s]) * xn_ref[pl.ds(tok * planes + s, 1), :]
            acc = term if acc is None else acc + term
        act = jnp.sum(acc, axis=-1, keepdims=True)
        gate = jnp.sum(jnp.where(lane == tok, gate_ref[...], 0.0), axis=-1, keepdims=True)
        w = gate * (0.5 * act * (1.0 + _erf(act * (2.0 ** -0.5))))
        out = jnp.concatenate(
            [jnp.sum(_packed_up(lax.bitcast_convert_type, buf[s]) * w, axis=0, keepdims=True)
             for s in range(planes)], axis=1)
        o_ref[pl.ds(tok, 1), :] = x1_ref[pl.ds(tok, 1), :] + out

    def group(g, carry):
        for u in range(PEER_NBUF):
            tok = g * PEER_NBUF + u
            issue(idx_ref, tok + PEER_LOOK, (u + PEER_LOOK) % PEER_NBUF)
            wait(u)
            compute(tok, u)
        return carry

    n_groups = PEER_TP // PEER_NBUF
    lax.fori_loop(0, n_groups - 1, group, 0)

    for u in range(PEER_NBUF):
        tok = (n_groups - 1) * PEER_NBUF + u
        nxt = tok + PEER_LOOK
        if nxt < PEER_TP:
            issue(idx_ref, nxt, (u + PEER_LOOK) % PEER_NBUF)
        else:
            issue(idx_next_ref, nxt - PEER_TP, (u + PEER_LOOK) % PEER_NBUF)
        wait(u)
        compute(tok, u)

    @pl.when(step == pl.num_programs(0) - 1)
    def _():
        for s in range(PEER_LOOK):
            wait(s)


def _peer_alias_body(prev_ref, after_ref, *rest):
    del prev_ref, after_ref
    _peer_body(*rest)


def _peer(idx3, xnp, gate_t, x1, table, blk0, nb, prev=None, after=None):
    t, d = x1.shape
    tp = PEER_TP
    planes = d // LANES
    in_specs = [
        pl.BlockSpec((1, tp, PEER_SLOTS), lambda i: (i + blk0, 0, 0), memory_space=pltpu.SMEM),
        pl.BlockSpec((1, tp, PEER_SLOTS), lambda i: (jnp.minimum(i + 1, nb - 1) + blk0, 0, 0),
                     memory_space=pltpu.SMEM),
        pl.BlockSpec((tp * planes, LANES), lambda i: (i + blk0, 0)),
        pl.BlockSpec((PEER_SLOTS, tp), lambda i: (0, i + blk0)),
        pl.BlockSpec((tp, d), lambda i: (i + blk0, 0)),
        pl.BlockSpec(memory_space=pl.ANY),
    ]
    args = (idx3, idx3, xnp, gate_t, x1, table)
    body, aliases = _peer_body, {}
    if prev is not None:
        body, aliases = _peer_alias_body, {0: 0}
        in_specs = [pl.BlockSpec(memory_space=pl.ANY), pl.BlockSpec(memory_space=pl.ANY)] + in_specs
        args = (prev, after) + args
    return pl.pallas_call(
        body,
        grid=(nb,),
        in_specs=in_specs,
        out_specs=pl.BlockSpec((tp, d), lambda i: (i + blk0, 0)),
        out_shape=jax.ShapeDtypeStruct((t, d), F32),
        scratch_shapes=[pltpu.VMEM((planes, PEER_SLOTS, LANES), jnp.int32) for _ in range(PEER_NBUF)]
        + [pltpu.SemaphoreType.DMA((PEER_NBUF,))],
        input_output_aliases=aliases,
        compiler_params=pltpu.CompilerParams(
            dimension_semantics=("arbitrary",), vmem_limit_bytes=VMEM_LIMIT),
        name="peer",
    )(*args)


SC_CORES = 2
SC_SUBCORES = 16
SC_LANES = 16
SC_WORKERS = SC_CORES * SC_SUBCORES
SC_ROWS = 32
SC_RB = 4


SC_VPR = LANES // SC_LANES
SC_VROWS = PEER_SLOTS // SC_VPR


def _sc_vec(j):
    return (j // SC_VPR, pl.ds((j % SC_VPR) * SC_LANES, SC_LANES))


def _sc_mesh():
    return plsc.VectorSubcoreMesh(core_axis_name="c", subcore_axis_name="s")


def _sc_worker():
    return lax.axis_index("s") * SC_CORES + lax.axis_index("c")


def _sc_pipeline(tw, base, nchunk, loads, gather, compute, finish):
    assert tw % 2 == 0 and nchunk % 2 == 0
    last = base + tw - 1

    def token(tok, s):
        nxt = jnp.minimum(tok + 1, last)
        for c in loads(nxt, 1 - s):
            c.start()
        for kc in range(nchunk):
            if kc + 1 < nchunk:
                gather(s, kc + 1).start()
            else:
                for c in loads(nxt, 1 - s):
                    c.wait()
                gather(1 - s, 0).start()
            gather(s, kc).wait()
            compute(s, kc)
        finish(tok, s)

    for c in loads(base, 0):
        c.start()
    for c in loads(base, 0):
        c.wait()
    gather(0, 0).start()

    def pair(i, carry):
        token(base + 2 * i, 0)
        token(base + 2 * i + 1, 1)
        return carry

    lax.fori_loop(0, tw // 2, pair, 0)
    gather(0, 0).wait()


def _sc_down(idx_flat, xn3, table, tok0, n_tok):
    planes = xn3.shape[1]
    tw = n_tok // SC_WORKERS
    nchunk = PEER_SLOTS // SC_ROWS
    per_plane = LANES // SC_LANES

    @functools.partial(
        pl.kernel, mesh=_sc_mesh(),
        out_type=jax.ShapeDtypeStruct((n_tok, SC_VROWS, LANES), F32),
        scratch_types=[
            pltpu.VMEM((2, PEER_SLOTS), jnp.int32),
            pltpu.VMEM((2, planes, LANES), F32),
            pltpu.VMEM((2, SC_ROWS, planes, LANES), jnp.int32),
            pltpu.VMEM((SC_VROWS, LANES), F32),
            pltpu.VMEM((planes, SC_VROWS, LANES), F32),
            pltpu.SemaphoreType.DMA((2,)),
            pltpu.SemaphoreType.DMA((2,)),
        ],
        compiler_params=pltpu.CompilerParams(needs_layout_passes=False),
        name="sc_down",
    )
    def run(idx_hbm, xn_hbm, tab_hbm, out_hbm, idx_v, x_v, buf, act_v, part_v, sem_g, sem_l):
        def loads(tok, s):
            return [pltpu.make_async_copy(idx_hbm.at[pl.ds(tok * PEER_SLOTS, PEER_SLOTS)], idx_v.at[s], sem_l.at[s]),
                    pltpu.make_async_copy(xn_hbm.at[tok], x_v.at[s], sem_l.at[s])]

        def gather(s, kc):
            return pltpu.make_async_copy(
                tab_hbm.at[idx_v.at[s, pl.ds(kc * SC_ROWS, SC_ROWS)]], buf.at[kc % 2], sem_g.at[kc % 2])

        def compute(s, kc):
            @plsc.parallel_loop(0, planes)
            def _(p):
                sls = [pl.ds(q * SC_LANES, SC_LANES) for q in range(per_plane)]
                xs = [x_v[s, p, sl] for sl in sls]
                for j0 in range(0, SC_ROWS, SC_RB):
                    terms = [[_packed_down(plsc.bitcast, buf[kc % 2, j0 + r, p, sls[q]]) * xs[q]
                              for q in range(per_plane)] for r in range(SC_RB)]
                    while len(terms[0]) > 1:
                        terms = [[row[i] + row[i + 1] for i in range(0, len(row), 2)] for row in terms]
                    for r in range(SC_RB):
                        part_v[(p,) + _sc_vec(kc * SC_ROWS + j0 + r)] = terms[r][0]

        def finish(tok, s):
            @plsc.parallel_loop(0, SC_VROWS)
            def _(r):
                for u in range(SC_VPR):
                    sl = pl.ds(u * SC_LANES, SC_LANES)
                    acc = part_v[0, r, sl]
                    for p in range(1, planes):
                        acc = acc + part_v[p, r, sl]
                    act_v[r, sl] = acc
            pltpu.sync_copy(act_v, out_hbm.at[tok - tok0])

        _sc_pipeline(tw, tok0 + _sc_worker() * tw, nchunk, loads, gather, compute, finish)

    return run(idx_flat, xn3, table)


def _sc_up(idx_flat, w16, x13, table, tok0, n_tok):
    planes = x13.shape[1]
    tw = n_tok // SC_WORKERS
    nchunk = PEER_SLOTS // SC_ROWS
    per_plane = LANES // SC_LANES

    @functools.partial(
        pl.kernel, mesh=_sc_mesh(),
        out_type=jax.ShapeDtypeStruct((n_tok, planes, LANES), F32),
        scratch_types=[
            pltpu.VMEM((2, PEER_SLOTS), jnp.int32),
            pltpu.VMEM((2, SC_VROWS, LANES), F32),
            pltpu.VMEM((2, planes, LANES), F32),
            pltpu.VMEM((2, SC_ROWS, planes, LANES), jnp.int32),
            pltpu.SemaphoreType.DMA((2,)),
            pltpu.SemaphoreType.DMA((2,)),
        ],
        compiler_params=pltpu.CompilerParams(needs_layout_passes=False),
        name="sc_up",
    )
    def run(idx_hbm, w_hbm, x1_hbm, tab_hbm, out_hbm, idx_v, w_v, out_v, buf, sem_g, sem_l):
        def loads(tok, s):
            return [pltpu.make_async_copy(idx_hbm.at[pl.ds(tok * PEER_SLOTS, PEER_SLOTS)], idx_v.at[s], sem_l.at[s]),
                    pltpu.make_async_copy(w_hbm.at[tok - tok0], w_v.at[s], sem_l.at[s]),
                    pltpu.make_async_copy(x1_hbm.at[tok], out_v.at[s], sem_l.at[s])]

        def gather(s, kc):
            return pltpu.make_async_copy(
                tab_hbm.at[idx_v.at[s, pl.ds(kc * SC_ROWS, SC_ROWS)]], buf.at[kc % 2], sem_g.at[kc % 2])

        def compute(s, kc):
            @plsc.parallel_loop(0, planes)
            def _(p):
                sls = [pl.ds(q * SC_LANES, SC_LANES) for q in range(per_plane)]
                accs = [out_v[s, p, sl] for sl in sls]
                for j in range(SC_ROWS):
                    wj = w_v[(s,) + _sc_vec(kc * SC_ROWS + j)]
                    accs = [a + wj * _packed_up(plsc.bitcast, buf[kc % 2, j, p, sl]) for a, sl in zip(accs, sls)]
                for a, sl in zip(accs, sls):
                    out_v[s, p, sl] = a

        def finish(tok, s):
            pltpu.sync_copy(out_v.at[s], out_hbm.at[tok - tok0])

        _sc_pipeline(tw, tok0 + _sc_worker() * tw, nchunk, loads, gather, compute, finish)

    return run(idx_flat, w16, x13, table)


ACT_TM = 256


def _split3_bf16(a):
    hi = a.astype(BF16)
    r = a - hi.astype(F32)
    mid = r.astype(BF16)
    lo = (r - mid.astype(F32)).astype(BF16)
    return hi, mid, lo


def _peer_act_body(after_ref, part_ref, gate_ref, w_ref):
    del after_ref
    li = lax.broadcasted_iota(jnp.int32, (LANES, LANES), 0)
    lj = lax.broadcasted_iota(jnp.int32, (LANES, LANES), 1)
    group_sum = (li // SC_LANES == lj // SC_LANES).astype(BF16)
    gate = _split3_bf16(gate_ref[...])
    for r in range(SC_VROWS):
        act = sum(_dot(p, group_sum) for p in _split3_bf16(part_ref[:, r, :]))
        spread = (li == r * SC_VPR + lj // SC_LANES).astype(BF16)
        g = sum(_dot(p, spread) for p in gate)
        w_ref[:, r, :] = g * (0.5 * act * (1.0 + _erf(act * (2.0 ** -0.5))))


def _peer_act(after, part3, gate_tok, blk0):
    n_tok = part3.shape[0]
    tm = ACT_TM
    return pl.pallas_call(
        _peer_act_body,
        grid=(n_tok // tm,),
        in_specs=[
            pl.BlockSpec(memory_space=pl.ANY),
            pl.BlockSpec((tm, SC_VROWS, LANES), lambda i: (i, 0, 0)),
            pl.BlockSpec((tm, PEER_SLOTS), lambda i: (i + blk0, 0)),
        ],
        out_specs=pl.BlockSpec((tm, SC_VROWS, LANES), lambda i: (i, 0, 0)),
        out_shape=jax.ShapeDtypeStruct(part3.shape, F32),
        compiler_params=pltpu.CompilerParams(
            dimension_semantics=("parallel",), vmem_limit_bytes=VMEM_LIMIT),
        name="peer_act",
    )(after, part3, gate_tok)


def _table_pack_body(down_ref, up_ref, tab_ref):
    te, d = down_ref.shape
    planes = d // LANES
    for p in range(planes):
        cols = slice(p * LANES, (p + 1) * LANES)
        hi = lax.bitcast_convert_type(down_ref[:, cols].astype(BF16).astype(F32), jnp.int32)
        lo = lax.bitcast_convert_type(up_ref[:, cols].astype(BF16).astype(F32), jnp.int32)
        tab_ref[pl.ds(p, te, stride=planes), :] = hi | lax.shift_right_logical(lo, 16)


def _table_pack(down, up, te=512):
    n_exp, d = down.shape
    rows = d // LANES
    return pl.pallas_call(
        _table_pack_body,
        grid=(n_exp // te,),
        in_specs=[pl.BlockSpec((te, d), lambda i: (i, 0)), pl.BlockSpec((te, d), lambda i: (i, 0))],
        out_specs=pl.BlockSpec((te * rows, LANES), lambda i: (i, 0)),
        out_shape=jax.ShapeDtypeStruct((n_exp * rows, LANES), jnp.int32),
        compiler_params=pltpu.CompilerParams(
            dimension_semantics=("parallel",), vmem_limit_bytes=VMEM_LIMIT),
        name="table_pack",
    )(down, up)


def _pad_lanes(a):
    return jnp.pad(a, ((0, 0), (0, LANES - a.shape[1])))


def kernel(x, norm_mix_w, w_in, a_q_norm_w, a_k_norm_w, a_rel_bias, b_conv_w, b_a_log, b_dt_bias, b_norm_w, b_merge, w_proj_a, w_proj_b, w_out, norm_ffn_w, peer_w_query, peer_keys_1, peer_keys_2, peer_down, peer_up):
    b, s, d = x.shape
    t = b * s
    depth = w_in.shape[0]
    for l in range(depth):
        x2 = x.reshape(t, d)

        wi = w_in[l]
        o_qkvb = 3 * A_WIDTH
        o_a = o_qkvb + 3 * B_WIDTH
        o_beta = o_a + B_HEADS
        o_gate = o_beta + B_HEADS
        o_merge = o_gate + B_WIDTH
        w_main = jnp.concatenate(
            [wi[:, o_qkvb:o_a], wi[:, o_gate:o_merge], wi[:, o_merge:], wi[:, :o_qkvb]], axis=1).astype(BF16)
        w_small = _pad_lanes(wi[:, o_a:o_gate]).astype(BF16)
        z_attn = Z_MERGE + 2 * d

        z, zs = _in_proj(x2, norm_mix_w[l][None, :], w_main, w_small, tm=1024, tn=1920)
        z3 = z.reshape(b, s, z.shape[1])

        seg = jnp.kron(jnp.eye(A_HEADS, dtype=F32), jnp.full((A_HEAD_DIM, A_HEAD_DIM), 1.0 / A_HEAD_DIM, F32))
        o_att = _band_attention(
            z3, _attn_bias_table(a_rel_bias[l]),
            jnp.tile(a_q_norm_w[l], A_HEADS)[None, :], jnp.tile(a_k_norm_w[l], A_HEADS)[None, :],
            seg.astype(BF16), z_attn // A_WIDTH)

        head_params = jnp.zeros((SUBLANES, LANES), F32)
        head_params = head_params.at[0, :B_HEADS].set(b_a_log[l]).at[1, :B_HEADS].set(b_dt_bias[l])
        head_params = head_params.at[2, :B_HEADS].set(1.0)
        o_gdn = _gated_deltanet(z3, zs.reshape(b, s, LANES), b_conv_w[l], head_params, b_norm_w[l][None, :])

        x1, x1p, xnp, idx, gate_t, gate_tok = _merge_route(
            o_att.reshape(t, A_WIDTH), o_gdn.reshape(t, B_WIDTH), z, x2, b_merge[l][None, :],
            w_proj_a[l].astype(BF16), w_proj_b[l].astype(BF16), w_out[l].astype(BF16), norm_ffn_w[l][None, :],
            peer_w_query[l].astype(BF16), peer_keys_1[l].astype(BF16), peer_keys_2[l].astype(BF16))

        planes = d // LANES
        n_exp = peer_down.shape[1]
        table = _table_pack(peer_down[l], peer_up[l]).reshape(n_exp, planes, LANES)
        idx3 = idx.reshape(t // PEER_TP, PEER_TP, PEER_SLOTS)

        n_sc = t * PEER_SC_SHARE_NUM // PEER_SC_SHARE_DEN
        t0 = t - n_sc
        nblk = t0 // PEER_TP
        nblk_a = nblk // 2
        idx_flat = idx.reshape(-1)
        part = _sc_down(idx_flat, xnp.reshape(t, planes, LANES), table, t0, n_sc)
        out = _peer(idx3, xnp, gate_t, x1, table, 0, nblk_a)
        w16 = _peer_act(out, part, gate_tok, t0 // ACT_TM)
        out = _peer(idx3, xnp, gate_t, x1, table, nblk_a, nblk - nblk_a, prev=out, after=w16)
        sc_out = _sc_up(idx_flat, w16, x1p.reshape(t, planes, LANES), table, t0, n_sc)
        out = lax.dynamic_update_slice(out, sc_out.reshape(n_sc, d), (t0, 0))
        x = out.reshape(b, s, d)
    return x
```

```python
import functools

import jax
import jax.numpy as jnp
from jax import lax
from jax.experimental import pallas as pl
from jax.experimental.pallas import tpu as pltpu
from jax.experimental.pallas import tpu_sc as plsc

F32 = jnp.float32
BF16 = jnp.bfloat16

EPS = 1e-6
NEG = -1e30

CHUNK = 64
A_HEADS = 8
A_HEAD_DIM = 64
A_WIDTH = A_HEADS * A_HEAD_DIM
A_LEFT_CHUNKS = 8
REL_CLIP = 128
B_HEADS = 8
B_HEAD_DIM = 128
B_WIDTH = B_HEADS * B_HEAD_DIM
CONV_WIDTH = 4
PEER_HEADS = 8
PEER_HALF = 128
PEER_NKEYS = 128
PEER_TOPK = 16
PEER_SLOTS = PEER_HEADS * PEER_TOPK

LANES = 128
SUBLANES = 8
VMEM_LIMIT = 56 * 1024 * 1024

Z_QKV_B = 0
Z_GATE_B = 3 * B_WIDTH
Z_MERGE = Z_GATE_B + B_WIDTH
Z_SMALL_A = 0
Z_SMALL_BETA = B_HEADS


def _dot(a, b):
    return jnp.dot(a, b, preferred_element_type=F32)


def _dot_nt(a, b):
    return lax.dot_general(a, b, (((1,), (1,)), ((), ())), preferred_element_type=F32)


def _dot_tn(a, b):
    return lax.dot_general(a, b, (((0,), (0,)), ((), ())), preferred_element_type=F32)


def _split_bf16(a):
    hi = a.astype(BF16)
    lo = (a - hi.astype(F32)).astype(BF16)
    return hi, lo


def _sigmoid(x):
    return 1.0 / (1.0 + jnp.exp(-x))


def _inproj_body(x_ref, nw_ref, w_ref, ws_ref, z_ref, zs_ref, h_scr):
    @pl.when(pl.program_id(1) == 0)
    def _():
        x = x_ref[...]
        ms = jnp.mean(x * x, axis=-1, keepdims=True)
        hb = (x * lax.rsqrt(ms + EPS) * nw_ref[...]).astype(BF16)
        h_scr[...] = hb
        zs_ref[...] = _dot(hb, ws_ref[...])

    z_ref[...] = _dot(h_scr[...], w_ref[...]).astype(z_ref.dtype)


def _in_proj(x2, norm_w, w_main, w_small, tm, tn, row0, t):
    d = x2.shape[1]
    n = w_main.shape[1]
    blk0 = row0 // tm
    return pl.pallas_call(
        _inproj_body,
        grid=(t // tm, n // tn),
        in_specs=[
            pl.BlockSpec((tm, d), lambda i, j: (i + blk0, 0)),
            pl.BlockSpec((1, d), lambda i, j: (0, 0)),
            pl.BlockSpec((d, tn), lambda i, j: (0, j)),
            pl.BlockSpec((d, LANES), lambda i, j: (0, 0)),
        ],
        out_specs=[
            pl.BlockSpec((tm, tn), lambda i, j: (i, j)),
            pl.BlockSpec((tm, LANES), lambda i, j: (i, 0)),
        ],
        out_shape=[
            jax.ShapeDtypeStruct((t, n), BF16),
            jax.ShapeDtypeStruct((t, LANES), F32),
        ],
        scratch_shapes=[pltpu.VMEM((tm, d), BF16)],
        compiler_params=pltpu.CompilerParams(
            dimension_semantics=("parallel", "arbitrary"), vmem_limit_bytes=VMEM_LIMIT),
        name="in_proj",
    )(x2, norm_w, w_main, w_small)


ATT_TQ = 256
ATT_WIN = ATT_TQ + A_LEFT_CHUNKS * CHUNK
ATT_NKB = ATT_WIN // ATT_TQ


def _attn_body(q_ref, k0_ref, k1_ref, k2_ref, v0_ref, v1_ref, v2_ref, bias_ref, qw_ref, kw_ref, seg_ref,
               o_ref):
    t = pl.program_id(1)
    seg = seg_ref[...]

    def head_rms(a, w):
        hi, lo = _split_bf16(a * a)
        ms = _dot(hi, seg) + _dot(lo, seg)
        return a * lax.rsqrt(ms + EPS) * w

    q = head_rms(q_ref[0].astype(F32), qw_ref[...]) * (A_HEAD_DIM ** -0.5)
    k = jnp.concatenate([k0_ref[0], k1_ref[0], k2_ref[0]], axis=0).astype(F32)
    k = head_rms(k, kw_ref[...]).astype(BF16)
    v = jnp.concatenate([v0_ref[0], v1_ref[0], v2_ref[0]], axis=0)

    kpos = t * ATT_TQ - A_LEFT_CHUNKS * CHUNK + lax.broadcasted_iota(jnp.int32, (1, ATT_WIN), 1)
    valid = kpos >= 0
    lane = lax.broadcasted_iota(jnp.int32, (1, LANES), 1)
    first = lane < A_HEAD_DIM

    for pair in range(A_HEADS // 2):
        cols = slice(pair * LANES, (pair + 1) * LANES)
        qp = q[:, cols]
        kp = k[:, cols]
        vp = v[:, cols]
        outs = []
        for half in range(2):
            h = 2 * pair + half
            qm = jnp.where(first if half == 0 else jnp.logical_not(first), qp, 0.0).astype(BF16)
            s = _dot_nt(qm, kp) + bias_ref[h]
            s = jnp.where(valid, s, NEG)
            mx = jnp.max(s, axis=-1, keepdims=True)
            p = jnp.exp(s - mx)
            den = jnp.sum(p, axis=-1, keepdims=True)
            outs.append(_dot(p.astype(BF16), vp) / den)
        o_ref[0, :, cols] = jnp.where(first, outs[0], outs[1]).astype(o_ref.dtype)


def _band_attention(z3, bias_tab, qw, kw, seg, col_q):
    b, s, _ = z3.shape
    nt = s // ATT_TQ

    def kv_spec(back, col):
        return pl.BlockSpec((1, ATT_TQ, A_WIDTH), lambda bi, ti: (bi, jnp.maximum(ti - back, 0), col))

    return pl.pallas_call(
        _attn_body,
        grid=(b, nt),
        in_specs=[
            pl.BlockSpec((1, ATT_TQ, A_WIDTH), lambda bi, ti: (bi, ti, col_q)),
            kv_spec(2, col_q + 1), kv_spec(1, col_q + 1), kv_spec(0, col_q + 1),
            kv_spec(2, col_q + 2), kv_spec(1, col_q + 2), kv_spec(0, col_q + 2),
            pl.BlockSpec((A_HEADS, ATT_TQ, ATT_WIN), lambda bi, ti: (0, 0, 0)),
            pl.BlockSpec((1, A_WIDTH), lambda bi, ti: (0, 0)),
            pl.BlockSpec((1, A_WIDTH), lambda bi, ti: (0, 0)),
            pl.BlockSpec((A_WIDTH, A_WIDTH), lambda bi, ti: (0, 0)),
        ],
        out_specs=pl.BlockSpec((1, ATT_TQ, A_WIDTH), lambda bi, ti: (bi, ti, 0)),
        out_shape=jax.ShapeDtypeStruct((b, s, A_WIDTH), BF16),
        compiler_params=pltpu.CompilerParams(
            dimension_semantics=("parallel", "parallel"), vmem_limit_bytes=VMEM_LIMIT),
        name="band_attn",
    )(z3, z3, z3, z3, z3, z3, z3, bias_tab, qw, kw, seg)


def _attn_bias_table(rel_bias):
    rb = rel_bias.astype(F32)
    heads = rb.shape[0]
    lo = A_LEFT_CHUNKS * CHUNK - (ATT_WIN - 1)
    hi = A_LEFT_CHUNKS * CHUNK + ATT_TQ - 1
    strip = jnp.concatenate([
        jnp.broadcast_to(rb[:, :1], (heads, -REL_CLIP - lo)), rb,
        jnp.broadcast_to(rb[:, -1:], (heads, hi - REL_CLIP))], axis=1)
    rev = strip[:, ::-1]
    n = rev.shape[1]
    flat = jnp.broadcast_to(rev[:, None, :], (heads, ATT_TQ, n)).reshape(heads, ATT_TQ * n)
    bias = flat[:, ATT_TQ - 1:ATT_TQ - 1 + ATT_TQ * (n - 1)].reshape(heads, ATT_TQ, n - 1)[:, :, :ATT_WIN]
    qc = jnp.arange(ATT_TQ)[:, None] // CHUNK
    kc = jnp.arange(ATT_WIN)[None, :] // CHUNK
    in_band = (kc >= qc) & (kc <= qc + A_LEFT_CHUNKS)
    return jnp.where(in_band[None], bias, NEG)


GDN_TAIL = SUBLANES


def _gdn_body(qkv_ref, zs_ref, gate_ref, convw_ref, hp_ref, nw_ref, o_ref, xbuf, s_scr):
    c = CHUNK
    hd = B_HEAD_DIM

    @pl.when(pl.program_id(1) == 0)
    def _():
        xbuf[0:GDN_TAIL, :] = jnp.zeros((GDN_TAIL, 3 * B_WIDTH), F32)
        s_scr[...] = jnp.zeros_like(s_scr)

    xbuf[GDN_TAIL:GDN_TAIL + c, :] = qkv_ref[0].astype(F32)
    y = None
    for j in range(CONV_WIDTH):
        tap = convw_ref[j:j + 1, :] * xbuf[pl.ds(GDN_TAIL - (CONV_WIDTH - 1) + j, c), :]
        y = tap if y is None else y + tap
    xbuf[0:GDN_TAIL, :] = xbuf[c:c + GDN_TAIL, :]
    y = y * _sigmoid(y)

    zs = zs_ref[0]
    a_neg = -(jnp.exp(hp_ref[0:1, :]) * hp_ref[2:3, :])
    xs = zs + hp_ref[1:2, :]
    g_all = a_neg * (jnp.maximum(xs, 0.0) + jnp.log(1.0 + jnp.exp(-jnp.abs(xs))))
    beta_all = _sigmoid(zs)

    row = lax.broadcasted_iota(jnp.int32, (c, c), 0)
    col = lax.broadcasted_iota(jnp.int32, (c, c), 1)
    tril = row >= col
    tril_strict = row > col
    eye = (row == col).astype(F32)
    lower = tril.astype(BF16)
    upper = (row <= col).astype(BF16)

    gh, gl = _split_bf16(g_all)
    gc_all = _dot(lower, gh) + _dot(lower, gl)
    gth, gtl = _split_bf16(g_all.T)
    gc_t = _dot(gth, upper) + _dot(gtl, upper)

    heads = range(B_HEADS)
    q, k, v, beta, gc, gc_row, g_last = [], [], [], [], [], [], []
    for h in heads:
        qh = y[:, h * hd:(h + 1) * hd]
        kh = y[:, B_WIDTH + h * hd:B_WIDTH + (h + 1) * hd]
        q.append(qh * lax.rsqrt(jnp.sum(qh * qh, axis=-1, keepdims=True) + EPS) * (hd ** -0.5))
        k.append(kh * lax.rsqrt(jnp.sum(kh * kh, axis=-1, keepdims=True) + EPS))
        v.append(y[:, 2 * B_WIDTH + h * hd:2 * B_WIDTH + (h + 1) * hd])
        beta.append(beta_all[:, Z_SMALL_BETA + h:Z_SMALL_BETA + h + 1])
        gc.append(gc_all[:, Z_SMALL_A + h:Z_SMALL_A + h + 1])
        gc_row.append(gc_t[Z_SMALL_A + h:Z_SMALL_A + h + 1, :])
        g_last.append(gc_row[h][:, c - 1:c])
    decay = [jnp.exp(jnp.where(tril, gc[h] - gc_row[h], NEG)) for h in heads]
    e_gc = [jnp.exp(gc[h]) for h in heads]
    kb = [k[h] * beta[h] for h in heads]
    kf = [k[h].astype(BF16) for h in heads]
    a_pow = [jnp.where(tril_strict, _dot_nt(kb[h].astype(BF16), kf[h]) * decay[h], 0.0) for h in heads]
    attn = [(_dot_nt(q[h].astype(BF16), kf[h]) * decay[h]).astype(BF16) for h in heads]

    t_mat = [eye - a_pow[h] for h in heads]
    for _ in range(5):
        ab = [a_pow[h].astype(BF16) for h in heads]
        a_pow = [_dot(ab[h], ab[h]) for h in heads]
        t_mat = [t_mat[h] + _dot(t_mat[h].astype(BF16), a_pow[h].astype(BF16)) for h in heads]
    tb = [t_mat[h].astype(BF16) for h in heads]
    u = [_dot(tb[h], (v[h] * beta[h]).astype(BF16)) for h in heads]
    w = [_dot(tb[h], (kb[h] * e_gc[h]).astype(BF16)).astype(BF16) for h in heads]

    state = [s_scr[h] for h in heads]
    sb = [state[h].astype(BF16) for h in heads]
    vnb = [(u[h] - _dot(w[h], sb[h])).astype(BF16) for h in heads]
    o = [_dot((q[h] * e_gc[h]).astype(BF16), sb[h]) + _dot(attn[h], vnb[h]) for h in heads]
    k_dec = [(k[h] * jnp.exp(g_last[h] - gc[h])).astype(BF16) for h in heads]
    for h in heads:
        s_scr[h] = state[h] * jnp.exp(g_last[h]) + _dot_tn(k_dec[h], vnb[h])
    for h in heads:
        cols = slice(h * hd, (h + 1) * hd)
        gate = gate_ref[0, :, cols].astype(F32)
        on = o[h] * lax.rsqrt(jnp.mean(o[h] * o[h], axis=-1, keepdims=True) + EPS) * nw_ref[...]
        o_ref[0, :, cols] = (on * (gate * _sigmoid(gate))).astype(o_ref.dtype)


def _gated_deltanet(z3, zs3, conv_w, head_params, norm_w):
    b, s, _ = z3.shape
    n = s // CHUNK
    return pl.pallas_call(
        _gdn_body,
        grid=(b, n),
        in_specs=[
            pl.BlockSpec((1, CHUNK, 3 * B_WIDTH), lambda bi, ni: (bi, ni, Z_QKV_B // (3 * B_WIDTH))),
            pl.BlockSpec((1, CHUNK, LANES), lambda bi, ni: (bi, ni, 0)),
            pl.BlockSpec((1, CHUNK, B_WIDTH), lambda bi, ni: (bi, ni, Z_GATE_B // B_WIDTH)),
            pl.BlockSpec((CONV_WIDTH, 3 * B_WIDTH), lambda bi, ni: (0, 0)),
            pl.BlockSpec((SUBLANES, LANES), lambda bi, ni: (0, 0)),
            pl.BlockSpec((1, B_HEAD_DIM), lambda bi, ni: (0, 0)),
        ],
        out_specs=pl.BlockSpec((1, CHUNK, B_WIDTH), lambda bi, ni: (bi, ni, 0)),
        out_shape=jax.ShapeDtypeStruct((b, s, B_WIDTH), BF16),
        scratch_shapes=[
            pltpu.VMEM((GDN_TAIL + CHUNK, 3 * B_WIDTH), F32),
            pltpu.VMEM((B_HEADS, B_HEAD_DIM, B_HEAD_DIM), F32),
        ],
        compiler_params=pltpu.CompilerParams(
            dimension_semantics=("parallel", "arbitrary"), vmem_limit_bytes=VMEM_LIMIT),
        name="gdn",
    )(z3, zs3, z3, conv_w, head_params, norm_w)


MERGE_TM = 256


def _top_k_rows(s, k, payload=None):
    n = s.shape[0]
    rows = lax.broadcasted_iota(jnp.int32, s.shape, 0)
    vals, picks = [], []
    for _ in range(k):
        m = jnp.max(s, axis=0, keepdims=True)
        i = jnp.min(jnp.where(s == m, rows, n), axis=0, keepdims=True)
        hit = rows == i
        vals.append(m)
        picks.append(i if payload is None else jnp.max(jnp.where(hit, payload, -1), axis=0, keepdims=True))
        s = jnp.where(hit, -jnp.inf, s)
    return jnp.concatenate(vals, axis=0), jnp.concatenate(picks, axis=0)


def _merge_body(oa_ref, ob_ref, mr_ref, x_ref, bm_ref, pa_ref, pb_ref, wo_ref, nw_ref, wq_ref, k1_ref, k2_ref,
                x1_ref, x1p_ref, xnp_ref, idx_ref, gate_ref, gate_tok_ref):
    tm, d = x_ref.shape
    planes = d // LANES

    def store_planes(ref, a):
        for p in range(planes):
            ref[pl.ds(p, tm, stride=planes), :] = a[:, p * LANES:(p + 1) * LANES]

    mr = mr_ref[...].astype(F32) + bm_ref[...]
    mixed = (_sigmoid(mr[:, :d]) * _dot(oa_ref[...], pa_ref[...])
             + _sigmoid(mr[:, d:]) * _dot(ob_ref[...], pb_ref[...]))
    x1 = x_ref[...] + _dot(mixed.astype(BF16), wo_ref[...])
    x1_ref[...] = x1
    store_planes(x1p_ref, x1)
    xn = x1 * lax.rsqrt(jnp.mean(x1 * x1, axis=-1, keepdims=True) + EPS) * nw_ref[...]
    store_planes(xnp_ref, xn)
    q = _dot(xn.astype(BF16), wq_ref[...]).astype(BF16)

    idx_rows, gate_rows = [], []
    for h in range(PEER_HEADS):
        q1 = q[:, (2 * h) * PEER_HALF:(2 * h + 1) * PEER_HALF]
        q2 = q[:, (2 * h + 1) * PEER_HALF:(2 * h + 2) * PEER_HALF]
        s1 = _dot_nt(k1_ref[h], q1)
        s2 = _dot_nt(k2_ref[h], q2)
        v1, i1 = _top_k_rows(s1, PEER_TOPK)
        v2, i2 = _top_k_rows(s2, PEER_TOPK)
        nb = [PEER_TOPK // (a + 1) for a in range(PEER_TOPK)]
        pad = -sum(nb) % SUBLANES
        tm = v1.shape[1]
        cand = jnp.concatenate([v1[a:a + 1] + v2[:nb[a]] for a in range(PEER_TOPK)]
                               + [jnp.full((pad, tm), -jnp.inf, F32)], axis=0)
        cand_idx = jnp.concatenate([i1[a:a + 1] * PEER_NKEYS + i2[:nb[a]] for a in range(PEER_TOPK)]
                                   + [jnp.zeros((pad, tm), jnp.int32)], axis=0)
        top_s, expert = _top_k_rows(cand, PEER_TOPK, payload=cand_idx)
        e = jnp.exp(top_s - top_s[0:1])
        gate_rows.append(e / jnp.sum(e, axis=0, keepdims=True))
        idx_rows.append(expert)
    idx_ref[...] = jnp.concatenate(idx_rows, axis=0).T
    gates = jnp.concatenate(gate_rows, axis=0)
    gate_ref[...] = gates
    gate_tok_ref[...] = gates.T


def _merge_route(o_a, o_b, z2, x2, row0, b_merge, pa, pb, wo, nw, wq, k1, k2):
    t, d = o_a.shape[0], x2.shape[1]
    tm = MERGE_TM
    xblk0 = row0 // tm
    full = lambda a: pl.BlockSpec(a.shape, lambda i: (0,) * a.ndim)
    return pl.pallas_call(
        _merge_body,
        grid=(t // tm,),
        in_specs=[
            pl.BlockSpec((tm, A_WIDTH), lambda i: (i, 0)),
            pl.BlockSpec((tm, B_WIDTH), lambda i: (i, 0)),
            pl.BlockSpec((tm, 2 * d), lambda i: (i, Z_MERGE // (2 * d))),
            pl.BlockSpec((tm, d), lambda i: (i + xblk0, 0)),
            full(b_merge), full(pa), full(pb), full(wo), full(nw), full(wq), full(k1), full(k2),
        ],
        out_specs=[
            pl.BlockSpec((tm, d), lambda i: (i, 0)),
            pl.BlockSpec((tm * d // LANES, LANES), lambda i: (i, 0)),
            pl.BlockSpec((tm * d // LANES, LANES), lambda i: (i, 0)),
            pl.BlockSpec((tm, PEER_SLOTS), lambda i: (i, 0)),
            pl.BlockSpec((PEER_SLOTS, tm), lambda i: (0, i)),
            pl.BlockSpec((tm, PEER_SLOTS), lambda i: (i, 0)),
        ],
        out_shape=[
            jax.ShapeDtypeStruct((t, d), F32),
            jax.ShapeDtypeStruct((t * d // LANES, LANES), F32),
            jax.ShapeDtypeStruct((t * d // LANES, LANES), F32),
            jax.ShapeDtypeStruct((t, PEER_SLOTS), jnp.int32),
            jax.ShapeDtypeStruct((PEER_SLOTS, t), F32),
            jax.ShapeDtypeStruct((t, PEER_SLOTS), F32),
        ],
        compiler_params=pltpu.CompilerParams(
            dimension_semantics=("parallel",), vmem_limit_bytes=VMEM_LIMIT),
        name="merge_route",
    )(o_a, o_b, z2, x2, b_merge, pa, pb, wo, nw, wq, k1, k2)


PEER_TP = 128
PEER_NBUF = 4
PEER_LOOK = PEER_NBUF - 1
PEER_SC_SHARE_NUM, PEER_SC_SHARE_DEN = 16, 32
PEER_TC_FIRST_NUM, PEER_TC_FIRST_DEN = 9, 16


def _erf(x):
    return lax.erf(x)


PACK_HI = -65536


def _packed_down(bitcast, word):
    return bitcast(word & PACK_HI, F32)


def _packed_up(bitcast, word):
    return bitcast(word << 16, F32)


def _peer_body(idx_ref, idx_next_ref, xn_ref, gate_ref, x1_ref, tab_ref, o_ref, *scratch):
    bufs, sem = scratch[:PEER_NBUF], scratch[PEER_NBUF]
    planes = x1_ref.shape[1] // LANES
    step = pl.program_id(0)

    def issue(ref, tok, slot):
        for j in range(PEER_SLOTS):
            pltpu.make_async_copy(tab_ref.at[ref[0, tok, j]], bufs[slot].at[:, j, :], sem.at[slot]).start()

    def wait(slot):
        pltpu.make_async_copy(bufs[slot], bufs[slot], sem.at[slot]).wait()

    @pl.when(step == 0)
    def _():
        for s in range(PEER_LOOK):
            issue(idx_ref, s, s)

    lane = lax.broadcasted_iota(jnp.int32, (PEER_SLOTS, PEER_TP), 1)

    def compute(tok, slot):
        buf = bufs[slot]
        acc = None
        for s in range(planes):
            term = _packed_down(lax.bitcast_convert_type, buf[s]) * xn_ref[pl.ds(tok * planes + s, 1), :]
            acc = term if acc is None else acc + term
        act = jnp.sum(acc, axis=-1, keepdims=True)
        gate = jnp.sum(jnp.where(lane == tok, gate_ref[...], 0.0), axis=-1, keepdims=True)
        w = gate * (0.5 * act * (1.0 + _erf(act * (2.0 ** -0.5))))
        out = jnp.concatenate(
            [jnp.sum(_packed_up(lax.bitcast_convert_type, buf[s]) * w, axis=0, keepdims=True)
             for s in range(planes)], axis=1)
        o_ref[pl.ds(tok, 1), :] = x1_ref[pl.ds(tok, 1), :] + out

    def group(g, carry):
        for u in range(PEER_NBUF):
            tok = g * PEER_NBUF + u
            issue(idx_ref, tok + PEER_LOOK, (u + PEER_LOOK) % PEER_NBUF)
            wait(u)
            compute(tok, u)
        return carry

    n_groups = PEER_TP // PEER_NBUF
    lax.fori_loop(0, n_groups - 1, group, 0)

    for u in range(PEER_NBUF):
        tok = (n_groups - 1) * PEER_NBUF + u
        nxt = tok + PEER_LOOK
        if nxt < PEER_TP:
            issue(idx_ref, nxt, (u + PEER_LOOK) % PEER_NBUF)
        else:
            issue(idx_next_ref, nxt - PEER_TP, (u + PEER_LOOK) % PEER_NBUF)
        wait(u)
        compute(tok, u)

    @pl.when(step == pl.num_programs(0) - 1)
    def _():
        for s in range(PEER_LOOK):
            wait(s)


def _peer_alias_body(prev_ref, after_ref, *rest):
    del prev_ref, after_ref
    _peer_body(*rest)


def _peer(idx3, xnp, gate_t, x1, table, blk0, nb, out_blk0, t_out, prev=None, after=None):
    d = x1.shape[1]
    tp = PEER_TP
    planes = d // LANES
    in_specs = [
        pl.BlockSpec((1, tp, PEER_SLOTS), lambda i: (i + blk0, 0, 0), memory_space=pltpu.SMEM),
        pl.BlockSpec((1, tp, PEER_SLOTS), lambda i: (jnp.minimum(i + 1, nb - 1) + blk0, 0, 0),
                     memory_space=pltpu.SMEM),
        pl.BlockSpec((tp * planes, LANES), lambda i: (i + blk0, 0)),
        pl.BlockSpec((PEER_SLOTS, tp), lambda i: (0, i + blk0)),
        pl.BlockSpec((tp, d), lambda i: (i + blk0, 0)),
        pl.BlockSpec(memory_space=pl.ANY),
    ]
    args = (idx3, idx3, xnp, gate_t, x1, table)
    body, aliases = _peer_body, {}
    if prev is not None:
        body, aliases = _peer_alias_body, {0: 0}
        in_specs = [pl.BlockSpec(memory_space=pl.ANY), pl.BlockSpec(memory_space=pl.ANY)] + in_specs
        args = (prev, after) + args
    return pl.pallas_call(
        body,
        grid=(nb,),
        in_specs=in_specs,
        out_specs=pl.BlockSpec((tp, d), lambda i: (i + blk0 + out_blk0, 0)),
        out_shape=jax.ShapeDtypeStruct((t_out, d), F32),
        scratch_shapes=[pltpu.VMEM((planes, PEER_SLOTS, LANES), jnp.int32) for _ in range(PEER_NBUF)]
        + [pltpu.SemaphoreType.DMA((PEER_NBUF,))],
        input_output_aliases=aliases,
        compiler_params=pltpu.CompilerParams(
            dimension_semantics=("arbitrary",), vmem_limit_bytes=VMEM_LIMIT),
        name="peer",
    )(*args)


SC_CORES = 2
SC_SUBCORES = 16
SC_LANES = 16
SC_WORKERS = SC_CORES * SC_SUBCORES
SC_ROWS = 32
SC_RB = 4


SC_VPR = LANES // SC_LANES
SC_VROWS = PEER_SLOTS // SC_VPR


def _sc_vec(j):
    return (j // SC_VPR, pl.ds((j % SC_VPR) * SC_LANES, SC_LANES))


def _sc_mesh():
    return plsc.VectorSubcoreMesh(core_axis_name="c", subcore_axis_name="s")


def _sc_worker():
    return lax.axis_index("s") * SC_CORES + lax.axis_index("c")


def _sc_pipeline(tw, base, nchunk, loads, gather, compute, finish):
    assert tw % 2 == 0 and nchunk % 2 == 0
    last = base + tw - 1

    def token(tok, s):
        nxt = jnp.minimum(tok + 1, last)
        for c in loads(nxt, 1 - s):
            c.start()
        for kc in range(nchunk):
            if kc + 1 < nchunk:
                gather(s, kc + 1).start()
            else:
                for c in loads(nxt, 1 - s):
                    c.wait()
                gather(1 - s, 0).start()
            gather(s, kc).wait()
            compute(s, kc)
        finish(tok, s)

    for c in loads(base, 0):
        c.start()
    for c in loads(base, 0):
        c.wait()
    gather(0, 0).start()

    def pair(i, carry):
        token(base + 2 * i, 0)
        token(base + 2 * i + 1, 1)
        return carry

    lax.fori_loop(0, tw // 2, pair, 0)
    gather(0, 0).wait()


def _sc_down(idx_flat, xn3, table, tok0, n_tok):
    planes = xn3.shape[1]
    tw = n_tok // SC_WORKERS
    nchunk = PEER_SLOTS // SC_ROWS
    per_plane = LANES // SC_LANES

    @functools.partial(
        pl.kernel, mesh=_sc_mesh(),
        out_type=jax.ShapeDtypeStruct((n_tok, SC_VROWS, LANES), F32),
        scratch_types=[
            pltpu.VMEM((2, PEER_SLOTS), jnp.int32),
            pltpu.VMEM((2, planes, LANES), F32),
            pltpu.VMEM((2, SC_ROWS, planes, LANES), jnp.int32),
            pltpu.VMEM((SC_VROWS, LANES), F32),
            pltpu.VMEM((planes, SC_VROWS, LANES), F32),
            pltpu.SemaphoreType.DMA((2,)),
            pltpu.SemaphoreType.DMA((2,)),
        ],
        compiler_params=pltpu.CompilerParams(needs_layout_passes=False),
        name="sc_down",
    )
    def run(idx_hbm, xn_hbm, tab_hbm, out_hbm, idx_v, x_v, buf, act_v, part_v, sem_g, sem_l):
        def loads(tok, s):
            return [pltpu.make_async_copy(idx_hbm.at[pl.ds(tok * PEER_SLOTS, PEER_SLOTS)], idx_v.at[s], sem_l.at[s]),
                    pltpu.make_async_copy(xn_hbm.at[tok], x_v.at[s], sem_l.at[s])]

        def gather(s, kc):
            return pltpu.make_async_copy(
                tab_hbm.at[idx_v.at[s, pl.ds(kc * SC_ROWS, SC_ROWS)]], buf.at[kc % 2], sem_g.at[kc % 2])

        def compute(s, kc):
            @plsc.parallel_loop(0, planes)
            def _(p):
                sls = [pl.ds(q * SC_LANES, SC_LANES) for q in range(per_plane)]
                xs = [x_v[s, p, sl] for sl in sls]
                for j0 in range(0, SC_ROWS, SC_RB):
                    terms = [[_packed_down(plsc.bitcast, buf[kc % 2, j0 + r, p, sls[q]]) * xs[q]
                              for q in range(per_plane)] for r in range(SC_RB)]
                    while len(terms[0]) > 1:
                        terms = [[row[i] + row[i + 1] for i in range(0, len(row), 2)] for row in terms]
                    for r in range(SC_RB):
                        part_v[(p,) + _sc_vec(kc * SC_ROWS + j0 + r)] = terms[r][0]

        def finish(tok, s):
            @plsc.parallel_loop(0, SC_VROWS)
            def _(r):
                for u in range(SC_VPR):
                    sl = pl.ds(u * SC_LANES, SC_LANES)
                    acc = part_v[0, r, sl]
                    for p in range(1, planes):
                        acc = acc + part_v[p, r, sl]
                    act_v[r, sl] = acc
            pltpu.sync_copy(act_v, out_hbm.at[tok - tok0])

        _sc_pipeline(tw, tok0 + _sc_worker() * tw, nchunk, loads, gather, compute, finish)

    return run(idx_flat, xn3, table)


def _sc_up(idx_flat, w16, x13, table, tok0, n_tok):
    planes = x13.shape[1]
    tw = n_tok // SC_WORKERS
    nchunk = PEER_SLOTS // SC_ROWS
    per_plane = LANES // SC_LANES

    @functools.partial(
        pl.kernel, mesh=_sc_mesh(),
        out_type=jax.ShapeDtypeStruct((n_tok, planes, LANES), F32),
        scratch_types=[
            pltpu.VMEM((2, PEER_SLOTS), jnp.int32),
            pltpu.VMEM((2, SC_VROWS, LANES), F32),
            pltpu.VMEM((2, planes, LANES), F32),
            pltpu.VMEM((2, SC_ROWS, planes, LANES), jnp.int32),
            pltpu.SemaphoreType.DMA((2,)),
            pltpu.SemaphoreType.DMA((2,)),
        ],
        compiler_params=pltpu.CompilerParams(needs_layout_passes=False),
        name="sc_up",
    )
    def run(idx_hbm, w_hbm, x1_hbm, tab_hbm, out_hbm, idx_v, w_v, out_v, buf, sem_g, sem_l):
        def loads(tok, s):
            return [pltpu.make_async_copy(idx_hbm.at[pl.ds(tok * PEER_SLOTS, PEER_SLOTS)], idx_v.at[s], sem_l.at[s]),
                    pltpu.make_async_copy(w_hbm.at[tok - tok0], w_v.at[s], sem_l.at[s]),
                    pltpu.make_async_copy(x1_hbm.at[tok], out_v.at[s], sem_l.at[s])]

        def gather(s, kc):
            return pltpu.make_async_copy(
                tab_hbm.at[idx_v.at[s, pl.ds(kc * SC_ROWS, SC_ROWS)]], buf.at[kc % 2], sem_g.at[kc % 2])

        def compute(s, kc):
            @plsc.parallel_loop(0, planes)
            def _(p):
                sls = [pl.ds(q * SC_LANES, SC_LANES) for q in range(per_plane)]
                accs = [out_v[s, p, sl] for sl in sls]
                for j in range(SC_ROWS):
                    wj = w_v[(s,) + _sc_vec(kc * SC_ROWS + j)]
                    accs = [a + wj * _packed_up(plsc.bitcast, buf[kc % 2, j, p, sl]) for a, sl in zip(accs, sls)]
                for a, sl in zip(accs, sls):
                    out_v[s, p, sl] = a

        def finish(tok, s):
            pltpu.sync_copy(out_v.at[s], out_hbm.at[tok - tok0])

        _sc_pipeline(tw, tok0 + _sc_worker() * tw, nchunk, loads, gather, compute, finish)

    return run(idx_flat, w16, x13, table)


ACT_TM = 256


def _peer_act_body(after_ref, part_ref, gate_ref, fold_ref, spread_ref, w_ref):
    del after_ref
    act = None
    for r in range(SC_VROWS):
        for piece in _split_bf16(part_ref[:, r, :]):
            term = _dot(piece, fold_ref[r])
            act = term if act is None else act + term
    w = _split_bf16(gate_ref[...] * (0.5 * act * (1.0 + _erf(act * (2.0 ** -0.5)))))
    for r in range(SC_VROWS):
        w_ref[:, r, :] = _dot(w[0], spread_ref[r]) + _dot(w[1], spread_ref[r])


def _peer_act(after, part3, gate_tok, blk0):
    n_tok = part3.shape[0]
    tm = ACT_TM
    lane = jnp.arange(LANES)
    fold = (lane[None, None, :] == (jnp.arange(SC_VROWS)[:, None, None] * SC_VPR
                                    + lane[None, :, None] // SC_LANES)).astype(BF16)
    spread = jnp.swapaxes(fold, 1, 2)
    const = lambda a: pl.BlockSpec(a.shape, lambda i: (0,) * a.ndim)
    return pl.pallas_call(
        _peer_act_body,
        grid=(n_tok // tm,),
        in_specs=[
            pl.BlockSpec(memory_space=pl.ANY),
            pl.BlockSpec((tm, SC_VROWS, LANES), lambda i: (i, 0, 0)),
            pl.BlockSpec((tm, PEER_SLOTS), lambda i: (i + blk0, 0)),
            const(fold), const(spread),
        ],
        out_specs=pl.BlockSpec((tm, SC_VROWS, LANES), lambda i: (i, 0, 0)),
        out_shape=jax.ShapeDtypeStruct(part3.shape, F32),
        compiler_params=pltpu.CompilerParams(
            dimension_semantics=("parallel",), vmem_limit_bytes=VMEM_LIMIT),
        name="peer_act",
    )(after, part3, gate_tok, fold, spread)


def _table_pack_body(down_ref, up_ref, tab_ref):
    te, d = down_ref.shape
    planes = d // LANES
    for p in range(planes):
        cols = slice(p * LANES, (p + 1) * LANES)
        hi = lax.bitcast_convert_type(down_ref[:, cols].astype(BF16).astype(F32), jnp.int32)
        lo = lax.bitcast_convert_type(up_ref[:, cols].astype(BF16).astype(F32), jnp.int32)
        tab_ref[pl.ds(p, te, stride=planes), :] = hi | lax.shift_right_logical(lo, 16)


def _table_pack(down, up, te=512):
    n_exp, d = down.shape
    rows = d // LANES
    return pl.pallas_call(
        _table_pack_body,
        grid=(n_exp // te,),
        in_specs=[pl.BlockSpec((te, d), lambda i: (i, 0)), pl.BlockSpec((te, d), lambda i: (i, 0))],
        out_specs=pl.BlockSpec((te * rows, LANES), lambda i: (i, 0)),
        out_shape=jax.ShapeDtypeStruct((n_exp * rows, LANES), jnp.int32),
        compiler_params=pltpu.CompilerParams(
            dimension_semantics=("parallel",), vmem_limit_bytes=VMEM_LIMIT),
        name="table_pack",
    )(down, up)


def _pad_lanes(a):
    return jnp.pad(a, ((0, 0), (0, LANES - a.shape[1])))


def kernel(x, norm_mix_w, w_in, a_q_norm_w, a_k_norm_w, a_rel_bias, b_conv_w, b_a_log, b_dt_bias, b_norm_w, b_merge, w_proj_a, w_proj_b, w_out, norm_ffn_w, peer_w_query, peer_keys_1, peer_keys_2, peer_down, peer_up):
    b, s, d = x.shape
    t = b * s
    depth = w_in.shape[0]
    for l in range(depth):
        x2 = x.reshape(t, d)

        wi = w_in[l]
        o_qkvb = 3 * A_WIDTH
        o_a = o_qkvb + 3 * B_WIDTH
        o_beta = o_a + B_HEADS
        o_gate = o_beta + B_HEADS
        o_merge = o_gate + B_WIDTH
        w_main = jnp.concatenate(
            [wi[:, o_qkvb:o_a], wi[:, o_gate:o_merge], wi[:, o_merge:], wi[:, :o_qkvb]], axis=1).astype(BF16)
        w_small = _pad_lanes(wi[:, o_a:o_gate]).astype(BF16)
        z_attn = Z_MERGE + 2 * d

        bias_tab = _attn_bias_table(a_rel_bias[l])
        seg = jnp.kron(jnp.eye(A_HEADS, dtype=F32), jnp.full((A_HEAD_DIM, A_HEAD_DIM), 1.0 / A_HEAD_DIM, F32))
        head_params = jnp.zeros((SUBLANES, LANES), F32)
        head_params = head_params.at[0, :B_HEADS].set(b_a_log[l]).at[1, :B_HEADS].set(b_dt_bias[l])
        head_params = head_params.at[2, :B_HEADS].set(1.0)

        def mixer_and_route(b0, nb):
            tg = nb * s
            z, zs = _in_proj(x2, norm_mix_w[l][None, :], w_main, w_small, min(1024, tg), 1920, b0 * s, tg)
            z3 = z.reshape(nb, s, z.shape[1])
            o_att = _band_attention(
                z3, bias_tab, jnp.tile(a_q_norm_w[l], A_HEADS)[None, :], jnp.tile(a_k_norm_w[l], A_HEADS)[None, :],
                seg.astype(BF16), z_attn // A_WIDTH)
            o_gdn = _gated_deltanet(z3, zs.reshape(nb, s, LANES), b_conv_w[l], head_params, b_norm_w[l][None, :])
            return _merge_route(
                o_att.reshape(tg, A_WIDTH), o_gdn.reshape(tg, B_WIDTH), z, x2, b0 * s, b_merge[l][None, :],
                w_proj_a[l].astype(BF16), w_proj_b[l].astype(BF16), w_out[l].astype(BF16),
                norm_ffn_w[l][None, :], peer_w_query[l].astype(BF16), peer_keys_1[l].astype(BF16),
                peer_keys_2[l].astype(BF16))

        planes = d // LANES
        n_exp = peer_down.shape[1]
        table = _table_pack(peer_down[l], peer_up[l]).reshape(n_exp, planes, LANES)

        b_sc = b * PEER_SC_SHARE_NUM // PEER_SC_SHARE_DEN
        n_sc = b_sc * s
        _, x1p_s, xnp_s, idx_s, _, gate_tok_s = mixer_and_route(0, b_sc)
        idx_s = idx_s.reshape(-1)
        part = _sc_down(idx_s, xnp_s.reshape(n_sc, planes, LANES), table, 0, n_sc)

        x1, _, xnp, idx, gate_t, _ = mixer_and_route(b_sc, b - b_sc)
        nblk = (t - n_sc) // PEER_TP
        nblk_a = nblk * PEER_TC_FIRST_NUM // PEER_TC_FIRST_DEN
        idx3 = idx.reshape(nblk, PEER_TP, PEER_SLOTS)
        out = _peer(idx3, xnp, gate_t, x1, table, 0, nblk_a, n_sc // PEER_TP, t)
        w16 = _peer_act(out, part, gate_tok_s, 0)
        out = _peer(idx3, xnp, gate_t, x1, table, nblk_a, nblk - nblk_a, n_sc // PEER_TP, t, prev=out, after=w16)
        sc_out = _sc_up(idx_s, w16, x1p_s.reshape(n_sc, planes, LANES), table, 0, n_sc)
        out = lax.dynamic_update_slice(out, sc_out.reshape(n_sc, d), (0, 0))
        x = out.reshape(b, s, d)
    return x
```

```python
import functools

import jax
import jax.numpy as jnp
from jax import lax
from jax.experimental import pallas as pl
from jax.experimental.pallas import tpu as pltpu
from jax.experimental.pallas import tpu_sc as plsc

F32 = jnp.float32
BF16 = jnp.bfloat16

EPS = 1e-6
NEG = -1e30

CHUNK = 64
A_HEADS = 8
A_HEAD_DIM = 64
A_WIDTH = A_HEADS * A_HEAD_DIM
A_LEFT_CHUNKS = 8
REL_CLIP = 128
B_HEADS = 8
B_HEAD_DIM = 128
B_WIDTH = B_HEADS * B_HEAD_DIM
CONV_WIDTH = 4
PEER_HEADS = 8
PEER_HALF = 128
PEER_NKEYS = 128
PEER_TOPK = 16
PEER_SLOTS = PEER_HEADS * PEER_TOPK

LANES = 128
SUBLANES = 8
VMEM_LIMIT = 56 * 1024 * 1024

Z_QKV_B = 0
Z_GATE_B = 3 * B_WIDTH
Z_MERGE = Z_GATE_B + B_WIDTH
Z_SMALL_A = 0
Z_SMALL_BETA = B_HEADS


def _dot(a, b):
    return jnp.dot(a, b, preferred_element_type=F32)


def _dot_nt(a, b):
    return lax.dot_general(a, b, (((1,), (1,)), ((), ())), preferred_element_type=F32)


def _dot_tn(a, b):
    return lax.dot_general(a, b, (((0,), (0,)), ((), ())), preferred_element_type=F32)


def _split_bf16(a):
    hi = a.astype(BF16)
    lo = (a - hi.astype(F32)).astype(BF16)
    return hi, lo


def _sigmoid(x):
    return 1.0 / (1.0 + jnp.exp(-x))


def _inproj_body(x_ref, nw_ref, w_ref, ws_ref, z_ref, zs_ref, h_scr):
    @pl.when(pl.program_id(1) == 0)
    def _():
        x = x_ref[...]
        ms = jnp.mean(x * x, axis=-1, keepdims=True)
        hb = (x * lax.rsqrt(ms + EPS) * nw_ref[...]).astype(BF16)
        h_scr[...] = hb
        zs_ref[...] = _dot(hb, ws_ref[...])

    z_ref[...] = _dot(h_scr[...], w_ref[...]).astype(z_ref.dtype)


def _in_proj(x2, norm_w, w_main, w_small, tm, tn, row0, t):
    d = x2.shape[1]
    n = w_main.shape[1]
    blk0 = row0 // tm
    return pl.pallas_call(
        _inproj_body,
        grid=(t // tm, n // tn),
        in_specs=[
            pl.BlockSpec((tm, d), lambda i, j: (i + blk0, 0)),
            pl.BlockSpec((1, d), lambda i, j: (0, 0)),
            pl.BlockSpec((d, tn), lambda i, j: (0, j)),
            pl.BlockSpec((d, LANES), lambda i, j: (0, 0)),
        ],
        out_specs=[
            pl.BlockSpec((tm, tn), lambda i, j: (i, j)),
            pl.BlockSpec((tm, LANES), lambda i, j: (i, 0)),
        ],
        out_shape=[
            jax.ShapeDtypeStruct((t, n), BF16),
            jax.ShapeDtypeStruct((t, LANES), F32),
        ],
        scratch_shapes=[pltpu.VMEM((tm, d), BF16)],
        compiler_params=pltpu.CompilerParams(
            dimension_semantics=("parallel", "arbitrary"), vmem_limit_bytes=VMEM_LIMIT),
        name="in_proj",
    )(x2, norm_w, w_main, w_small)


ATT_TQ = 256
ATT_WIN = ATT_TQ + A_LEFT_CHUNKS * CHUNK
ATT_NKB = ATT_WIN // ATT_TQ


def _attn_body(q_ref, k0_ref, k1_ref, k2_ref, v0_ref, v1_ref, v2_ref, bias_ref, qw_ref, kw_ref, seg_ref,
               o_ref):
    t = pl.program_id(1)
    seg = seg_ref[...]

    def head_rms(a, w):
        hi, lo = _split_bf16(a * a)
        ms = _dot(hi, seg) + _dot(lo, seg)
        return a * lax.rsqrt(ms + EPS) * w

    q = head_rms(q_ref[0].astype(F32), qw_ref[...]) * (A_HEAD_DIM ** -0.5)
    k = jnp.concatenate([k0_ref[0], k1_ref[0], k2_ref[0]], axis=0).astype(F32)
    k = head_rms(k, kw_ref[...]).astype(BF16)
    v = jnp.concatenate([v0_ref[0], v1_ref[0], v2_ref[0]], axis=0)

    kpos = t * ATT_TQ - A_LEFT_CHUNKS * CHUNK + lax.broadcasted_iota(jnp.int32, (1, ATT_WIN), 1)
    valid = kpos >= 0
    lane = lax.broadcasted_iota(jnp.int32, (1, LANES), 1)
    first = lane < A_HEAD_DIM

    for pair in range(A_HEADS // 2):
        cols = slice(pair * LANES, (pair + 1) * LANES)
        qp = q[:, cols]
        kp = k[:, cols]
        vp = v[:, cols]
        outs = []
        for half in range(2):
            h = 2 * pair + half
            qm = jnp.where(first if half == 0 else jnp.logical_not(first), qp, 0.0).astype(BF16)
            s = _dot_nt(qm, kp) + bias_ref[h]
            s = jnp.where(valid, s, NEG)
            mx = jnp.max(s, axis=-1, keepdims=True)
            p = jnp.exp(s - mx)
            den = jnp.sum(p, axis=-1, keepdims=True)
            outs.append(_dot(p.astype(BF16), vp) / den)
        o_ref[0, :, cols] = jnp.where(first, outs[0], outs[1]).astype(o_ref.dtype)


def _band_attention(z3, bias_tab, qw, kw, seg, col_q):
    b, s, _ = z3.shape
    nt = s // ATT_TQ

    def kv_spec(back, col):
        return pl.BlockSpec((1, ATT_TQ, A_WIDTH), lambda bi, ti: (bi, jnp.maximum(ti - back, 0), col))

    return pl.pallas_call(
        _attn_body,
        grid=(b, nt),
        in_specs=[
            pl.BlockSpec((1, ATT_TQ, A_WIDTH), lambda bi, ti: (bi, ti, col_q)),
            kv_spec(2, col_q + 1), kv_spec(1, col_q + 1), kv_spec(0, col_q + 1),
            kv_spec(2, col_q + 2), kv_spec(1, col_q + 2), kv_spec(0, col_q + 2),
            pl.BlockSpec((A_HEADS, ATT_TQ, ATT_WIN), lambda bi, ti: (0, 0, 0)),
            pl.BlockSpec((1, A_WIDTH), lambda bi, ti: (0, 0)),
            pl.BlockSpec((1, A_WIDTH), lambda bi, ti: (0, 0)),
            pl.BlockSpec((A_WIDTH, A_WIDTH), lambda bi, ti: (0, 0)),
        ],
        out_specs=pl.BlockSpec((1, ATT_TQ, A_WIDTH), lambda bi, ti: (bi, ti, 0)),
        out_shape=jax.ShapeDtypeStruct((b, s, A_WIDTH), BF16),
        compiler_params=pltpu.CompilerParams(
            dimension_semantics=("parallel", "parallel"), vmem_limit_bytes=VMEM_LIMIT),
        name="band_attn",
    )(z3, z3, z3, z3, z3, z3, z3, bias_tab, qw, kw, seg)


def _attn_bias_table(rel_bias):
    rb = rel_bias.astype(F32)
    heads = rb.shape[0]
    lo = A_LEFT_CHUNKS * CHUNK - (ATT_WIN - 1)
    hi = A_LEFT_CHUNKS * CHUNK + ATT_TQ - 1
    strip = jnp.concatenate([
        jnp.broadcast_to(rb[:, :1], (heads, -REL_CLIP - lo)), rb,
        jnp.broadcast_to(rb[:, -1:], (heads, hi - REL_CLIP))], axis=1)
    rev = strip[:, ::-1]
    n = rev.shape[1]
    flat = jnp.broadcast_to(rev[:, None, :], (heads, ATT_TQ, n)).reshape(heads, ATT_TQ * n)
    bias = flat[:, ATT_TQ - 1:ATT_TQ - 1 + ATT_TQ * (n - 1)].reshape(heads, ATT_TQ, n - 1)[:, :, :ATT_WIN]
    qc = jnp.arange(ATT_TQ)[:, None] // CHUNK
    kc = jnp.arange(ATT_WIN)[None, :] // CHUNK
    in_band = (kc >= qc) & (kc <= qc + A_LEFT_CHUNKS)
    return jnp.where(in_band[None], bias, NEG)


GDN_TAIL = SUBLANES


def _gdn_body(qkv_ref, zs_ref, gate_ref, convw_ref, hp_ref, nw_ref, o_ref, xbuf, s_scr):
    c = CHUNK
    hd = B_HEAD_DIM

    @pl.when(pl.program_id(1) == 0)
    def _():
        xbuf[0:GDN_TAIL, :] = jnp.zeros((GDN_TAIL, 3 * B_WIDTH), F32)
        s_scr[...] = jnp.zeros_like(s_scr)

    xbuf[GDN_TAIL:GDN_TAIL + c, :] = qkv_ref[0].astype(F32)
    y = None
    for j in range(CONV_WIDTH):
        tap = convw_ref[j:j + 1, :] * xbuf[pl.ds(GDN_TAIL - (CONV_WIDTH - 1) + j, c), :]
        y = tap if y is None else y + tap
    xbuf[0:GDN_TAIL, :] = xbuf[c:c + GDN_TAIL, :]
    y = y * _sigmoid(y)

    zs = zs_ref[0]
    a_neg = -(jnp.exp(hp_ref[0:1, :]) * hp_ref[2:3, :])
    xs = zs + hp_ref[1:2, :]
    g_all = a_neg * (jnp.maximum(xs, 0.0) + jnp.log(1.0 + jnp.exp(-jnp.abs(xs))))
    beta_all = _sigmoid(zs)

    row = lax.broadcasted_iota(jnp.int32, (c, c), 0)
    col = lax.broadcasted_iota(jnp.int32, (c, c), 1)
    tril = row >= col
    tril_strict = row > col
    eye = (row == col).astype(F32)
    lower = tril.astype(BF16)
    upper = (row <= col).astype(BF16)

    gh, gl = _split_bf16(g_all)
    gc_all = _dot(lower, gh) + _dot(lower, gl)
    gth, gtl = _split_bf16(g_all.T)
    gc_t = _dot(gth, upper) + _dot(gtl, upper)

    heads = range(B_HEADS)
    q, k, v, beta, gc, gc_row, g_last = [], [], [], [], [], [], []
    for h in heads:
        qh = y[:, h * hd:(h + 1) * hd]
        kh = y[:, B_WIDTH + h * hd:B_WIDTH + (h + 1) * hd]
        q.append(qh * lax.rsqrt(jnp.sum(qh * qh, axis=-1, keepdims=True) + EPS) * (hd ** -0.5))
        k.append(kh * lax.rsqrt(jnp.sum(kh * kh, axis=-1, keepdims=True) + EPS))
        v.append(y[:, 2 * B_WIDTH + h * hd:2 * B_WIDTH + (h + 1) * hd])
        beta.append(beta_all[:, Z_SMALL_BETA + h:Z_SMALL_BETA + h + 1])
        gc.append(gc_all[:, Z_SMALL_A + h:Z_SMALL_A + h + 1])
        gc_row.append(gc_t[Z_SMALL_A + h:Z_SMALL_A + h + 1, :])
        g_last.append(gc_row[h][:, c - 1:c])
    decay = [jnp.exp(jnp.where(tril, gc[h] - gc_row[h], NEG)) for h in heads]
    e_gc = [jnp.exp(gc[h]) for h in heads]
    kb = [k[h] * beta[h] for h in heads]
    kf = [k[h].astype(BF16) for h in heads]
    a_pow = [jnp.where(tril_strict, _dot_nt(kb[h].astype(BF16), kf[h]) * decay[h], 0.0) for h in heads]
    attn = [(_dot_nt(q[h].astype(BF16), kf[h]) * decay[h]).astype(BF16) for h in heads]

    t_mat = [eye - a_pow[h] for h in heads]
    for _ in range(5):
        ab = [a_pow[h].astype(BF16) for h in heads]
        a_pow = [_dot(ab[h], ab[h]) for h in heads]
        t_mat = [t_mat[h] + _dot(t_mat[h].astype(BF16), a_pow[h].astype(BF16)) for h in heads]
    tb = [t_mat[h].astype(BF16) for h in heads]
    u = [_dot(tb[h], (v[h] * beta[h]).astype(BF16)) for h in heads]
    w = [_dot(tb[h], (kb[h] * e_gc[h]).astype(BF16)).astype(BF16) for h in heads]

    state = [s_scr[h] for h in heads]
    sb = [state[h].astype(BF16) for h in heads]
    vnb = [(u[h] - _dot(w[h], sb[h])).astype(BF16) for h in heads]
    o = [_dot((q[h] * e_gc[h]).astype(BF16), sb[h]) + _dot(attn[h], vnb[h]) for h in heads]
    k_dec = [(k[h] * jnp.exp(g_last[h] - gc[h])).astype(BF16) for h in heads]
    for h in heads:
        s_scr[h] = state[h] * jnp.exp(g_last[h]) + _dot_tn(k_dec[h], vnb[h])
    for h in heads:
        cols = slice(h * hd, (h + 1) * hd)
        gate = gate_ref[0, :, cols].astype(F32)
        on = o[h] * lax.rsqrt(jnp.mean(o[h] * o[h], axis=-1, keepdims=True) + EPS) * nw_ref[...]
        o_ref[0, :, cols] = (on * (gate * _sigmoid(gate))).astype(o_ref.dtype)


def _gated_deltanet(z3, zs3, conv_w, head_params, norm_w):
    b, s, _ = z3.shape
    n = s // CHUNK
    return pl.pallas_call(
        _gdn_body,
        grid=(b, n),
        in_specs=[
            pl.BlockSpec((1, CHUNK, 3 * B_WIDTH), lambda bi, ni: (bi, ni, Z_QKV_B // (3 * B_WIDTH))),
            pl.BlockSpec((1, CHUNK, LANES), lambda bi, ni: (bi, ni, 0)),
            pl.BlockSpec((1, CHUNK, B_WIDTH), lambda bi, ni: (bi, ni, Z_GATE_B // B_WIDTH)),
            pl.BlockSpec((CONV_WIDTH, 3 * B_WIDTH), lambda bi, ni: (0, 0)),
            pl.BlockSpec((SUBLANES, LANES), lambda bi, ni: (0, 0)),
            pl.BlockSpec((1, B_HEAD_DIM), lambda bi, ni: (0, 0)),
        ],
        out_specs=pl.BlockSpec((1, CHUNK, B_WIDTH), lambda bi, ni: (bi, ni, 0)),
        out_shape=jax.ShapeDtypeStruct((b, s, B_WIDTH), BF16),
        scratch_shapes=[
            pltpu.VMEM((GDN_TAIL + CHUNK, 3 * B_WIDTH), F32),
            pltpu.VMEM((B_HEADS, B_HEAD_DIM, B_HEAD_DIM), F32),
        ],
        compiler_params=pltpu.CompilerParams(
            dimension_semantics=("parallel", "arbitrary"), vmem_limit_bytes=VMEM_LIMIT),
        name="gdn",
    )(z3, zs3, z3, conv_w, head_params, norm_w)


MERGE_TM = 256


def _top_k_rows(s, k, payload=None):
    n = s.shape[0]
    rows = lax.broadcasted_iota(jnp.int32, s.shape, 0)
    vals, picks = [], []
    for _ in range(k):
        m = jnp.max(s, axis=0, keepdims=True)
        i = jnp.min(jnp.where(s == m, rows, n), axis=0, keepdims=True)
        hit = rows == i
        vals.append(m)
        picks.append(i if payload is None else jnp.max(jnp.where(hit, payload, -1), axis=0, keepdims=True))
        s = jnp.where(hit, -jnp.inf, s)
    return jnp.concatenate(vals, axis=0), jnp.concatenate(picks, axis=0)


def _merge_body(oa_ref, ob_ref, mr_ref, x_ref, bm_ref, pa_ref, pb_ref, wo_ref, nw_ref, wq_ref, k1_ref, k2_ref,
                x1_ref, x1p_ref, xnp_ref, idx_ref, gate_ref, gate_tok_ref):
    tm, d = x_ref.shape
    planes = d // LANES

    def store_planes(ref, a):
        for p in range(planes):
            ref[pl.ds(p, tm, stride=planes), :] = a[:, p * LANES:(p + 1) * LANES]

    mr = mr_ref[...].astype(F32) + bm_ref[...]
    mixed = (_sigmoid(mr[:, :d]) * _dot(oa_ref[...], pa_ref[...])
             + _sigmoid(mr[:, d:]) * _dot(ob_ref[...], pb_ref[...]))
    x1 = x_ref[...] + _dot(mixed.astype(BF16), wo_ref[...])
    x1_ref[...] = x1
    store_planes(x1p_ref, x1)
    xn = x1 * lax.rsqrt(jnp.mean(x1 * x1, axis=-1, keepdims=True) + EPS) * nw_ref[...]
    store_planes(xnp_ref, xn)
    q = _dot(xn.astype(BF16), wq_ref[...]).astype(BF16)

    idx_rows, gate_rows = [], []
    for h in range(PEER_HEADS):
        q1 = q[:, (2 * h) * PEER_HALF:(2 * h + 1) * PEER_HALF]
        q2 = q[:, (2 * h + 1) * PEER_HALF:(2 * h + 2) * PEER_HALF]
        s1 = _dot_nt(k1_ref[h], q1)
        s2 = _dot_nt(k2_ref[h], q2)
        v1, i1 = _top_k_rows(s1, PEER_TOPK)
        v2, i2 = _top_k_rows(s2, PEER_TOPK)
        nb = [PEER_TOPK // (a + 1) for a in range(PEER_TOPK)]
        pad = -sum(nb) % SUBLANES
        tm = v1.shape[1]
        cand = jnp.concatenate([v1[a:a + 1] + v2[:nb[a]] for a in range(PEER_TOPK)]
                               + [jnp.full((pad, tm), -jnp.inf, F32)], axis=0)
        cand_idx = jnp.concatenate([i1[a:a + 1] * PEER_NKEYS + i2[:nb[a]] for a in range(PEER_TOPK)]
                                   + [jnp.zeros((pad, tm), jnp.int32)], axis=0)
        top_s, expert = _top_k_rows(cand, PEER_TOPK, payload=cand_idx)
        e = jnp.exp(top_s - top_s[0:1])
        gate_rows.append(e / jnp.sum(e, axis=0, keepdims=True))
        idx_rows.append(expert)
    idx_ref[...] = jnp.concatenate(idx_rows, axis=0).T
    gates = jnp.concatenate(gate_rows, axis=0)
    gate_ref[...] = gates
    gate_tok_ref[...] = gates.T


def _merge_route(o_a, o_b, z2, x2, row0, b_merge, pa, pb, wo, nw, wq, k1, k2):
    t, d = o_a.shape[0], x2.shape[1]
    tm = MERGE_TM
    xblk0 = row0 // tm
    full = lambda a: pl.BlockSpec(a.shape, lambda i: (0,) * a.ndim)
    return pl.pallas_call(
        _merge_body,
        grid=(t // tm,),
        in_specs=[
            pl.BlockSpec((tm, A_WIDTH), lambda i: (i, 0)),
            pl.BlockSpec((tm, B_WIDTH), lambda i: (i, 0)),
            pl.BlockSpec((tm, 2 * d), lambda i: (i, Z_MERGE // (2 * d))),
            pl.BlockSpec((tm, d), lambda i: (i + xblk0, 0)),
            full(b_merge), full(pa), full(pb), full(wo), full(nw), full(wq), full(k1), full(k2),
        ],
        out_specs=[
            pl.BlockSpec((tm, d), lambda i: (i, 0)),
            pl.BlockSpec((tm * d // LANES, LANES), lambda i: (i, 0)),
            pl.BlockSpec((tm * d // LANES, LANES), lambda i: (i, 0)),
            pl.BlockSpec((tm, PEER_SLOTS), lambda i: (i, 0)),
            pl.BlockSpec((PEER_SLOTS, tm), lambda i: (0, i)),
            pl.BlockSpec((tm, PEER_SLOTS), lambda i: (i, 0)),
        ],
        out_shape=[
            jax.ShapeDtypeStruct((t, d), F32),
            jax.ShapeDtypeStruct((t * d // LANES, LANES), F32),
            jax.ShapeDtypeStruct((t * d // LANES, LANES), F32),
            jax.ShapeDtypeStruct((t, PEER_SLOTS), jnp.int32),
            jax.ShapeDtypeStruct((PEER_SLOTS, t), F32),
            jax.ShapeDtypeStruct((t, PEER_SLOTS), F32),
        ],
        compiler_params=pltpu.CompilerParams(
            dimension_semantics=("parallel",), vmem_limit_bytes=VMEM_LIMIT),
        name="merge_route",
    )(o_a, o_b, z2, x2, b_merge, pa, pb, wo, nw, wq, k1, k2)


PEER_TP = 128
PEER_NBUF = 4
PEER_LOOK = PEER_NBUF - 1
PEER_SC_SHARE_NUM, PEER_SC_SHARE_DEN = 18, 32
PEER_TC_FIRST_NUM, PEER_TC_FIRST_DEN = 7, 16


def _erf(x):
    return lax.erf(x)


PACK_HI = -65536


def _packed_down(bitcast, word):
    return bitcast(word & PACK_HI, F32)


def _packed_up(bitcast, word):
    return bitcast(word << 16, F32)


def _peer_body(idx_ref, idx_next_ref, xn_ref, gate_ref, x1_ref, tab_ref, o_ref, *scratch):
    bufs, sem = scratch[:PEER_NBUF], scratch[PEER_NBUF]
    planes = x1_ref.shape[1] // LANES
    step = pl.program_id(0)

    def issue(ref, tok, slot):
        for j in range(PEER_SLOTS):
            pltpu.make_async_copy(tab_ref.at[ref[0, tok, j]], bufs[slot].at[:, j, :], sem.at[slot]).start()

    def wait(slot):
        pltpu.make_async_copy(bufs[slot], bufs[slot], sem.at[slot]).wait()

    @pl.when(step == 0)
    def _():
        for s in range(PEER_LOOK):
            issue(idx_ref, s, s)

    lane = lax.broadcasted_iota(jnp.int32, (PEER_SLOTS, PEER_TP), 1)

    def compute(tok, slot):
        buf = bufs[slot]
        acc = None
        for s in range(planes):
            term = _packed_down(lax.bitcast_convert_type, buf[s]) * xn_ref[pl.ds(tok * planes + s, 1), :]
            acc = term if acc is None else acc + term
        act = jnp.sum(acc, axis=-1, keepdims=True)
        gate = jnp.sum(jnp.where(lane == tok, gate_ref[...], 0.0), axis=-1, keepdims=True)
        w = gate * (0.5 * act * (1.0 + _erf(act * (2.0 ** -0.5))))
        out = jnp.concatenate(
            [jnp.sum(_packed_up(lax.bitcast_convert_type, buf[s]) * w, axis=0, keepdims=True)
             for s in range(planes)], axis=1)
        o_ref[pl.ds(tok, 1), :] = x1_ref[pl.ds(tok, 1), :] + out

    def group(g, carry):
        for u in range(PEER_NBUF):
            tok = g * PEER_NBUF + u
            issue(idx_ref, tok + PEER_LOOK, (u + PEER_LOOK) % PEER_NBUF)
            wait(u)
            compute(tok, u)
        return carry

    n_groups = PEER_TP // PEER_NBUF
    lax.fori_loop(0, n_groups - 1, group, 0)

    for u in range(PEER_NBUF):
        tok = (n_groups - 1) * PEER_NBUF + u
        nxt = tok + PEER_LOOK
        if nxt < PEER_TP:
            issue(idx_ref, nxt, (u + PEER_LOOK) % PEER_NBUF)
        else:
            issue(idx_next_ref, nxt - PEER_TP, (u + PEER_LOOK) % PEER_NBUF)
        wait(u)
        compute(tok, u)

    @pl.when(step == pl.num_programs(0) - 1)
    def _():
        for s in range(PEER_LOOK):
            wait(s)


def _peer_alias_body(prev_ref, after_ref, *rest):
    del prev_ref, after_ref
    _peer_body(*rest)


def _peer(idx3, xnp, gate_t, x1, table, blk0, nb, out_blk0, t_out, prev=None, after=None):
    d = x1.shape[1]
    tp = PEER_TP
    planes = d // LANES
    in_specs = [
        pl.BlockSpec((1, tp, PEER_SLOTS), lambda i: (i + blk0, 0, 0), memory_space=pltpu.SMEM),
        pl.BlockSpec((1, tp, PEER_SLOTS), lambda i: (jnp.minimum(i + 1, nb - 1) + blk0, 0, 0),
                     memory_space=pltpu.SMEM),
        pl.BlockSpec((tp * planes, LANES), lambda i: (i + blk0, 0)),
        pl.BlockSpec((PEER_SLOTS, tp), lambda i: (0, i + blk0)),
        pl.BlockSpec((tp, d), lambda i: (i + blk0, 0)),
        pl.BlockSpec(memory_space=pl.ANY),
    ]
    args = (idx3, idx3, xnp, gate_t, x1, table)
    body, aliases = _peer_body, {}
    if prev is not None:
        body, aliases = _peer_alias_body, {0: 0}
        in_specs = [pl.BlockSpec(memory_space=pl.ANY), pl.BlockSpec(memory_space=pl.ANY)] + in_specs
        args = (prev, after) + args
    return pl.pallas_call(
        body,
        grid=(nb,),
        in_specs=in_specs,
        out_specs=pl.BlockSpec((tp, d), lambda i: (i + blk0 + out_blk0, 0)),
        out_shape=jax.ShapeDtypeStruct((t_out, d), F32),
        scratch_shapes=[pltpu.VMEM((planes, PEER_SLOTS, LANES), jnp.int32) for _ in range(PEER_NBUF)]
        + [pltpu.SemaphoreType.DMA((PEER_NBUF,))],
        input_output_aliases=aliases,
        compiler_params=pltpu.CompilerParams(
            dimension_semantics=("arbitrary",), vmem_limit_bytes=VMEM_LIMIT),
        name="peer",
    )(*args)


SC_CORES = 2
SC_SUBCORES = 16
SC_LANES = 16
SC_WORKERS = SC_CORES * SC_SUBCORES
SC_ROWS = 32
SC_RB = 4


SC_VPR = LANES // SC_LANES
SC_VROWS = PEER_SLOTS // SC_VPR


def _sc_vec(j):
    return (j // SC_VPR, pl.ds((j % SC_VPR) * SC_LANES, SC_LANES))


def _sc_mesh():
    return plsc.VectorSubcoreMesh(core_axis_name="c", subcore_axis_name="s")


def _sc_worker():
    return lax.axis_index("s") * SC_CORES + lax.axis_index("c")


def _sc_pipeline(tw, base, nchunk, loads, gather, compute, finish):
    assert tw % 2 == 0 and nchunk % 2 == 0
    last = base + tw - 1

    def token(tok, s):
        nxt = jnp.minimum(tok + 1, last)
        for c in loads(nxt, 1 - s):
            c.start()
        for kc in range(nchunk):
            if kc + 1 < nchunk:
                gather(s, kc + 1).start()
            else:
                for c in loads(nxt, 1 - s):
                    c.wait()
                gather(1 - s, 0).start()
            gather(s, kc).wait()
            compute(s, kc)
        finish(tok, s)

    for c in loads(base, 0):
        c.start()
    for c in loads(base, 0):
        c.wait()
    gather(0, 0).start()

    def pair(i, carry):
        token(base + 2 * i, 0)
        token(base + 2 * i + 1, 1)
        return carry

    lax.fori_loop(0, tw // 2, pair, 0)
    gather(0, 0).wait()


def _sc_down(idx_flat, xn3, table, tok0, n_tok):
    planes = xn3.shape[1]
    tw = n_tok // SC_WORKERS
    nchunk = PEER_SLOTS // SC_ROWS
    per_plane = LANES // SC_LANES

    @functools.partial(
        pl.kernel, mesh=_sc_mesh(),
        out_type=jax.ShapeDtypeStruct((n_tok, SC_VROWS, LANES), F32),
        scratch_types=[
            pltpu.VMEM((2, PEER_SLOTS), jnp.int32),
            pltpu.VMEM((2, planes, LANES), F32),
            pltpu.VMEM((2, SC_ROWS, planes, LANES), jnp.int32),
            pltpu.VMEM((SC_VROWS, LANES), F32),
            pltpu.VMEM((planes, SC_VROWS, LANES), F32),
            pltpu.SemaphoreType.DMA((2,)),
            pltpu.SemaphoreType.DMA((2,)),
        ],
        compiler_params=pltpu.CompilerParams(needs_layout_passes=False),
        name="sc_down",
    )
    def run(idx_hbm, xn_hbm, tab_hbm, out_hbm, idx_v, x_v, buf, act_v, part_v, sem_g, sem_l):
        def loads(tok, s):
            return [pltpu.make_async_copy(idx_hbm.at[pl.ds(tok * PEER_SLOTS, PEER_SLOTS)], idx_v.at[s], sem_l.at[s]),
                    pltpu.make_async_copy(xn_hbm.at[tok], x_v.at[s], sem_l.at[s])]

        def gather(s, kc):
            return pltpu.make_async_copy(
                tab_hbm.at[idx_v.at[s, pl.ds(kc * SC_ROWS, SC_ROWS)]], buf.at[kc % 2], sem_g.at[kc % 2])

        def compute(s, kc):
            @plsc.parallel_loop(0, planes)
            def _(p):
                sls = [pl.ds(q * SC_LANES, SC_LANES) for q in range(per_plane)]
                xs = [x_v[s, p, sl] for sl in sls]
                for j0 in range(0, SC_ROWS, SC_RB):
                    terms = [[_packed_down(plsc.bitcast, buf[kc % 2, j0 + r, p, sls[q]]) * xs[q]
                              for q in range(per_plane)] for r in range(SC_RB)]
                    while len(terms[0]) > 1:
                        terms = [[row[i] + row[i + 1] for i in range(0, len(row), 2)] for row in terms]
                    for r in range(SC_RB):
                        part_v[(p,) + _sc_vec(kc * SC_ROWS + j0 + r)] = terms[r][0]

        def finish(tok, s):
            @plsc.parallel_loop(0, SC_VROWS)
            def _(r):
                for u in range(SC_VPR):
                    sl = pl.ds(u * SC_LANES, SC_LANES)
                    acc = part_v[0, r, sl]
                    for p in range(1, planes):
                        acc = acc + part_v[p, r, sl]
                    act_v[r, sl] = acc
            pltpu.sync_copy(act_v, out_hbm.at[tok - tok0])

        _sc_pipeline(tw, tok0 + _sc_worker() * tw, nchunk, loads, gather, compute, finish)

    return run(idx_flat, xn3, table)


def _sc_up(idx_flat, w16, x13, table, tok0, n_tok):
    planes = x13.shape[1]
    tw = n_tok // SC_WORKERS
    nchunk = PEER_SLOTS // SC_ROWS
    per_plane = LANES // SC_LANES

    @functools.partial(
        pl.kernel, mesh=_sc_mesh(),
        out_type=jax.ShapeDtypeStruct((n_tok, planes, LANES), F32),
        scratch_types=[
            pltpu.VMEM((2, PEER_SLOTS), jnp.int32),
            pltpu.VMEM((2, SC_VROWS, LANES), F32),
            pltpu.VMEM((2, planes, LANES), F32),
            pltpu.VMEM((2, SC_ROWS, planes, LANES), jnp.int32),
            pltpu.SemaphoreType.DMA((2,)),
            pltpu.SemaphoreType.DMA((2,)),
        ],
        compiler_params=pltpu.CompilerParams(needs_layout_passes=False),
        name="sc_up",
    )
    def run(idx_hbm, w_hbm, x1_hbm, tab_hbm, out_hbm, idx_v, w_v, out_v, buf, sem_g, sem_l):
        def loads(tok, s):
            return [pltpu.make_async_copy(idx_hbm.at[pl.ds(tok * PEER_SLOTS, PEER_SLOTS)], idx_v.at[s], sem_l.at[s]),
                    pltpu.make_async_copy(w_hbm.at[tok - tok0], w_v.at[s], sem_l.at[s]),
                    pltpu.make_async_copy(x1_hbm.at[tok], out_v.at[s], sem_l.at[s])]

        def gather(s, kc):
            return pltpu.make_async_copy(
                tab_hbm.at[idx_v.at[s, pl.ds(kc * SC_ROWS, SC_ROWS)]], buf.at[kc % 2], sem_g.at[kc % 2])

        def compute(s, kc):
            @plsc.parallel_loop(0, planes)
            def _(p):
                sls = [pl.ds(q * SC_LANES, SC_LANES) for q in range(per_plane)]
                accs = [out_v[s, p, sl] for sl in sls]
                for j in range(SC_ROWS):
                    wj = w_v[(s,) + _sc_vec(kc * SC_ROWS + j)]
                    accs = [a + wj * _packed_up(plsc.bitcast, buf[kc % 2, j, p, sl]) for a, sl in zip(accs, sls)]
                for a, sl in zip(accs, sls):
                    out_v[s, p, sl] = a

        def finish(tok, s):
            pltpu.sync_copy(out_v.at[s], out_hbm.at[tok - tok0])

        _sc_pipeline(tw, tok0 + _sc_worker() * tw, nchunk, loads, gather, compute, finish)

    return run(idx_flat, w16, x13, table)


ACT_TM = 256


def _peer_act_body(after_ref, part_ref, gate_ref, fold_ref, spread_ref, own_ref, w_ref):
    del after_ref
    tm = part_ref.shape[0]
    own = own_ref[...]
    hi, lo = _split_bf16(part_ref[...].reshape(tm * SC_VROWS, LANES))
    sums = (_dot(hi, fold_ref[...]) + _dot(lo, fold_ref[...])).reshape(tm, SC_VROWS, LANES)
    act = jnp.sum(sums * own, axis=1)
    w = gate_ref[...] * (0.5 * act * (1.0 + _erf(act * (2.0 ** -0.5))))
    hi, lo = _split_bf16((w[:, None, :] * own).reshape(tm * SC_VROWS, LANES))
    w_ref[...] = (_dot(hi, spread_ref[...]) + _dot(lo, spread_ref[...])).reshape(tm, SC_VROWS, LANES)


def _peer_act(after, part3, gate_tok, blk0):
    n_tok = part3.shape[0]
    tm = ACT_TM
    lane = jnp.arange(LANES)
    fold = (lane[:, None] // SC_LANES == lane[None, :] % SC_VPR).astype(BF16)
    spread = fold.T
    own = (lane[None, :] // SC_VPR == jnp.arange(SC_VROWS)[:, None]).astype(F32)
    const = lambda a: pl.BlockSpec(a.shape, lambda i: (0,) * a.ndim)
    return pl.pallas_call(
        _peer_act_body,
        grid=(n_tok // tm,),
        in_specs=[
            pl.BlockSpec(memory_space=pl.ANY),
            pl.BlockSpec((tm, SC_VROWS, LANES), lambda i: (i, 0, 0)),
            pl.BlockSpec((tm, PEER_SLOTS), lambda i: (i + blk0, 0)),
            const(fold), const(spread), const(own),
        ],
        out_specs=pl.BlockSpec((tm, SC_VROWS, LANES), lambda i: (i, 0, 0)),
        out_shape=jax.ShapeDtypeStruct(part3.shape, F32),
        compiler_params=pltpu.CompilerParams(
            dimension_semantics=("parallel",), vmem_limit_bytes=VMEM_LIMIT),
        name="peer_act",
    )(after, part3, gate_tok, fold, spread, own)


def _table_pack_body(down_ref, up_ref, tab_ref):
    te, d = down_ref.shape
    planes = d // LANES
    for p in range(planes):
        cols = slice(p * LANES, (p + 1) * LANES)
        hi = lax.bitcast_convert_type(down_ref[:, cols].astype(BF16).astype(F32), jnp.int32)
        lo = lax.bitcast_convert_type(up_ref[:, cols].astype(BF16).astype(F32), jnp.int32)
        tab_ref[pl.ds(p, te, stride=planes), :] = hi | lax.shift_right_logical(lo, 16)


def _table_pack(down, up, te=512):
    n_exp, d = down.shape
    rows = d // LANES
    return pl.pallas_call(
        _table_pack_body,
        grid=(n_exp // te,),
        in_specs=[pl.BlockSpec((te, d), lambda i: (i, 0)), pl.BlockSpec((te, d), lambda i: (i, 0))],
        out_specs=pl.BlockSpec((te * rows, LANES), lambda i: (i, 0)),
        out_shape=jax.ShapeDtypeStruct((n_exp * rows, LANES), jnp.int32),
        compiler_params=pltpu.CompilerParams(
            dimension_semantics=("parallel",), vmem_limit_bytes=VMEM_LIMIT),
        name="table_pack",
    )(down, up)


def _pad_lanes(a):
    return jnp.pad(a, ((0, 0), (0, LANES - a.shape[1])))


def kernel(x, norm_mix_w, w_in, a_q_norm_w, a_k_norm_w, a_rel_bias, b_conv_w, b_a_log, b_dt_bias, b_norm_w, b_merge, w_proj_a, w_proj_b, w_out, norm_ffn_w, peer_w_query, peer_keys_1, peer_keys_2, peer_down, peer_up):
    b, s, d = x.shape
    t = b * s
    depth = w_in.shape[0]
    for l in range(depth):
        x2 = x.reshape(t, d)

        wi = w_in[l]
        o_qkvb = 3 * A_WIDTH
        o_a = o_qkvb + 3 * B_WIDTH
        o_beta = o_a + B_HEADS
        o_gate = o_beta + B_HEADS
        o_merge = o_gate + B_WIDTH
        w_main = jnp.concatenate(
            [wi[:, o_qkvb:o_a], wi[:, o_gate:o_merge], wi[:, o_merge:], wi[:, :o_qkvb]], axis=1).astype(BF16)
        w_small = _pad_lanes(wi[:, o_a:o_gate]).astype(BF16)
        z_attn = Z_MERGE + 2 * d

        bias_tab = _attn_bias_table(a_rel_bias[l])
        seg = jnp.kron(jnp.eye(A_HEADS, dtype=F32), jnp.full((A_HEAD_DIM, A_HEAD_DIM), 1.0 / A_HEAD_DIM, F32))
        head_params = jnp.zeros((SUBLANES, LANES), F32)
        head_params = head_params.at[0, :B_HEADS].set(b_a_log[l]).at[1, :B_HEADS].set(b_dt_bias[l])
        head_params = head_params.at[2, :B_HEADS].set(1.0)

        def mixer_and_route(b0, nb):
            tg = nb * s
            z, zs = _in_proj(x2, norm_mix_w[l][None, :], w_main, w_small, min(1024, tg), 1920, b0 * s, tg)
            z3 = z.reshape(nb, s, z.shape[1])
            o_att = _band_attention(
                z3, bias_tab, jnp.tile(a_q_norm_w[l], A_HEADS)[None, :], jnp.tile(a_k_norm_w[l], A_HEADS)[None, :],
                seg.astype(BF16), z_attn // A_WIDTH)
            o_gdn = _gated_deltanet(z3, zs.reshape(nb, s, LANES), b_conv_w[l], head_params, b_norm_w[l][None, :])
            return _merge_route(
                o_att.reshape(tg, A_WIDTH), o_gdn.reshape(tg, B_WIDTH), z, x2, b0 * s, b_merge[l][None, :],
                w_proj_a[l].astype(BF16), w_proj_b[l].astype(BF16), w_out[l].astype(BF16),
                norm_ffn_w[l][None, :], peer_w_query[l].astype(BF16), peer_keys_1[l].astype(BF16),
                peer_keys_2[l].astype(BF16))

        planes = d // LANES
        n_exp = peer_down.shape[1]
        table = _table_pack(peer_down[l], peer_up[l]).reshape(n_exp, planes, LANES)

        b_sc = b * PEER_SC_SHARE_NUM // PEER_SC_SHARE_DEN
        n_sc = b_sc * s
        _, x1p_s, xnp_s, idx_s, _, gate_tok_s = mixer_and_route(0, b_sc)
        idx_s = idx_s.reshape(-1)
        part = _sc_down(idx_s, xnp_s.reshape(n_sc, planes, LANES), table, 0, n_sc)

        x1, _, xnp, idx, gate_t, _ = mixer_and_route(b_sc, b - b_sc)
        nblk = (t - n_sc) // PEER_TP
        nblk_a = nblk * PEER_TC_FIRST_NUM // PEER_TC_FIRST_DEN
        idx3 = idx.reshape(nblk, PEER_TP, PEER_SLOTS)
        out = _peer(idx3, xnp, gate_t, x1, table, 0, nblk_a, n_sc // PEER_TP, t)
        w16 = _peer_act(out, part, gate_tok_s, 0)
        out = _peer(idx3, xnp, gate_t, x1, table, nblk_a, nblk - nblk_a, n_sc // PEER_TP, t, prev=out, after=w16)
        sc_out = _sc_up(idx_s, w16, x1p_s.reshape(n_sc, planes, LANES), table, 0, n_sc)
        out = lax.dynamic_update_slice(out, sc_out.reshape(n_sc, d), (0, 0))
        x = out.reshape(b, s, d)
    return x
```

```python
import functools

import jax
import jax.numpy as jnp
from jax import lax
from jax.experimental import pallas as pl
from jax.experimental.pallas import tpu as pltpu
from jax.experimental.pallas import tpu_sc as plsc

F32 = jnp.float32
BF16 = jnp.bfloat16

EPS = 1e-6
NEG = -1e30

CHUNK = 64
A_HEADS = 8
A_HEAD_DIM = 64
A_WIDTH = A_HEADS * A_HEAD_DIM
A_LEFT_CHUNKS = 8
REL_CLIP = 128
B_HEADS = 8
B_HEAD_DIM = 128
B_WIDTH = B_HEADS * B_HEAD_DIM
CONV_WIDTH = 4
PEER_HEADS = 8
PEER_HALF = 128
PEER_NKEYS = 128
PEER_TOPK = 16
PEER_SLOTS = PEER_HEADS * PEER_TOPK

LANES = 128
SUBLANES = 8
VMEM_LIMIT = 56 * 1024 * 1024

Z_QKV_B = 0
Z_GATE_B = 3 * B_WIDTH
Z_MERGE = Z_GATE_B + B_WIDTH
Z_SMALL_A = 0
Z_SMALL_BETA = B_HEADS


def _dot(a, b):
    return jnp.dot(a, b, preferred_element_type=F32)


def _dot_nt(a, b):
    return lax.dot_general(a, b, (((1,), (1,)), ((), ())), preferred_element_type=F32)


def _dot_tn(a, b):
    return lax.dot_general(a, b, (((0,), (0,)), ((), ())), preferred_element_type=F32)


def _split_bf16(a):
    hi = a.astype(BF16)
    lo = (a - hi.astype(F32)).astype(BF16)
    return hi, lo


def _sigmoid(x):
    return 1.0 / (1.0 + jnp.exp(-x))


def _inproj_body(x_ref, nw_ref, w_ref, ws_ref, z_ref, zs_ref, h_scr):
    @pl.when(pl.program_id(1) == 0)
    def _():
        x = x_ref[...]
        ms = jnp.mean(x * x, axis=-1, keepdims=True)
        hb = (x * lax.rsqrt(ms + EPS) * nw_ref[...]).astype(BF16)
        h_scr[...] = hb
        zs_ref[...] = _dot(hb, ws_ref[...])

    z_ref[...] = _dot(h_scr[...], w_ref[...]).astype(z_ref.dtype)


def _in_proj(x2, norm_w, w_main, w_small, tm, tn, row0, t):
    d = x2.shape[1]
    n = w_main.shape[1]
    blk0 = row0 // tm
    return pl.pallas_call(
        _inproj_body,
        grid=(t // tm, n // tn),
        in_specs=[
            pl.BlockSpec((tm, d), lambda i, j: (i + blk0, 0)),
            pl.BlockSpec((1, d), lambda i, j: (0, 0)),
            pl.BlockSpec((d, tn), lambda i, j: (0, j)),
            pl.BlockSpec((d, LANES), lambda i, j: (0, 0)),
        ],
        out_specs=[
            pl.BlockSpec((tm, tn), lambda i, j: (i, j)),
            pl.BlockSpec((tm, LANES), lambda i, j: (i, 0)),
        ],
        out_shape=[
            jax.ShapeDtypeStruct((t, n), BF16),
            jax.ShapeDtypeStruct((t, LANES), F32),
        ],
        scratch_shapes=[pltpu.VMEM((tm, d), BF16)],
        compiler_params=pltpu.CompilerParams(
            dimension_semantics=("parallel", "arbitrary"), vmem_limit_bytes=VMEM_LIMIT),
        name="in_proj",
    )(x2, norm_w, w_main, w_small)


ATT_TQ = 256
ATT_WIN = ATT_TQ + A_LEFT_CHUNKS * CHUNK
ATT_NKB = ATT_WIN // ATT_TQ


def _attn_body(q_ref, k0_ref, k1_ref, k2_ref, v0_ref, v1_ref, v2_ref, bias_ref, qw_ref, kw_ref, seg_ref,
               o_ref):
    t = pl.program_id(1)
    seg = seg_ref[...]

    def head_rms(a, w):
        hi, lo = _split_bf16(a * a)
        ms = _dot(hi, seg) + _dot(lo, seg)
        return a * lax.rsqrt(ms + EPS) * w

    q = head_rms(q_ref[0].astype(F32), qw_ref[...]) * (A_HEAD_DIM ** -0.5)
    k = jnp.concatenate([k0_ref[0], k1_ref[0], k2_ref[0]], axis=0).astype(F32)
    k = head_rms(k, kw_ref[...]).astype(BF16)
    v = jnp.concatenate([v0_ref[0], v1_ref[0], v2_ref[0]], axis=0)

    kpos = t * ATT_TQ - A_LEFT_CHUNKS * CHUNK + lax.broadcasted_iota(jnp.int32, (1, ATT_WIN), 1)
    valid = kpos >= 0
    lane = lax.broadcasted_iota(jnp.int32, (1, LANES), 1)
    first = lane < A_HEAD_DIM

    for pair in range(A_HEADS // 2):
        cols = slice(pair * LANES, (pair + 1) * LANES)
        qp = q[:, cols]
        kp = k[:, cols]
        vp = v[:, cols]
        outs = []
        for half in range(2):
            h = 2 * pair + half
            qm = jnp.where(first if half == 0 else jnp.logical_not(first), qp, 0.0).astype(BF16)
            s = _dot_nt(qm, kp) + bias_ref[h]
            s = jnp.where(valid, s, NEG)
            mx = jnp.max(s, axis=-1, keepdims=True)
            p = jnp.exp(s - mx)
            den = jnp.sum(p, axis=-1, keepdims=True)
            outs.append(_dot(p.astype(BF16), vp) / den)
        o_ref[0, :, cols] = jnp.where(first, outs[0], outs[1]).astype(o_ref.dtype)


def _band_attention(z3, bias_tab, qw, kw, seg, col_q):
    b, s, _ = z3.shape
    nt = s // ATT_TQ

    def kv_spec(back, col):
        return pl.BlockSpec((1, ATT_TQ, A_WIDTH), lambda bi, ti: (bi, jnp.maximum(ti - back, 0), col))

    return pl.pallas_call(
        _attn_body,
        grid=(b, nt),
        in_specs=[
            pl.BlockSpec((1, ATT_TQ, A_WIDTH), lambda bi, ti: (bi, ti, col_q)),
            kv_spec(2, col_q + 1), kv_spec(1, col_q + 1), kv_spec(0, col_q + 1),
            kv_spec(2, col_q + 2), kv_spec(1, col_q + 2), kv_spec(0, col_q + 2),
            pl.BlockSpec((A_HEADS, ATT_TQ, ATT_WIN), lambda bi, ti: (0, 0, 0)),
            pl.BlockSpec((1, A_WIDTH), lambda bi, ti: (0, 0)),
            pl.BlockSpec((1, A_WIDTH), lambda bi, ti: (0, 0)),
            pl.BlockSpec((A_WIDTH, A_WIDTH), lambda bi, ti: (0, 0)),
        ],
        out_specs=pl.BlockSpec((1, ATT_TQ, A_WIDTH), lambda bi, ti: (bi, ti, 0)),
        out_shape=jax.ShapeDtypeStruct((b, s, A_WIDTH), BF16),
        compiler_params=pltpu.CompilerParams(
            dimension_semantics=("parallel", "parallel"), vmem_limit_bytes=VMEM_LIMIT),
        name="band_attn",
    )(z3, z3, z3, z3, z3, z3, z3, bias_tab, qw, kw, seg)


def _attn_bias_table(rel_bias):
    rb = rel_bias.astype(F32)
    heads = rb.shape[0]
    lo = A_LEFT_CHUNKS * CHUNK - (ATT_WIN - 1)
    hi = A_LEFT_CHUNKS * CHUNK + ATT_TQ - 1
    strip = jnp.concatenate([
        jnp.broadcast_to(rb[:, :1], (heads, -REL_CLIP - lo)), rb,
        jnp.broadcast_to(rb[:, -1:], (heads, hi - REL_CLIP))], axis=1)
    rev = strip[:, ::-1]
    n = rev.shape[1]
    flat = jnp.broadcast_to(rev[:, None, :], (heads, ATT_TQ, n)).reshape(heads, ATT_TQ * n)
    bias = flat[:, ATT_TQ - 1:ATT_TQ - 1 + ATT_TQ * (n - 1)].reshape(heads, ATT_TQ, n - 1)[:, :, :ATT_WIN]
    qc = jnp.arange(ATT_TQ)[:, None] // CHUNK
    kc = jnp.arange(ATT_WIN)[None, :] // CHUNK
    in_band = (kc >= qc) & (kc <= qc + A_LEFT_CHUNKS)
    return jnp.where(in_band[None], bias, NEG)


GDN_TAIL = SUBLANES


def _gdn_body(qkv_ref, zs_ref, gate_ref, convw_ref, hp_ref, nw_ref, o_ref, xbuf, s_scr):
    c = CHUNK
    hd = B_HEAD_DIM

    @pl.when(pl.program_id(1) == 0)
    def _():
        xbuf[0:GDN_TAIL, :] = jnp.zeros((GDN_TAIL, 3 * B_WIDTH), F32)
        s_scr[...] = jnp.zeros_like(s_scr)

    xbuf[GDN_TAIL:GDN_TAIL + c, :] = qkv_ref[0].astype(F32)
    y = None
    for j in range(CONV_WIDTH):
        tap = convw_ref[j:j + 1, :] * xbuf[pl.ds(GDN_TAIL - (CONV_WIDTH - 1) + j, c), :]
        y = tap if y is None else y + tap
    xbuf[0:GDN_TAIL, :] = xbuf[c:c + GDN_TAIL, :]
    y = y * _sigmoid(y)

    zs = zs_ref[0]
    a_neg = -(jnp.exp(hp_ref[0:1, :]) * hp_ref[2:3, :])
    xs = zs + hp_ref[1:2, :]
    g_all = a_neg * (jnp.maximum(xs, 0.0) + jnp.log(1.0 + jnp.exp(-jnp.abs(xs))))
    beta_all = _sigmoid(zs)

    row = lax.broadcasted_iota(jnp.int32, (c, c), 0)
    col = lax.broadcasted_iota(jnp.int32, (c, c), 1)
    tril = row >= col
    tril_strict = row > col
    eye = (row == col).astype(F32)
    lower = tril.astype(BF16)
    upper = (row <= col).astype(BF16)

    gh, gl = _split_bf16(g_all)
    gc_all = _dot(lower, gh) + _dot(lower, gl)
    gth, gtl = _split_bf16(g_all.T)
    gc_t = _dot(gth, upper) + _dot(gtl, upper)

    heads = range(B_HEADS)
    q, k, v, beta, gc, gc_row, g_last = [], [], [], [], [], [], []
    for h in heads:
        qh = y[:, h * hd:(h + 1) * hd]
        kh = y[:, B_WIDTH + h * hd:B_WIDTH + (h + 1) * hd]
        q.append(qh * lax.rsqrt(jnp.sum(qh * qh, axis=-1, keepdims=True) + EPS) * (hd ** -0.5))
        k.append(kh * lax.rsqrt(jnp.sum(kh * kh, axis=-1, keepdims=True) + EPS))
        v.append(y[:, 2 * B_WIDTH + h * hd:2 * B_WIDTH + (h + 1) * hd])
        beta.append(beta_all[:, Z_SMALL_BETA + h:Z_SMALL_BETA + h + 1])
        gc.append(gc_all[:, Z_SMALL_A + h:Z_SMALL_A + h + 1])
        gc_row.append(gc_t[Z_SMALL_A + h:Z_SMALL_A + h + 1, :])
        g_last.append(gc_row[h][:, c - 1:c])
    decay = [jnp.exp(jnp.where(tril, gc[h] - gc_row[h], NEG)) for h in heads]
    e_gc = [jnp.exp(gc[h]) for h in heads]
    kb = [k[h] * beta[h] for h in heads]
    kf = [k[h].astype(BF16) for h in heads]
    a_pow = [jnp.where(tril_strict, _dot_nt(kb[h].astype(BF16), kf[h]) * decay[h], 0.0) for h in heads]
    attn = [(_dot_nt(q[h].astype(BF16), kf[h]) * decay[h]).astype(BF16) for h in heads]

    t_mat = [eye - a_pow[h] for h in heads]
    for _ in range(5):
        ab = [a_pow[h].astype(BF16) for h in heads]
        a_pow = [_dot(ab[h], ab[h]) for h in heads]
        t_mat = [t_mat[h] + _dot(t_mat[h].astype(BF16), a_pow[h].astype(BF16)) for h in heads]
    tb = [t_mat[h].astype(BF16) for h in heads]
    u = [_dot(tb[h], (v[h] * beta[h]).astype(BF16)) for h in heads]
    w = [_dot(tb[h], (kb[h] * e_gc[h]).astype(BF16)).astype(BF16) for h in heads]

    state = [s_scr[h] for h in heads]
    sb = [state[h].astype(BF16) for h in heads]
    vnb = [(u[h] - _dot(w[h], sb[h])).astype(BF16) for h in heads]
    o = [_dot((q[h] * e_gc[h]).astype(BF16), sb[h]) + _dot(attn[h], vnb[h]) for h in heads]
    k_dec = [(k[h] * jnp.exp(g_last[h] - gc[h])).astype(BF16) for h in heads]
    for h in heads:
        s_scr[h] = state[h] * jnp.exp(g_last[h]) + _dot_tn(k_dec[h], vnb[h])
    for h in heads:
        cols = slice(h * hd, (h + 1) * hd)
        gate = gate_ref[0, :, cols].astype(F32)
        on = o[h] * lax.rsqrt(jnp.mean(o[h] * o[h], axis=-1, keepdims=True) + EPS) * nw_ref[...]
        o_ref[0, :, cols] = (on * (gate * _sigmoid(gate))).astype(o_ref.dtype)


def _gated_deltanet(z3, zs3, conv_w, head_params, norm_w):
    b, s, _ = z3.shape
    n = s // CHUNK
    return pl.pallas_call(
        _gdn_body,
        grid=(b, n),
        in_specs=[
            pl.BlockSpec((1, CHUNK, 3 * B_WIDTH), lambda bi, ni: (bi, ni, Z_QKV_B // (3 * B_WIDTH))),
            pl.BlockSpec((1, CHUNK, LANES), lambda bi, ni: (bi, ni, 0)),
            pl.BlockSpec((1, CHUNK, B_WIDTH), lambda bi, ni: (bi, ni, Z_GATE_B // B_WIDTH)),
            pl.BlockSpec((CONV_WIDTH, 3 * B_WIDTH), lambda bi, ni: (0, 0)),
            pl.BlockSpec((SUBLANES, LANES), lambda bi, ni: (0, 0)),
            pl.BlockSpec((1, B_HEAD_DIM), lambda bi, ni: (0, 0)),
        ],
        out_specs=pl.BlockSpec((1, CHUNK, B_WIDTH), lambda bi, ni: (bi, ni, 0)),
        out_shape=jax.ShapeDtypeStruct((b, s, B_WIDTH), BF16),
        scratch_shapes=[
            pltpu.VMEM((GDN_TAIL + CHUNK, 3 * B_WIDTH), F32),
            pltpu.VMEM((B_HEADS, B_HEAD_DIM, B_HEAD_DIM), F32),
        ],
        compiler_params=pltpu.CompilerParams(
            dimension_semantics=("parallel", "arbitrary"), vmem_limit_bytes=VMEM_LIMIT),
        name="gdn",
    )(z3, zs3, z3, conv_w, head_params, norm_w)


MERGE_TM = 256


def _top_k_rows(s, k, payload=None):
    n = s.shape[0]
    rows = lax.broadcasted_iota(jnp.int32, s.shape, 0)
    vals, picks = [], []
    for _ in range(k):
        m = jnp.max(s, axis=0, keepdims=True)
        i = jnp.min(jnp.where(s == m, rows, n), axis=0, keepdims=True)
        hit = rows == i
        vals.append(m)
        picks.append(i if payload is None else jnp.max(jnp.where(hit, payload, -1), axis=0, keepdims=True))
        s = jnp.where(hit, -jnp.inf, s)
    return jnp.concatenate(vals, axis=0), jnp.concatenate(picks, axis=0)


def _merge_body(oa_ref, ob_ref, mr_ref, x_ref, bm_ref, pa_ref, pb_ref, wo_ref, nw_ref, wq_ref, k1_ref, k2_ref,
                x1_ref, x1p_ref, xnp_ref, idx_ref, gate_ref, gate_tok_ref):
    tm, d = x_ref.shape
    planes = d // LANES

    def store_planes(ref, a):
        for p in range(planes):
            ref[pl.ds(p, tm, stride=planes), :] = a[:, p * LANES:(p + 1) * LANES]

    mr = mr_ref[...].astype(F32) + bm_ref[...]
    mixed = (_sigmoid(mr[:, :d]) * _dot(oa_ref[...], pa_ref[...])
             + _sigmoid(mr[:, d:]) * _dot(ob_ref[...], pb_ref[...]))
    x1 = x_ref[...] + _dot(mixed.astype(BF16), wo_ref[...])
    x1_ref[...] = x1
    store_planes(x1p_ref, x1)
    xn = x1 * lax.rsqrt(jnp.mean(x1 * x1, axis=-1, keepdims=True) + EPS) * nw_ref[...]
    store_planes(xnp_ref, xn)
    q = _dot(xn.astype(BF16), wq_ref[...]).astype(BF16)

    idx_rows, gate_rows = [], []
    for h in range(PEER_HEADS):
        q1 = q[:, (2 * h) * PEER_HALF:(2 * h + 1) * PEER_HALF]
        q2 = q[:, (2 * h + 1) * PEER_HALF:(2 * h + 2) * PEER_HALF]
        s1 = _dot_nt(k1_ref[h], q1)
        s2 = _dot_nt(k2_ref[h], q2)
        v1, i1 = _top_k_rows(s1, PEER_TOPK)
        v2, i2 = _top_k_rows(s2, PEER_TOPK)
        nb = [PEER_TOPK // (a + 1) for a in range(PEER_TOPK)]
        pad = -sum(nb) % SUBLANES
        tm = v1.shape[1]
        cand = jnp.concatenate([v1[a:a + 1] + v2[:nb[a]] for a in range(PEER_TOPK)]
                               + [jnp.full((pad, tm), -jnp.inf, F32)], axis=0)
        cand_idx = jnp.concatenate([i1[a:a + 1] * PEER_NKEYS + i2[:nb[a]] for a in range(PEER_TOPK)]
                                   + [jnp.zeros((pad, tm), jnp.int32)], axis=0)
        top_s, expert = _top_k_rows(cand, PEER_TOPK, payload=cand_idx)
        e = jnp.exp(top_s - top_s[0:1])
        gate_rows.append(e / jnp.sum(e, axis=0, keepdims=True))
        idx_rows.append(expert)
    idx_ref[...] = jnp.concatenate(idx_rows, axis=0).T
    gates = jnp.concatenate(gate_rows, axis=0)
    gate_ref[...] = gates
    gate_tok_ref[...] = gates.T


def _merge_route(o_a, o_b, z2, x2, row0, b_merge, pa, pb, wo, nw, wq, k1, k2):
    t, d = o_a.shape[0], x2.shape[1]
    tm = MERGE_TM
    xblk0 = row0 // tm
    full = lambda a: pl.BlockSpec(a.shape, lambda i: (0,) * a.ndim)
    return pl.pallas_call(
        _merge_body,
        grid=(t // tm,),
        in_specs=[
            pl.BlockSpec((tm, A_WIDTH), lambda i: (i, 0)),
            pl.BlockSpec((tm, B_WIDTH), lambda i: (i, 0)),
            pl.BlockSpec((tm, 2 * d), lambda i: (i, Z_MERGE // (2 * d))),
            pl.BlockSpec((tm, d), lambda i: (i + xblk0, 0)),
            full(b_merge), full(pa), full(pb), full(wo), full(nw), full(wq), full(k1), full(k2),
        ],
        out_specs=[
            pl.BlockSpec((tm, d), lambda i: (i, 0)),
            pl.BlockSpec((tm * d // LANES, LANES), lambda i: (i, 0)),
            pl.BlockSpec((tm * d // LANES, LANES), lambda i: (i, 0)),
            pl.BlockSpec((tm, PEER_SLOTS), lambda i: (i, 0)),
            pl.BlockSpec((PEER_SLOTS, tm), lambda i: (0, i)),
            pl.BlockSpec((tm, PEER_SLOTS), lambda i: (i, 0)),
        ],
        out_shape=[
            jax.ShapeDtypeStruct((t, d), F32),
            jax.ShapeDtypeStruct((t * d // LANES, LANES), F32),
            jax.ShapeDtypeStruct((t * d // LANES, LANES), F32),
            jax.ShapeDtypeStruct((t, PEER_SLOTS), jnp.int32),
            jax.ShapeDtypeStruct((PEER_SLOTS, t), F32),
            jax.ShapeDtypeStruct((t, PEER_SLOTS), F32),
        ],
        compiler_params=pltpu.CompilerParams(
            dimension_semantics=("parallel",), vmem_limit_bytes=VMEM_LIMIT),
        name="merge_route",
    )(o_a, o_b, z2, x2, b_merge, pa, pb, wo, nw, wq, k1, k2)


PEER_TP = 128
PEER_NBUF = 4
PEER_LOOK = PEER_NBUF - 1
PEER_SC_SHARE_NUM, PEER_SC_SHARE_DEN = 18, 32
PEER_TC_FIRST_NUM, PEER_TC_FIRST_DEN = 4, 16


def _erf(x):
    return lax.erf(x)


PACK_HI = -65536


def _packed_down(bitcast, word):
    return bitcast(word & PACK_HI, F32)


def _packed_up(bitcast, word):
    return bitcast(word << 16, F32)


def _peer_body(idx_ref, idx_next_ref, xn_ref, gate_ref, x1_ref, tab_ref, o_ref, *scratch):
    bufs, sem = scratch[:PEER_NBUF], scratch[PEER_NBUF]
    planes = x1_ref.shape[1] // LANES
    step = pl.program_id(0)

    def issue(ref, tok, slot):
        for j in range(PEER_SLOTS):
            pltpu.make_async_copy(tab_ref.at[ref[0, tok, j]], bufs[slot].at[:, j, :], sem.at[slot]).start()

    def wait(slot):
        pltpu.make_async_copy(bufs[slot], bufs[slot], sem.at[slot]).wait()

    @pl.when(step == 0)
    def _():
        for s in range(PEER_LOOK):
            issue(idx_ref, s, s)

    lane = lax.broadcasted_iota(jnp.int32, (PEER_SLOTS, PEER_TP), 1)

    def compute(tok, slot):
        buf = bufs[slot]
        acc = None
        for s in range(planes):
            term = _packed_down(lax.bitcast_convert_type, buf[s]) * xn_ref[pl.ds(tok * planes + s, 1), :]
            acc = term if acc is None else acc + term
        act = jnp.sum(acc, axis=-1, keepdims=True)
        gate = jnp.sum(jnp.where(lane == tok, gate_ref[...], 0.0), axis=-1, keepdims=True)
        w = gate * (0.5 * act * (1.0 + _erf(act * (2.0 ** -0.5))))
        out = jnp.concatenate(
            [jnp.sum(_packed_up(lax.bitcast_convert_type, buf[s]) * w, axis=0, keepdims=True)
             for s in range(planes)], axis=1)
        o_ref[pl.ds(tok, 1), :] = x1_ref[pl.ds(tok, 1), :] + out

    def group(g, carry):
        for u in range(PEER_NBUF):
            tok = g * PEER_NBUF + u
            issue(idx_ref, tok + PEER_LOOK, (u + PEER_LOOK) % PEER_NBUF)
            wait(u)
            compute(tok, u)
        return carry

    n_groups = PEER_TP // PEER_NBUF
    lax.fori_loop(0, n_groups - 1, group, 0)

    for u in range(PEER_NBUF):
        tok = (n_groups - 1) * PEER_NBUF + u
        nxt = tok + PEER_LOOK
        if nxt < PEER_TP:
            issue(idx_ref, nxt, (u + PEER_LOOK) % PEER_NBUF)
        else:
            issue(idx_next_ref, nxt - PEER_TP, (u + PEER_LOOK) % PEER_NBUF)
        wait(u)
        compute(tok, u)

    @pl.when(step == pl.num_programs(0) - 1)
    def _():
        for s in range(PEER_LOOK):
            wait(s)


def _peer_alias_body(prev_ref, after_ref, *rest):
    del prev_ref, after_ref
    _peer_body(*rest)


def _peer(idx3, xnp, gate_t, x1, table, blk0, nb, out_blk0, t_out, prev=None, after=None):
    d = x1.shape[1]
    tp = PEER_TP
    planes = d // LANES
    in_specs = [
        pl.BlockSpec((1, tp, PEER_SLOTS), lambda i: (i + blk0, 0, 0), memory_space=pltpu.SMEM),
        pl.BlockSpec((1, tp, PEER_SLOTS), lambda i: (jnp.minimum(i + 1, nb - 1) + blk0, 0, 0),
                     memory_space=pltpu.SMEM),
        pl.BlockSpec((tp * planes, LANES), lambda i: (i + blk0, 0)),
        pl.BlockSpec((PEER_SLOTS, tp), lambda i: (0, i + blk0)),
        pl.BlockSpec((tp, d), lambda i: (i + blk0, 0)),
        pl.BlockSpec(memory_space=pl.ANY),
    ]
    args = (idx3, idx3, xnp, gate_t, x1, table)
    body, aliases = _peer_body, {}
    if prev is not None:
        body, aliases = _peer_alias_body, {0: 0}
        in_specs = [pl.BlockSpec(memory_space=pl.ANY), pl.BlockSpec(memory_space=pl.ANY)] + in_specs
        args = (prev, after) + args
    return pl.pallas_call(
        body,
        grid=(nb,),
        in_specs=in_specs,
        out_specs=pl.BlockSpec((tp, d), lambda i: (i + blk0 + out_blk0, 0)),
        out_shape=jax.ShapeDtypeStruct((t_out, d), F32),
        scratch_shapes=[pltpu.VMEM((planes, PEER_SLOTS, LANES), jnp.int32) for _ in range(PEER_NBUF)]
        + [pltpu.SemaphoreType.DMA((PEER_NBUF,))],
        input_output_aliases=aliases,
        compiler_params=pltpu.CompilerParams(
            dimension_semantics=("arbitrary",), vmem_limit_bytes=VMEM_LIMIT),
        name="peer",
    )(*args)


SC_CORES = 2
SC_SUBCORES = 16
SC_LANES = 16
SC_WORKERS = SC_CORES * SC_SUBCORES
SC_ROWS = 32
SC_RB = 2


SC_VPR = LANES // SC_LANES
SC_VROWS = PEER_SLOTS // SC_VPR


def _sc_vec(j):
    return (j // SC_VPR, pl.ds((j % SC_VPR) * SC_LANES, SC_LANES))


def _sc_mesh():
    return plsc.VectorSubcoreMesh(core_axis_name="c", subcore_axis_name="s")


def _sc_worker():
    return lax.axis_index("s") * SC_CORES + lax.axis_index("c")


def _sc_pipeline(tw, base, nchunk, loads, gather, compute, finish):
    assert tw % 2 == 0 and nchunk % 2 == 0
    last = base + tw - 1

    def token(tok, s):
        nxt = jnp.minimum(tok + 1, last)
        for c in loads(nxt, 1 - s):
            c.start()
        for kc in range(nchunk):
            if kc + 1 < nchunk:
                gather(s, kc + 1).start()
            else:
                for c in loads(nxt, 1 - s):
                    c.wait()
                gather(1 - s, 0).start()
            gather(s, kc).wait()
            compute(s, kc)
        finish(tok, s)

    for c in loads(base, 0):
        c.start()
    for c in loads(base, 0):
        c.wait()
    gather(0, 0).start()

    def pair(i, carry):
        token(base + 2 * i, 0)
        token(base + 2 * i + 1, 1)
        return carry

    lax.fori_loop(0, tw // 2, pair, 0)
    gather(0, 0).wait()


def _sc_down(idx_flat, xn3, table, tok0, n_tok):
    planes = xn3.shape[1]
    tw = n_tok // SC_WORKERS
    nchunk = PEER_SLOTS // SC_ROWS
    per_plane = LANES // SC_LANES

    @functools.partial(
        pl.kernel, mesh=_sc_mesh(),
        out_type=jax.ShapeDtypeStruct((n_tok, SC_VROWS, LANES), F32),
        scratch_types=[
            pltpu.VMEM((2, PEER_SLOTS), jnp.int32),
            pltpu.VMEM((2, planes, LANES), F32),
            pltpu.VMEM((2, SC_ROWS, planes, LANES), jnp.int32),
            pltpu.VMEM((SC_VROWS, LANES), F32),
            pltpu.VMEM((planes, SC_VROWS, LANES), F32),
            pltpu.SemaphoreType.DMA((2,)),
            pltpu.SemaphoreType.DMA((2,)),
        ],
        compiler_params=pltpu.CompilerParams(needs_layout_passes=False),
        name="sc_down",
    )
    def run(idx_hbm, xn_hbm, tab_hbm, out_hbm, idx_v, x_v, buf, act_v, part_v, sem_g, sem_l):
        def loads(tok, s):
            return [pltpu.make_async_copy(idx_hbm.at[pl.ds(tok * PEER_SLOTS, PEER_SLOTS)], idx_v.at[s], sem_l.at[s]),
                    pltpu.make_async_copy(xn_hbm.at[tok], x_v.at[s], sem_l.at[s])]

        def gather(s, kc):
            return pltpu.make_async_copy(
                tab_hbm.at[idx_v.at[s, pl.ds(kc * SC_ROWS, SC_ROWS)]], buf.at[kc % 2], sem_g.at[kc % 2])

        def compute(s, kc):
            @plsc.parallel_loop(0, planes)
            def _(p):
                sls = [pl.ds(q * SC_LANES, SC_LANES) for q in range(per_plane)]
                xs = [x_v[s, p, sl] for sl in sls]
                def load_rows(j0):
                    return [[buf[kc % 2, j0 + r, p, sls[q]] for q in range(per_plane)] for r in range(SC_RB)]

                words = load_rows(0)
                for j0 in range(0, SC_ROWS, SC_RB):
                    ahead = load_rows(j0 + SC_RB) if j0 + SC_RB < SC_ROWS else None
                    terms = [[_packed_down(plsc.bitcast, words[r][q]) * xs[q] for q in range(per_plane)]
                             for r in range(SC_RB)]
                    while len(terms[0]) > 1:
                        terms = [[row[i] + row[i + 1] for i in range(0, len(row), 2)] for row in terms]
                    for r in range(SC_RB):
                        part_v[(p,) + _sc_vec(kc * SC_ROWS + j0 + r)] = terms[r][0]
                    words = ahead

        def finish(tok, s):
            @plsc.parallel_loop(0, SC_VROWS)
            def _(r):
                for u in range(SC_VPR):
                    sl = pl.ds(u * SC_LANES, SC_LANES)
                    acc = part_v[0, r, sl]
                    for p in range(1, planes):
                        acc = acc + part_v[p, r, sl]
                    act_v[r, sl] = acc
            pltpu.sync_copy(act_v, out_hbm.at[tok - tok0])

        _sc_pipeline(tw, tok0 + _sc_worker() * tw, nchunk, loads, gather, compute, finish)

    return run(idx_flat, xn3, table)


def _sc_up(idx_flat, w16, x13, table, tok0, n_tok):
    planes = x13.shape[1]
    tw = n_tok // SC_WORKERS
    nchunk = PEER_SLOTS // SC_ROWS
    per_plane = LANES // SC_LANES

    @functools.partial(
        pl.kernel, mesh=_sc_mesh(),
        out_type=jax.ShapeDtypeStruct((n_tok, planes, LANES), F32),
        scratch_types=[
            pltpu.VMEM((2, PEER_SLOTS), jnp.int32),
            pltpu.VMEM((2, SC_VROWS, LANES), F32),
            pltpu.VMEM((2, planes, LANES), F32),
            pltpu.VMEM((2, SC_ROWS, planes, LANES), jnp.int32),
            pltpu.SemaphoreType.DMA((2,)),
            pltpu.SemaphoreType.DMA((2,)),
        ],
        compiler_params=pltpu.CompilerParams(needs_layout_passes=False),
        name="sc_up",
    )
    def run(idx_hbm, w_hbm, x1_hbm, tab_hbm, out_hbm, idx_v, w_v, out_v, buf, sem_g, sem_l):
        def loads(tok, s):
            return [pltpu.make_async_copy(idx_hbm.at[pl.ds(tok * PEER_SLOTS, PEER_SLOTS)], idx_v.at[s], sem_l.at[s]),
                    pltpu.make_async_copy(w_hbm.at[tok - tok0], w_v.at[s], sem_l.at[s]),
                    pltpu.make_async_copy(x1_hbm.at[tok], out_v.at[s], sem_l.at[s])]

        def gather(s, kc):
            return pltpu.make_async_copy(
                tab_hbm.at[idx_v.at[s, pl.ds(kc * SC_ROWS, SC_ROWS)]], buf.at[kc % 2], sem_g.at[kc % 2])

        def compute(s, kc):
            @plsc.parallel_loop(0, planes)
            def _(p):
                sls = [pl.ds(q * SC_LANES, SC_LANES) for q in range(per_plane)]
                accs = [out_v[s, p, sl] for sl in sls]
                for j in range(SC_ROWS):
                    wj = w_v[(s,) + _sc_vec(kc * SC_ROWS + j)]
                    accs = [a + wj * _packed_up(plsc.bitcast, buf[kc % 2, j, p, sl]) for a, sl in zip(accs, sls)]
                for a, sl in zip(accs, sls):
                    out_v[s, p, sl] = a

        def finish(tok, s):
            pltpu.sync_copy(out_v.at[s], out_hbm.at[tok - tok0])

        _sc_pipeline(tw, tok0 + _sc_worker() * tw, nchunk, loads, gather, compute, finish)

    return run(idx_flat, w16, x13, table)


ACT_TM = 256


def _peer_act_body(after_ref, part_ref, gate_ref, fold_ref, spread_ref, own_ref, w_ref):
    del after_ref
    tm = part_ref.shape[0]
    own = own_ref[...]
    hi, lo = _split_bf16(part_ref[...].reshape(tm * SC_VROWS, LANES))
    sums = (_dot(hi, fold_ref[...]) + _dot(lo, fold_ref[...])).reshape(tm, SC_VROWS, LANES)
    act = jnp.sum(sums * own, axis=1)
    w = gate_ref[...] * (0.5 * act * (1.0 + _erf(act * (2.0 ** -0.5))))
    hi, lo = _split_bf16((w[:, None, :] * own).reshape(tm * SC_VROWS, LANES))
    w_ref[...] = (_dot(hi, spread_ref[...]) + _dot(lo, spread_ref[...])).reshape(tm, SC_VROWS, LANES)


def _peer_act(after, part3, gate_tok, blk0):
    n_tok = part3.shape[0]
    tm = ACT_TM
    lane = jnp.arange(LANES)
    fold = (lane[:, None] // SC_LANES == lane[None, :] % SC_VPR).astype(BF16)
    spread = fold.T
    own = (lane[None, :] // SC_VPR == jnp.arange(SC_VROWS)[:, None]).astype(F32)
    const = lambda a: pl.BlockSpec(a.shape, lambda i: (0,) * a.ndim)
    return pl.pallas_call(
        _peer_act_body,
        grid=(n_tok // tm,),
        in_specs=[
            pl.BlockSpec(memory_space=pl.ANY),
            pl.BlockSpec((tm, SC_VROWS, LANES), lambda i: (i, 0, 0)),
            pl.BlockSpec((tm, PEER_SLOTS), lambda i: (i + blk0, 0)),
            const(fold), const(spread), const(own),
        ],
        out_specs=pl.BlockSpec((tm, SC_VROWS, LANES), lambda i: (i, 0, 0)),
        out_shape=jax.ShapeDtypeStruct(part3.shape, F32),
        compiler_params=pltpu.CompilerParams(
            dimension_semantics=("parallel",), vmem_limit_bytes=VMEM_LIMIT),
        name="peer_act",
    )(after, part3, gate_tok, fold, spread, own)


def _table_pack_body(down_ref, up_ref, tab_ref):
    te, d = down_ref.shape
    planes = d // LANES
    for p in range(planes):
        cols = slice(p * LANES, (p + 1) * LANES)
        hi = lax.bitcast_convert_type(down_ref[:, cols].astype(BF16).astype(F32), jnp.int32)
        lo = lax.bitcast_convert_type(up_ref[:, cols].astype(BF16).astype(F32), jnp.int32)
        tab_ref[pl.ds(p, te, stride=planes), :] = hi | lax.shift_right_logical(lo, 16)


def _table_pack(down, up, te=512):
    n_exp, d = down.shape
    rows = d // LANES
    return pl.pallas_call(
        _table_pack_body,
        grid=(n_exp // te,),
        in_specs=[pl.BlockSpec((te, d), lambda i: (i, 0)), pl.BlockSpec((te, d), lambda i: (i, 0))],
        out_specs=pl.BlockSpec((te * rows, LANES), lambda i: (i, 0)),
        out_shape=jax.ShapeDtypeStruct((n_exp * rows, LANES), jnp.int32),
        compiler_params=pltpu.CompilerParams(
            dimension_semantics=("parallel",), vmem_limit_bytes=VMEM_LIMIT),
        name="table_pack",
    )(down, up)


def _pad_lanes(a):
    return jnp.pad(a, ((0, 0), (0, LANES - a.shape[1])))


def kernel(x, norm_mix_w, w_in, a_q_norm_w, a_k_norm_w, a_rel_bias, b_conv_w, b_a_log, b_dt_bias, b_norm_w, b_merge, w_proj_a, w_proj_b, w_out, norm_ffn_w, peer_w_query, peer_keys_1, peer_keys_2, peer_down, peer_up):
    b, s, d = x.shape
    t = b * s
    depth = w_in.shape[0]
    for l in range(depth):
        x2 = x.reshape(t, d)

        wi = w_in[l]
        o_qkvb = 3 * A_WIDTH
        o_a = o_qkvb + 3 * B_WIDTH
        o_beta = o_a + B_HEADS
        o_gate = o_beta + B_HEADS
        o_merge = o_gate + B_WIDTH
        w_main = jnp.concatenate(
            [wi[:, o_qkvb:o_a], wi[:, o_gate:o_merge], wi[:, o_merge:], wi[:, :o_qkvb]], axis=1).astype(BF16)
        w_small = _pad_lanes(wi[:, o_a:o_gate]).astype(BF16)
        z_attn = Z_MERGE + 2 * d

        bias_tab = _attn_bias_table(a_rel_bias[l])
        seg = jnp.kron(jnp.eye(A_HEADS, dtype=F32), jnp.full((A_HEAD_DIM, A_HEAD_DIM), 1.0 / A_HEAD_DIM, F32))
        head_params = jnp.zeros((SUBLANES, LANES), F32)
        head_params = head_params.at[0, :B_HEADS].set(b_a_log[l]).at[1, :B_HEADS].set(b_dt_bias[l])
        head_params = head_params.at[2, :B_HEADS].set(1.0)

        def mixer_and_route(b0, nb):
            tg = nb * s
            z, zs = _in_proj(x2, norm_mix_w[l][None, :], w_main, w_small, min(1024, tg), 1920, b0 * s, tg)
            z3 = z.reshape(nb, s, z.shape[1])
            o_att = _band_attention(
                z3, bias_tab, jnp.tile(a_q_norm_w[l], A_HEADS)[None, :], jnp.tile(a_k_norm_w[l], A_HEADS)[None, :],
                seg.astype(BF16), z_attn // A_WIDTH)
            o_gdn = _gated_deltanet(z3, zs.reshape(nb, s, LANES), b_conv_w[l], head_params, b_norm_w[l][None, :])
            return _merge_route(
                o_att.reshape(tg, A_WIDTH), o_gdn.reshape(tg, B_WIDTH), z, x2, b0 * s, b_merge[l][None, :],
                w_proj_a[l].astype(BF16), w_proj_b[l].astype(BF16), w_out[l].astype(BF16),
                norm_ffn_w[l][None, :], peer_w_query[l].astype(BF16), peer_keys_1[l].astype(BF16),
                peer_keys_2[l].astype(BF16))

        planes = d // LANES
        n_exp = peer_down.shape[1]
        table = _table_pack(peer_down[l], peer_up[l]).reshape(n_exp, planes, LANES)

        b_sc = b * PEER_SC_SHARE_NUM // PEER_SC_SHARE_DEN
        n_sc = b_sc * s
        _, x1p_s, xnp_s, idx_s, _, gate_tok_s = mixer_and_route(0, b_sc)
        idx_s = idx_s.reshape(-1)
        part = _sc_down(idx_s, xnp_s.reshape(n_sc, planes, LANES), table, 0, n_sc)

        x1, _, xnp, idx, gate_t, _ = mixer_and_route(b_sc, b - b_sc)
        nblk = (t - n_sc) // PEER_TP
        nblk_a = nblk * PEER_TC_FIRST_NUM // PEER_TC_FIRST_DEN
        idx3 = idx.reshape(nblk, PEER_TP, PEER_SLOTS)
        out = _peer(idx3, xnp, gate_t, x1, table, 0, nblk_a, n_sc // PEER_TP, t)
        w16 = _peer_act(out, part, gate_tok_s, 0)
        out = _peer(idx3, xnp, gate_t, x1, table, nblk_a, nblk - nblk_a, n_sc // PEER_TP, t, prev=out, after=w16)
        sc_out = _sc_up(idx_s, w16, x1p_s.reshape(n_sc, planes, LANES), table, 0, n_sc)
        out = lax.dynamic_update_slice(out, sc_out.reshape(n_sc, d), (0, 0))
        x = out.reshape(b, s, d)
    return x
```

```python
import functools

import jax
import jax.numpy as jnp
from jax import lax
from jax.experimental import pallas as pl
from jax.experimental.pallas import tpu as pltpu
from jax.experimental.pallas import tpu_sc as plsc

F32 = jnp.float32
BF16 = jnp.bfloat16

EPS = 1e-6
NEG = -1e30

CHUNK = 64
A_HEADS = 8
A_HEAD_DIM = 64
A_WIDTH = A_HEADS * A_HEAD_DIM
A_LEFT_CHUNKS = 8
REL_CLIP = 128
B_HEADS = 8
B_HEAD_DIM = 128
B_WIDTH = B_HEADS * B_HEAD_DIM
CONV_WIDTH = 4
PEER_HEADS = 8
PEER_HALF = 128
PEER_NKEYS = 128
PEER_TOPK = 16
PEER_SLOTS = PEER_HEADS * PEER_TOPK

LANES = 128
SUBLANES = 8
VMEM_LIMIT = 56 * 1024 * 1024

Z_QKV_B = 0
Z_GATE_B = 3 * B_WIDTH
Z_MERGE = Z_GATE_B + B_WIDTH
Z_SMALL_A = 0
Z_SMALL_BETA = B_HEADS


def _dot(a, b):
    return jnp.dot(a, b, preferred_element_type=F32)


def _dot_nt(a, b):
    return lax.dot_general(a, b, (((1,), (1,)), ((), ())), preferred_element_type=F32)


def _dot_tn(a, b):
    return lax.dot_general(a, b, (((0,), (0,)), ((), ())), preferred_element_type=F32)


def _split_bf16(a):
    hi = a.astype(BF16)
    lo = (a - hi.astype(F32)).astype(BF16)
    return hi, lo


def _sigmoid(x):
    return 1.0 / (1.0 + jnp.exp(-x))


def _inproj_body(x_ref, nw_ref, w_ref, ws_ref, z_ref, zs_ref, h_scr):
    @pl.when(pl.program_id(1) == 0)
    def _():
        x = x_ref[...]
        ms = jnp.mean(x * x, axis=-1, keepdims=True)
        hb = (x * lax.rsqrt(ms + EPS) * nw_ref[...]).astype(BF16)
        h_scr[...] = hb
        zs_ref[...] = _dot(hb, ws_ref[...])

    z_ref[...] = _dot(h_scr[...], w_ref[...]).astype(z_ref.dtype)


def _in_proj(x2, norm_w, w_main, w_small, tm, tn, row0, t):
    d = x2.shape[1]
    n = w_main.shape[1]
    blk0 = row0 // tm
    return pl.pallas_call(
        _inproj_body,
        grid=(t // tm, n // tn),
        in_specs=[
            pl.BlockSpec((tm, d), lambda i, j: (i + blk0, 0)),
            pl.BlockSpec((1, d), lambda i, j: (0, 0)),
            pl.BlockSpec((d, tn), lambda i, j: (0, j)),
            pl.BlockSpec((d, LANES), lambda i, j: (0, 0)),
        ],
        out_specs=[
            pl.BlockSpec((tm, tn), lambda i, j: (i, j)),
            pl.BlockSpec((tm, LANES), lambda i, j: (i, 0)),
        ],
        out_shape=[
            jax.ShapeDtypeStruct((t, n), BF16),
            jax.ShapeDtypeStruct((t, LANES), F32),
        ],
        scratch_shapes=[pltpu.VMEM((tm, d), BF16)],
        compiler_params=pltpu.CompilerParams(
            dimension_semantics=("parallel", "arbitrary"), vmem_limit_bytes=VMEM_LIMIT),
        name="in_proj",
    )(x2, norm_w, w_main, w_small)


ATT_TQ = 256
ATT_WIN = ATT_TQ + A_LEFT_CHUNKS * CHUNK
ATT_NKB = ATT_WIN // ATT_TQ


def _attn_body(q_ref, k0_ref, k1_ref, k2_ref, v0_ref, v1_ref, v2_ref, bias_ref, qw_ref, kw_ref, seg_ref,
               o_ref):
    t = pl.program_id(1)
    seg = seg_ref[...]

    def head_rms(a, w):
        hi, lo = _split_bf16(a * a)
        ms = _dot(hi, seg) + _dot(lo, seg)
        return a * lax.rsqrt(ms + EPS) * w

    q = head_rms(q_ref[0].astype(F32), qw_ref[...]) * (A_HEAD_DIM ** -0.5)
    k = jnp.concatenate([k0_ref[0], k1_ref[0], k2_ref[0]], axis=0).astype(F32)
    k = head_rms(k, kw_ref[...]).astype(BF16)
    v = jnp.concatenate([v0_ref[0], v1_ref[0], v2_ref[0]], axis=0)

    kpos = t * ATT_TQ - A_LEFT_CHUNKS * CHUNK + lax.broadcasted_iota(jnp.int32, (1, ATT_WIN), 1)
    valid = kpos >= 0
    lane = lax.broadcasted_iota(jnp.int32, (1, LANES), 1)
    first = lane < A_HEAD_DIM

    for pair in range(A_HEADS // 2):
        cols = slice(pair * LANES, (pair + 1) * LANES)
        qp = q[:, cols]
        kp = k[:, cols]
        vp = v[:, cols]
        outs = []
        for half in range(2):
            h = 2 * pair + half
            qm = jnp.where(first if half == 0 else jnp.logical_not(first), qp, 0.0).astype(BF16)
            s = _dot_nt(qm, kp) + bias_ref[h]
            s = jnp.where(valid, s, NEG)
            mx = jnp.max(s, axis=-1, keepdims=True)
            p = jnp.exp(s - mx)
            den = jnp.sum(p, axis=-1, keepdims=True)
            outs.append(_dot(p.astype(BF16), vp) / den)
        o_ref[0, :, cols] = jnp.where(first, outs[0], outs[1]).astype(o_ref.dtype)


def _band_attention(z3, bias_tab, qw, kw, seg, col_q):
    b, s, _ = z3.shape
    nt = s // ATT_TQ

    def kv_spec(back, col):
        return pl.BlockSpec((1, ATT_TQ, A_WIDTH), lambda bi, ti: (bi, jnp.maximum(ti - back, 0), col))

    return pl.pallas_call(
        _attn_body,
        grid=(b, nt),
        in_specs=[
            pl.BlockSpec((1, ATT_TQ, A_WIDTH), lambda bi, ti: (bi, ti, col_q)),
            kv_spec(2, col_q + 1), kv_spec(1, col_q + 1), kv_spec(0, col_q + 1),
            kv_spec(2, col_q + 2), kv_spec(1, col_q + 2), kv_spec(0, col_q + 2),
            pl.BlockSpec((A_HEADS, ATT_TQ, ATT_WIN), lambda bi, ti: (0, 0, 0)),
            pl.BlockSpec((1, A_WIDTH), lambda bi, ti: (0, 0)),
            pl.BlockSpec((1, A_WIDTH), lambda bi, ti: (0, 0)),
            pl.BlockSpec((A_WIDTH, A_WIDTH), lambda bi, ti: (0, 0)),
        ],
        out_specs=pl.BlockSpec((1, ATT_TQ, A_WIDTH), lambda bi, ti: (bi, ti, 0)),
        out_shape=jax.ShapeDtypeStruct((b, s, A_WIDTH), BF16),
        compiler_params=pltpu.CompilerParams(
            dimension_semantics=("parallel", "parallel"), vmem_limit_bytes=VMEM_LIMIT),
        name="band_attn",
    )(z3, z3, z3, z3, z3, z3, z3, bias_tab, qw, kw, seg)


def _attn_bias_table(rel_bias):
    rb = rel_bias.astype(F32)
    heads = rb.shape[0]
    lo = A_LEFT_CHUNKS * CHUNK - (ATT_WIN - 1)
    hi = A_LEFT_CHUNKS * CHUNK + ATT_TQ - 1
    strip = jnp.concatenate([
        jnp.broadcast_to(rb[:, :1], (heads, -REL_CLIP - lo)), rb,
        jnp.broadcast_to(rb[:, -1:], (heads, hi - REL_CLIP))], axis=1)
    rev = strip[:, ::-1]
    n = rev.shape[1]
    flat = jnp.broadcast_to(rev[:, None, :], (heads, ATT_TQ, n)).reshape(heads, ATT_TQ * n)
    bias = flat[:, ATT_TQ - 1:ATT_TQ - 1 + ATT_TQ * (n - 1)].reshape(heads, ATT_TQ, n - 1)[:, :, :ATT_WIN]
    qc = jnp.arange(ATT_TQ)[:, None] // CHUNK
    kc = jnp.arange(ATT_WIN)[None, :] // CHUNK
    in_band = (kc >= qc) & (kc <= qc + A_LEFT_CHUNKS)
    return jnp.where(in_band[None], bias, NEG)


GDN_TAIL = SUBLANES


def _gdn_body(qkv_ref, zs_ref, gate_ref, convw_ref, hp_ref, nw_ref, o_ref, xbuf, s_scr):
    c = CHUNK
    hd = B_HEAD_DIM

    @pl.when(pl.program_id(1) == 0)
    def _():
        xbuf[0:GDN_TAIL, :] = jnp.zeros((GDN_TAIL, 3 * B_WIDTH), F32)
        s_scr[...] = jnp.zeros_like(s_scr)

    xbuf[GDN_TAIL:GDN_TAIL + c, :] = qkv_ref[0].astype(F32)
    y = None
    for j in range(CONV_WIDTH):
        tap = convw_ref[j:j + 1, :] * xbuf[pl.ds(GDN_TAIL - (CONV_WIDTH - 1) + j, c), :]
        y = tap if y is None else y + tap
    xbuf[0:GDN_TAIL, :] = xbuf[c:c + GDN_TAIL, :]
    y = y * _sigmoid(y)

    zs = zs_ref[0]
    a_neg = -(jnp.exp(hp_ref[0:1, :]) * hp_ref[2:3, :])
    xs = zs + hp_ref[1:2, :]
    g_all = a_neg * (jnp.maximum(xs, 0.0) + jnp.log(1.0 + jnp.exp(-jnp.abs(xs))))
    beta_all = _sigmoid(zs)

    row = lax.broadcasted_iota(jnp.int32, (c, c), 0)
    col = lax.broadcasted_iota(jnp.int32, (c, c), 1)
    tril = row >= col
    tril_strict = row > col
    eye = (row == col).astype(F32)
    lower = tril.astype(BF16)
    upper = (row <= col).astype(BF16)

    gh, gl = _split_bf16(g_all)
    gc_all = _dot(lower, gh) + _dot(lower, gl)
    gth, gtl = _split_bf16(g_all.T)
    gc_t = _dot(gth, upper) + _dot(gtl, upper)

    heads = range(B_HEADS)
    q, k, v, beta, gc, gc_row, g_last = [], [], [], [], [], [], []
    for h in heads:
        qh = y[:, h * hd:(h + 1) * hd]
        kh = y[:, B_WIDTH + h * hd:B_WIDTH + (h + 1) * hd]
        q.append(qh * lax.rsqrt(jnp.sum(qh * qh, axis=-1, keepdims=True) + EPS) * (hd ** -0.5))
        k.append(kh * lax.rsqrt(jnp.sum(kh * kh, axis=-1, keepdims=True) + EPS))
        v.append(y[:, 2 * B_WIDTH + h * hd:2 * B_WIDTH + (h + 1) * hd])
        beta.append(beta_all[:, Z_SMALL_BETA + h:Z_SMALL_BETA + h + 1])
        gc.append(gc_all[:, Z_SMALL_A + h:Z_SMALL_A + h + 1])
        gc_row.append(gc_t[Z_SMALL_A + h:Z_SMALL_A + h + 1, :])
        g_last.append(gc_row[h][:, c - 1:c])
    decay = [jnp.exp(jnp.where(tril, gc[h] - gc_row[h], NEG)) for h in heads]
    e_gc = [jnp.exp(gc[h]) for h in heads]
    kb = [k[h] * beta[h] for h in heads]
    kf = [k[h].astype(BF16) for h in heads]
    a_pow = [jnp.where(tril_strict, _dot_nt(kb[h].astype(BF16), kf[h]) * decay[h], 0.0) for h in heads]
    attn = [(_dot_nt(q[h].astype(BF16), kf[h]) * decay[h]).astype(BF16) for h in heads]

    t_mat = [eye - a_pow[h] for h in heads]
    for _ in range(5):
        ab = [a_pow[h].astype(BF16) for h in heads]
        a_pow = [_dot(ab[h], ab[h]) for h in heads]
        t_mat = [t_mat[h] + _dot(t_mat[h].astype(BF16), a_pow[h].astype(BF16)) for h in heads]
    tb = [t_mat[h].astype(BF16) for h in heads]
    u = [_dot(tb[h], (v[h] * beta[h]).astype(BF16)) for h in heads]
    w = [_dot(tb[h], (kb[h] * e_gc[h]).astype(BF16)).astype(BF16) for h in heads]

    state = [s_scr[h] for h in heads]
    sb = [state[h].astype(BF16) for h in heads]
    vnb = [(u[h] - _dot(w[h], sb[h])).astype(BF16) for h in heads]
    o = [_dot((q[h] * e_gc[h]).astype(BF16), sb[h]) + _dot(attn[h], vnb[h]) for h in heads]
    k_dec = [(k[h] * jnp.exp(g_last[h] - gc[h])).astype(BF16) for h in heads]
    for h in heads:
        s_scr[h] = state[h] * jnp.exp(g_last[h]) + _dot_tn(k_dec[h], vnb[h])
    for h in heads:
        cols = slice(h * hd, (h + 1) * hd)
        gate = gate_ref[0, :, cols].astype(F32)
        on = o[h] * lax.rsqrt(jnp.mean(o[h] * o[h], axis=-1, keepdims=True) + EPS) * nw_ref[...]
        o_ref[0, :, cols] = (on * (gate * _sigmoid(gate))).astype(o_ref.dtype)


def _gated_deltanet(z3, zs3, conv_w, head_params, norm_w):
    b, s, _ = z3.shape
    n = s // CHUNK
    return pl.pallas_call(
        _gdn_body,
        grid=(b, n),
        in_specs=[
            pl.BlockSpec((1, CHUNK, 3 * B_WIDTH), lambda bi, ni: (bi, ni, Z_QKV_B // (3 * B_WIDTH))),
            pl.BlockSpec((1, CHUNK, LANES), lambda bi, ni: (bi, ni, 0)),
            pl.BlockSpec((1, CHUNK, B_WIDTH), lambda bi, ni: (bi, ni, Z_GATE_B // B_WIDTH)),
            pl.BlockSpec((CONV_WIDTH, 3 * B_WIDTH), lambda bi, ni: (0, 0)),
            pl.BlockSpec((SUBLANES, LANES), lambda bi, ni: (0, 0)),
            pl.BlockSpec((1, B_HEAD_DIM), lambda bi, ni: (0, 0)),
        ],
        out_specs=pl.BlockSpec((1, CHUNK, B_WIDTH), lambda bi, ni: (bi, ni, 0)),
        out_shape=jax.ShapeDtypeStruct((b, s, B_WIDTH), BF16),
        scratch_shapes=[
            pltpu.VMEM((GDN_TAIL + CHUNK, 3 * B_WIDTH), F32),
            pltpu.VMEM((B_HEADS, B_HEAD_DIM, B_HEAD_DIM), F32),
        ],
        compiler_params=pltpu.CompilerParams(
            dimension_semantics=("parallel", "arbitrary"), vmem_limit_bytes=VMEM_LIMIT),
        name="gdn",
    )(z3, zs3, z3, conv_w, head_params, norm_w)


MERGE_TM = 256


def _top_k_rows(s, k, payload=None):
    n = s.shape[0]
    rows = lax.broadcasted_iota(jnp.int32, s.shape, 0)
    vals, picks = [], []
    for _ in range(k):
        m = jnp.max(s, axis=0, keepdims=True)
        i = jnp.min(jnp.where(s == m, rows, n), axis=0, keepdims=True)
        hit = rows == i
        vals.append(m)
        picks.append(i if payload is None else jnp.max(jnp.where(hit, payload, -1), axis=0, keepdims=True))
        s = jnp.where(hit, -jnp.inf, s)
    return jnp.concatenate(vals, axis=0), jnp.concatenate(picks, axis=0)


def _merge_body(oa_ref, ob_ref, mr_ref, x_ref, bm_ref, pa_ref, pb_ref, wo_ref, nw_ref, wq_ref, k1_ref, k2_ref,
                x1_ref, x1p_ref, xnp_ref, idx_ref, gate_ref, gate_tok_ref):
    tm, d = x_ref.shape
    planes = d // LANES

    def store_planes(ref, a):
        for p in range(planes):
            ref[pl.ds(p, tm, stride=planes), :] = a[:, p * LANES:(p + 1) * LANES]

    mr = mr_ref[...].astype(F32) + bm_ref[...]
    mixed = (_sigmoid(mr[:, :d]) * _dot(oa_ref[...], pa_ref[...])
             + _sigmoid(mr[:, d:]) * _dot(ob_ref[...], pb_ref[...]))
    x1 = x_ref[...] + _dot(mixed.astype(BF16), wo_ref[...])
    x1_ref[...] = x1
    store_planes(x1p_ref, x1)
    xn = x1 * lax.rsqrt(jnp.mean(x1 * x1, axis=-1, keepdims=True) + EPS) * nw_ref[...]
    store_planes(xnp_ref, xn)
    q = _dot(xn.astype(BF16), wq_ref[...]).astype(BF16)

    idx_rows, gate_rows = [], []
    for h in range(PEER_HEADS):
        q1 = q[:, (2 * h) * PEER_HALF:(2 * h + 1) * PEER_HALF]
        q2 = q[:, (2 * h + 1) * PEER_HALF:(2 * h + 2) * PEER_HALF]
        s1 = _dot_nt(k1_ref[h], q1)
        s2 = _dot_nt(k2_ref[h], q2)
        v1, i1 = _top_k_rows(s1, PEER_TOPK)
        v2, i2 = _top_k_rows(s2, PEER_TOPK)
        nb = [PEER_TOPK // (a + 1) for a in range(PEER_TOPK)]
        pad = -sum(nb) % SUBLANES
        tm = v1.shape[1]
        cand = jnp.concatenate([v1[a:a + 1] + v2[:nb[a]] for a in range(PEER_TOPK)]
                               + [jnp.full((pad, tm), -jnp.inf, F32)], axis=0)
        cand_idx = jnp.concatenate([i1[a:a + 1] * PEER_NKEYS + i2[:nb[a]] for a in range(PEER_TOPK)]
                                   + [jnp.zeros((pad, tm), jnp.int32)], axis=0)
        top_s, expert = _top_k_rows(cand, PEER_TOPK, payload=cand_idx)
        e = jnp.exp(top_s - top_s[0:1])
        gate_rows.append(e / jnp.sum(e, axis=0, keepdims=True))
        idx_rows.append(expert)
    idx_ref[...] = jnp.concatenate(idx_rows, axis=0).T
    gates = jnp.concatenate(gate_rows, axis=0)
    gate_ref[...] = gates
    gate_tok_ref[...] = gates.T


def _merge_route(o_a, o_b, z2, x2, row0, b_merge, pa, pb, wo, nw, wq, k1, k2):
    t, d = o_a.shape[0], x2.shape[1]
    tm = MERGE_TM
    xblk0 = row0 // tm
    full = lambda a: pl.BlockSpec(a.shape, lambda i: (0,) * a.ndim)
    return pl.pallas_call(
        _merge_body,
        grid=(t // tm,),
        in_specs=[
            pl.BlockSpec((tm, A_WIDTH), lambda i: (i, 0)),
            pl.BlockSpec((tm, B_WIDTH), lambda i: (i, 0)),
            pl.BlockSpec((tm, 2 * d), lambda i: (i, Z_MERGE // (2 * d))),
            pl.BlockSpec((tm, d), lambda i: (i + xblk0, 0)),
            full(b_merge), full(pa), full(pb), full(wo), full(nw), full(wq), full(k1), full(k2),
        ],
        out_specs=[
            pl.BlockSpec((tm, d), lambda i: (i, 0)),
            pl.BlockSpec((tm * d // LANES, LANES), lambda i: (i, 0)),
            pl.BlockSpec((tm * d // LANES, LANES), lambda i: (i, 0)),
            pl.BlockSpec((tm, PEER_SLOTS), lambda i: (i, 0)),
            pl.BlockSpec((PEER_SLOTS, tm), lambda i: (0, i)),
            pl.BlockSpec((tm, PEER_SLOTS), lambda i: (i, 0)),
        ],
        out_shape=[
            jax.ShapeDtypeStruct((t, d), F32),
            jax.ShapeDtypeStruct((t * d // LANES, LANES), F32),
            jax.ShapeDtypeStruct((t * d // LANES, LANES), F32),
            jax.ShapeDtypeStruct((t, PEER_SLOTS), jnp.int32),
            jax.ShapeDtypeStruct((PEER_SLOTS, t), F32),
            jax.ShapeDtypeStruct((t, PEER_SLOTS), F32),
        ],
        compiler_params=pltpu.CompilerParams(
            dimension_semantics=("parallel",), vmem_limit_bytes=VMEM_LIMIT),
        name="merge_route",
    )(o_a, o_b, z2, x2, b_merge, pa, pb, wo, nw, wq, k1, k2)


PEER_TP = 128
PEER_NBUF = 4
PEER_LOOK = PEER_NBUF - 1
PEER_SC_SHARE_NUM, PEER_SC_SHARE_DEN = 18, 32
PEER_TC_FIRST_NUM, PEER_TC_FIRST_DEN = 5, 16


def _erf(x):
    return lax.erf(x)


PACK_HI = -65536


def _packed_down(bitcast, word):
    return bitcast(word & PACK_HI, F32)


def _packed_up(bitcast, word):
    return bitcast(word << 16, F32)


def _peer_body(idx_ref, idx_next_ref, xn_ref, gate_ref, x1_ref, tab_ref, o_ref, *scratch):
    bufs, sem = scratch[:PEER_NBUF], scratch[PEER_NBUF]
    planes = x1_ref.shape[1] // LANES
    step = pl.program_id(0)

    def issue(ref, tok, slot):
        for j in range(PEER_SLOTS):
            pltpu.make_async_copy(tab_ref.at[ref[0, tok, j]], bufs[slot].at[:, j, :], sem.at[slot]).start()

    def wait(slot):
        pltpu.make_async_copy(bufs[slot], bufs[slot], sem.at[slot]).wait()

    @pl.when(step == 0)
    def _():
        for s in range(PEER_LOOK):
            issue(idx_ref, s, s)

    lane = lax.broadcasted_iota(jnp.int32, (PEER_SLOTS, PEER_TP), 1)

    def compute(tok, slot):
        buf = bufs[slot]
        acc = None
        for s in range(planes):
            term = _packed_down(lax.bitcast_convert_type, buf[s]) * xn_ref[pl.ds(tok * planes + s, 1), :]
            acc = term if acc is None else acc + term
        act = jnp.sum(acc, axis=-1, keepdims=True)
        gate = jnp.sum(jnp.where(lane == tok, gate_ref[...], 0.0), axis=-1, keepdims=True)
        w = gate * (0.5 * act * (1.0 + _erf(act * (2.0 ** -0.5))))
        out = jnp.concatenate(
            [jnp.sum(_packed_up(lax.bitcast_convert_type, buf[s]) * w, axis=0, keepdims=True)
             for s in range(planes)], axis=1)
        o_ref[pl.ds(tok, 1), :] = x1_ref[pl.ds(tok, 1), :] + out

    def group(g, carry):
        for u in range(PEER_NBUF):
            tok = g * PEER_NBUF + u
            issue(idx_ref, tok + PEER_LOOK, (u + PEER_LOOK) % PEER_NBUF)
            wait(u)
            compute(tok, u)
        return carry

    n_groups = PEER_TP // PEER_NBUF
    lax.fori_loop(0, n_groups - 1, group, 0)

    for u in range(PEER_NBUF):
        tok = (n_groups - 1) * PEER_NBUF + u
        nxt = tok + PEER_LOOK
        if nxt < PEER_TP:
            issue(idx_ref, nxt, (u + PEER_LOOK) % PEER_NBUF)
        else:
            issue(idx_next_ref, nxt - PEER_TP, (u + PEER_LOOK) % PEER_NBUF)
        wait(u)
        compute(tok, u)

    @pl.when(step == pl.num_programs(0) - 1)
    def _():
        for s in range(PEER_LOOK):
            wait(s)


def _peer_alias_body(prev_ref, after_ref, *rest):
    del prev_ref, after_ref
    _peer_body(*rest)


def _peer(idx3, xnp, gate_t, x1, table, blk0, nb, out_blk0, t_out, prev=None, after=None):
    d = x1.shape[1]
    tp = PEER_TP
    planes = d // LANES
    in_specs = [
        pl.BlockSpec((1, tp, PEER_SLOTS), lambda i: (i + blk0, 0, 0), memory_space=pltpu.SMEM),
        pl.BlockSpec((1, tp, PEER_SLOTS), lambda i: (jnp.minimum(i + 1, nb - 1) + blk0, 0, 0),
                     memory_space=pltpu.SMEM),
        pl.BlockSpec((tp * planes, LANES), lambda i: (i + blk0, 0)),
        pl.BlockSpec((PEER_SLOTS, tp), lambda i: (0, i + blk0)),
        pl.BlockSpec((tp, d), lambda i: (i + blk0, 0)),
        pl.BlockSpec(memory_space=pl.ANY),
    ]
    args = (idx3, idx3, xnp, gate_t, x1, table)
    body, aliases = _peer_body, {}
    if prev is not None:
        body, aliases = _peer_alias_body, {0: 0}
        in_specs = [pl.BlockSpec(memory_space=pl.ANY), pl.BlockSpec(memory_space=pl.ANY)] + in_specs
        args = (prev, after) + args
    return pl.pallas_call(
        body,
        grid=(nb,),
        in_specs=in_specs,
        out_specs=pl.BlockSpec((tp, d), lambda i: (i + blk0 + out_blk0, 0)),
        out_shape=jax.ShapeDtypeStruct((t_out, d), F32),
        scratch_shapes=[pltpu.VMEM((planes, PEER_SLOTS, LANES), jnp.int32) for _ in range(PEER_NBUF)]
        + [pltpu.SemaphoreType.DMA((PEER_NBUF,))],
        input_output_aliases=aliases,
        compiler_params=pltpu.CompilerParams(
            dimension_semantics=("arbitrary",), vmem_limit_bytes=VMEM_LIMIT),
        name="peer",
    )(*args)


SC_CORES = 2
SC_SUBCORES = 16
SC_LANES = 16
SC_WORKERS = SC_CORES * SC_SUBCORES
SC_ROWS = 32


SC_VPR = LANES // SC_LANES
SC_VROWS = PEER_SLOTS // SC_VPR


def _sc_vec(j):
    return (j // SC_VPR, pl.ds((j % SC_VPR) * SC_LANES, SC_LANES))


def _sc_mesh():
    return plsc.VectorSubcoreMesh(core_axis_name="c", subcore_axis_name="s")


def _sc_worker():
    return lax.axis_index("s") * SC_CORES + lax.axis_index("c")


def _sc_pipeline(tw, base, nchunk, loads, gather, compute, finish):
    assert tw % 2 == 0 and nchunk % 2 == 0
    last = base + tw - 1

    def token(tok, s):
        nxt = jnp.minimum(tok + 1, last)
        for c in loads(nxt, 1 - s):
            c.start()
        for kc in range(nchunk):
            if kc + 1 < nchunk:
                gather(s, kc + 1).start()
            else:
                for c in loads(nxt, 1 - s):
                    c.wait()
                gather(1 - s, 0).start()
            gather(s, kc).wait()
            compute(s, kc)
        finish(tok, s)

    for c in loads(base, 0):
        c.start()
    for c in loads(base, 0):
        c.wait()
    gather(0, 0).start()

    def pair(i, carry):
        token(base + 2 * i, 0)
        token(base + 2 * i + 1, 1)
        return carry

    lax.fori_loop(0, tw // 2, pair, 0)
    gather(0, 0).wait()


def _sc_down(idx_flat, xn3, table, tok0, n_tok):
    planes = xn3.shape[1]
    half = planes // 2
    tw = n_tok // SC_WORKERS
    nchunk = PEER_SLOTS // SC_ROWS
    per_plane = LANES // SC_LANES

    @functools.partial(
        pl.kernel, mesh=_sc_mesh(),
        out_type=jax.ShapeDtypeStruct((n_tok, SC_VROWS, LANES), F32),
        scratch_types=[
            pltpu.VMEM((2, PEER_SLOTS), jnp.int32),
            pltpu.VMEM((2, planes, LANES), F32),
            pltpu.VMEM((2, SC_ROWS, half, LANES), jnp.int32),
            pltpu.VMEM((SC_VROWS, LANES), F32),
            pltpu.VMEM((half, SC_VROWS, LANES), F32),
            pltpu.SemaphoreType.DMA((2,)),
            pltpu.SemaphoreType.DMA((2,)),
        ],
        compiler_params=pltpu.CompilerParams(needs_layout_passes=False),
        name="sc_down",
    )
    def run(idx_hbm, xn_hbm, tab_hbm, out_hbm, idx_v, x_v, buf, act_v, part_v, sem_g, sem_l):
        def loads(tok, s):
            return [pltpu.make_async_copy(idx_hbm.at[pl.ds(tok * PEER_SLOTS, PEER_SLOTS)], idx_v.at[s], sem_l.at[s]),
                    pltpu.make_async_copy(xn_hbm.at[tok], x_v.at[s], sem_l.at[s])]

        def gather(s, kc):
            return pltpu.make_async_copy(
                tab_hbm.at[idx_v.at[s, pl.ds(kc * SC_ROWS, SC_ROWS)]], buf.at[kc % 2], sem_g.at[kc % 2])

        def compute(s, kc):
            @plsc.parallel_loop(0, half)
            def _(p):
                sls = [pl.ds(q * SC_LANES, SC_LANES) for q in range(per_plane)]
                x_hi = [x_v[s, p, sl] for sl in sls]
                x_lo = [x_v[s, half + p, sl] for sl in sls]

                def load_row(j):
                    return [buf[kc % 2, j, p, sl] for sl in sls]

                words = load_row(0)
                for j in range(SC_ROWS):
                    ahead = load_row(j + 1) if j + 1 < SC_ROWS else None
                    terms = ([_packed_down(plsc.bitcast, w) * x for w, x in zip(words, x_hi)]
                             + [_packed_up(plsc.bitcast, w) * x for w, x in zip(words, x_lo)])
                    while len(terms) > 1:
                        terms = [terms[i] + terms[i + 1] for i in range(0, len(terms), 2)]
                    part_v[(p,) + _sc_vec(kc * SC_ROWS + j)] = terms[0]
                    words = ahead

        def finish(tok, s):
            @plsc.parallel_loop(0, SC_VROWS)
            def _(r):
                for u in range(SC_VPR):
                    sl = pl.ds(u * SC_LANES, SC_LANES)
                    acc = part_v[0, r, sl]
                    for p in range(1, half):
                        acc = acc + part_v[p, r, sl]
                    act_v[r, sl] = acc
            pltpu.sync_copy(act_v, out_hbm.at[tok - tok0])

        _sc_pipeline(tw, tok0 + _sc_worker() * tw, nchunk, loads, gather, compute, finish)

    return run(idx_flat, xn3, table)


def _sc_up(idx_flat, w16, x13, table, tok0, n_tok):
    planes = x13.shape[1]
    half = planes // 2
    tw = n_tok // SC_WORKERS
    nchunk = PEER_SLOTS // SC_ROWS
    per_plane = LANES // SC_LANES

    @functools.partial(
        pl.kernel, mesh=_sc_mesh(),
        out_type=jax.ShapeDtypeStruct((n_tok, planes, LANES), F32),
        scratch_types=[
            pltpu.VMEM((2, PEER_SLOTS), jnp.int32),
            pltpu.VMEM((2, SC_VROWS, LANES), F32),
            pltpu.VMEM((2, planes, LANES), F32),
            pltpu.VMEM((2, SC_ROWS, half, LANES), jnp.int32),
            pltpu.SemaphoreType.DMA((2,)),
            pltpu.SemaphoreType.DMA((2,)),
        ],
        compiler_params=pltpu.CompilerParams(needs_layout_passes=False),
        name="sc_up",
    )
    def run(idx_hbm, w_hbm, x1_hbm, tab_hbm, out_hbm, idx_v, w_v, out_v, buf, sem_g, sem_l):
        def loads(tok, s):
            return [pltpu.make_async_copy(idx_hbm.at[pl.ds(tok * PEER_SLOTS, PEER_SLOTS)], idx_v.at[s], sem_l.at[s]),
                    pltpu.make_async_copy(w_hbm.at[tok - tok0], w_v.at[s], sem_l.at[s]),
                    pltpu.make_async_copy(x1_hbm.at[tok], out_v.at[s], sem_l.at[s])]

        def gather(s, kc):
            return pltpu.make_async_copy(
                tab_hbm.at[idx_v.at[s, pl.ds(kc * SC_ROWS, SC_ROWS)]], buf.at[kc % 2], sem_g.at[kc % 2])

        def compute(s, kc):
            @plsc.parallel_loop(0, half)
            def _(p):
                sls = [pl.ds(q * SC_LANES, SC_LANES) for q in range(per_plane)]
                acc_hi = [out_v[s, p, sl] for sl in sls]
                acc_lo = [out_v[s, half + p, sl] for sl in sls]
                for j in range(SC_ROWS):
                    wj = w_v[(s,) + _sc_vec(kc * SC_ROWS + j)]
                    words = [buf[kc % 2, j, p, sl] for sl in sls]
                    acc_hi = [a + wj * _packed_down(plsc.bitcast, w) for a, w in zip(acc_hi, words)]
                    acc_lo = [a + wj * _packed_up(plsc.bitcast, w) for a, w in zip(acc_lo, words)]
                for a, sl in zip(acc_hi, sls):
                    out_v[s, p, sl] = a
                for a, sl in zip(acc_lo, sls):
                    out_v[s, half + p, sl] = a

        def finish(tok, s):
            pltpu.sync_copy(out_v.at[s], out_hbm.at[tok - tok0])

        _sc_pipeline(tw, tok0 + _sc_worker() * tw, nchunk, loads, gather, compute, finish)

    return run(idx_flat, w16, x13, table)


ACT_TM = 256


def _peer_act_body(after_ref, part_ref, gate_ref, fold_ref, spread_ref, own_ref, w_ref):
    del after_ref
    tm = part_ref.shape[0]
    own = own_ref[...]
    hi, lo = _split_bf16(part_ref[...].reshape(tm * SC_VROWS, LANES))
    sums = (_dot(hi, fold_ref[...]) + _dot(lo, fold_ref[...])).reshape(tm, SC_VROWS, LANES)
    act = jnp.sum(sums * own, axis=1)
    w = gate_ref[...] * (0.5 * act * (1.0 + _erf(act * (2.0 ** -0.5))))
    hi, lo = _split_bf16((w[:, None, :] * own).reshape(tm * SC_VROWS, LANES))
    w_ref[...] = (_dot(hi, spread_ref[...]) + _dot(lo, spread_ref[...])).reshape(tm, SC_VROWS, LANES)


def _peer_act(after, part3, gate_tok, blk0):
    n_tok = part3.shape[0]
    tm = ACT_TM
    lane = jnp.arange(LANES)
    fold = (lane[:, None] // SC_LANES == lane[None, :] % SC_VPR).astype(BF16)
    spread = fold.T
    own = (lane[None, :] // SC_VPR == jnp.arange(SC_VROWS)[:, None]).astype(F32)
    const = lambda a: pl.BlockSpec(a.shape, lambda i: (0,) * a.ndim)
    return pl.pallas_call(
        _peer_act_body,
        grid=(n_tok // tm,),
        in_specs=[
            pl.BlockSpec(memory_space=pl.ANY),
            pl.BlockSpec((tm, SC_VROWS, LANES), lambda i: (i, 0, 0)),
            pl.BlockSpec((tm, PEER_SLOTS), lambda i: (i + blk0, 0)),
            const(fold), const(spread), const(own),
        ],
        out_specs=pl.BlockSpec((tm, SC_VROWS, LANES), lambda i: (i, 0, 0)),
        out_shape=jax.ShapeDtypeStruct(part3.shape, F32),
        compiler_params=pltpu.CompilerParams(
            dimension_semantics=("parallel",), vmem_limit_bytes=VMEM_LIMIT),
        name="peer_act",
    )(after, part3, gate_tok, fold, spread, own)


def _table_pack_body(down_ref, up_ref, tab_ref, dn2_ref, up2_ref):
    te, d = down_ref.shape
    planes = d // LANES
    half = planes // 2

    def bits(ref, p):
        return lax.bitcast_convert_type(ref[:, p * LANES:(p + 1) * LANES].astype(BF16).astype(F32), jnp.int32)

    def pack(hi, lo):
        return hi | lax.shift_right_logical(lo, 16)

    for p in range(planes):
        tab_ref[pl.ds(p, te, stride=planes), :] = pack(bits(down_ref, p), bits(up_ref, p))
    for p in range(half):
        dn2_ref[pl.ds(p, te, stride=half), :] = pack(bits(down_ref, p), bits(down_ref, half + p))
        up2_ref[pl.ds(p, te, stride=half), :] = pack(bits(up_ref, p), bits(up_ref, half + p))


def _table_pack(down, up, te=512):
    n_exp, d = down.shape
    rows = d // LANES
    return pl.pallas_call(
        _table_pack_body,
        grid=(n_exp // te,),
        in_specs=[pl.BlockSpec((te, d), lambda i: (i, 0)), pl.BlockSpec((te, d), lambda i: (i, 0))],
        out_specs=[pl.BlockSpec((te * rows, LANES), lambda i: (i, 0)),
                   pl.BlockSpec((te * rows // 2, LANES), lambda i: (i, 0)),
                   pl.BlockSpec((te * rows // 2, LANES), lambda i: (i, 0))],
        out_shape=[jax.ShapeDtypeStruct((n_exp * rows, LANES), jnp.int32),
                   jax.ShapeDtypeStruct((n_exp * rows // 2, LANES), jnp.int32),
                   jax.ShapeDtypeStruct((n_exp * rows // 2, LANES), jnp.int32)],
        compiler_params=pltpu.CompilerParams(
            dimension_semantics=("parallel",), vmem_limit_bytes=VMEM_LIMIT),
        name="table_pack",
    )(down, up)


def _pad_lanes(a):
    return jnp.pad(a, ((0, 0), (0, LANES - a.shape[1])))


def kernel(x, norm_mix_w, w_in, a_q_norm_w, a_k_norm_w, a_rel_bias, b_conv_w, b_a_log, b_dt_bias, b_norm_w, b_merge, w_proj_a, w_proj_b, w_out, norm_ffn_w, peer_w_query, peer_keys_1, peer_keys_2, peer_down, peer_up):
    b, s, d = x.shape
    t = b * s
    depth = w_in.shape[0]
    for l in range(depth):
        x2 = x.reshape(t, d)

        wi = w_in[l]
        o_qkvb = 3 * A_WIDTH
        o_a = o_qkvb + 3 * B_WIDTH
        o_beta = o_a + B_HEADS
        o_gate = o_beta + B_HEADS
        o_merge = o_gate + B_WIDTH
        w_main = jnp.concatenate(
            [wi[:, o_qkvb:o_a], wi[:, o_gate:o_merge], wi[:, o_merge:], wi[:, :o_qkvb]], axis=1).astype(BF16)
        w_small = _pad_lanes(wi[:, o_a:o_gate]).astype(BF16)
        z_attn = Z_MERGE + 2 * d

        bias_tab = _attn_bias_table(a_rel_bias[l])
        seg = jnp.kron(jnp.eye(A_HEADS, dtype=F32), jnp.full((A_HEAD_DIM, A_HEAD_DIM), 1.0 / A_HEAD_DIM, F32))
        head_params = jnp.zeros((SUBLANES, LANES), F32)
        head_params = head_params.at[0, :B_HEADS].set(b_a_log[l]).at[1, :B_HEADS].set(b_dt_bias[l])
        head_params = head_params.at[2, :B_HEADS].set(1.0)

        def mixer_and_route(b0, nb):
            tg = nb * s
            z, zs = _in_proj(x2, norm_mix_w[l][None, :], w_main, w_small, min(1024, tg), 1920, b0 * s, tg)
            z3 = z.reshape(nb, s, z.shape[1])
            o_att = _band_attention(
                z3, bias_tab, jnp.tile(a_q_norm_w[l], A_HEADS)[None, :], jnp.tile(a_k_norm_w[l], A_HEADS)[None, :],
                seg.astype(BF16), z_attn // A_WIDTH)
            o_gdn = _gated_deltanet(z3, zs.reshape(nb, s, LANES), b_conv_w[l], head_params, b_norm_w[l][None, :])
            return _merge_route(
                o_att.reshape(tg, A_WIDTH), o_gdn.reshape(tg, B_WIDTH), z, x2, b0 * s, b_merge[l][None, :],
                w_proj_a[l].astype(BF16), w_proj_b[l].astype(BF16), w_out[l].astype(BF16),
                norm_ffn_w[l][None, :], peer_w_query[l].astype(BF16), peer_keys_1[l].astype(BF16),
                peer_keys_2[l].astype(BF16))

        planes = d // LANES
        n_exp = peer_down.shape[1]
        table, down2, up2 = _table_pack(peer_down[l], peer_up[l])
        table = table.reshape(n_exp, planes, LANES)
        down2 = down2.reshape(n_exp, planes // 2, LANES)
        up2 = up2.reshape(n_exp, planes // 2, LANES)

        b_sc = b * PEER_SC_SHARE_NUM // PEER_SC_SHARE_DEN
        n_sc = b_sc * s
        _, x1p_s, xnp_s, idx_s, _, gate_tok_s = mixer_and_route(0, b_sc)
        idx_s = idx_s.reshape(-1)
        part = _sc_down(idx_s, xnp_s.reshape(n_sc, planes, LANES), down2, 0, n_sc)

        x1, _, xnp, idx, gate_t, _ = mixer_and_route(b_sc, b - b_sc)
        nblk = (t - n_sc) // PEER_TP
        nblk_a = nblk * PEER_TC_FIRST_NUM // PEER_TC_FIRST_DEN
        idx3 = idx.reshape(nblk, PEER_TP, PEER_SLOTS)
        out = _peer(idx3, xnp, gate_t, x1, table, 0, nblk_a, n_sc // PEER_TP, t)
        w16 = _peer_act(out, part, gate_tok_s, 0)
        out = _peer(idx3, xnp, gate_t, x1, table, nblk_a, nblk - nblk_a, n_sc // PEER_TP, t, prev=out, after=w16)
        sc_out = _sc_up(idx_s, w16, x1p_s.reshape(n_sc, planes, LANES), up2, 0, n_sc)
        out = lax.dynamic_update_slice(out, sc_out.reshape(n_sc, d), (0, 0))
        x = out.reshape(b, s, d)
    return x
```

```python
import functools

import jax
import jax.numpy as jnp
from jax import lax
from jax.experimental import pallas as pl
from jax.experimental.pallas import tpu as pltpu
from jax.experimental.pallas import tpu_sc as plsc

F32 = jnp.float32
BF16 = jnp.bfloat16

EPS = 1e-6
NEG = -1e30

CHUNK = 64
A_HEADS = 8
A_HEAD_DIM = 64
A_WIDTH = A_HEADS * A_HEAD_DIM
A_LEFT_CHUNKS = 8
REL_CLIP = 128
B_HEADS = 8
B_HEAD_DIM = 128
B_WIDTH = B_HEADS * B_HEAD_DIM
CONV_WIDTH = 4
PEER_HEADS = 8
PEER_HALF = 128
PEER_NKEYS = 128
PEER_TOPK = 16
PEER_SLOTS = PEER_HEADS * PEER_TOPK

LANES = 128
SUBLANES = 8
VMEM_LIMIT = 56 * 1024 * 1024

Z_QKV_B = 0
Z_GATE_B = 3 * B_WIDTH
Z_MERGE = Z_GATE_B + B_WIDTH
Z_SMALL_A = 0
Z_SMALL_BETA = B_HEADS


def _dot(a, b):
    return jnp.dot(a, b, preferred_element_type=F32)


def _dot_nt(a, b):
    return lax.dot_general(a, b, (((1,), (1,)), ((), ())), preferred_element_type=F32)


def _dot_tn(a, b):
    return lax.dot_general(a, b, (((0,), (0,)), ((), ())), preferred_element_type=F32)


def _split_bf16(a):
    hi = a.astype(BF16)
    lo = (a - hi.astype(F32)).astype(BF16)
    return hi, lo


def _sigmoid(x):
    return 1.0 / (1.0 + jnp.exp(-x))


def _inproj_body(x_ref, nw_ref, w_ref, ws_ref, z_ref, zs_ref, h_scr):
    @pl.when(pl.program_id(1) == 0)
    def _():
        x = x_ref[...]
        ms = jnp.mean(x * x, axis=-1, keepdims=True)
        hb = (x * lax.rsqrt(ms + EPS) * nw_ref[...]).astype(BF16)
        h_scr[...] = hb
        zs_ref[...] = _dot(hb, ws_ref[...])

    z_ref[...] = _dot(h_scr[...], w_ref[...]).astype(z_ref.dtype)


def _in_proj(x2, norm_w, w_main, w_small, tm, tn, row0, t):
    d = x2.shape[1]
    n = w_main.shape[1]
    blk0 = row0 // tm
    return pl.pallas_call(
        _inproj_body,
        grid=(t // tm, n // tn),
        in_specs=[
            pl.BlockSpec((tm, d), lambda i, j: (i + blk0, 0)),
            pl.BlockSpec((1, d), lambda i, j: (0, 0)),
            pl.BlockSpec((d, tn), lambda i, j: (0, j)),
            pl.BlockSpec((d, LANES), lambda i, j: (0, 0)),
        ],
        out_specs=[
            pl.BlockSpec((tm, tn), lambda i, j: (i, j)),
            pl.BlockSpec((tm, LANES), lambda i, j: (i, 0)),
        ],
        out_shape=[
            jax.ShapeDtypeStruct((t, n), BF16),
            jax.ShapeDtypeStruct((t, LANES), F32),
        ],
        scratch_shapes=[pltpu.VMEM((tm, d), BF16)],
        compiler_params=pltpu.CompilerParams(
            dimension_semantics=("parallel", "arbitrary"), vmem_limit_bytes=VMEM_LIMIT),
        name="in_proj",
    )(x2, norm_w, w_main, w_small)


ATT_TQ = 256
ATT_WIN = ATT_TQ + A_LEFT_CHUNKS * CHUNK
ATT_NKB = ATT_WIN // ATT_TQ


def _attn_body(q_ref, k0_ref, k1_ref, k2_ref, v0_ref, v1_ref, v2_ref, bias_ref, qw_ref, kw_ref, seg_ref,
               o_ref):
    t = pl.program_id(1)
    seg = seg_ref[...]

    def head_rms(a, w):
        hi, lo = _split_bf16(a * a)
        ms = _dot(hi, seg) + _dot(lo, seg)
        return a * lax.rsqrt(ms + EPS) * w

    q = head_rms(q_ref[0].astype(F32), qw_ref[...]) * (A_HEAD_DIM ** -0.5)
    k = jnp.concatenate([k0_ref[0], k1_ref[0], k2_ref[0]], axis=0).astype(F32)
    k = head_rms(k, kw_ref[...]).astype(BF16)
    v = jnp.concatenate([v0_ref[0], v1_ref[0], v2_ref[0]], axis=0)

    kpos = t * ATT_TQ - A_LEFT_CHUNKS * CHUNK + lax.broadcasted_iota(jnp.int32, (1, ATT_WIN), 1)
    valid = kpos >= 0
    lane = lax.broadcasted_iota(jnp.int32, (1, LANES), 1)
    first = lane < A_HEAD_DIM

    for pair in range(A_HEADS // 2):
        cols = slice(pair * LANES, (pair + 1) * LANES)
        qp = q[:, cols]
        kp = k[:, cols]
        vp = v[:, cols]
        outs = []
        for half in range(2):
            h = 2 * pair + half
            qm = jnp.where(first if half == 0 else jnp.logical_not(first), qp, 0.0).astype(BF16)
            s = _dot_nt(qm, kp) + bias_ref[h]
            s = jnp.where(valid, s, NEG)
            mx = jnp.max(s, axis=-1, keepdims=True)
            p = jnp.exp(s - mx)
            den = jnp.sum(p, axis=-1, keepdims=True)
            outs.append(_dot(p.astype(BF16), vp) / den)
        o_ref[0, :, cols] = jnp.where(first, outs[0], outs[1]).astype(o_ref.dtype)


def _band_attention(z3, bias_tab, qw, kw, seg, col_q):
    b, s, _ = z3.shape
    nt = s // ATT_TQ

    def kv_spec(back, col):
        return pl.BlockSpec((1, ATT_TQ, A_WIDTH), lambda bi, ti: (bi, jnp.maximum(ti - back, 0), col))

    return pl.pallas_call(
        _attn_body,
        grid=(b, nt),
        in_specs=[
            pl.BlockSpec((1, ATT_TQ, A_WIDTH), lambda bi, ti: (bi, ti, col_q)),
            kv_spec(2, col_q + 1), kv_spec(1, col_q + 1), kv_spec(0, col_q + 1),
            kv_spec(2, col_q + 2), kv_spec(1, col_q + 2), kv_spec(0, col_q + 2),
            pl.BlockSpec((A_HEADS, ATT_TQ, ATT_WIN), lambda bi, ti: (0, 0, 0)),
            pl.BlockSpec((1, A_WIDTH), lambda bi, ti: (0, 0)),
            pl.BlockSpec((1, A_WIDTH), lambda bi, ti: (0, 0)),
            pl.BlockSpec((A_WIDTH, A_WIDTH), lambda bi, ti: (0, 0)),
        ],
        out_specs=pl.BlockSpec((1, ATT_TQ, A_WIDTH), lambda bi, ti: (bi, ti, 0)),
        out_shape=jax.ShapeDtypeStruct((b, s, A_WIDTH), BF16),
        compiler_params=pltpu.CompilerParams(
            dimension_semantics=("parallel", "parallel"), vmem_limit_bytes=VMEM_LIMIT),
        name="band_attn",
    )(z3, z3, z3, z3, z3, z3, z3, bias_tab, qw, kw, seg)


def _attn_bias_table(rel_bias):
    rb = rel_bias.astype(F32)
    heads = rb.shape[0]
    lo = A_LEFT_CHUNKS * CHUNK - (ATT_WIN - 1)
    hi = A_LEFT_CHUNKS * CHUNK + ATT_TQ - 1
    strip = jnp.concatenate([
        jnp.broadcast_to(rb[:, :1], (heads, -REL_CLIP - lo)), rb,
        jnp.broadcast_to(rb[:, -1:], (heads, hi - REL_CLIP))], axis=1)
    rev = strip[:, ::-1]
    n = rev.shape[1]
    flat = jnp.broadcast_to(rev[:, None, :], (heads, ATT_TQ, n)).reshape(heads, ATT_TQ * n)
    bias = flat[:, ATT_TQ - 1:ATT_TQ - 1 + ATT_TQ * (n - 1)].reshape(heads, ATT_TQ, n - 1)[:, :, :ATT_WIN]
    qc = jnp.arange(ATT_TQ)[:, None] // CHUNK
    kc = jnp.arange(ATT_WIN)[None, :] // CHUNK
    in_band = (kc >= qc) & (kc <= qc + A_LEFT_CHUNKS)
    return jnp.where(in_band[None], bias, NEG)


GDN_TAIL = SUBLANES


def _gdn_body(qkv_ref, zs_ref, gate_ref, convw_ref, hp_ref, nw_ref, o_ref, xbuf, s_scr):
    c = CHUNK
    hd = B_HEAD_DIM

    @pl.when(pl.program_id(1) == 0)
    def _():
        xbuf[0:GDN_TAIL, :] = jnp.zeros((GDN_TAIL, 3 * B_WIDTH), F32)
        s_scr[...] = jnp.zeros_like(s_scr)

    xbuf[GDN_TAIL:GDN_TAIL + c, :] = qkv_ref[0].astype(F32)
    y = None
    for j in range(CONV_WIDTH):
        tap = convw_ref[j:j + 1, :] * xbuf[pl.ds(GDN_TAIL - (CONV_WIDTH - 1) + j, c), :]
        y = tap if y is None else y + tap
    xbuf[0:GDN_TAIL, :] = xbuf[c:c + GDN_TAIL, :]
    y = y * _sigmoid(y)

    zs = zs_ref[0]
    a_neg = -(jnp.exp(hp_ref[0:1, :]) * hp_ref[2:3, :])
    xs = zs + hp_ref[1:2, :]
    g_all = a_neg * (jnp.maximum(xs, 0.0) + jnp.log(1.0 + jnp.exp(-jnp.abs(xs))))
    beta_all = _sigmoid(zs)

    row = lax.broadcasted_iota(jnp.int32, (c, c), 0)
    col = lax.broadcasted_iota(jnp.int32, (c, c), 1)
    tril = row >= col
    tril_strict = row > col
    eye = (row == col).astype(F32)
    lower = tril.astype(BF16)
    upper = (row <= col).astype(BF16)

    gh, gl = _split_bf16(g_all)
    gc_all = _dot(lower, gh) + _dot(lower, gl)
    gth, gtl = _split_bf16(g_all.T)
    gc_t = _dot(gth, upper) + _dot(gtl, upper)

    heads = range(B_HEADS)
    q, k, v, beta, gc, gc_row, g_last = [], [], [], [], [], [], []
    for h in heads:
        qh = y[:, h * hd:(h + 1) * hd]
        kh = y[:, B_WIDTH + h * hd:B_WIDTH + (h + 1) * hd]
        q.append(qh * lax.rsqrt(jnp.sum(qh * qh, axis=-1, keepdims=True) + EPS) * (hd ** -0.5))
        k.append(kh * lax.rsqrt(jnp.sum(kh * kh, axis=-1, keepdims=True) + EPS))
        v.append(y[:, 2 * B_WIDTH + h * hd:2 * B_WIDTH + (h + 1) * hd])
        beta.append(beta_all[:, Z_SMALL_BETA + h:Z_SMALL_BETA + h + 1])
        gc.append(gc_all[:, Z_SMALL_A + h:Z_SMALL_A + h + 1])
        gc_row.append(gc_t[Z_SMALL_A + h:Z_SMALL_A + h + 1, :])
        g_last.append(gc_row[h][:, c - 1:c])
    decay = [jnp.exp(jnp.where(tril, gc[h] - gc_row[h], NEG)) for h in heads]
    e_gc = [jnp.exp(gc[h]) for h in heads]
    kb = [k[h] * beta[h] for h in heads]
    kf = [k[h].astype(BF16) for h in heads]
    a_pow = [jnp.where(tril_strict, _dot_nt(kb[h].astype(BF16), kf[h]) * decay[h], 0.0) for h in heads]
    attn = [(_dot_nt(q[h].astype(BF16), kf[h]) * decay[h]).astype(BF16) for h in heads]

    t_mat = [eye - a_pow[h] for h in heads]
    for _ in range(5):
        ab = [a_pow[h].astype(BF16) for h in heads]
        a_pow = [_dot(ab[h], ab[h]) for h in heads]
        t_mat = [t_mat[h] + _dot(t_mat[h].astype(BF16), a_pow[h].astype(BF16)) for h in heads]
    tb = [t_mat[h].astype(BF16) for h in heads]
    u = [_dot(tb[h], (v[h] * beta[h]).astype(BF16)) for h in heads]
    w = [_dot(tb[h], (kb[h] * e_gc[h]).astype(BF16)).astype(BF16) for h in heads]

    state = [s_scr[h] for h in heads]
    sb = [state[h].astype(BF16) for h in heads]
    vnb = [(u[h] - _dot(w[h], sb[h])).astype(BF16) for h in heads]
    o = [_dot((q[h] * e_gc[h]).astype(BF16), sb[h]) + _dot(attn[h], vnb[h]) for h in heads]
    k_dec = [(k[h] * jnp.exp(g_last[h] - gc[h])).astype(BF16) for h in heads]
    for h in heads:
        s_scr[h] = state[h] * jnp.exp(g_last[h]) + _dot_tn(k_dec[h], vnb[h])
    for h in heads:
        cols = slice(h * hd, (h + 1) * hd)
        gate = gate_ref[0, :, cols].astype(F32)
        on = o[h] * lax.rsqrt(jnp.mean(o[h] * o[h], axis=-1, keepdims=True) + EPS) * nw_ref[...]
        o_ref[0, :, cols] = (on * (gate * _sigmoid(gate))).astype(o_ref.dtype)


def _gated_deltanet(z3, zs3, conv_w, head_params, norm_w):
    b, s, _ = z3.shape
    n = s // CHUNK
    return pl.pallas_call(
        _gdn_body,
        grid=(b, n),
        in_specs=[
            pl.BlockSpec((1, CHUNK, 3 * B_WIDTH), lambda bi, ni: (bi, ni, Z_QKV_B // (3 * B_WIDTH))),
            pl.BlockSpec((1, CHUNK, LANES), lambda bi, ni: (bi, ni, 0)),
            pl.BlockSpec((1, CHUNK, B_WIDTH), lambda bi, ni: (bi, ni, Z_GATE_B // B_WIDTH)),
            pl.BlockSpec((CONV_WIDTH, 3 * B_WIDTH), lambda bi, ni: (0, 0)),
            pl.BlockSpec((SUBLANES, LANES), lambda bi, ni: (0, 0)),
            pl.BlockSpec((1, B_HEAD_DIM), lambda bi, ni: (0, 0)),
        ],
        out_specs=pl.BlockSpec((1, CHUNK, B_WIDTH), lambda bi, ni: (bi, ni, 0)),
        out_shape=jax.ShapeDtypeStruct((b, s, B_WIDTH), BF16),
        scratch_shapes=[
            pltpu.VMEM((GDN_TAIL + CHUNK, 3 * B_WIDTH), F32),
            pltpu.VMEM((B_HEADS, B_HEAD_DIM, B_HEAD_DIM), F32),
        ],
        compiler_params=pltpu.CompilerParams(
            dimension_semantics=("parallel", "arbitrary"), vmem_limit_bytes=VMEM_LIMIT),
        name="gdn",
    )(z3, zs3, z3, conv_w, head_params, norm_w)


MERGE_TM = 256


def _top_k_rows(s, k, payload=None):
    n = s.shape[0]
    rows = lax.broadcasted_iota(jnp.int32, s.shape, 0)
    vals, picks = [], []
    for _ in range(k):
        m = jnp.max(s, axis=0, keepdims=True)
        i = jnp.min(jnp.where(s == m, rows, n), axis=0, keepdims=True)
        hit = rows == i
        vals.append(m)
        picks.append(i if payload is None else jnp.max(jnp.where(hit, payload, -1), axis=0, keepdims=True))
        s = jnp.where(hit, -jnp.inf, s)
    return jnp.concatenate(vals, axis=0), jnp.concatenate(picks, axis=0)


def _merge_body(oa_ref, ob_ref, mr_ref, x_ref, bm_ref, pa_ref, pb_ref, wo_ref, nw_ref, wq_ref, k1_ref, k2_ref,
                x1_ref, x1p_ref, xnp_ref, idx_ref, gate_ref, gate_tok_ref):
    tm, d = x_ref.shape
    planes = d // LANES

    def store_planes(ref, a):
        for p in range(planes):
            ref[pl.ds(p, tm, stride=planes), :] = a[:, p * LANES:(p + 1) * LANES]

    mr = mr_ref[...].astype(F32) + bm_ref[...]
    mixed = (_sigmoid(mr[:, :d]) * _dot(oa_ref[...], pa_ref[...])
             + _sigmoid(mr[:, d:]) * _dot(ob_ref[...], pb_ref[...]))
    x1 = x_ref[...] + _dot(mixed.astype(BF16), wo_ref[...])
    x1_ref[...] = x1
    store_planes(x1p_ref, x1)
    xn = x1 * lax.rsqrt(jnp.mean(x1 * x1, axis=-1, keepdims=True) + EPS) * nw_ref[...]
    store_planes(xnp_ref, xn)
    q = _dot(xn.astype(BF16), wq_ref[...]).astype(BF16)

    idx_rows, gate_rows = [], []
    for h in range(PEER_HEADS):
        q1 = q[:, (2 * h) * PEER_HALF:(2 * h + 1) * PEER_HALF]
        q2 = q[:, (2 * h + 1) * PEER_HALF:(2 * h + 2) * PEER_HALF]
        s1 = _dot_nt(k1_ref[h], q1)
        s2 = _dot_nt(k2_ref[h], q2)
        v1, i1 = _top_k_rows(s1, PEER_TOPK)
        v2, i2 = _top_k_rows(s2, PEER_TOPK)
        nb = [PEER_TOPK // (a + 1) for a in range(PEER_TOPK)]
        pad = -sum(nb) % SUBLANES
        tm = v1.shape[1]
        cand = jnp.concatenate([v1[a:a + 1] + v2[:nb[a]] for a in range(PEER_TOPK)]
                               + [jnp.full((pad, tm), -jnp.inf, F32)], axis=0)
        cand_idx = jnp.concatenate([i1[a:a + 1] * PEER_NKEYS + i2[:nb[a]] for a in range(PEER_TOPK)]
                                   + [jnp.zeros((pad, tm), jnp.int32)], axis=0)
        top_s, expert = _top_k_rows(cand, PEER_TOPK, payload=cand_idx)
        e = jnp.exp(top_s - top_s[0:1])
        gate_rows.append(e / jnp.sum(e, axis=0, keepdims=True))
        idx_rows.append(expert)
    idx_ref[...] = jnp.concatenate(idx_rows, axis=0).T
    gates = jnp.concatenate(gate_rows, axis=0)
    gate_ref[...] = gates
    gate_tok_ref[...] = gates.T


def _merge_route(o_a, o_b, z2, x2, row0, b_merge, pa, pb, wo, nw, wq, k1, k2):
    t, d = o_a.shape[0], x2.shape[1]
    tm = MERGE_TM
    xblk0 = row0 // tm
    full = lambda a: pl.BlockSpec(a.shape, lambda i: (0,) * a.ndim)
    return pl.pallas_call(
        _merge_body,
        grid=(t // tm,),
        in_specs=[
            pl.BlockSpec((tm, A_WIDTH), lambda i: (i, 0)),
            pl.BlockSpec((tm, B_WIDTH), lambda i: (i, 0)),
            pl.BlockSpec((tm, 2 * d), lambda i: (i, Z_MERGE // (2 * d))),
            pl.BlockSpec((tm, d), lambda i: (i + xblk0, 0)),
            full(b_merge), full(pa), full(pb), full(wo), full(nw), full(wq), full(k1), full(k2),
        ],
        out_specs=[
            pl.BlockSpec((tm, d), lambda i: (i, 0)),
            pl.BlockSpec((tm * d // LANES, LANES), lambda i: (i, 0)),
            pl.BlockSpec((tm * d // LANES, LANES), lambda i: (i, 0)),
            pl.BlockSpec((tm, PEER_SLOTS), lambda i: (i, 0)),
            pl.BlockSpec((PEER_SLOTS, tm), lambda i: (0, i)),
            pl.BlockSpec((tm, PEER_SLOTS), lambda i: (i, 0)),
        ],
        out_shape=[
            jax.ShapeDtypeStruct((t, d), F32),
            jax.ShapeDtypeStruct((t * d // LANES, LANES), F32),
            jax.ShapeDtypeStruct((t * d // LANES, LANES), F32),
            jax.ShapeDtypeStruct((t, PEER_SLOTS), jnp.int32),
            jax.ShapeDtypeStruct((PEER_SLOTS, t), F32),
            jax.ShapeDtypeStruct((t, PEER_SLOTS), F32),
        ],
        compiler_params=pltpu.CompilerParams(
            dimension_semantics=("parallel",), vmem_limit_bytes=VMEM_LIMIT),
        name="merge_route",
    )(o_a, o_b, z2, x2, b_merge, pa, pb, wo, nw, wq, k1, k2)


PEER_TP = 128
PEER_NBUF = 4
PEER_LOOK = PEER_NBUF - 1
PEER_SC_SHARE_NUM, PEER_SC_SHARE_DEN = 18, 32
PEER_TC_FIRST_NUM, PEER_TC_FIRST_DEN = 6, 16


def _erf(x):
    return lax.erf(x)


PACK_HI = -65536


def _packed_down(bitcast, word):
    return bitcast(word & PACK_HI, F32)


def _packed_up(bitcast, word):
    return bitcast(word << 16, F32)


def _peer_body(idx_ref, idx_next_ref, xn_ref, gate_ref, x1_ref, tab_ref, o_ref, *scratch):
    bufs, sem = scratch[:PEER_NBUF], scratch[PEER_NBUF]
    planes = x1_ref.shape[1] // LANES
    step = pl.program_id(0)

    def issue(ref, tok, slot):
        for j in range(PEER_SLOTS):
            pltpu.make_async_copy(tab_ref.at[ref[0, tok, j]], bufs[slot].at[:, j, :], sem.at[slot]).start()

    def wait(slot):
        pltpu.make_async_copy(bufs[slot], bufs[slot], sem.at[slot]).wait()

    @pl.when(step == 0)
    def _():
        for s in range(PEER_LOOK):
            issue(idx_ref, s, s)

    lane = lax.broadcasted_iota(jnp.int32, (PEER_SLOTS, PEER_TP), 1)

    def compute(tok, slot):
        buf = bufs[slot]
        acc = None
        for s in range(planes):
            term = _packed_down(lax.bitcast_convert_type, buf[s]) * xn_ref[pl.ds(tok * planes + s, 1), :]
            acc = term if acc is None else acc + term
        act = jnp.sum(acc, axis=-1, keepdims=True)
        gate = jnp.sum(jnp.where(lane == tok, gate_ref[...], 0.0), axis=-1, keepdims=True)
        w = gate * (0.5 * act * (1.0 + _erf(act * (2.0 ** -0.5))))
        out = jnp.concatenate(
            [jnp.sum(_packed_up(lax.bitcast_convert_type, buf[s]) * w, axis=0, keepdims=True)
             for s in range(planes)], axis=1)
        o_ref[pl.ds(tok, 1), :] = x1_ref[pl.ds(tok, 1), :] + out

    def group(g, carry):
        for u in range(PEER_NBUF):
            tok = g * PEER_NBUF + u
            issue(idx_ref, tok + PEER_LOOK, (u + PEER_LOOK) % PEER_NBUF)
            wait(u)
            compute(tok, u)
        return carry

    n_groups = PEER_TP // PEER_NBUF
    lax.fori_loop(0, n_groups - 1, group, 0)

    for u in range(PEER_NBUF):
        tok = (n_groups - 1) * PEER_NBUF + u
        nxt = tok + PEER_LOOK
        if nxt < PEER_TP:
            issue(idx_ref, nxt, (u + PEER_LOOK) % PEER_NBUF)
        else:
            issue(idx_next_ref, nxt - PEER_TP, (u + PEER_LOOK) % PEER_NBUF)
        wait(u)
        compute(tok, u)

    @pl.when(step == pl.num_programs(0) - 1)
    def _():
        for s in range(PEER_LOOK):
            wait(s)


def _peer_alias_body(prev_ref, after_ref, *rest):
    del prev_ref, after_ref
    _peer_body(*rest)


def _peer(idx3, xnp, gate_t, x1, table, blk0, nb, out_blk0, t_out, prev=None, after=None):
    d = x1.shape[1]
    tp = PEER_TP
    planes = d // LANES
    in_specs = [
        pl.BlockSpec((1, tp, PEER_SLOTS), lambda i: (i + blk0, 0, 0), memory_space=pltpu.SMEM),
        pl.BlockSpec((1, tp, PEER_SLOTS), lambda i: (jnp.minimum(i + 1, nb - 1) + blk0, 0, 0),
                     memory_space=pltpu.SMEM),
        pl.BlockSpec((tp * planes, LANES), lambda i: (i + blk0, 0)),
        pl.BlockSpec((PEER_SLOTS, tp), lambda i: (0, i + blk0)),
        pl.BlockSpec((tp, d), lambda i: (i + blk0, 0)),
        pl.BlockSpec(memory_space=pl.ANY),
    ]
    args = (idx3, idx3, xnp, gate_t, x1, table)
    body, aliases = _peer_body, {}
    if prev is not None:
        body, aliases = _peer_alias_body, {0: 0}
        in_specs = [pl.BlockSpec(memory_space=pl.ANY), pl.BlockSpec(memory_space=pl.ANY)] + in_specs
        args = (prev, after) + args
    return pl.pallas_call(
        body,
        grid=(nb,),
        in_specs=in_specs,
        out_specs=pl.BlockSpec((tp, d), lambda i: (i + blk0 + out_blk0, 0)),
        out_shape=jax.ShapeDtypeStruct((t_out, d), F32),
        scratch_shapes=[pltpu.VMEM((planes, PEER_SLOTS, LANES), jnp.int32) for _ in range(PEER_NBUF)]
        + [pltpu.SemaphoreType.DMA((PEER_NBUF,))],
        input_output_aliases=aliases,
        compiler_params=pltpu.CompilerParams(
            dimension_semantics=("arbitrary",), vmem_limit_bytes=VMEM_LIMIT),
        name="peer",
    )(*args)


SC_CORES = 2
SC_SUBCORES = 16
SC_LANES = 16
SC_WORKERS = SC_CORES * SC_SUBCORES
SC_ROWS = 64


SC_VPR = LANES // SC_LANES
SC_VROWS = PEER_SLOTS // SC_VPR


def _sc_vec(j):
    return (j // SC_VPR, pl.ds((j % SC_VPR) * SC_LANES, SC_LANES))


def _sc_mesh():
    return plsc.VectorSubcoreMesh(core_axis_name="c", subcore_axis_name="s")


def _sc_worker():
    return lax.axis_index("s") * SC_CORES + lax.axis_index("c")


def _sc_pipeline(tw, base, nchunk, loads, gather, compute, finish):
    assert tw % 2 == 0 and nchunk % 2 == 0
    last = base + tw - 1

    def token(tok, s):
        nxt = jnp.minimum(tok + 1, last)
        for c in loads(nxt, 1 - s):
            c.start()
        for kc in range(nchunk):
            if kc + 1 < nchunk:
                gather(s, kc + 1).start()
            else:
                for c in loads(nxt, 1 - s):
                    c.wait()
                gather(1 - s, 0).start()
            gather(s, kc).wait()
            compute(s, kc)
        finish(tok, s)

    for c in loads(base, 0):
        c.start()
    for c in loads(base, 0):
        c.wait()
    gather(0, 0).start()

    def pair(i, carry):
        token(base + 2 * i, 0)
        token(base + 2 * i + 1, 1)
        return carry

    lax.fori_loop(0, tw // 2, pair, 0)
    gather(0, 0).wait()


def _sc_down(idx_flat, xn3, table, tok0, n_tok):
    planes = xn3.shape[1]
    half = planes // 2
    tw = n_tok // SC_WORKERS
    nchunk = PEER_SLOTS // SC_ROWS
    per_plane = LANES // SC_LANES

    @functools.partial(
        pl.kernel, mesh=_sc_mesh(),
        out_type=jax.ShapeDtypeStruct((n_tok, SC_VROWS, LANES), F32),
        scratch_types=[
            pltpu.VMEM((2, PEER_SLOTS), jnp.int32),
            pltpu.VMEM((2, planes, LANES), F32),
            pltpu.VMEM((2, SC_ROWS, half, LANES), jnp.int32),
            pltpu.VMEM((SC_VROWS, LANES), F32),
            pltpu.VMEM((half, SC_VROWS, LANES), F32),
            pltpu.SemaphoreType.DMA((2,)),
            pltpu.SemaphoreType.DMA((2,)),
        ],
        compiler_params=pltpu.CompilerParams(needs_layout_passes=False),
        name="sc_down",
    )
    def run(idx_hbm, xn_hbm, tab_hbm, out_hbm, idx_v, x_v, buf, act_v, part_v, sem_g, sem_l):
        def loads(tok, s):
            return [pltpu.make_async_copy(idx_hbm.at[pl.ds(tok * PEER_SLOTS, PEER_SLOTS)], idx_v.at[s], sem_l.at[s]),
                    pltpu.make_async_copy(xn_hbm.at[tok], x_v.at[s], sem_l.at[s])]

        def gather(s, kc):
            return pltpu.make_async_copy(
                tab_hbm.at[idx_v.at[s, pl.ds(kc * SC_ROWS, SC_ROWS)]], buf.at[kc % 2], sem_g.at[kc % 2])

        def compute(s, kc):
            @plsc.parallel_loop(0, half)
            def _(p):
                sls = [pl.ds(q * SC_LANES, SC_LANES) for q in range(per_plane)]
                x_hi = [x_v[s, p, sl] for sl in sls]
                x_lo = [x_v[s, half + p, sl] for sl in sls]

                def load_row(j):
                    return [buf[kc % 2, j, p, sl] for sl in sls]

                words = load_row(0)
                for j in range(SC_ROWS):
                    ahead = load_row(j + 1) if j + 1 < SC_ROWS else None
                    terms = ([_packed_down(plsc.bitcast, w) * x for w, x in zip(words, x_hi)]
                             + [_packed_up(plsc.bitcast, w) * x for w, x in zip(words, x_lo)])
                    while len(terms) > 1:
                        terms = [terms[i] + terms[i + 1] for i in range(0, len(terms), 2)]
                    part_v[(p,) + _sc_vec(kc * SC_ROWS + j)] = terms[0]
                    words = ahead

        def finish(tok, s):
            @plsc.parallel_loop(0, SC_VROWS)
            def _(r):
                for u in range(SC_VPR):
                    sl = pl.ds(u * SC_LANES, SC_LANES)
                    acc = part_v[0, r, sl]
                    for p in range(1, half):
                        acc = acc + part_v[p, r, sl]
                    act_v[r, sl] = acc
            pltpu.sync_copy(act_v, out_hbm.at[tok - tok0])

        _sc_pipeline(tw, tok0 + _sc_worker() * tw, nchunk, loads, gather, compute, finish)

    return run(idx_flat, xn3, table)


def _sc_up(idx_flat, w16, x13, table, tok0, n_tok):
    planes = x13.shape[1]
    half = planes // 2
    tw = n_tok // SC_WORKERS
    nchunk = PEER_SLOTS // SC_ROWS
    per_plane = LANES // SC_LANES

    @functools.partial(
        pl.kernel, mesh=_sc_mesh(),
        out_type=jax.ShapeDtypeStruct((n_tok, planes, LANES), F32),
        scratch_types=[
            pltpu.VMEM((2, PEER_SLOTS), jnp.int32),
            pltpu.VMEM((2, SC_VROWS, LANES), F32),
            pltpu.VMEM((2, planes, LANES), F32),
            pltpu.VMEM((2, SC_ROWS, half, LANES), jnp.int32),
            pltpu.SemaphoreType.DMA((2,)),
            pltpu.SemaphoreType.DMA((2,)),
        ],
        compiler_params=pltpu.CompilerParams(needs_layout_passes=False),
        name="sc_up",
    )
    def run(idx_hbm, w_hbm, x1_hbm, tab_hbm, out_hbm, idx_v, w_v, out_v, buf, sem_g, sem_l):
        def loads(tok, s):
            return [pltpu.make_async_copy(idx_hbm.at[pl.ds(tok * PEER_SLOTS, PEER_SLOTS)], idx_v.at[s], sem_l.at[s]),
                    pltpu.make_async_copy(w_hbm.at[tok - tok0], w_v.at[s], sem_l.at[s]),
                    pltpu.make_async_copy(x1_hbm.at[tok], out_v.at[s], sem_l.at[s])]

        def gather(s, kc):
            return pltpu.make_async_copy(
                tab_hbm.at[idx_v.at[s, pl.ds(kc * SC_ROWS, SC_ROWS)]], buf.at[kc % 2], sem_g.at[kc % 2])

        def compute(s, kc):
            @plsc.parallel_loop(0, half)
            def _(p):
                sls = [pl.ds(q * SC_LANES, SC_LANES) for q in range(per_plane)]
                acc_hi = [out_v[s, p, sl] for sl in sls]
                acc_lo = [out_v[s, half + p, sl] for sl in sls]
                for j in range(SC_ROWS):
                    wj = w_v[(s,) + _sc_vec(kc * SC_ROWS + j)]
                    words = [buf[kc % 2, j, p, sl] for sl in sls]
                    acc_hi = [a + wj * _packed_down(plsc.bitcast, w) for a, w in zip(acc_hi, words)]
                    acc_lo = [a + wj * _packed_up(plsc.bitcast, w) for a, w in zip(acc_lo, words)]
                for a, sl in zip(acc_hi, sls):
                    out_v[s, p, sl] = a
                for a, sl in zip(acc_lo, sls):
                    out_v[s, half + p, sl] = a

        def finish(tok, s):
            pltpu.sync_copy(out_v.at[s], out_hbm.at[tok - tok0])

        _sc_pipeline(tw, tok0 + _sc_worker() * tw, nchunk, loads, gather, compute, finish)

    return run(idx_flat, w16, x13, table)


ACT_TM = 256


def _peer_act_body(after_ref, part_ref, gate_ref, fold_ref, spread_ref, own_ref, w_ref):
    del after_ref
    tm = part_ref.shape[0]
    own = own_ref[...]
    hi, lo = _split_bf16(part_ref[...].reshape(tm * SC_VROWS, LANES))
    sums = (_dot(hi, fold_ref[...]) + _dot(lo, fold_ref[...])).reshape(tm, SC_VROWS, LANES)
    act = jnp.sum(sums * own, axis=1)
    w = gate_ref[...] * (0.5 * act * (1.0 + _erf(act * (2.0 ** -0.5))))
    hi, lo = _split_bf16((w[:, None, :] * own).reshape(tm * SC_VROWS, LANES))
    w_ref[...] = (_dot(hi, spread_ref[...]) + _dot(lo, spread_ref[...])).reshape(tm, SC_VROWS, LANES)


def _peer_act(after, part3, gate_tok, blk0):
    n_tok = part3.shape[0]
    tm = ACT_TM
    lane = jnp.arange(LANES)
    fold = (lane[:, None] // SC_LANES == lane[None, :] % SC_VPR).astype(BF16)
    spread = fold.T
    own = (lane[None, :] // SC_VPR == jnp.arange(SC_VROWS)[:, None]).astype(F32)
    const = lambda a: pl.BlockSpec(a.shape, lambda i: (0,) * a.ndim)
    return pl.pallas_call(
        _peer_act_body,
        grid=(n_tok // tm,),
        in_specs=[
            pl.BlockSpec(memory_space=pl.ANY),
            pl.BlockSpec((tm, SC_VROWS, LANES), lambda i: (i, 0, 0)),
            pl.BlockSpec((tm, PEER_SLOTS), lambda i: (i + blk0, 0)),
            const(fold), const(spread), const(own),
        ],
        out_specs=pl.BlockSpec((tm, SC_VROWS, LANES), lambda i: (i, 0, 0)),
        out_shape=jax.ShapeDtypeStruct(part3.shape, F32),
        compiler_params=pltpu.CompilerParams(
            dimension_semantics=("parallel",), vmem_limit_bytes=VMEM_LIMIT),
        name="peer_act",
    )(after, part3, gate_tok, fold, spread, own)


def _table_pack_body(down_ref, up_ref, tab_ref, dn2_ref, up2_ref):
    te, d = down_ref.shape
    planes = d // LANES
    half = planes // 2

    def bits(ref, p):
        return lax.bitcast_convert_type(ref[:, p * LANES:(p + 1) * LANES].astype(BF16).astype(F32), jnp.int32)

    def pack(hi, lo):
        return hi | lax.shift_right_logical(lo, 16)

    for p in range(planes):
        tab_ref[pl.ds(p, te, stride=planes), :] = pack(bits(down_ref, p), bits(up_ref, p))
    for p in range(half):
        dn2_ref[pl.ds(p, te, stride=half), :] = pack(bits(down_ref, p), bits(down_ref, half + p))
        up2_ref[pl.ds(p, te, stride=half), :] = pack(bits(up_ref, p), bits(up_ref, half + p))


def _table_pack(down, up, te=512):
    n_exp, d = down.shape
    rows = d // LANES
    return pl.pallas_call(
        _table_pack_body,
        grid=(n_exp // te,),
        in_specs=[pl.BlockSpec((te, d), lambda i: (i, 0)), pl.BlockSpec((te, d), lambda i: (i, 0))],
        out_specs=[pl.BlockSpec((te * rows, LANES), lambda i: (i, 0)),
                   pl.BlockSpec((te * rows // 2, LANES), lambda i: (i, 0)),
                   pl.BlockSpec((te * rows // 2, LANES), lambda i: (i, 0))],
        out_shape=[jax.ShapeDtypeStruct((n_exp * rows, LANES), jnp.int32),
                   jax.ShapeDtypeStruct((n_exp * rows // 2, LANES), jnp.int32),
                   jax.ShapeDtypeStruct((n_exp * rows // 2, LANES), jnp.int32)],
        compiler_params=pltpu.CompilerParams(
            dimension_semantics=("parallel",), vmem_limit_bytes=VMEM_LIMIT),
        name="table_pack",
    )(down, up)


def _pad_lanes(a):
    return jnp.pad(a, ((0, 0), (0, LANES - a.shape[1])))


def kernel(x, norm_mix_w, w_in, a_q_norm_w, a_k_norm_w, a_rel_bias, b_conv_w, b_a_log, b_dt_bias, b_norm_w, b_merge, w_proj_a, w_proj_b, w_out, norm_ffn_w, peer_w_query, peer_keys_1, peer_keys_2, peer_down, peer_up):
    b, s, d = x.shape
    t = b * s
    depth = w_in.shape[0]
    for l in range(depth):
        x2 = x.reshape(t, d)

        wi = w_in[l]
        o_qkvb = 3 * A_WIDTH
        o_a = o_qkvb + 3 * B_WIDTH
        o_beta = o_a + B_HEADS
        o_gate = o_beta + B_HEADS
        o_merge = o_gate + B_WIDTH
        w_main = jnp.concatenate(
            [wi[:, o_qkvb:o_a], wi[:, o_gate:o_merge], wi[:, o_merge:], wi[:, :o_qkvb]], axis=1).astype(BF16)
        w_small = _pad_lanes(wi[:, o_a:o_gate]).astype(BF16)
        z_attn = Z_MERGE + 2 * d

        bias_tab = _attn_bias_table(a_rel_bias[l])
        seg = jnp.kron(jnp.eye(A_HEADS, dtype=F32), jnp.full((A_HEAD_DIM, A_HEAD_DIM), 1.0 / A_HEAD_DIM, F32))
        head_params = jnp.zeros((SUBLANES, LANES), F32)
        head_params = head_params.at[0, :B_HEADS].set(b_a_log[l]).at[1, :B_HEADS].set(b_dt_bias[l])
        head_params = head_params.at[2, :B_HEADS].set(1.0)

        def mixer_and_route(b0, nb):
            tg = nb * s
            z, zs = _in_proj(x2, norm_mix_w[l][None, :], w_main, w_small, min(1024, tg), 1920, b0 * s, tg)
            z3 = z.reshape(nb, s, z.shape[1])
            o_att = _band_attention(
                z3, bias_tab, jnp.tile(a_q_norm_w[l], A_HEADS)[None, :], jnp.tile(a_k_norm_w[l], A_HEADS)[None, :],
                seg.astype(BF16), z_attn // A_WIDTH)
            o_gdn = _gated_deltanet(z3, zs.reshape(nb, s, LANES), b_conv_w[l], head_params, b_norm_w[l][None, :])
            return _merge_route(
                o_att.reshape(tg, A_WIDTH), o_gdn.reshape(tg, B_WIDTH), z, x2, b0 * s, b_merge[l][None, :],
                w_proj_a[l].astype(BF16), w_proj_b[l].astype(BF16), w_out[l].astype(BF16),
                norm_ffn_w[l][None, :], peer_w_query[l].astype(BF16), peer_keys_1[l].astype(BF16),
                peer_keys_2[l].astype(BF16))

        planes = d // LANES
        n_exp = peer_down.shape[1]
        table, down2, up2 = _table_pack(peer_down[l], peer_up[l])
        table = table.reshape(n_exp, planes, LANES)
        down2 = down2.reshape(n_exp, planes // 2, LANES)
        up2 = up2.reshape(n_exp, planes // 2, LANES)

        b_sc = b * PEER_SC_SHARE_NUM // PEER_SC_SHARE_DEN
        n_sc = b_sc * s
        _, x1p_s, xnp_s, idx_s, _, gate_tok_s = mixer_and_route(0, b_sc)
        idx_s = idx_s.reshape(-1)
        part = _sc_down(idx_s, xnp_s.reshape(n_sc, planes, LANES), down2, 0, n_sc)

        x1, _, xnp, idx, gate_t, _ = mixer_and_route(b_sc, b - b_sc)
        nblk = (t - n_sc) // PEER_TP
        nblk_a = nblk * PEER_TC_FIRST_NUM // PEER_TC_FIRST_DEN
        idx3 = idx.reshape(nblk, PEER_TP, PEER_SLOTS)
        out = _peer(idx3, xnp, gate_t, x1, table, 0, nblk_a, n_sc // PEER_TP, t)
        w16 = _peer_act(out, part, gate_tok_s, 0)
        out = _peer(idx3, xnp, gate_t, x1, table, nblk_a, nblk - nblk_a, n_sc // PEER_TP, t, prev=out, after=w16)
        sc_out = _sc_up(idx_s, w16, x1p_s.reshape(n_sc, planes, LANES), up2, 0, n_sc)
        out = lax.dynamic_update_slice(out, sc_out.reshape(n_sc, d), (0, 0))
        x = out.reshape(b, s, d)
    return x
```

```python
import functools

import jax
import jax.numpy as jnp
from jax import lax
from jax.experimental import pallas as pl
from jax.experimental.pallas import tpu as pltpu
from jax.experimental.pallas import tpu_sc as plsc

F32 = jnp.float32
BF16 = jnp.bfloat16

EPS = 1e-6
NEG = -1e30

CHUNK = 64
A_HEADS = 8
A_HEAD_DIM = 64
A_WIDTH = A_HEADS * A_HEAD_DIM
A_LEFT_CHUNKS = 8
REL_CLIP = 128
B_HEADS = 8
B_HEAD_DIM = 128
B_WIDTH = B_HEADS * B_HEAD_DIM
CONV_WIDTH = 4
PEER_HEADS = 8
PEER_HALF = 128
PEER_NKEYS = 128
PEER_TOPK = 16
PEER_SLOTS = PEER_HEADS * PEER_TOPK

LANES = 128
SUBLANES = 8
VMEM_LIMIT = 56 * 1024 * 1024

Z_QKV_B = 0
Z_GATE_B = 3 * B_WIDTH
Z_MERGE = Z_GATE_B + B_WIDTH
Z_SMALL_A = 0
Z_SMALL_BETA = B_HEADS


def _dot(a, b):
    return jnp.dot(a, b, preferred_element_type=F32)


def _dot_nt(a, b):
    return lax.dot_general(a, b, (((1,), (1,)), ((), ())), preferred_element_type=F32)


def _dot_tn(a, b):
    return lax.dot_general(a, b, (((0,), (0,)), ((), ())), preferred_element_type=F32)


def _split_bf16(a):
    hi = a.astype(BF16)
    lo = (a - hi.astype(F32)).astype(BF16)
    return hi, lo


def _sigmoid(x):
    return 1.0 / (1.0 + jnp.exp(-x))


def _inproj_body(x_ref, nw_ref, w_ref, ws_ref, z_ref, zs_ref, h_scr):
    @pl.when(pl.program_id(1) == 0)
    def _():
        x = x_ref[...]
        ms = jnp.mean(x * x, axis=-1, keepdims=True)
        hb = (x * lax.rsqrt(ms + EPS) * nw_ref[...]).astype(BF16)
        h_scr[...] = hb
        zs_ref[...] = _dot(hb, ws_ref[...])

    z_ref[...] = _dot(h_scr[...], w_ref[...]).astype(z_ref.dtype)


def _in_proj(x2, norm_w, w_main, w_small, tm, tn, row0, t):
    d = x2.shape[1]
    n = w_main.shape[1]
    blk0 = row0 // tm
    return pl.pallas_call(
        _inproj_body,
        grid=(t // tm, n // tn),
        in_specs=[
            pl.BlockSpec((tm, d), lambda i, j: (i + blk0, 0)),
            pl.BlockSpec((1, d), lambda i, j: (0, 0)),
            pl.BlockSpec((d, tn), lambda i, j: (0, j)),
            pl.BlockSpec((d, LANES), lambda i, j: (0, 0)),
        ],
        out_specs=[
            pl.BlockSpec((tm, tn), lambda i, j: (i, j)),
            pl.BlockSpec((tm, LANES), lambda i, j: (i, 0)),
        ],
        out_shape=[
            jax.ShapeDtypeStruct((t, n), BF16),
            jax.ShapeDtypeStruct((t, LANES), F32),
        ],
        scratch_shapes=[pltpu.VMEM((tm, d), BF16)],
        compiler_params=pltpu.CompilerParams(
            dimension_semantics=("parallel", "arbitrary"), vmem_limit_bytes=VMEM_LIMIT),
        name="in_proj",
    )(x2, norm_w, w_main, w_small)


ATT_TQ = 256
ATT_WIN = ATT_TQ + A_LEFT_CHUNKS * CHUNK
ATT_NKB = ATT_WIN // ATT_TQ


def _attn_body(q_ref, k0_ref, k1_ref, k2_ref, v0_ref, v1_ref, v2_ref, bias_ref, qw_ref, kw_ref, seg_ref,
               o_ref):
    t = pl.program_id(1)
    seg = seg_ref[...]

    def head_rms(a, w):
        hi, lo = _split_bf16(a * a)
        ms = _dot(hi, seg) + _dot(lo, seg)
        return a * lax.rsqrt(ms + EPS) * w

    q = head_rms(q_ref[0].astype(F32), qw_ref[...]) * (A_HEAD_DIM ** -0.5)
    k = jnp.concatenate([k0_ref[0], k1_ref[0], k2_ref[0]], axis=0).astype(F32)
    k = head_rms(k, kw_ref[...]).astype(BF16)
    v = jnp.concatenate([v0_ref[0], v1_ref[0], v2_ref[0]], axis=0)

    kpos = t * ATT_TQ - A_LEFT_CHUNKS * CHUNK + lax.broadcasted_iota(jnp.int32, (1, ATT_WIN), 1)
    valid = kpos >= 0
    lane = lax.broadcasted_iota(jnp.int32, (1, LANES), 1)
    first = lane < A_HEAD_DIM

    for pair in range(A_HEADS // 2):
        cols = slice(pair * LANES, (pair + 1) * LANES)
        qp = q[:, cols]
        kp = k[:, cols]
        vp = v[:, cols]
        outs = []
        for half in range(2):
            h = 2 * pair + half
            qm = jnp.where(first if half == 0 else jnp.logical_not(first), qp, 0.0).astype(BF16)
            s = _dot_nt(qm, kp) + bias_ref[h]
            s = jnp.where(valid, s, NEG)
            mx = jnp.max(s, axis=-1, keepdims=True)
            p = jnp.exp(s - mx)
            den = jnp.sum(p, axis=-1, keepdims=True)
            outs.append(_dot(p.astype(BF16), vp) / den)
        o_ref[0, :, cols] = jnp.where(first, outs[0], outs[1]).astype(o_ref.dtype)


def _band_attention(z3, bias_tab, qw, kw, seg, col_q):
    b, s, _ = z3.shape
    nt = s // ATT_TQ

    def kv_spec(back, col):
        return pl.BlockSpec((1, ATT_TQ, A_WIDTH), lambda bi, ti: (bi, jnp.maximum(ti - back, 0), col))

    return pl.pallas_call(
        _attn_body,
        grid=(b, nt),
        in_specs=[
            pl.BlockSpec((1, ATT_TQ, A_WIDTH), lambda bi, ti: (bi, ti, col_q)),
            kv_spec(2, col_q + 1), kv_spec(1, col_q + 1), kv_spec(0, col_q + 1),
            kv_spec(2, col_q + 2), kv_spec(1, col_q + 2), kv_spec(0, col_q + 2),
            pl.BlockSpec((A_HEADS, ATT_TQ, ATT_WIN), lambda bi, ti: (0, 0, 0)),
            pl.BlockSpec((1, A_WIDTH), lambda bi, ti: (0, 0)),
            pl.BlockSpec((1, A_WIDTH), lambda bi, ti: (0, 0)),
            pl.BlockSpec((A_WIDTH, A_WIDTH), lambda bi, ti: (0, 0)),
        ],
        out_specs=pl.BlockSpec((1, ATT_TQ, A_WIDTH), lambda bi, ti: (bi, ti, 0)),
        out_shape=jax.ShapeDtypeStruct((b, s, A_WIDTH), BF16),
        compiler_params=pltpu.CompilerParams(
            dimension_semantics=("parallel", "parallel"), vmem_limit_bytes=VMEM_LIMIT),
        name="band_attn",
    )(z3, z3, z3, z3, z3, z3, z3, bias_tab, qw, kw, seg)


def _attn_bias_table(rel_bias):
    rb = rel_bias.astype(F32)
    heads = rb.shape[0]
    lo = A_LEFT_CHUNKS * CHUNK - (ATT_WIN - 1)
    hi = A_LEFT_CHUNKS * CHUNK + ATT_TQ - 1
    strip = jnp.concatenate([
        jnp.broadcast_to(rb[:, :1], (heads, -REL_CLIP - lo)), rb,
        jnp.broadcast_to(rb[:, -1:], (heads, hi - REL_CLIP))], axis=1)
    rev = strip[:, ::-1]
    n = rev.shape[1]
    flat = jnp.broadcast_to(rev[:, None, :], (heads, ATT_TQ, n)).reshape(heads, ATT_TQ * n)
    bias = flat[:, ATT_TQ - 1:ATT_TQ - 1 + ATT_TQ * (n - 1)].reshape(heads, ATT_TQ, n - 1)[:, :, :ATT_WIN]
    qc = jnp.arange(ATT_TQ)[:, None] // CHUNK
    kc = jnp.arange(ATT_WIN)[None, :] // CHUNK
    in_band = (kc >= qc) & (kc <= qc + A_LEFT_CHUNKS)
    return jnp.where(in_band[None], bias, NEG)


GDN_TAIL = SUBLANES


def _gdn_body(qkv_ref, zs_ref, gate_ref, convw_ref, hp_ref, nw_ref, o_ref, xbuf, s_scr):
    c = CHUNK
    hd = B_HEAD_DIM

    @pl.when(pl.program_id(1) == 0)
    def _():
        xbuf[0:GDN_TAIL, :] = jnp.zeros((GDN_TAIL, 3 * B_WIDTH), F32)
        s_scr[...] = jnp.zeros_like(s_scr)

    xbuf[GDN_TAIL:GDN_TAIL + c, :] = qkv_ref[0].astype(F32)
    y = None
    for j in range(CONV_WIDTH):
        tap = convw_ref[j:j + 1, :] * xbuf[pl.ds(GDN_TAIL - (CONV_WIDTH - 1) + j, c), :]
        y = tap if y is None else y + tap
    xbuf[0:GDN_TAIL, :] = xbuf[c:c + GDN_TAIL, :]
    y = y * _sigmoid(y)

    zs = zs_ref[0]
    a_neg = -(jnp.exp(hp_ref[0:1, :]) * hp_ref[2:3, :])
    xs = zs + hp_ref[1:2, :]
    g_all = a_neg * (jnp.maximum(xs, 0.0) + jnp.log(1.0 + jnp.exp(-jnp.abs(xs))))
    beta_all = _sigmoid(zs)

    row = lax.broadcasted_iota(jnp.int32, (c, c), 0)
    col = lax.broadcasted_iota(jnp.int32, (c, c), 1)
    tril = row >= col
    tril_strict = row > col
    eye = (row == col).astype(F32)
    lower = tril.astype(BF16)
    upper = (row <= col).astype(BF16)

    gh, gl = _split_bf16(g_all)
    gc_all = _dot(lower, gh) + _dot(lower, gl)
    gth, gtl = _split_bf16(g_all.T)
    gc_t = _dot(gth, upper) + _dot(gtl, upper)

    heads = range(B_HEADS)
    q, k, v, beta, gc, gc_row, g_last = [], [], [], [], [], [], []
    for h in heads:
        qh = y[:, h * hd:(h + 1) * hd]
        kh = y[:, B_WIDTH + h * hd:B_WIDTH + (h + 1) * hd]
        q.append(qh * lax.rsqrt(jnp.sum(qh * qh, axis=-1, keepdims=True) + EPS) * (hd ** -0.5))
        k.append(kh * lax.rsqrt(jnp.sum(kh * kh, axis=-1, keepdims=True) + EPS))
        v.append(y[:, 2 * B_WIDTH + h * hd:2 * B_WIDTH + (h + 1) * hd])
        beta.append(beta_all[:, Z_SMALL_BETA + h:Z_SMALL_BETA + h + 1])
        gc.append(gc_all[:, Z_SMALL_A + h:Z_SMALL_A + h + 1])
        gc_row.append(gc_t[Z_SMALL_A + h:Z_SMALL_A + h + 1, :])
        g_last.append(gc_row[h][:, c - 1:c])
    decay = [jnp.exp(jnp.where(tril, gc[h] - gc_row[h], NEG)) for h in heads]
    e_gc = [jnp.exp(gc[h]) for h in heads]
    kb = [k[h] * beta[h] for h in heads]
    kf = [k[h].astype(BF16) for h in heads]
    a_pow = [jnp.where(tril_strict, _dot_nt(kb[h].astype(BF16), kf[h]) * decay[h], 0.0) for h in heads]
    attn = [(_dot_nt(q[h].astype(BF16), kf[h]) * decay[h]).astype(BF16) for h in heads]

    t_mat = [eye - a_pow[h] for h in heads]
    for _ in range(5):
        ab = [a_pow[h].astype(BF16) for h in heads]
        a_pow = [_dot(ab[h], ab[h]) for h in heads]
        t_mat = [t_mat[h] + _dot(t_mat[h].astype(BF16), a_pow[h].astype(BF16)) for h in heads]
    tb = [t_mat[h].astype(BF16) for h in heads]
    u = [_dot(tb[h], (v[h] * beta[h]).astype(BF16)) for h in heads]
    w = [_dot(tb[h], (kb[h] * e_gc[h]).astype(BF16)).astype(BF16) for h in heads]

    state = [s_scr[h] for h in heads]
    sb = [state[h].astype(BF16) for h in heads]
    vnb = [(u[h] - _dot(w[h], sb[h])).astype(BF16) for h in heads]
    o = [_dot((q[h] * e_gc[h]).astype(BF16), sb[h]) + _dot(attn[h], vnb[h]) for h in heads]
    k_dec = [(k[h] * jnp.exp(g_last[h] - gc[h])).astype(BF16) for h in heads]
    for h in heads:
        s_scr[h] = state[h] * jnp.exp(g_last[h]) + _dot_tn(k_dec[h], vnb[h])
    for h in heads:
        cols = slice(h * hd, (h + 1) * hd)
        gate = gate_ref[0, :, cols].astype(F32)
        on = o[h] * lax.rsqrt(jnp.mean(o[h] * o[h], axis=-1, keepdims=True) + EPS) * nw_ref[...]
        o_ref[0, :, cols] = (on * (gate * _sigmoid(gate))).astype(o_ref.dtype)


def _gated_deltanet(z3, zs3, conv_w, head_params, norm_w):
    b, s, _ = z3.shape
    n = s // CHUNK
    return pl.pallas_call(
        _gdn_body,
        grid=(b, n),
        in_specs=[
            pl.BlockSpec((1, CHUNK, 3 * B_WIDTH), lambda bi, ni: (bi, ni, Z_QKV_B // (3 * B_WIDTH))),
            pl.BlockSpec((1, CHUNK, LANES), lambda bi, ni: (bi, ni, 0)),
            pl.BlockSpec((1, CHUNK, B_WIDTH), lambda bi, ni: (bi, ni, Z_GATE_B // B_WIDTH)),
            pl.BlockSpec((CONV_WIDTH, 3 * B_WIDTH), lambda bi, ni: (0, 0)),
            pl.BlockSpec((SUBLANES, LANES), lambda bi, ni: (0, 0)),
            pl.BlockSpec((1, B_HEAD_DIM), lambda bi, ni: (0, 0)),
        ],
        out_specs=pl.BlockSpec((1, CHUNK, B_WIDTH), lambda bi, ni: (bi, ni, 0)),
        out_shape=jax.ShapeDtypeStruct((b, s, B_WIDTH), BF16),
        scratch_shapes=[
            pltpu.VMEM((GDN_TAIL + CHUNK, 3 * B_WIDTH), F32),
            pltpu.VMEM((B_HEADS, B_HEAD_DIM, B_HEAD_DIM), F32),
        ],
        compiler_params=pltpu.CompilerParams(
            dimension_semantics=("parallel", "arbitrary"), vmem_limit_bytes=VMEM_LIMIT),
        name="gdn",
    )(z3, zs3, z3, conv_w, head_params, norm_w)


MERGE_TM = 256


def _top_k_rows(s, k, payload=None):
    n = s.shape[0]
    rows = lax.broadcasted_iota(jnp.int32, s.shape, 0)
    vals, picks = [], []
    for _ in range(k):
        m = jnp.max(s, axis=0, keepdims=True)
        i = jnp.min(jnp.where(s == m, rows, n), axis=0, keepdims=True)
        hit = rows == i
        vals.append(m)
        picks.append(i if payload is None else jnp.max(jnp.where(hit, payload, -1), axis=0, keepdims=True))
        s = jnp.where(hit, -jnp.inf, s)
    return jnp.concatenate(vals, axis=0), jnp.concatenate(picks, axis=0)


def _merge_body(oa_ref, ob_ref, mr_ref, x_ref, bm_ref, pa_ref, pb_ref, wo_ref, nw_ref, wq_ref, k1_ref, k2_ref,
                x1_ref, x1p_ref, xnp_ref, idx_ref, gate_ref, gate_tok_ref):
    tm, d = x_ref.shape
    planes = d // LANES

    def store_planes(ref, a):
        for p in range(planes):
            ref[pl.ds(p, tm, stride=planes), :] = a[:, p * LANES:(p + 1) * LANES]

    mr = mr_ref[...].astype(F32) + bm_ref[...]
    mixed = (_sigmoid(mr[:, :d]) * _dot(oa_ref[...], pa_ref[...])
             + _sigmoid(mr[:, d:]) * _dot(ob_ref[...], pb_ref[...]))
    x1 = x_ref[...] + _dot(mixed.astype(BF16), wo_ref[...])
    x1_ref[...] = x1
    store_planes(x1p_ref, x1)
    xn = x1 * lax.rsqrt(jnp.mean(x1 * x1, axis=-1, keepdims=True) + EPS) * nw_ref[...]
    store_planes(xnp_ref, xn)
    q = _dot(xn.astype(BF16), wq_ref[...]).astype(BF16)

    idx_rows, gate_rows = [], []
    for h in range(PEER_HEADS):
        q1 = q[:, (2 * h) * PEER_HALF:(2 * h + 1) * PEER_HALF]
        q2 = q[:, (2 * h + 1) * PEER_HALF:(2 * h + 2) * PEER_HALF]
        s1 = _dot_nt(k1_ref[h], q1)
        s2 = _dot_nt(k2_ref[h], q2)
        v1, i1 = _top_k_rows(s1, PEER_TOPK)
        v2, i2 = _top_k_rows(s2, PEER_TOPK)
        nb = [PEER_TOPK // (a + 1) for a in range(PEER_TOPK)]
        pad = -sum(nb) % SUBLANES
        tm = v1.shape[1]
        cand = jnp.concatenate([v1[a:a + 1] + v2[:nb[a]] for a in range(PEER_TOPK)]
                               + [jnp.full((pad, tm), -jnp.inf, F32)], axis=0)
        cand_idx = jnp.concatenate([i1[a:a + 1] * PEER_NKEYS + i2[:nb[a]] for a in range(PEER_TOPK)]
                                   + [jnp.zeros((pad, tm), jnp.int32)], axis=0)
        top_s, expert = _top_k_rows(cand, PEER_TOPK, payload=cand_idx)
        e = jnp.exp(top_s - top_s[0:1])
        gate_rows.append(e / jnp.sum(e, axis=0, keepdims=True))
        idx_rows.append(expert)
    idx_ref[...] = jnp.concatenate(idx_rows, axis=0).T
    gates = jnp.concatenate(gate_rows, axis=0)
    gate_ref[...] = gates
    gate_tok_ref[...] = gates.T


def _merge_route(o_a, o_b, z2, x2, row0, b_merge, pa, pb, wo, nw, wq, k1, k2):
    t, d = o_a.shape[0], x2.shape[1]
    tm = MERGE_TM
    xblk0 = row0 // tm
    full = lambda a: pl.BlockSpec(a.shape, lambda i: (0,) * a.ndim)
    return pl.pallas_call(
        _merge_body,
        grid=(t // tm,),
        in_specs=[
            pl.BlockSpec((tm, A_WIDTH), lambda i: (i, 0)),
            pl.BlockSpec((tm, B_WIDTH), lambda i: (i, 0)),
            pl.BlockSpec((tm, 2 * d), lambda i: (i, Z_MERGE // (2 * d))),
            pl.BlockSpec((tm, d), lambda i: (i + xblk0, 0)),
            full(b_merge), full(pa), full(pb), full(wo), full(nw), full(wq), full(k1), full(k2),
        ],
        out_specs=[
            pl.BlockSpec((tm, d), lambda i: (i, 0)),
            pl.BlockSpec((tm * d // LANES, LANES), lambda i: (i, 0)),
            pl.BlockSpec((tm * d // LANES, LANES), lambda i: (i, 0)),
            pl.BlockSpec((tm, PEER_SLOTS), lambda i: (i, 0)),
            pl.BlockSpec((PEER_SLOTS, tm), lambda i: (0, i)),
            pl.BlockSpec((tm, PEER_SLOTS), lambda i: (i, 0)),
        ],
        out_shape=[
            jax.ShapeDtypeStruct((t, d), F32),
            jax.ShapeDtypeStruct((t * d // LANES, LANES), F32),
            jax.ShapeDtypeStruct((t * d // LANES, LANES), F32),
            jax.ShapeDtypeStruct((t, PEER_SLOTS), jnp.int32),
            jax.ShapeDtypeStruct((PEER_SLOTS, t), F32),
            jax.ShapeDtypeStruct((t, PEER_SLOTS), F32),
        ],
        compiler_params=pltpu.CompilerParams(
            dimension_semantics=("parallel",), vmem_limit_bytes=VMEM_LIMIT),
        name="merge_route",
    )(o_a, o_b, z2, x2, b_merge, pa, pb, wo, nw, wq, k1, k2)


PEER_TP = 128
PEER_NBUF = 4
PEER_LOOK = PEER_NBUF - 1
PEER_SC_SHARE_NUM, PEER_SC_SHARE_DEN = 16, 32
PEER_TC_FIRST_NUM, PEER_TC_FIRST_DEN = 5, 16


def _erf(x):
    return lax.erf(x)


PACK_HI = -65536


def _packed_down(bitcast, word):
    return bitcast(word & PACK_HI, F32)


def _packed_up(bitcast, word):
    return bitcast(word << 16, F32)


def _peer_body(idx_ref, idx_next_ref, xn_ref, gate_ref, x1_ref, tab_ref, o_ref, *scratch):
    bufs, sem = scratch[:PEER_NBUF], scratch[PEER_NBUF]
    planes = x1_ref.shape[1] // LANES
    step = pl.program_id(0)

    def issue(ref, tok, slot):
        for j in range(PEER_SLOTS):
            pltpu.make_async_copy(
                tab_ref.at[ref[0, tok, j]], bufs[slot].at[:, j, :], sem.at[slot]).start(priority=j % 2)

    def wait(slot):
        pltpu.make_async_copy(bufs[slot], bufs[slot], sem.at[slot]).wait()

    @pl.when(step == 0)
    def _():
        for s in range(PEER_LOOK):
            issue(idx_ref, s, s)

    lane = lax.broadcasted_iota(jnp.int32, (PEER_SLOTS, PEER_TP), 1)

    def compute(tok, slot):
        buf = bufs[slot]
        acc = None
        for s in range(planes):
            term = _packed_down(lax.bitcast_convert_type, buf[s]) * xn_ref[pl.ds(tok * planes + s, 1), :]
            acc = term if acc is None else acc + term
        act = jnp.sum(acc, axis=-1, keepdims=True)
        gate = jnp.sum(jnp.where(lane == tok, gate_ref[...], 0.0), axis=-1, keepdims=True)
        w = gate * (0.5 * act * (1.0 + _erf(act * (2.0 ** -0.5))))
        out = jnp.concatenate(
            [jnp.sum(_packed_up(lax.bitcast_convert_type, buf[s]) * w, axis=0, keepdims=True)
             for s in range(planes)], axis=1)
        o_ref[pl.ds(tok, 1), :] = x1_ref[pl.ds(tok, 1), :] + out

    def group(g, carry):
        for u in range(PEER_NBUF):
            tok = g * PEER_NBUF + u
            issue(idx_ref, tok + PEER_LOOK, (u + PEER_LOOK) % PEER_NBUF)
            wait(u)
            compute(tok, u)
        return carry

    n_groups = PEER_TP // PEER_NBUF
    lax.fori_loop(0, n_groups - 1, group, 0)

    for u in range(PEER_NBUF):
        tok = (n_groups - 1) * PEER_NBUF + u
        nxt = tok + PEER_LOOK
        if nxt < PEER_TP:
            issue(idx_ref, nxt, (u + PEER_LOOK) % PEER_NBUF)
        else:
            issue(idx_next_ref, nxt - PEER_TP, (u + PEER_LOOK) % PEER_NBUF)
        wait(u)
        compute(tok, u)

    @pl.when(step == pl.num_programs(0) - 1)
    def _():
        for s in range(PEER_LOOK):
            wait(s)


def _peer_alias_body(prev_ref, after_ref, *rest):
    del prev_ref, after_ref
    _peer_body(*rest)


def _peer(idx3, xnp, gate_t, x1, table, blk0, nb, out_blk0, t_out, prev=None, after=None):
    d = x1.shape[1]
    tp = PEER_TP
    planes = d // LANES
    in_specs = [
        pl.BlockSpec((1, tp, PEER_SLOTS), lambda i: (i + blk0, 0, 0), memory_space=pltpu.SMEM),
        pl.BlockSpec((1, tp, PEER_SLOTS), lambda i: (jnp.minimum(i + 1, nb - 1) + blk0, 0, 0),
                     memory_space=pltpu.SMEM),
        pl.BlockSpec((tp * planes, LANES), lambda i: (i + blk0, 0)),
        pl.BlockSpec((PEER_SLOTS, tp), lambda i: (0, i + blk0)),
        pl.BlockSpec((tp, d), lambda i: (i + blk0, 0)),
        pl.BlockSpec(memory_space=pl.ANY),
    ]
    args = (idx3, idx3, xnp, gate_t, x1, table)
    body, aliases = _peer_body, {}
    if prev is not None:
        body, aliases = _peer_alias_body, {0: 0}
        in_specs = [pl.BlockSpec(memory_space=pl.ANY), pl.BlockSpec(memory_space=pl.ANY)] + in_specs
        args = (prev, after) + args
    return pl.pallas_call(
        body,
        grid=(nb,),
        in_specs=in_specs,
        out_specs=pl.BlockSpec((tp, d), lambda i: (i + blk0 + out_blk0, 0)),
        out_shape=jax.ShapeDtypeStruct((t_out, d), F32),
        scratch_shapes=[pltpu.VMEM((planes, PEER_SLOTS, LANES), jnp.int32) for _ in range(PEER_NBUF)]
        + [pltpu.SemaphoreType.DMA((PEER_NBUF,))],
        input_output_aliases=aliases,
        compiler_params=pltpu.CompilerParams(
            dimension_semantics=("arbitrary",), vmem_limit_bytes=VMEM_LIMIT),
        name="peer",
    )(*args)


SC_CORES = 2
SC_SUBCORES = 16
SC_LANES = 16
SC_WORKERS = SC_CORES * SC_SUBCORES
SC_ROWS = 64


SC_VPR = LANES // SC_LANES
SC_VROWS = PEER_SLOTS // SC_VPR


def _sc_vec(j):
    return (j // SC_VPR, pl.ds((j % SC_VPR) * SC_LANES, SC_LANES))


def _sc_mesh():
    return plsc.VectorSubcoreMesh(core_axis_name="c", subcore_axis_name="s")


def _sc_worker():
    return lax.axis_index("s") * SC_CORES + lax.axis_index("c")


def _sc_pipeline(tw, base, nchunk, loads, gather, compute, finish):
    assert tw % 2 == 0 and nchunk % 2 == 0
    last = base + tw - 1

    def token(tok, s):
        nxt = jnp.minimum(tok + 1, last)
        for c in loads(nxt, 1 - s):
            c.start()
        for kc in range(nchunk):
            if kc + 1 < nchunk:
                gather(s, kc + 1).start()
            else:
                for c in loads(nxt, 1 - s):
                    c.wait()
                gather(1 - s, 0).start()
            gather(s, kc).wait()
            compute(s, kc)
        finish(tok, s)

    for c in loads(base, 0):
        c.start()
    for c in loads(base, 0):
        c.wait()
    gather(0, 0).start()

    def pair(i, carry):
        token(base + 2 * i, 0)
        token(base + 2 * i + 1, 1)
        return carry

    lax.fori_loop(0, tw // 2, pair, 0)
    gather(0, 0).wait()


def _sc_down(idx_flat, xn3, table, tok0, n_tok):
    planes = xn3.shape[1]
    half = planes // 2
    tw = n_tok // SC_WORKERS
    nchunk = PEER_SLOTS // SC_ROWS
    per_plane = LANES // SC_LANES

    @functools.partial(
        pl.kernel, mesh=_sc_mesh(),
        out_type=jax.ShapeDtypeStruct((n_tok, SC_VROWS, LANES), F32),
        scratch_types=[
            pltpu.VMEM((2, PEER_SLOTS), jnp.int32),
            pltpu.VMEM((2, planes, LANES), F32),
            pltpu.VMEM((2, SC_ROWS, half, LANES), jnp.int32),
            pltpu.VMEM((SC_VROWS, LANES), F32),
            pltpu.VMEM((half, SC_VROWS, LANES), F32),
            pltpu.SemaphoreType.DMA((2,)),
            pltpu.SemaphoreType.DMA((2,)),
        ],
        compiler_params=pltpu.CompilerParams(needs_layout_passes=False),
        name="sc_down",
    )
    def run(idx_hbm, xn_hbm, tab_hbm, out_hbm, idx_v, x_v, buf, act_v, part_v, sem_g, sem_l):
        def loads(tok, s):
            return [pltpu.make_async_copy(idx_hbm.at[pl.ds(tok * PEER_SLOTS, PEER_SLOTS)], idx_v.at[s], sem_l.at[s]),
                    pltpu.make_async_copy(xn_hbm.at[tok], x_v.at[s], sem_l.at[s])]

        def gather(s, kc):
            return pltpu.make_async_copy(
                tab_hbm.at[idx_v.at[s, pl.ds(kc * SC_ROWS, SC_ROWS)]], buf.at[kc % 2], sem_g.at[kc % 2])

        def compute(s, kc):
            @plsc.parallel_loop(0, half)
            def _(p):
                sls = [pl.ds(q * SC_LANES, SC_LANES) for q in range(per_plane)]
                x_hi = [x_v[s, p, sl] for sl in sls]
                x_lo = [x_v[s, half + p, sl] for sl in sls]

                def load_row(j):
                    return [buf[kc % 2, j, p, sl] for sl in sls]

                words = load_row(0)
                for j in range(SC_ROWS):
                    ahead = load_row(j + 1) if j + 1 < SC_ROWS else None
                    terms = ([_packed_down(plsc.bitcast, w) * x for w, x in zip(words, x_hi)]
                             + [_packed_up(plsc.bitcast, w) * x for w, x in zip(words, x_lo)])
                    while len(terms) > 1:
                        terms = [terms[i] + terms[i + 1] for i in range(0, len(terms), 2)]
                    part_v[(p,) + _sc_vec(kc * SC_ROWS + j)] = terms[0]
                    words = ahead

        def finish(tok, s):
            @plsc.parallel_loop(0, SC_VROWS)
            def _(r):
                for u in range(SC_VPR):
                    sl = pl.ds(u * SC_LANES, SC_LANES)
                    acc = part_v[0, r, sl]
                    for p in range(1, half):
                        acc = acc + part_v[p, r, sl]
                    act_v[r, sl] = acc
            pltpu.sync_copy(act_v, out_hbm.at[tok - tok0])

        _sc_pipeline(tw, tok0 + _sc_worker() * tw, nchunk, loads, gather, compute, finish)

    return run(idx_flat, xn3, table)


def _sc_up(idx_flat, w16, x13, table, tok0, n_tok):
    planes = x13.shape[1]
    half = planes // 2
    tw = n_tok // SC_WORKERS
    nchunk = PEER_SLOTS // SC_ROWS
    per_plane = LANES // SC_LANES

    @functools.partial(
        pl.kernel, mesh=_sc_mesh(),
        out_type=jax.ShapeDtypeStruct((n_tok, planes, LANES), F32),
        scratch_types=[
            pltpu.VMEM((2, PEER_SLOTS), jnp.int32),
            pltpu.VMEM((2, SC_VROWS, LANES), F32),
            pltpu.VMEM((2, planes, LANES), F32),
            pltpu.VMEM((2, SC_ROWS, half, LANES), jnp.int32),
            pltpu.SemaphoreType.DMA((2,)),
            pltpu.SemaphoreType.DMA((2,)),
        ],
        compiler_params=pltpu.CompilerParams(needs_layout_passes=False),
        name="sc_up",
    )
    def run(idx_hbm, w_hbm, x1_hbm, tab_hbm, out_hbm, idx_v, w_v, out_v, buf, sem_g, sem_l):
        def loads(tok, s):
            return [pltpu.make_async_copy(idx_hbm.at[pl.ds(tok * PEER_SLOTS, PEER_SLOTS)], idx_v.at[s], sem_l.at[s]),
                    pltpu.make_async_copy(w_hbm.at[tok - tok0], w_v.at[s], sem_l.at[s]),
                    pltpu.make_async_copy(x1_hbm.at[tok], out_v.at[s], sem_l.at[s])]

        def gather(s, kc):
            return pltpu.make_async_copy(
                tab_hbm.at[idx_v.at[s, pl.ds(kc * SC_ROWS, SC_ROWS)]], buf.at[kc % 2], sem_g.at[kc % 2])

        def compute(s, kc):
            @plsc.parallel_loop(0, half)
            def _(p):
                sls = [pl.ds(q * SC_LANES, SC_LANES) for q in range(per_plane)]
                acc_hi = [out_v[s, p, sl] for sl in sls]
                acc_lo = [out_v[s, half + p, sl] for sl in sls]
                for j in range(SC_ROWS):
                    wj = w_v[(s,) + _sc_vec(kc * SC_ROWS + j)]
                    words = [buf[kc % 2, j, p, sl] for sl in sls]
                    acc_hi = [a + wj * _packed_down(plsc.bitcast, w) for a, w in zip(acc_hi, words)]
                    acc_lo = [a + wj * _packed_up(plsc.bitcast, w) for a, w in zip(acc_lo, words)]
                for a, sl in zip(acc_hi, sls):
                    out_v[s, p, sl] = a
                for a, sl in zip(acc_lo, sls):
                    out_v[s, half + p, sl] = a

        def finish(tok, s):
            pltpu.sync_copy(out_v.at[s], out_hbm.at[tok - tok0])

        _sc_pipeline(tw, tok0 + _sc_worker() * tw, nchunk, loads, gather, compute, finish)

    return run(idx_flat, w16, x13, table)


ACT_TM = 256


def _peer_act_body(after_ref, part_ref, gate_ref, fold_ref, spread_ref, own_ref, w_ref):
    del after_ref
    tm = part_ref.shape[0]
    own = own_ref[...]
    hi, lo = _split_bf16(part_ref[...].reshape(tm * SC_VROWS, LANES))
    sums = (_dot(hi, fold_ref[...]) + _dot(lo, fold_ref[...])).reshape(tm, SC_VROWS, LANES)
    act = jnp.sum(sums * own, axis=1)
    w = gate_ref[...] * (0.5 * act * (1.0 + _erf(act * (2.0 ** -0.5))))
    hi, lo = _split_bf16((w[:, None, :] * own).reshape(tm * SC_VROWS, LANES))
    w_ref[...] = (_dot(hi, spread_ref[...]) + _dot(lo, spread_ref[...])).reshape(tm, SC_VROWS, LANES)


def _peer_act(after, part3, gate_tok, blk0):
    n_tok = part3.shape[0]
    tm = ACT_TM
    lane = jnp.arange(LANES)
    fold = (lane[:, None] // SC_LANES == lane[None, :] % SC_VPR).astype(BF16)
    spread = fold.T
    own = (lane[None, :] // SC_VPR == jnp.arange(SC_VROWS)[:, None]).astype(F32)
    const = lambda a: pl.BlockSpec(a.shape, lambda i: (0,) * a.ndim)
    return pl.pallas_call(
        _peer_act_body,
        grid=(n_tok // tm,),
        in_specs=[
            pl.BlockSpec(memory_space=pl.ANY),
            pl.BlockSpec((tm, SC_VROWS, LANES), lambda i: (i, 0, 0)),
            pl.BlockSpec((tm, PEER_SLOTS), lambda i: (i + blk0, 0)),
            const(fold), const(spread), const(own),
        ],
        out_specs=pl.BlockSpec((tm, SC_VROWS, LANES), lambda i: (i, 0, 0)),
        out_shape=jax.ShapeDtypeStruct(part3.shape, F32),
        compiler_params=pltpu.CompilerParams(
            dimension_semantics=("parallel",), vmem_limit_bytes=VMEM_LIMIT),
        name="peer_act",
    )(after, part3, gate_tok, fold, spread, own)


def _table_pack_body(down_ref, up_ref, tab_ref, dn2_ref, up2_ref):
    te, d = down_ref.shape
    planes = d // LANES
    half = planes // 2

    def bits(ref, p):
        return lax.bitcast_convert_type(ref[:, p * LANES:(p + 1) * LANES].astype(BF16).astype(F32), jnp.int32)

    def pack(hi, lo):
        return hi | lax.shift_right_logical(lo, 16)

    for p in range(planes):
        tab_ref[pl.ds(p, te, stride=planes), :] = pack(bits(down_ref, p), bits(up_ref, p))
    for p in range(half):
        dn2_ref[pl.ds(p, te, stride=half), :] = pack(bits(down_ref, p), bits(down_ref, half + p))
        up2_ref[pl.ds(p, te, stride=half), :] = pack(bits(up_ref, p), bits(up_ref, half + p))


def _table_pack(down, up, te=512):
    n_exp, d = down.shape
    rows = d // LANES
    return pl.pallas_call(
        _table_pack_body,
        grid=(n_exp // te,),
        in_specs=[pl.BlockSpec((te, d), lambda i: (i, 0)), pl.BlockSpec((te, d), lambda i: (i, 0))],
        out_specs=[pl.BlockSpec((te * rows, LANES), lambda i: (i, 0)),
                   pl.BlockSpec((te * rows // 2, LANES), lambda i: (i, 0)),
                   pl.BlockSpec((te * rows // 2, LANES), lambda i: (i, 0))],
        out_shape=[jax.ShapeDtypeStruct((n_exp * rows, LANES), jnp.int32),
                   jax.ShapeDtypeStruct((n_exp * rows // 2, LANES), jnp.int32),
                   jax.ShapeDtypeStruct((n_exp * rows // 2, LANES), jnp.int32)],
        compiler_params=pltpu.CompilerParams(
            dimension_semantics=("parallel",), vmem_limit_bytes=VMEM_LIMIT),
        name="table_pack",
    )(down, up)


def _pad_lanes(a):
    return jnp.pad(a, ((0, 0), (0, LANES - a.shape[1])))


def kernel(x, norm_mix_w, w_in, a_q_norm_w, a_k_norm_w, a_rel_bias, b_conv_w, b_a_log, b_dt_bias, b_norm_w, b_merge, w_proj_a, w_proj_b, w_out, norm_ffn_w, peer_w_query, peer_keys_1, peer_keys_2, peer_down, peer_up):
    b, s, d = x.shape
    t = b * s
    depth = w_in.shape[0]
    for l in range(depth):
        x2 = x.reshape(t, d)

        wi = w_in[l]
        o_qkvb = 3 * A_WIDTH
        o_a = o_qkvb + 3 * B_WIDTH
        o_beta = o_a + B_HEADS
        o_gate = o_beta + B_HEADS
        o_merge = o_gate + B_WIDTH
        w_main = jnp.concatenate(
            [wi[:, o_qkvb:o_a], wi[:, o_gate:o_merge], wi[:, o_merge:], wi[:, :o_qkvb]], axis=1).astype(BF16)
        w_small = _pad_lanes(wi[:, o_a:o_gate]).astype(BF16)
        z_attn = Z_MERGE + 2 * d

        bias_tab = _attn_bias_table(a_rel_bias[l])
        seg = jnp.kron(jnp.eye(A_HEADS, dtype=F32), jnp.full((A_HEAD_DIM, A_HEAD_DIM), 1.0 / A_HEAD_DIM, F32))
        head_params = jnp.zeros((SUBLANES, LANES), F32)
        head_params = head_params.at[0, :B_HEADS].set(b_a_log[l]).at[1, :B_HEADS].set(b_dt_bias[l])
        head_params = head_params.at[2, :B_HEADS].set(1.0)

        def mixer_and_route(b0, nb):
            tg = nb * s
            z, zs = _in_proj(x2, norm_mix_w[l][None, :], w_main, w_small, min(1024, tg), 1920, b0 * s, tg)
            z3 = z.reshape(nb, s, z.shape[1])
            o_att = _band_attention(
                z3, bias_tab, jnp.tile(a_q_norm_w[l], A_HEADS)[None, :], jnp.tile(a_k_norm_w[l], A_HEADS)[None, :],
                seg.astype(BF16), z_attn // A_WIDTH)
            o_gdn = _gated_deltanet(z3, zs.reshape(nb, s, LANES), b_conv_w[l], head_params, b_norm_w[l][None, :])
            return _merge_route(
                o_att.reshape(tg, A_WIDTH), o_gdn.reshape(tg, B_WIDTH), z, x2, b0 * s, b_merge[l][None, :],
                w_proj_a[l].astype(BF16), w_proj_b[l].astype(BF16), w_out[l].astype(BF16),
                norm_ffn_w[l][None, :], peer_w_query[l].astype(BF16), peer_keys_1[l].astype(BF16),
                peer_keys_2[l].astype(BF16))

        planes = d // LANES
        n_exp = peer_down.shape[1]
        table, down2, up2 = _table_pack(peer_down[l], peer_up[l])
        table = table.reshape(n_exp, planes, LANES)
        down2 = down2.reshape(n_exp, planes // 2, LANES)
        up2 = up2.reshape(n_exp, planes // 2, LANES)

        b_sc = b * PEER_SC_SHARE_NUM // PEER_SC_SHARE_DEN
        n_sc = b_sc * s
        _, x1p_s, xnp_s, idx_s, _, gate_tok_s = mixer_and_route(0, b_sc)
        idx_s = idx_s.reshape(-1)
        part = _sc_down(idx_s, xnp_s.reshape(n_sc, planes, LANES), down2, 0, n_sc)

        x1, _, xnp, idx, gate_t, _ = mixer_and_route(b_sc, b - b_sc)
        nblk = (t - n_sc) // PEER_TP
        nblk_a = nblk * PEER_TC_FIRST_NUM // PEER_TC_FIRST_DEN
        idx3 = idx.reshape(nblk, PEER_TP, PEER_SLOTS)
        out = _peer(idx3, xnp, gate_t, x1, table, 0, nblk_a, n_sc // PEER_TP, t)
        w16 = _peer_act(out, part, gate_tok_s, 0)
        out = _peer(idx3, xnp, gate_t, x1, table, nblk_a, nblk - nblk_a, n_sc // PEER_TP, t, prev=out, after=w16)
        sc_out = _sc_up(idx_s, w16, x1p_s.reshape(n_sc, planes, LANES), up2, 0, n_sc)
        out = lax.dynamic_update_slice(out, sc_out.reshape(n_sc, d), (0, 0))
        x = out.reshape(b, s, d)
    return x
```

```python
import functools

import jax
import jax.numpy as jnp
from jax import lax
from jax.experimental import pallas as pl
from jax.experimental.pallas import tpu as pltpu
from jax.experimental.pallas import tpu_sc as plsc

F32 = jnp.float32
BF16 = jnp.bfloat16

EPS = 1e-6
NEG = -1e30

CHUNK = 64
A_HEADS = 8
A_HEAD_DIM = 64
A_WIDTH = A_HEADS * A_HEAD_DIM
A_LEFT_CHUNKS = 8
REL_CLIP = 128
B_HEADS = 8
B_HEAD_DIM = 128
B_WIDTH = B_HEADS * B_HEAD_DIM
CONV_WIDTH = 4
PEER_HEADS = 8
PEER_HALF = 128
PEER_NKEYS = 128
PEER_TOPK = 16
PEER_SLOTS = PEER_HEADS * PEER_TOPK

LANES = 128
SUBLANES = 8
VMEM_LIMIT = 56 * 1024 * 1024

Z_QKV_B = 0
Z_GATE_B = 3 * B_WIDTH
Z_MERGE = Z_GATE_B + B_WIDTH
Z_SMALL_A = 0
Z_SMALL_BETA = B_HEADS


def _dot(a, b):
    return jnp.dot(a, b, preferred_element_type=F32)


def _dot_nt(a, b):
    return lax.dot_general(a, b, (((1,), (1,)), ((), ())), preferred_element_type=F32)


def _dot_tn(a, b):
    return lax.dot_general(a, b, (((0,), (0,)), ((), ())), preferred_element_type=F32)


def _split_bf16(a):
    hi = a.astype(BF16)
    lo = (a - hi.astype(F32)).astype(BF16)
    return hi, lo


def _sigmoid(x):
    return 1.0 / (1.0 + jnp.exp(-x))


def _inproj_body(x_ref, nw_ref, w_ref, ws_ref, z_ref, zs_ref, h_scr):
    @pl.when(pl.program_id(1) == 0)
    def _():
        x = x_ref[...]
        ms = jnp.mean(x * x, axis=-1, keepdims=True)
        hb = (x * lax.rsqrt(ms + EPS) * nw_ref[...]).astype(BF16)
        h_scr[...] = hb
        zs_ref[...] = _dot(hb, ws_ref[...])

    z_ref[...] = _dot(h_scr[...], w_ref[...]).astype(z_ref.dtype)


def _in_proj(x2, norm_w, w_main, w_small, tm, tn, row0, t):
    d = x2.shape[1]
    n = w_main.shape[1]
    blk0 = row0 // tm
    return pl.pallas_call(
        _inproj_body,
        grid=(t // tm, n // tn),
        in_specs=[
            pl.BlockSpec((tm, d), lambda i, j: (i + blk0, 0)),
            pl.BlockSpec((1, d), lambda i, j: (0, 0)),
            pl.BlockSpec((d, tn), lambda i, j: (0, j)),
            pl.BlockSpec((d, LANES), lambda i, j: (0, 0)),
        ],
        out_specs=[
            pl.BlockSpec((tm, tn), lambda i, j: (i, j)),
            pl.BlockSpec((tm, LANES), lambda i, j: (i, 0)),
        ],
        out_shape=[
            jax.ShapeDtypeStruct((t, n), BF16),
            jax.ShapeDtypeStruct((t, LANES), F32),
        ],
        scratch_shapes=[pltpu.VMEM((tm, d), BF16)],
        compiler_params=pltpu.CompilerParams(
            dimension_semantics=("parallel", "arbitrary"), vmem_limit_bytes=VMEM_LIMIT),
        name="in_proj",
    )(x2, norm_w, w_main, w_small)


ATT_TQ = 256
ATT_WIN = ATT_TQ + A_LEFT_CHUNKS * CHUNK
ATT_NKB = ATT_WIN // ATT_TQ


def _attn_body(q_ref, k0_ref, k1_ref, k2_ref, v0_ref, v1_ref, v2_ref, bias_ref, qw_ref, kw_ref, seg_ref,
               o_ref):
    t = pl.program_id(1)
    seg = seg_ref[...]

    def head_rms(a, w):
        hi, lo = _split_bf16(a * a)
        ms = _dot(hi, seg) + _dot(lo, seg)
        return a * lax.rsqrt(ms + EPS) * w

    q = head_rms(q_ref[0].astype(F32), qw_ref[...]) * (A_HEAD_DIM ** -0.5)
    k = jnp.concatenate([k0_ref[0], k1_ref[0], k2_ref[0]], axis=0).astype(F32)
    k = head_rms(k, kw_ref[...]).astype(BF16)
    v = jnp.concatenate([v0_ref[0], v1_ref[0], v2_ref[0]], axis=0)

    kpos = t * ATT_TQ - A_LEFT_CHUNKS * CHUNK + lax.broadcasted_iota(jnp.int32, (1, ATT_WIN), 1)
    valid = kpos >= 0
    lane = lax.broadcasted_iota(jnp.int32, (1, LANES), 1)
    first = lane < A_HEAD_DIM

    for pair in range(A_HEADS // 2):
        cols = slice(pair * LANES, (pair + 1) * LANES)
        qp = q[:, cols]
        kp = k[:, cols]
        vp = v[:, cols]
        outs = []
        for half in range(2):
            h = 2 * pair + half
            qm = jnp.where(first if half == 0 else jnp.logical_not(first), qp, 0.0).astype(BF16)
            s = _dot_nt(qm, kp) + bias_ref[h]
            s = jnp.where(valid, s, NEG)
            mx = jnp.max(s, axis=-1, keepdims=True)
            p = jnp.exp(s - mx)
            den = jnp.sum(p, axis=-1, keepdims=True)
            outs.append(_dot(p.astype(BF16), vp) / den)
        o_ref[0, :, cols] = jnp.where(first, outs[0], outs[1]).astype(o_ref.dtype)


def _band_attention(z3, bias_tab, qw, kw, seg, col_q):
    b, s, _ = z3.shape
    nt = s // ATT_TQ

    def kv_spec(back, col):
        return pl.BlockSpec((1, ATT_TQ, A_WIDTH), lambda bi, ti: (bi, jnp.maximum(ti - back, 0), col))

    return pl.pallas_call(
        _attn_body,
        grid=(b, nt),
        in_specs=[
            pl.BlockSpec((1, ATT_TQ, A_WIDTH), lambda bi, ti: (bi, ti, col_q)),
            kv_spec(2, col_q + 1), kv_spec(1, col_q + 1), kv_spec(0, col_q + 1),
            kv_spec(2, col_q + 2), kv_spec(1, col_q + 2), kv_spec(0, col_q + 2),
            pl.BlockSpec((A_HEADS, ATT_TQ, ATT_WIN), lambda bi, ti: (0, 0, 0)),
            pl.BlockSpec((1, A_WIDTH), lambda bi, ti: (0, 0)),
            pl.BlockSpec((1, A_WIDTH), lambda bi, ti: (0, 0)),
            pl.BlockSpec((A_WIDTH, A_WIDTH), lambda bi, ti: (0, 0)),
        ],
        out_specs=pl.BlockSpec((1, ATT_TQ, A_WIDTH), lambda bi, ti: (bi, ti, 0)),
        out_shape=jax.ShapeDtypeStruct((b, s, A_WIDTH), BF16),
        compiler_params=pltpu.CompilerParams(
            dimension_semantics=("parallel", "parallel"), vmem_limit_bytes=VMEM_LIMIT),
        name="band_attn",
    )(z3, z3, z3, z3, z3, z3, z3, bias_tab, qw, kw, seg)


def _attn_bias_table(rel_bias):
    rb = rel_bias.astype(F32)
    heads = rb.shape[0]
    lo = A_LEFT_CHUNKS * CHUNK - (ATT_WIN - 1)
    hi = A_LEFT_CHUNKS * CHUNK + ATT_TQ - 1
    strip = jnp.concatenate([
        jnp.broadcast_to(rb[:, :1], (heads, -REL_CLIP - lo)), rb,
        jnp.broadcast_to(rb[:, -1:], (heads, hi - REL_CLIP))], axis=1)
    rev = strip[:, ::-1]
    n = rev.shape[1]
    flat = jnp.broadcast_to(rev[:, None, :], (heads, ATT_TQ, n)).reshape(heads, ATT_TQ * n)
    bias = flat[:, ATT_TQ - 1:ATT_TQ - 1 + ATT_TQ * (n - 1)].reshape(heads, ATT_TQ, n - 1)[:, :, :ATT_WIN]
    qc = jnp.arange(ATT_TQ)[:, None] // CHUNK
    kc = jnp.arange(ATT_WIN)[None, :] // CHUNK
    in_band = (kc >= qc) & (kc <= qc + A_LEFT_CHUNKS)
    return jnp.where(in_band[None], bias, NEG)


GDN_TAIL = SUBLANES


def _gdn_body(qkv_ref, zs_ref, gate_ref, convw_ref, hp_ref, nw_ref, o_ref, xbuf, s_scr):
    c = CHUNK
    hd = B_HEAD_DIM

    @pl.when(pl.program_id(1) == 0)
    def _():
        xbuf[0:GDN_TAIL, :] = jnp.zeros((GDN_TAIL, 3 * B_WIDTH), F32)
        s_scr[...] = jnp.zeros_like(s_scr)

    xbuf[GDN_TAIL:GDN_TAIL + c, :] = qkv_ref[0].astype(F32)
    y = None
    for j in range(CONV_WIDTH):
        tap = convw_ref[j:j + 1, :] * xbuf[pl.ds(GDN_TAIL - (CONV_WIDTH - 1) + j, c), :]
        y = tap if y is None else y + tap
    xbuf[0:GDN_TAIL, :] = xbuf[c:c + GDN_TAIL, :]
    y = y * _sigmoid(y)

    zs = zs_ref[0]
    a_neg = -(jnp.exp(hp_ref[0:1, :]) * hp_ref[2:3, :])
    xs = zs + hp_ref[1:2, :]
    g_all = a_neg * (jnp.maximum(xs, 0.0) + jnp.log(1.0 + jnp.exp(-jnp.abs(xs))))
    beta_all = _sigmoid(zs)

    row = lax.broadcasted_iota(jnp.int32, (c, c), 0)
    col = lax.broadcasted_iota(jnp.int32, (c, c), 1)
    tril = row >= col
    tril_strict = row > col
    eye = (row == col).astype(F32)
    lower = tril.astype(BF16)
    upper = (row <= col).astype(BF16)

    gh, gl = _split_bf16(g_all)
    gc_all = _dot(lower, gh) + _dot(lower, gl)
    gth, gtl = _split_bf16(g_all.T)
    gc_t = _dot(gth, upper) + _dot(gtl, upper)

    heads = range(B_HEADS)
    q, k, v, beta, gc, gc_row, g_last = [], [], [], [], [], [], []
    for h in heads:
        qh = y[:, h * hd:(h + 1) * hd]
        kh = y[:, B_WIDTH + h * hd:B_WIDTH + (h + 1) * hd]
        q.append(qh * lax.rsqrt(jnp.sum(qh * qh, axis=-1, keepdims=True) + EPS) * (hd ** -0.5))
        k.append(kh * lax.rsqrt(jnp.sum(kh * kh, axis=-1, keepdims=True) + EPS))
        v.append(y[:, 2 * B_WIDTH + h * hd:2 * B_WIDTH + (h + 1) * hd])
        beta.append(beta_all[:, Z_SMALL_BETA + h:Z_SMALL_BETA + h + 1])
        gc.append(gc_all[:, Z_SMALL_A + h:Z_SMALL_A + h + 1])
        gc_row.append(gc_t[Z_SMALL_A + h:Z_SMALL_A + h + 1, :])
        g_last.append(gc_row[h][:, c - 1:c])
    decay = [jnp.exp(jnp.where(tril, gc[h] - gc_row[h], NEG)) for h in heads]
    e_gc = [jnp.exp(gc[h]) for h in heads]
    kb = [k[h] * beta[h] for h in heads]
    kf = [k[h].astype(BF16) for h in heads]
    a_pow = [jnp.where(tril_strict, _dot_nt(kb[h].astype(BF16), kf[h]) * decay[h], 0.0) for h in heads]
    attn = [(_dot_nt(q[h].astype(BF16), kf[h]) * decay[h]).astype(BF16) for h in heads]

    t_mat = [eye - a_pow[h] for h in heads]
    for _ in range(5):
        ab = [a_pow[h].astype(BF16) for h in heads]
        a_pow = [_dot(ab[h], ab[h]) for h in heads]
        t_mat = [t_mat[h] + _dot(t_mat[h].astype(BF16), a_pow[h].astype(BF16)) for h in heads]
    tb = [t_mat[h].astype(BF16) for h in heads]
    u = [_dot(tb[h], (v[h] * beta[h]).astype(BF16)) for h in heads]
    w = [_dot(tb[h], (kb[h] * e_gc[h]).astype(BF16)).astype(BF16) for h in heads]

    state = [s_scr[h] for h in heads]
    sb = [state[h].astype(BF16) for h in heads]
    vnb = [(u[h] - _dot(w[h], sb[h])).astype(BF16) for h in heads]
    o = [_dot((q[h] * e_gc[h]).astype(BF16), sb[h]) + _dot(attn[h], vnb[h]) for h in heads]
    k_dec = [(k[h] * jnp.exp(g_last[h] - gc[h])).astype(BF16) for h in heads]
    for h in heads:
        s_scr[h] = state[h] * jnp.exp(g_last[h]) + _dot_tn(k_dec[h], vnb[h])
    for h in heads:
        cols = slice(h * hd, (h + 1) * hd)
        gate = gate_ref[0, :, cols].astype(F32)
        on = o[h] * lax.rsqrt(jnp.mean(o[h] * o[h], axis=-1, keepdims=True) + EPS) * nw_ref[...]
        o_ref[0, :, cols] = (on * (gate * _sigmoid(gate))).astype(o_ref.dtype)


def _gated_deltanet(z3, zs3, conv_w, head_params, norm_w):
    b, s, _ = z3.shape
    n = s // CHUNK
    return pl.pallas_call(
        _gdn_body,
        grid=(b, n),
        in_specs=[
            pl.BlockSpec((1, CHUNK, 3 * B_WIDTH), lambda bi, ni: (bi, ni, Z_QKV_B // (3 * B_WIDTH))),
            pl.BlockSpec((1, CHUNK, LANES), lambda bi, ni: (bi, ni, 0)),
            pl.BlockSpec((1, CHUNK, B_WIDTH), lambda bi, ni: (bi, ni, Z_GATE_B // B_WIDTH)),
            pl.BlockSpec((CONV_WIDTH, 3 * B_WIDTH), lambda bi, ni: (0, 0)),
            pl.BlockSpec((SUBLANES, LANES), lambda bi, ni: (0, 0)),
            pl.BlockSpec((1, B_HEAD_DIM), lambda bi, ni: (0, 0)),
        ],
        out_specs=pl.BlockSpec((1, CHUNK, B_WIDTH), lambda bi, ni: (bi, ni, 0)),
        out_shape=jax.ShapeDtypeStruct((b, s, B_WIDTH), BF16),
        scratch_shapes=[
            pltpu.VMEM((GDN_TAIL + CHUNK, 3 * B_WIDTH), F32),
            pltpu.VMEM((B_HEADS, B_HEAD_DIM, B_HEAD_DIM), F32),
        ],
        compiler_params=pltpu.CompilerParams(
            dimension_semantics=("parallel", "arbitrary"), vmem_limit_bytes=VMEM_LIMIT),
        name="gdn",
    )(z3, zs3, z3, conv_w, head_params, norm_w)


MERGE_TM = 256


def _top_k_rows(s, k, payload=None):
    n = s.shape[0]
    rows = lax.broadcasted_iota(jnp.int32, s.shape, 0)
    vals, picks = [], []
    for _ in range(k):
        m = jnp.max(s, axis=0, keepdims=True)
        i = jnp.min(jnp.where(s == m, rows, n), axis=0, keepdims=True)
        hit = rows == i
        vals.append(m)
        picks.append(i if payload is None else jnp.max(jnp.where(hit, payload, -1), axis=0, keepdims=True))
        s = jnp.where(hit, -jnp.inf, s)
    return jnp.concatenate(vals, axis=0), jnp.concatenate(picks, axis=0)


def _merge_body(oa_ref, ob_ref, mr_ref, x_ref, bm_ref, pa_ref, pb_ref, wo_ref, nw_ref, wq_ref, k1_ref, k2_ref,
                x1_ref, x1p_ref, xnp_ref, idx_ref, gate_ref, gate_tok_ref):
    tm, d = x_ref.shape
    planes = d // LANES

    def store_planes(ref, a):
        for p in range(planes):
            ref[pl.ds(p, tm, stride=planes), :] = a[:, p * LANES:(p + 1) * LANES]

    mr = mr_ref[...].astype(F32) + bm_ref[...]
    mixed = (_sigmoid(mr[:, :d]) * _dot(oa_ref[...], pa_ref[...])
             + _sigmoid(mr[:, d:]) * _dot(ob_ref[...], pb_ref[...]))
    x1 = x_ref[...] + _dot(mixed.astype(BF16), wo_ref[...])
    x1_ref[...] = x1
    store_planes(x1p_ref, x1)
    xn = x1 * lax.rsqrt(jnp.mean(x1 * x1, axis=-1, keepdims=True) + EPS) * nw_ref[...]
    store_planes(xnp_ref, xn)
    q = _dot(xn.astype(BF16), wq_ref[...]).astype(BF16)

    idx_rows, gate_rows = [], []
    for h in range(PEER_HEADS):
        q1 = q[:, (2 * h) * PEER_HALF:(2 * h + 1) * PEER_HALF]
        q2 = q[:, (2 * h + 1) * PEER_HALF:(2 * h + 2) * PEER_HALF]
        s1 = _dot_nt(k1_ref[h], q1)
        s2 = _dot_nt(k2_ref[h], q2)
        v1, i1 = _top_k_rows(s1, PEER_TOPK)
        v2, i2 = _top_k_rows(s2, PEER_TOPK)
        nb = [PEER_TOPK // (a + 1) for a in range(PEER_TOPK)]
        pad = -sum(nb) % SUBLANES
        tm = v1.shape[1]
        cand = jnp.concatenate([v1[a:a + 1] + v2[:nb[a]] for a in range(PEER_TOPK)]
                               + [jnp.full((pad, tm), -jnp.inf, F32)], axis=0)
        cand_idx = jnp.concatenate([i1[a:a + 1] * PEER_NKEYS + i2[:nb[a]] for a in range(PEER_TOPK)]
                                   + [jnp.zeros((pad, tm), jnp.int32)], axis=0)
        top_s, expert = _top_k_rows(cand, PEER_TOPK, payload=cand_idx)
        e = jnp.exp(top_s - top_s[0:1])
        gate_rows.append(e / jnp.sum(e, axis=0, keepdims=True))
        idx_rows.append(expert)
    idx_ref[...] = jnp.concatenate(idx_rows, axis=0).T
    gates = jnp.concatenate(gate_rows, axis=0)
    gate_ref[...] = gates
    gate_tok_ref[...] = gates.T


def _merge_route(o_a, o_b, z2, x2, row0, b_merge, pa, pb, wo, nw, wq, k1, k2):
    t, d = o_a.shape[0], x2.shape[1]
    tm = MERGE_TM
    xblk0 = row0 // tm
    full = lambda a: pl.BlockSpec(a.shape, lambda i: (0,) * a.ndim)
    return pl.pallas_call(
        _merge_body,
        grid=(t // tm,),
        in_specs=[
            pl.BlockSpec((tm, A_WIDTH), lambda i: (i, 0)),
            pl.BlockSpec((tm, B_WIDTH), lambda i: (i, 0)),
            pl.BlockSpec((tm, 2 * d), lambda i: (i, Z_MERGE // (2 * d))),
            pl.BlockSpec((tm, d), lambda i: (i + xblk0, 0)),
            full(b_merge), full(pa), full(pb), full(wo), full(nw), full(wq), full(k1), full(k2),
        ],
        out_specs=[
            pl.BlockSpec((tm, d), lambda i: (i, 0)),
            pl.BlockSpec((tm * d // LANES, LANES), lambda i: (i, 0)),
            pl.BlockSpec((tm * d // LANES, LANES), lambda i: (i, 0)),
            pl.BlockSpec((tm, PEER_SLOTS), lambda i: (i, 0)),
            pl.BlockSpec((PEER_SLOTS, tm), lambda i: (0, i)),
            pl.BlockSpec((tm, PEER_SLOTS), lambda i: (i, 0)),
        ],
        out_shape=[
            jax.ShapeDtypeStruct((t, d), F32),
            jax.ShapeDtypeStruct((t * d // LANES, LANES), F32),
            jax.ShapeDtypeStruct((t * d // LANES, LANES), F32),
            jax.ShapeDtypeStruct((t, PEER_SLOTS), jnp.int32),
            jax.ShapeDtypeStruct((PEER_SLOTS, t), F32),
            jax.ShapeDtypeStruct((t, PEER_SLOTS), F32),
        ],
        compiler_params=pltpu.CompilerParams(
            dimension_semantics=("parallel",), vmem_limit_bytes=VMEM_LIMIT),
        name="merge_route",
    )(o_a, o_b, z2, x2, b_merge, pa, pb, wo, nw, wq, k1, k2)


PEER_TP = 128
PEER_NBUF = 4
PEER_LOOK = PEER_NBUF - 1
PEER_SC_SHARE_NUM, PEER_SC_SHARE_DEN = 18, 32
PEER_TC_FIRST_NUM, PEER_TC_FIRST_DEN = 13, 32


def _erf(x):
    return lax.erf(x)


PACK_HI = -65536


def _packed_down(bitcast, word):
    return bitcast(word & PACK_HI, F32)


def _packed_up(bitcast, word):
    return bitcast(word << 16, F32)


def _peer_body(idx_ref, idx_next_ref, xn_ref, gate_ref, x1_ref, tab_ref, o_ref, *scratch):
    bufs, sem = scratch[:PEER_NBUF], scratch[PEER_NBUF]
    planes = x1_ref.shape[1] // LANES
    step = pl.program_id(0)

    def issue(ref, tok, slot):
        for j in range(PEER_SLOTS):
            pltpu.make_async_copy(
                tab_ref.at[ref[0, tok, j]], bufs[slot].at[:, j, :], sem.at[slot]).start(priority=j % 2)

    def wait(slot):
        pltpu.make_async_copy(bufs[slot], bufs[slot], sem.at[slot]).wait()

    @pl.when(step == 0)
    def _():
        for s in range(PEER_LOOK):
            issue(idx_ref, s, s)

    lane = lax.broadcasted_iota(jnp.int32, (PEER_SLOTS, PEER_TP), 1)

    def compute(tok, slot):
        buf = bufs[slot]
        acc = None
        for s in range(planes):
            term = _packed_down(lax.bitcast_convert_type, buf[s]) * xn_ref[pl.ds(tok * planes + s, 1), :]
            acc = term if acc is None else acc + term
        act = jnp.sum(acc, axis=-1, keepdims=True)
        gate = jnp.sum(jnp.where(lane == tok, gate_ref[...], 0.0), axis=-1, keepdims=True)
        w = gate * (0.5 * act * (1.0 + _erf(act * (2.0 ** -0.5))))
        out = jnp.concatenate(
            [jnp.sum(_packed_up(lax.bitcast_convert_type, buf[s]) * w, axis=0, keepdims=True)
             for s in range(planes)], axis=1)
        o_ref[pl.ds(tok, 1), :] = x1_ref[pl.ds(tok, 1), :] + out

    def group(g, carry):
        for u in range(PEER_NBUF):
            tok = g * PEER_NBUF + u
            issue(idx_ref, tok + PEER_LOOK, (u + PEER_LOOK) % PEER_NBUF)
            wait(u)
            compute(tok, u)
        return carry

    n_groups = PEER_TP // PEER_NBUF
    lax.fori_loop(0, n_groups - 1, group, 0)

    for u in range(PEER_NBUF):
        tok = (n_groups - 1) * PEER_NBUF + u
        nxt = tok + PEER_LOOK
        if nxt < PEER_TP:
            issue(idx_ref, nxt, (u + PEER_LOOK) % PEER_NBUF)
        else:
            issue(idx_next_ref, nxt - PEER_TP, (u + PEER_LOOK) % PEER_NBUF)
        wait(u)
        compute(tok, u)

    @pl.when(step == pl.num_programs(0) - 1)
    def _():
        for s in range(PEER_LOOK):
            wait(s)


def _peer_alias_body(prev_ref, after_ref, *rest):
    del prev_ref, after_ref
    _peer_body(*rest)


def _peer(idx3, xnp, gate_t, x1, table, blk0, nb, out_blk0, t_out, prev=None, after=None):
    d = x1.shape[1]
    tp = PEER_TP
    planes = d // LANES
    in_specs = [
        pl.BlockSpec((1, tp, PEER_SLOTS), lambda i: (i + blk0, 0, 0), memory_space=pltpu.SMEM),
        pl.BlockSpec((1, tp, PEER_SLOTS), lambda i: (jnp.minimum(i + 1, nb - 1) + blk0, 0, 0),
                     memory_space=pltpu.SMEM),
        pl.BlockSpec((tp * planes, LANES), lambda i: (i + blk0, 0)),
        pl.BlockSpec((PEER_SLOTS, tp), lambda i: (0, i + blk0)),
        pl.BlockSpec((tp, d), lambda i: (i + blk0, 0)),
        pl.BlockSpec(memory_space=pl.ANY),
    ]
    args = (idx3, idx3, xnp, gate_t, x1, table)
    body, aliases = _peer_body, {}
    if prev is not None:
        body, aliases = _peer_alias_body, {0: 0}
        in_specs = [pl.BlockSpec(memory_space=pl.ANY), pl.BlockSpec(memory_space=pl.ANY)] + in_specs
        args = (prev, after) + args
    return pl.pallas_call(
        body,
        grid=(nb,),
        in_specs=in_specs,
        out_specs=pl.BlockSpec((tp, d), lambda i: (i + blk0 + out_blk0, 0)),
        out_shape=jax.ShapeDtypeStruct((t_out, d), F32),
        scratch_shapes=[pltpu.VMEM((planes, PEER_SLOTS, LANES), jnp.int32) for _ in range(PEER_NBUF)]
        + [pltpu.SemaphoreType.DMA((PEER_NBUF,))],
        input_output_aliases=aliases,
        compiler_params=pltpu.CompilerParams(
            dimension_semantics=("arbitrary",), vmem_limit_bytes=VMEM_LIMIT),
        name="peer",
    )(*args)


SC_CORES = 2
SC_SUBCORES = 16
SC_LANES = 16
SC_WORKERS = SC_CORES * SC_SUBCORES
SC_ROWS = 64


SC_VPR = LANES // SC_LANES
SC_VROWS = PEER_SLOTS // SC_VPR


def _sc_vec(j):
    return (j // SC_VPR, pl.ds((j % SC_VPR) * SC_LANES, SC_LANES))


def _sc_mesh():
    return plsc.VectorSubcoreMesh(core_axis_name="c", subcore_axis_name="s")


def _sc_worker():
    return lax.axis_index("s") * SC_CORES + lax.axis_index("c")


def _sc_pipeline(tw, base, nchunk, loads, gather, compute, finish):
    assert tw % 2 == 0 and nchunk % 2 == 0
    last = base + tw - 1

    def token(tok, s):
        nxt = jnp.minimum(tok + 1, last)
        for c in loads(nxt, 1 - s):
            c.start()
        for kc in range(nchunk):
            if kc + 1 < nchunk:
                gather(s, kc + 1).start()
            else:
                for c in loads(nxt, 1 - s):
                    c.wait()
                gather(1 - s, 0).start()
            gather(s, kc).wait()
            compute(s, kc)
        finish(tok, s)

    for c in loads(base, 0):
        c.start()
    for c in loads(base, 0):
        c.wait()
    gather(0, 0).start()

    def pair(i, carry):
        token(base + 2 * i, 0)
        token(base + 2 * i + 1, 1)
        return carry

    lax.fori_loop(0, tw // 2, pair, 0)
    gather(0, 0).wait()


def _sc_down(idx_flat, xn3, table, tok0, n_tok):
    planes = xn3.shape[1]
    half = planes // 2
    tw = n_tok // SC_WORKERS
    nchunk = PEER_SLOTS // SC_ROWS
    per_plane = LANES // SC_LANES

    @functools.partial(
        pl.kernel, mesh=_sc_mesh(),
        out_type=jax.ShapeDtypeStruct((n_tok, SC_VROWS, LANES), F32),
        scratch_types=[
            pltpu.VMEM((2, PEER_SLOTS), jnp.int32),
            pltpu.VMEM((2, planes, LANES), F32),
            pltpu.VMEM((2, SC_ROWS, half, LANES), jnp.int32),
            pltpu.VMEM((SC_VROWS, LANES), F32),
            pltpu.VMEM((half, SC_VROWS, LANES), F32),
            pltpu.SemaphoreType.DMA((2,)),
            pltpu.SemaphoreType.DMA((2,)),
        ],
        compiler_params=pltpu.CompilerParams(needs_layout_passes=False),
        name="sc_down",
    )
    def run(idx_hbm, xn_hbm, tab_hbm, out_hbm, idx_v, x_v, buf, act_v, part_v, sem_g, sem_l):
        def loads(tok, s):
            return [pltpu.make_async_copy(idx_hbm.at[pl.ds(tok * PEER_SLOTS, PEER_SLOTS)], idx_v.at[s], sem_l.at[s]),
                    pltpu.make_async_copy(xn_hbm.at[tok], x_v.at[s], sem_l.at[s])]

        def gather(s, kc):
            return pltpu.make_async_copy(
                tab_hbm.at[idx_v.at[s, pl.ds(kc * SC_ROWS, SC_ROWS)]], buf.at[kc % 2], sem_g.at[kc % 2])

        def compute(s, kc):
            @plsc.parallel_loop(0, half)
            def _(p):
                sls = [pl.ds(q * SC_LANES, SC_LANES) for q in range(per_plane)]
                x_hi = [x_v[s, p, sl] for sl in sls]
                x_lo = [x_v[s, half + p, sl] for sl in sls]

                def load_row(j):
                    return [buf[kc % 2, j, p, sl] for sl in sls]

                words = load_row(0)
                for j in range(SC_ROWS):
                    ahead = load_row(j + 1) if j + 1 < SC_ROWS else None
                    terms = ([_packed_down(plsc.bitcast, w) * x for w, x in zip(words, x_hi)]
                             + [_packed_up(plsc.bitcast, w) * x for w, x in zip(words, x_lo)])
                    while len(terms) > 1:
                        terms = [terms[i] + terms[i + 1] for i in range(0, len(terms), 2)]
                    part_v[(p,) + _sc_vec(kc * SC_ROWS + j)] = terms[0]
                    words = ahead

        def finish(tok, s):
            @plsc.parallel_loop(0, SC_VROWS)
            def _(r):
                for u in range(SC_VPR):
                    sl = pl.ds(u * SC_LANES, SC_LANES)
                    acc = part_v[0, r, sl]
                    for p in range(1, half):
                        acc = acc + part_v[p, r, sl]
                    act_v[r, sl] = acc
            pltpu.sync_copy(act_v, out_hbm.at[tok - tok0])

        _sc_pipeline(tw, tok0 + _sc_worker() * tw, nchunk, loads, gather, compute, finish)

    return run(idx_flat, xn3, table)


def _sc_up(idx_flat, w16, x13, table, tok0, n_tok):
    planes = x13.shape[1]
    half = planes // 2
    tw = n_tok // SC_WORKERS
    nchunk = PEER_SLOTS // SC_ROWS
    per_plane = LANES // SC_LANES

    @functools.partial(
        pl.kernel, mesh=_sc_mesh(),
        out_type=jax.ShapeDtypeStruct((n_tok, planes, LANES), F32),
        scratch_types=[
            pltpu.VMEM((2, PEER_SLOTS), jnp.int32),
            pltpu.VMEM((2, SC_VROWS, LANES), F32),
            pltpu.VMEM((2, planes, LANES), F32),
            pltpu.VMEM((2, SC_ROWS, half, LANES), jnp.int32),
            pltpu.SemaphoreType.DMA((2,)),
            pltpu.SemaphoreType.DMA((2,)),
        ],
        compiler_params=pltpu.CompilerParams(needs_layout_passes=False),
        name="sc_up",
    )
    def run(idx_hbm, w_hbm, x1_hbm, tab_hbm, out_hbm, idx_v, w_v, out_v, buf, sem_g, sem_l):
        def loads(tok, s):
            return [pltpu.make_async_copy(idx_hbm.at[pl.ds(tok * PEER_SLOTS, PEER_SLOTS)], idx_v.at[s], sem_l.at[s]),
                    pltpu.make_async_copy(w_hbm.at[tok - tok0], w_v.at[s], sem_l.at[s]),
                    pltpu.make_async_copy(x1_hbm.at[tok], out_v.at[s], sem_l.at[s])]

        def gather(s, kc):
            return pltpu.make_async_copy(
                tab_hbm.at[idx_v.at[s, pl.ds(kc * SC_ROWS, SC_ROWS)]], buf.at[kc % 2], sem_g.at[kc % 2])

        def compute(s, kc):
            @plsc.parallel_loop(0, half)
            def _(p):
                sls = [pl.ds(q * SC_LANES, SC_LANES) for q in range(per_plane)]
                acc_hi = [out_v[s, p, sl] for sl in sls]
                acc_lo = [out_v[s, half + p, sl] for sl in sls]
                for j in range(SC_ROWS):
                    wj = w_v[(s,) + _sc_vec(kc * SC_ROWS + j)]
                    words = [buf[kc % 2, j, p, sl] for sl in sls]
                    acc_hi = [a + wj * _packed_down(plsc.bitcast, w) for a, w in zip(acc_hi, words)]
                    acc_lo = [a + wj * _packed_up(plsc.bitcast, w) for a, w in zip(acc_lo, words)]
                for a, sl in zip(acc_hi, sls):
                    out_v[s, p, sl] = a
                for a, sl in zip(acc_lo, sls):
                    out_v[s, half + p, sl] = a

        def finish(tok, s):
            pltpu.sync_copy(out_v.at[s], out_hbm.at[tok - tok0])

        _sc_pipeline(tw, tok0 + _sc_worker() * tw, nchunk, loads, gather, compute, finish)

    return run(idx_flat, w16, x13, table)


ACT_TM = 256


def _peer_act_body(after_ref, part_ref, gate_ref, fold_ref, spread_ref, own_ref, w_ref):
    del after_ref
    tm = part_ref.shape[0]
    own = own_ref[...]
    hi, lo = _split_bf16(part_ref[...].reshape(tm * SC_VROWS, LANES))
    sums = (_dot(hi, fold_ref[...]) + _dot(lo, fold_ref[...])).reshape(tm, SC_VROWS, LANES)
    act = jnp.sum(sums * own, axis=1)
    w = gate_ref[...] * (0.5 * act * (1.0 + _erf(act * (2.0 ** -0.5))))
    hi, lo = _split_bf16((w[:, None, :] * own).reshape(tm * SC_VROWS, LANES))
    w_ref[...] = (_dot(hi, spread_ref[...]) + _dot(lo, spread_ref[...])).reshape(tm, SC_VROWS, LANES)


def _peer_act(after, part3, gate_tok, blk0):
    n_tok = part3.shape[0]
    tm = ACT_TM
    lane = jnp.arange(LANES)
    fold = (lane[:, None] // SC_LANES == lane[None, :] % SC_VPR).astype(BF16)
    spread = fold.T
    own = (lane[None, :] // SC_VPR == jnp.arange(SC_VROWS)[:, None]).astype(F32)
    const = lambda a: pl.BlockSpec(a.shape, lambda i: (0,) * a.ndim)
    return pl.pallas_call(
        _peer_act_body,
        grid=(n_tok // tm,),
        in_specs=[
            pl.BlockSpec(memory_space=pl.ANY),
            pl.BlockSpec((tm, SC_VROWS, LANES), lambda i: (i, 0, 0)),
            pl.BlockSpec((tm, PEER_SLOTS), lambda i: (i + blk0, 0)),
            const(fold), const(spread), const(own),
        ],
        out_specs=pl.BlockSpec((tm, SC_VROWS, LANES), lambda i: (i, 0, 0)),
        out_shape=jax.ShapeDtypeStruct(part3.shape, F32),
        compiler_params=pltpu.CompilerParams(
            dimension_semantics=("parallel",), vmem_limit_bytes=VMEM_LIMIT),
        name="peer_act",
    )(after, part3, gate_tok, fold, spread, own)


def _table_pack_body(down_ref, up_ref, tab_ref, dn2_ref, up2_ref):
    te, d = down_ref.shape
    planes = d // LANES
    half = planes // 2

    def bits(ref, p):
        return lax.bitcast_convert_type(ref[:, p * LANES:(p + 1) * LANES].astype(BF16).astype(F32), jnp.int32)

    def pack(hi, lo):
        return hi | lax.shift_right_logical(lo, 16)

    for p in range(planes):
        tab_ref[pl.ds(p, te, stride=planes), :] = pack(bits(down_ref, p), bits(up_ref, p))
    for p in range(half):
        dn2_ref[pl.ds(p, te, stride=half), :] = pack(bits(down_ref, p), bits(down_ref, half + p))
        up2_ref[pl.ds(p, te, stride=half), :] = pack(bits(up_ref, p), bits(up_ref, half + p))


def _table_pack(down, up, te=512):
    n_exp, d = down.shape
    rows = d // LANES
    return pl.pallas_call(
        _table_pack_body,
        grid=(n_exp // te,),
        in_specs=[pl.BlockSpec((te, d), lambda i: (i, 0)), pl.BlockSpec((te, d), lambda i: (i, 0))],
        out_specs=[pl.BlockSpec((te * rows, LANES), lambda i: (i, 0)),
                   pl.BlockSpec((te * rows // 2, LANES), lambda i: (i, 0)),
                   pl.BlockSpec((te * rows // 2, LANES), lambda i: (i, 0))],
        out_shape=[jax.ShapeDtypeStruct((n_exp * rows, LANES), jnp.int32),
                   jax.ShapeDtypeStruct((n_exp * rows // 2, LANES), jnp.int32),
                   jax.ShapeDtypeStruct((n_exp * rows // 2, LANES), jnp.int32)],
        compiler_params=pltpu.CompilerParams(
            dimension_semantics=("parallel",), vmem_limit_bytes=VMEM_LIMIT),
        name="table_pack",
    )(down, up)


def _pad_lanes(a):
    return jnp.pad(a, ((0, 0), (0, LANES - a.shape[1])))


def kernel(x, norm_mix_w, w_in, a_q_norm_w, a_k_norm_w, a_rel_bias, b_conv_w, b_a_log, b_dt_bias, b_norm_w, b_merge, w_proj_a, w_proj_b, w_out, norm_ffn_w, peer_w_query, peer_keys_1, peer_keys_2, peer_down, peer_up):
    b, s, d = x.shape
    t = b * s
    depth = w_in.shape[0]
    for l in range(depth):
        x2 = x.reshape(t, d)

        wi = w_in[l]
        o_qkvb = 3 * A_WIDTH
        o_a = o_qkvb + 3 * B_WIDTH
        o_beta = o_a + B_HEADS
        o_gate = o_beta + B_HEADS
        o_merge = o_gate + B_WIDTH
        w_main = jnp.concatenate(
            [wi[:, o_qkvb:o_a], wi[:, o_gate:o_merge], wi[:, o_merge:], wi[:, :o_qkvb]], axis=1).astype(BF16)
        w_small = _pad_lanes(wi[:, o_a:o_gate]).astype(BF16)
        z_attn = Z_MERGE + 2 * d

        bias_tab = _attn_bias_table(a_rel_bias[l])
        seg = jnp.kron(jnp.eye(A_HEADS, dtype=F32), jnp.full((A_HEAD_DIM, A_HEAD_DIM), 1.0 / A_HEAD_DIM, F32))
        head_params = jnp.zeros((SUBLANES, LANES), F32)
        head_params = head_params.at[0, :B_HEADS].set(b_a_log[l]).at[1, :B_HEADS].set(b_dt_bias[l])
        head_params = head_params.at[2, :B_HEADS].set(1.0)

        def mixer_and_route(b0, nb):
            tg = nb * s
            z, zs = _in_proj(x2, norm_mix_w[l][None, :], w_main, w_small, min(1024, tg), 1920, b0 * s, tg)
            z3 = z.reshape(nb, s, z.shape[1])
            o_att = _band_attention(
                z3, bias_tab, jnp.tile(a_q_norm_w[l], A_HEADS)[None, :], jnp.tile(a_k_norm_w[l], A_HEADS)[None, :],
                seg.astype(BF16), z_attn // A_WIDTH)
            o_gdn = _gated_deltanet(z3, zs.reshape(nb, s, LANES), b_conv_w[l], head_params, b_norm_w[l][None, :])
            return _merge_route(
                o_att.reshape(tg, A_WIDTH), o_gdn.reshape(tg, B_WIDTH), z, x2, b0 * s, b_merge[l][None, :],
                w_proj_a[l].astype(BF16), w_proj_b[l].astype(BF16), w_out[l].astype(BF16),
                norm_ffn_w[l][None, :], peer_w_query[l].astype(BF16), peer_keys_1[l].astype(BF16),
                peer_keys_2[l].astype(BF16))

        planes = d // LANES
        n_exp = peer_down.shape[1]
        table, down2, up2 = _table_pack(peer_down[l], peer_up[l])
        table = table.reshape(n_exp, planes, LANES)
        down2 = down2.reshape(n_exp, planes // 2, LANES)
        up2 = up2.reshape(n_exp, planes // 2, LANES)

        b_sc = b * PEER_SC_SHARE_NUM // PEER_SC_SHARE_DEN
        n_sc = b_sc * s
        _, x1p_s, xnp_s, idx_s, _, gate_tok_s = mixer_and_route(0, b_sc)
        idx_s = idx_s.reshape(-1)
        part = _sc_down(idx_s, xnp_s.reshape(n_sc, planes, LANES), down2, 0, n_sc)

        x1, _, xnp, idx, gate_t, _ = mixer_and_route(b_sc, b - b_sc)
        nblk = (t - n_sc) // PEER_TP
        nblk_a = nblk * PEER_TC_FIRST_NUM // PEER_TC_FIRST_DEN
        idx3 = idx.reshape(nblk, PEER_TP, PEER_SLOTS)
        out = _peer(idx3, xnp, gate_t, x1, table, 0, nblk_a, n_sc // PEER_TP, t)
        w16 = _peer_act(out, part, gate_tok_s, 0)
        out = _peer(idx3, xnp, gate_t, x1, table, nblk_a, nblk - nblk_a, n_sc // PEER_TP, t, prev=out, after=w16)
        sc_out = _sc_up(idx_s, w16, x1p_s.reshape(n_sc, planes, LANES), up2, 0, n_sc)
        out = lax.dynamic_update_slice(out, sc_out.reshape(n_sc, d), (0, 0))
        x = out.reshape(b, s, d)
    return x
```

```python
import functools

import jax
import jax.numpy as jnp
from jax import lax
from jax.experimental import pallas as pl
from jax.experimental.pallas import tpu as pltpu
from jax.experimental.pallas import tpu_sc as plsc

F32 = jnp.float32
BF16 = jnp.bfloat16

EPS = 1e-6
NEG = -1e30

CHUNK = 64
A_HEADS = 8
A_HEAD_DIM = 64
A_WIDTH = A_HEADS * A_HEAD_DIM
A_LEFT_CHUNKS = 8
REL_CLIP = 128
B_HEADS = 8
B_HEAD_DIM = 128
B_WIDTH = B_HEADS * B_HEAD_DIM
CONV_WIDTH = 4
PEER_HEADS = 8
PEER_HALF = 128
PEER_NKEYS = 128
PEER_TOPK = 16
PEER_SLOTS = PEER_HEADS * PEER_TOPK

LANES = 128
SUBLANES = 8
VMEM_LIMIT = 56 * 1024 * 1024

Z_QKV_B = 0
Z_GATE_B = 3 * B_WIDTH
Z_MERGE = Z_GATE_B + B_WIDTH
Z_SMALL_A = 0
Z_SMALL_BETA = B_HEADS


def _dot(a, b):
    return jnp.dot(a, b, preferred_element_type=F32)


def _dot_nt(a, b):
    return lax.dot_general(a, b, (((1,), (1,)), ((), ())), preferred_element_type=F32)


def _dot_tn(a, b):
    return lax.dot_general(a, b, (((0,), (0,)), ((), ())), preferred_element_type=F32)


def _split_bf16(a):
    hi = a.astype(BF16)
    lo = (a - hi.astype(F32)).astype(BF16)
    return hi, lo


def _sigmoid(x):
    return 1.0 / (1.0 + jnp.exp(-x))


IN_PROJ_TM = 1024
IN_PROJ_TN = 1920


def _inproj_body(x_ref, nw_ref, w_ref, ws_ref, z_ref, zs_ref, h_scr):
    @pl.when(pl.program_id(1) == 0)
    def _():
        x = x_ref[...]
        ms = jnp.mean(x * x, axis=-1, keepdims=True)
        hb = (x * lax.rsqrt(ms + EPS) * nw_ref[...]).astype(BF16)
        h_scr[...] = hb
        zs_ref[...] = _dot(hb, ws_ref[...])

    z_ref[...] = _dot(h_scr[...], w_ref[...]).astype(z_ref.dtype)


def _in_proj(x2, norm_w, w_main, w_small, tm, tn, row0, t):
    d = x2.shape[1]
    n = w_main.shape[1]
    blk0 = row0 // tm
    return pl.pallas_call(
        _inproj_body,
        grid=(t // tm, n // tn),
        in_specs=[
            pl.BlockSpec((tm, d), lambda i, j: (i + blk0, 0)),
            pl.BlockSpec((1, d), lambda i, j: (0, 0)),
            pl.BlockSpec((d, tn), lambda i, j: (0, j)),
            pl.BlockSpec((d, LANES), lambda i, j: (0, 0)),
        ],
        out_specs=[
            pl.BlockSpec((tm, tn), lambda i, j: (i, j)),
            pl.BlockSpec((tm, LANES), lambda i, j: (i, 0)),
        ],
        out_shape=[
            jax.ShapeDtypeStruct((t, n), BF16),
            jax.ShapeDtypeStruct((t, LANES), F32),
        ],
        scratch_shapes=[pltpu.VMEM((tm, d), BF16)],
        compiler_params=pltpu.CompilerParams(
            dimension_semantics=("parallel", "arbitrary"), vmem_limit_bytes=VMEM_LIMIT),
        name="in_proj",
    )(x2, norm_w, w_main, w_small)


ATT_TQ = 256
ATT_WIN = ATT_TQ + A_LEFT_CHUNKS * CHUNK


def _attn_body(q_ref, k0_ref, k1_ref, k2_ref, v0_ref, v1_ref, v2_ref, bias_ref, qw_ref, kw_ref, seg_ref,
               o_ref):
    t = pl.program_id(1)
    seg = seg_ref[...]

    def head_rms(a, w):
        hi, lo = _split_bf16(a * a)
        ms = _dot(hi, seg) + _dot(lo, seg)
        return a * lax.rsqrt(ms + EPS) * w

    q = head_rms(q_ref[0].astype(F32), qw_ref[...]) * (A_HEAD_DIM ** -0.5)
    k = jnp.concatenate([k0_ref[0], k1_ref[0], k2_ref[0]], axis=0).astype(F32)
    k = head_rms(k, kw_ref[...]).astype(BF16)
    v = jnp.concatenate([v0_ref[0], v1_ref[0], v2_ref[0]], axis=0)

    kpos = t * ATT_TQ - A_LEFT_CHUNKS * CHUNK + lax.broadcasted_iota(jnp.int32, (1, ATT_WIN), 1)
    valid = kpos >= 0
    lane = lax.broadcasted_iota(jnp.int32, (1, LANES), 1)
    first = lane < A_HEAD_DIM

    for pair in range(A_HEADS // 2):
        cols = slice(pair * LANES, (pair + 1) * LANES)
        qp = q[:, cols]
        kp = k[:, cols]
        vp = v[:, cols]
        outs = []
        for half in range(2):
            h = 2 * pair + half
            qm = jnp.where(first if half == 0 else jnp.logical_not(first), qp, 0.0).astype(BF16)
            s = _dot_nt(qm, kp) + bias_ref[h]
            s = jnp.where(valid, s, NEG)
            mx = jnp.max(s, axis=-1, keepdims=True)
            p = jnp.exp(s - mx)
            den = jnp.sum(p, axis=-1, keepdims=True)
            outs.append(_dot(p.astype(BF16), vp) / den)
        o_ref[0, :, cols] = jnp.where(first, outs[0], outs[1]).astype(o_ref.dtype)


def _band_attention(z3, bias_tab, qw, kw, seg, col_q):
    b, s, _ = z3.shape
    nt = s // ATT_TQ

    def kv_spec(back, col):
        return pl.BlockSpec((1, ATT_TQ, A_WIDTH), lambda bi, ti: (bi, jnp.maximum(ti - back, 0), col))

    return pl.pallas_call(
        _attn_body,
        grid=(b, nt),
        in_specs=[
            pl.BlockSpec((1, ATT_TQ, A_WIDTH), lambda bi, ti: (bi, ti, col_q)),
            kv_spec(2, col_q + 1), kv_spec(1, col_q + 1), kv_spec(0, col_q + 1),
            kv_spec(2, col_q + 2), kv_spec(1, col_q + 2), kv_spec(0, col_q + 2),
            pl.BlockSpec((A_HEADS, ATT_TQ, ATT_WIN), lambda bi, ti: (0, 0, 0)),
            pl.BlockSpec((1, A_WIDTH), lambda bi, ti: (0, 0)),
            pl.BlockSpec((1, A_WIDTH), lambda bi, ti: (0, 0)),
            pl.BlockSpec((A_WIDTH, A_WIDTH), lambda bi, ti: (0, 0)),
        ],
        out_specs=pl.BlockSpec((1, ATT_TQ, A_WIDTH), lambda bi, ti: (bi, ti, 0)),
        out_shape=jax.ShapeDtypeStruct((b, s, A_WIDTH), BF16),
        compiler_params=pltpu.CompilerParams(
            dimension_semantics=("parallel", "parallel"), vmem_limit_bytes=VMEM_LIMIT),
        name="band_attn",
    )(z3, z3, z3, z3, z3, z3, z3, bias_tab, qw, kw, seg)


def _attn_bias_table(rel_bias):
    rb = rel_bias.astype(F32)
    heads = rb.shape[0]
    lo = A_LEFT_CHUNKS * CHUNK - (ATT_WIN - 1)
    hi = A_LEFT_CHUNKS * CHUNK + ATT_TQ - 1
    strip = jnp.concatenate([
        jnp.broadcast_to(rb[:, :1], (heads, -REL_CLIP - lo)), rb,
        jnp.broadcast_to(rb[:, -1:], (heads, hi - REL_CLIP))], axis=1)
    rev = strip[:, ::-1]
    n = rev.shape[1]
    flat = jnp.broadcast_to(rev[:, None, :], (heads, ATT_TQ, n)).reshape(heads, ATT_TQ * n)
    bias = flat[:, ATT_TQ - 1:ATT_TQ - 1 + ATT_TQ * (n - 1)].reshape(heads, ATT_TQ, n - 1)[:, :, :ATT_WIN]
    qc = jnp.arange(ATT_TQ)[:, None] // CHUNK
    kc = jnp.arange(ATT_WIN)[None, :] // CHUNK
    in_band = (kc >= qc) & (kc <= qc + A_LEFT_CHUNKS)
    return jnp.where(in_band[None], bias, NEG)


GDN_TAIL = SUBLANES


def _gdn_body(qkv_ref, zs_ref, gate_ref, convw_ref, hp_ref, nw_ref, o_ref, xbuf, s_scr):
    c = CHUNK
    hd = B_HEAD_DIM

    @pl.when(pl.program_id(1) == 0)
    def _():
        xbuf[0:GDN_TAIL, :] = jnp.zeros((GDN_TAIL, 3 * B_WIDTH), F32)
        s_scr[...] = jnp.zeros_like(s_scr)

    xbuf[GDN_TAIL:GDN_TAIL + c, :] = qkv_ref[0].astype(F32)
    y = None
    for j in range(CONV_WIDTH):
        tap = convw_ref[j:j + 1, :] * xbuf[pl.ds(GDN_TAIL - (CONV_WIDTH - 1) + j, c), :]
        y = tap if y is None else y + tap
    xbuf[0:GDN_TAIL, :] = xbuf[c:c + GDN_TAIL, :]
    y = y * _sigmoid(y)

    zs = zs_ref[0]
    a_neg = -(jnp.exp(hp_ref[0:1, :]) * hp_ref[2:3, :])
    xs = zs + hp_ref[1:2, :]
    g_all = a_neg * (jnp.maximum(xs, 0.0) + jnp.log(1.0 + jnp.exp(-jnp.abs(xs))))
    beta_all = _sigmoid(zs)

    row = lax.broadcasted_iota(jnp.int32, (c, c), 0)
    col = lax.broadcasted_iota(jnp.int32, (c, c), 1)
    tril = row >= col
    tril_strict = row > col
    eye = (row == col).astype(F32)
    lower = tril.astype(BF16)
    upper = (row <= col).astype(BF16)

    gh, gl = _split_bf16(g_all)
    gc_all = _dot(lower, gh) + _dot(lower, gl)
    gth, gtl = _split_bf16(g_all.T)
    gc_t = _dot(gth, upper) + _dot(gtl, upper)

    heads = range(B_HEADS)
    q, k, v, beta, gc, gc_row, g_last = [], [], [], [], [], [], []
    for h in heads:
        qh = y[:, h * hd:(h + 1) * hd]
        kh = y[:, B_WIDTH + h * hd:B_WIDTH + (h + 1) * hd]
        q.append(qh * lax.rsqrt(jnp.sum(qh * qh, axis=-1, keepdims=True) + EPS) * (hd ** -0.5))
        k.append(kh * lax.rsqrt(jnp.sum(kh * kh, axis=-1, keepdims=True) + EPS))
        v.append(y[:, 2 * B_WIDTH + h * hd:2 * B_WIDTH + (h + 1) * hd])
        beta.append(beta_all[:, Z_SMALL_BETA + h:Z_SMALL_BETA + h + 1])
        gc.append(gc_all[:, Z_SMALL_A + h:Z_SMALL_A + h + 1])
        gc_row.append(gc_t[Z_SMALL_A + h:Z_SMALL_A + h + 1, :])
        g_last.append(gc_row[h][:, c - 1:c])
    decay = [jnp.exp(jnp.where(tril, gc[h] - gc_row[h], NEG)) for h in heads]
    e_gc = [jnp.exp(gc[h]) for h in heads]
    kb = [k[h] * beta[h] for h in heads]
    kf = [k[h].astype(BF16) for h in heads]
    a_pow = [jnp.where(tril_strict, _dot_nt(kb[h].astype(BF16), kf[h]) * decay[h], 0.0) for h in heads]
    attn = [(_dot_nt(q[h].astype(BF16), kf[h]) * decay[h]).astype(BF16) for h in heads]

    t_mat = [eye - a_pow[h] for h in heads]
    for _ in range(5):
        ab = [a_pow[h].astype(BF16) for h in heads]
        a_pow = [_dot(ab[h], ab[h]) for h in heads]
        t_mat = [t_mat[h] + _dot(t_mat[h].astype(BF16), a_pow[h].astype(BF16)) for h in heads]
    tb = [t_mat[h].astype(BF16) for h in heads]
    u = [_dot(tb[h], (v[h] * beta[h]).astype(BF16)) for h in heads]
    w = [_dot(tb[h], (kb[h] * e_gc[h]).astype(BF16)).astype(BF16) for h in heads]

    state = [s_scr[h] for h in heads]
    sb = [state[h].astype(BF16) for h in heads]
    vnb = [(u[h] - _dot(w[h], sb[h])).astype(BF16) for h in heads]
    o = [_dot((q[h] * e_gc[h]).astype(BF16), sb[h]) + _dot(attn[h], vnb[h]) for h in heads]
    k_dec = [(k[h] * jnp.exp(g_last[h] - gc[h])).astype(BF16) for h in heads]
    for h in heads:
        s_scr[h] = state[h] * jnp.exp(g_last[h]) + _dot_tn(k_dec[h], vnb[h])
    for h in heads:
        cols = slice(h * hd, (h + 1) * hd)
        gate = gate_ref[0, :, cols].astype(F32)
        on = o[h] * lax.rsqrt(jnp.mean(o[h] * o[h], axis=-1, keepdims=True) + EPS) * nw_ref[...]
        o_ref[0, :, cols] = (on * (gate * _sigmoid(gate))).astype(o_ref.dtype)


def _gated_deltanet(z3, zs3, conv_w, head_params, norm_w):
    b, s, _ = z3.shape
    n = s // CHUNK
    return pl.pallas_call(
        _gdn_body,
        grid=(b, n),
        in_specs=[
            pl.BlockSpec((1, CHUNK, 3 * B_WIDTH), lambda bi, ni: (bi, ni, Z_QKV_B // (3 * B_WIDTH))),
            pl.BlockSpec((1, CHUNK, LANES), lambda bi, ni: (bi, ni, 0)),
            pl.BlockSpec((1, CHUNK, B_WIDTH), lambda bi, ni: (bi, ni, Z_GATE_B // B_WIDTH)),
            pl.BlockSpec((CONV_WIDTH, 3 * B_WIDTH), lambda bi, ni: (0, 0)),
            pl.BlockSpec((SUBLANES, LANES), lambda bi, ni: (0, 0)),
            pl.BlockSpec((1, B_HEAD_DIM), lambda bi, ni: (0, 0)),
        ],
        out_specs=pl.BlockSpec((1, CHUNK, B_WIDTH), lambda bi, ni: (bi, ni, 0)),
        out_shape=jax.ShapeDtypeStruct((b, s, B_WIDTH), BF16),
        scratch_shapes=[
            pltpu.VMEM((GDN_TAIL + CHUNK, 3 * B_WIDTH), F32),
            pltpu.VMEM((B_HEADS, B_HEAD_DIM, B_HEAD_DIM), F32),
        ],
        compiler_params=pltpu.CompilerParams(
            dimension_semantics=("parallel", "arbitrary"), vmem_limit_bytes=VMEM_LIMIT),
        name="gdn",
    )(z3, zs3, z3, conv_w, head_params, norm_w)


MERGE_TM = 256


def _top_k_rows(s, k, payload=None):
    n = s.shape[0]
    rows = lax.broadcasted_iota(jnp.int32, s.shape, 0)
    vals, picks = [], []
    for _ in range(k):
        m = jnp.max(s, axis=0, keepdims=True)
        i = jnp.min(jnp.where(s == m, rows, n), axis=0, keepdims=True)
        hit = rows == i
        vals.append(m)
        picks.append(i if payload is None else jnp.max(jnp.where(hit, payload, -1), axis=0, keepdims=True))
        s = jnp.where(hit, -jnp.inf, s)
    return jnp.concatenate(vals, axis=0), jnp.concatenate(picks, axis=0)


def _merge_body(oa_ref, ob_ref, mr_ref, x_ref, bm_ref, pa_ref, pb_ref, wo_ref, nw_ref, wq_ref, k1_ref, k2_ref,
                x1_ref, x1p_ref, xnp_ref, idx_ref, gate_ref, gate_tok_ref):
    tm, d = x_ref.shape
    planes = d // LANES

    def store_planes(ref, a):
        for p in range(planes):
            ref[pl.ds(p, tm, stride=planes), :] = a[:, p * LANES:(p + 1) * LANES]

    mr = mr_ref[...].astype(F32) + bm_ref[...]
    mixed = (_sigmoid(mr[:, :d]) * _dot(oa_ref[...], pa_ref[...])
             + _sigmoid(mr[:, d:]) * _dot(ob_ref[...], pb_ref[...]))
    x1 = x_ref[...] + _dot(mixed.astype(BF16), wo_ref[...])
    x1_ref[...] = x1
    store_planes(x1p_ref, x1)
    xn = x1 * lax.rsqrt(jnp.mean(x1 * x1, axis=-1, keepdims=True) + EPS) * nw_ref[...]
    store_planes(xnp_ref, xn)
    q = _dot(xn.astype(BF16), wq_ref[...]).astype(BF16)

    idx_rows, gate_rows = [], []
    for h in range(PEER_HEADS):
        q1 = q[:, (2 * h) * PEER_HALF:(2 * h + 1) * PEER_HALF]
        q2 = q[:, (2 * h + 1) * PEER_HALF:(2 * h + 2) * PEER_HALF]
        s1 = _dot_nt(k1_ref[h], q1)
        s2 = _dot_nt(k2_ref[h], q2)
        v1, i1 = _top_k_rows(s1, PEER_TOPK)
        v2, i2 = _top_k_rows(s2, PEER_TOPK)
        nb = [PEER_TOPK // (a + 1) for a in range(PEER_TOPK)]
        pad = -sum(nb) % SUBLANES
        tm = v1.shape[1]
        cand = jnp.concatenate([v1[a:a + 1] + v2[:nb[a]] for a in range(PEER_TOPK)]
                               + [jnp.full((pad, tm), -jnp.inf, F32)], axis=0)
        cand_idx = jnp.concatenate([i1[a:a + 1] * PEER_NKEYS + i2[:nb[a]] for a in range(PEER_TOPK)]
                                   + [jnp.zeros((pad, tm), jnp.int32)], axis=0)
        top_s, expert = _top_k_rows(cand, PEER_TOPK, payload=cand_idx)
        e = jnp.exp(top_s - top_s[0:1])
        gate_rows.append(e / jnp.sum(e, axis=0, keepdims=True))
        idx_rows.append(expert)
    idx_ref[...] = jnp.concatenate(idx_rows, axis=0).T
    gates = jnp.concatenate(gate_rows, axis=0)
    gate_ref[...] = gates
    gate_tok_ref[...] = gates.T


def _merge_route(o_a, o_b, z2, x2, row0, b_merge, pa, pb, wo, nw, wq, k1, k2):
    t, d = o_a.shape[0], x2.shape[1]
    tm = MERGE_TM
    xblk0 = row0 // tm
    full = lambda a: pl.BlockSpec(a.shape, lambda i: (0,) * a.ndim)
    return pl.pallas_call(
        _merge_body,
        grid=(t // tm,),
        in_specs=[
            pl.BlockSpec((tm, A_WIDTH), lambda i: (i, 0)),
            pl.BlockSpec((tm, B_WIDTH), lambda i: (i, 0)),
            pl.BlockSpec((tm, 2 * d), lambda i: (i, Z_MERGE // (2 * d))),
            pl.BlockSpec((tm, d), lambda i: (i + xblk0, 0)),
            full(b_merge), full(pa), full(pb), full(wo), full(nw), full(wq), full(k1), full(k2),
        ],
        out_specs=[
            pl.BlockSpec((tm, d), lambda i: (i, 0)),
            pl.BlockSpec((tm * d // LANES, LANES), lambda i: (i, 0)),
            pl.BlockSpec((tm * d // LANES, LANES), lambda i: (i, 0)),
            pl.BlockSpec((tm, PEER_SLOTS), lambda i: (i, 0)),
            pl.BlockSpec((PEER_SLOTS, tm), lambda i: (0, i)),
            pl.BlockSpec((tm, PEER_SLOTS), lambda i: (i, 0)),
        ],
        out_shape=[
            jax.ShapeDtypeStruct((t, d), F32),
            jax.ShapeDtypeStruct((t * d // LANES, LANES), F32),
            jax.ShapeDtypeStruct((t * d // LANES, LANES), F32),
            jax.ShapeDtypeStruct((t, PEER_SLOTS), jnp.int32),
            jax.ShapeDtypeStruct((PEER_SLOTS, t), F32),
            jax.ShapeDtypeStruct((t, PEER_SLOTS), F32),
        ],
        compiler_params=pltpu.CompilerParams(
            dimension_semantics=("parallel",), vmem_limit_bytes=VMEM_LIMIT),
        name="merge_route",
    )(o_a, o_b, z2, x2, b_merge, pa, pb, wo, nw, wq, k1, k2)


PEER_TP = 128
PEER_NBUF = 8
PEER_LOOK = PEER_NBUF - 1
PEER_SC_SHARE_NUM, PEER_SC_SHARE_DEN = 16, 32
PEER_TC_FIRST_NUM, PEER_TC_FIRST_DEN = 9, 32


def _erf(x):
    return lax.erf(x)


PACK_HI = -65536


def _packed_down(bitcast, word):
    return bitcast(word & PACK_HI, F32)


def _packed_up(bitcast, word):
    return bitcast(word << 16, F32)


def _peer_body(idx_ref, idx_next_ref, xn_ref, gate_ref, x1_ref, tab_ref, o_ref, *scratch):
    bufs, sem = scratch[:PEER_NBUF], scratch[PEER_NBUF]
    planes = x1_ref.shape[1] // LANES
    step = pl.program_id(0)

    def issue(ref, tok, slot):
        for j in range(PEER_SLOTS):
            pltpu.make_async_copy(
                tab_ref.at[ref[0, tok, j]], bufs[slot].at[:, j, :], sem.at[slot]).start(priority=j % 2)

    def wait(slot):
        pltpu.make_async_copy(bufs[slot], bufs[slot], sem.at[slot]).wait()

    @pl.when(step == 0)
    def _():
        for s in range(PEER_LOOK):
            issue(idx_ref, s, s)

    lane = lax.broadcasted_iota(jnp.int32, (PEER_SLOTS, PEER_TP), 1)

    def compute(tok, slot):
        buf = bufs[slot]
        acc = None
        for s in range(planes):
            term = _packed_down(lax.bitcast_convert_type, buf[s]) * xn_ref[pl.ds(tok * planes + s, 1), :]
            acc = term if acc is None else acc + term
        act = jnp.sum(acc, axis=-1, keepdims=True)
        gate = jnp.sum(jnp.where(lane == tok, gate_ref[...], 0.0), axis=-1, keepdims=True)
        w = gate * (0.5 * act * (1.0 + _erf(act * (2.0 ** -0.5))))
        out = jnp.concatenate(
            [jnp.sum(_packed_up(lax.bitcast_convert_type, buf[s]) * w, axis=0, keepdims=True)
             for s in range(planes)], axis=1)
        o_ref[pl.ds(tok, 1), :] = x1_ref[pl.ds(tok, 1), :] + out

    def group(g, carry):
        for u in range(PEER_NBUF):
            tok = g * PEER_NBUF + u
            issue(idx_ref, tok + PEER_LOOK, (u + PEER_LOOK) % PEER_NBUF)
            wait(u)
            compute(tok, u)
        return carry

    n_groups = PEER_TP // PEER_NBUF
    lax.fori_loop(0, n_groups - 1, group, 0)

    for u in range(PEER_NBUF):
        tok = (n_groups - 1) * PEER_NBUF + u
        nxt = tok + PEER_LOOK
        if nxt < PEER_TP:
            issue(idx_ref, nxt, (u + PEER_LOOK) % PEER_NBUF)
        else:
            issue(idx_next_ref, nxt - PEER_TP, (u + PEER_LOOK) % PEER_NBUF)
        wait(u)
        compute(tok, u)

    @pl.when(step == pl.num_programs(0) - 1)
    def _():
        for s in range(PEER_LOOK):
            wait(s)


def _peer_alias_body(prev_ref, after_ref, *rest):
    del prev_ref, after_ref
    _peer_body(*rest)


def _peer(idx3, xnp, gate_t, x1, table, blk0, nb, out_blk0, t_out, prev=None, after=None):
    d = x1.shape[1]
    tp = PEER_TP
    planes = d // LANES
    in_specs = [
        pl.BlockSpec((1, tp, PEER_SLOTS), lambda i: (i + blk0, 0, 0), memory_space=pltpu.SMEM),
        pl.BlockSpec((1, tp, PEER_SLOTS), lambda i: (jnp.minimum(i + 1, nb - 1) + blk0, 0, 0),
                     memory_space=pltpu.SMEM),
        pl.BlockSpec((tp * planes, LANES), lambda i: (i + blk0, 0)),
        pl.BlockSpec((PEER_SLOTS, tp), lambda i: (0, i + blk0)),
        pl.BlockSpec((tp, d), lambda i: (i + blk0, 0)),
        pl.BlockSpec(memory_space=pl.ANY),
    ]
    args = (idx3, idx3, xnp, gate_t, x1, table)
    body, aliases = _peer_body, {}
    if prev is not None:
        body, aliases = _peer_alias_body, {0: 0}
        in_specs = [pl.BlockSpec(memory_space=pl.ANY), pl.BlockSpec(memory_space=pl.ANY)] + in_specs
        args = (prev, after) + args
    return pl.pallas_call(
        body,
        grid=(nb,),
        in_specs=in_specs,
        out_specs=pl.BlockSpec((tp, d), lambda i: (i + blk0 + out_blk0, 0)),
        out_shape=jax.ShapeDtypeStruct((t_out, d), F32),
        scratch_shapes=[pltpu.VMEM((planes, PEER_SLOTS, LANES), jnp.int32) for _ in range(PEER_NBUF)]
        + [pltpu.SemaphoreType.DMA((PEER_NBUF,))],
        input_output_aliases=aliases,
        compiler_params=pltpu.CompilerParams(
            dimension_semantics=("arbitrary",), vmem_limit_bytes=VMEM_LIMIT),
        name="peer",
    )(*args)


SC_CORES = 2
SC_SUBCORES = 16
SC_LANES = 16
SC_WORKERS = SC_CORES * SC_SUBCORES
SC_ROWS = 64


SC_VPR = LANES // SC_LANES
SC_VROWS = PEER_SLOTS // SC_VPR


def _sc_vec(j):
    return (j // SC_VPR, pl.ds((j % SC_VPR) * SC_LANES, SC_LANES))


def _sc_mesh():
    return plsc.VectorSubcoreMesh(core_axis_name="c", subcore_axis_name="s")


def _sc_worker():
    return lax.axis_index("s") * SC_CORES + lax.axis_index("c")


def _sc_pipeline(tw, base, nchunk, loads, gather, compute, finish):
    assert tw % 2 == 0 and nchunk % 2 == 0
    last = base + tw - 1

    def token(tok, s):
        nxt = jnp.minimum(tok + 1, last)
        for c in loads(nxt, 1 - s):
            c.start()
        for kc in range(nchunk):
            if kc + 1 < nchunk:
                gather(s, kc + 1).start()
            else:
                for c in loads(nxt, 1 - s):
                    c.wait()
                gather(1 - s, 0).start()
            gather(s, kc).wait()
            compute(s, kc)
        finish(tok, s)

    for c in loads(base, 0):
        c.start()
    for c in loads(base, 0):
        c.wait()
    gather(0, 0).start()

    def pair(i, carry):
        token(base + 2 * i, 0)
        token(base + 2 * i + 1, 1)
        return carry

    lax.fori_loop(0, tw // 2, pair, 0)
    gather(0, 0).wait()


def _sc_down(idx_flat, xn3, table, tok0, n_tok):
    planes = xn3.shape[1]
    half = planes // 2
    tw = n_tok // SC_WORKERS
    nchunk = PEER_SLOTS // SC_ROWS
    per_plane = LANES // SC_LANES

    @functools.partial(
        pl.kernel, mesh=_sc_mesh(),
        out_type=jax.ShapeDtypeStruct((n_tok, SC_VROWS, LANES), F32),
        scratch_types=[
            pltpu.VMEM((2, PEER_SLOTS), jnp.int32),
            pltpu.VMEM((2, planes, LANES), F32),
            pltpu.VMEM((2, SC_ROWS, half, LANES), jnp.int32),
            pltpu.VMEM((SC_VROWS, LANES), F32),
            pltpu.VMEM((half, SC_VROWS, LANES), F32),
            pltpu.SemaphoreType.DMA((2,)),
            pltpu.SemaphoreType.DMA((2,)),
        ],
        compiler_params=pltpu.CompilerParams(needs_layout_passes=False),
        name="sc_down",
    )
    def run(idx_hbm, xn_hbm, tab_hbm, out_hbm, idx_v, x_v, buf, act_v, part_v, sem_g, sem_l):
        def loads(tok, s):
            return [pltpu.make_async_copy(idx_hbm.at[pl.ds(tok * PEER_SLOTS, PEER_SLOTS)], idx_v.at[s], sem_l.at[s]),
                    pltpu.make_async_copy(xn_hbm.at[tok], x_v.at[s], sem_l.at[s])]

        def gather(s, kc):
            return pltpu.make_async_copy(
                tab_hbm.at[idx_v.at[s, pl.ds(kc * SC_ROWS, SC_ROWS)]], buf.at[kc % 2], sem_g.at[kc % 2])

        def compute(s, kc):
            @plsc.parallel_loop(0, half)
            def _(p):
                sls = [pl.ds(q * SC_LANES, SC_LANES) for q in range(per_plane)]
                x_hi = [x_v[s, p, sl] for sl in sls]
                x_lo = [x_v[s, half + p, sl] for sl in sls]

                def load_row(j):
                    return [buf[kc % 2, j, p, sl] for sl in sls]

                words = load_row(0)
                for j in range(SC_ROWS):
                    ahead = load_row(j + 1) if j + 1 < SC_ROWS else None
                    terms = ([_packed_down(plsc.bitcast, w) * x for w, x in zip(words, x_hi)]
                             + [_packed_up(plsc.bitcast, w) * x for w, x in zip(words, x_lo)])
                    while len(terms) > 1:
                        terms = [terms[i] + terms[i + 1] for i in range(0, len(terms), 2)]
                    part_v[(p,) + _sc_vec(kc * SC_ROWS + j)] = terms[0]
                    words = ahead

        def finish(tok, s):
            @plsc.parallel_loop(0, SC_VROWS)
            def _(r):
                for u in range(SC_VPR):
                    sl = pl.ds(u * SC_LANES, SC_LANES)
                    acc = part_v[0, r, sl]
                    for p in range(1, half):
                        acc = acc + part_v[p, r, sl]
                    act_v[r, sl] = acc
            pltpu.sync_copy(act_v, out_hbm.at[tok - tok0])

        _sc_pipeline(tw, tok0 + _sc_worker() * tw, nchunk, loads, gather, compute, finish)

    return run(idx_flat, xn3, table)


def _sc_up(idx_flat, w16, x13, table, tok0, n_tok):
    planes = x13.shape[1]
    half = planes // 2
    tw = n_tok // SC_WORKERS
    nchunk = PEER_SLOTS // SC_ROWS
    per_plane = LANES // SC_LANES

    @functools.partial(
        pl.kernel, mesh=_sc_mesh(),
        out_type=jax.ShapeDtypeStruct((n_tok, planes, LANES), F32),
        scratch_types=[
            pltpu.VMEM((2, PEER_SLOTS), jnp.int32),
            pltpu.VMEM((2, SC_VROWS, LANES), F32),
            pltpu.VMEM((2, planes, LANES), F32),
            pltpu.VMEM((2, SC_ROWS, half, LANES), jnp.int32),
            pltpu.SemaphoreType.DMA((2,)),
            pltpu.SemaphoreType.DMA((2,)),
        ],
        compiler_params=pltpu.CompilerParams(needs_layout_passes=False),
        name="sc_up",
    )
    def run(idx_hbm, w_hbm, x1_hbm, tab_hbm, out_hbm, idx_v, w_v, out_v, buf, sem_g, sem_l):
        def loads(tok, s):
            return [pltpu.make_async_copy(idx_hbm.at[pl.ds(tok * PEER_SLOTS, PEER_SLOTS)], idx_v.at[s], sem_l.at[s]),
                    pltpu.make_async_copy(w_hbm.at[tok - tok0], w_v.at[s], sem_l.at[s]),
                    pltpu.make_async_copy(x1_hbm.at[tok], out_v.at[s], sem_l.at[s])]

        def gather(s, kc):
            return pltpu.make_async_copy(
                tab_hbm.at[idx_v.at[s, pl.ds(kc * SC_ROWS, SC_ROWS)]], buf.at[kc % 2], sem_g.at[kc % 2])

        def compute(s, kc):
            @plsc.parallel_loop(0, half)
            def _(p):
                sls = [pl.ds(q * SC_LANES, SC_LANES) for q in range(per_plane)]
                acc_hi = [out_v[s, p, sl] for sl in sls]
                acc_lo = [out_v[s, half + p, sl] for sl in sls]
                for j in range(SC_ROWS):
                    wj = w_v[(s,) + _sc_vec(kc * SC_ROWS + j)]
                    words = [buf[kc % 2, j, p, sl] for sl in sls]
                    acc_hi = [a + wj * _packed_down(plsc.bitcast, w) for a, w in zip(acc_hi, words)]
                    acc_lo = [a + wj * _packed_up(plsc.bitcast, w) for a, w in zip(acc_lo, words)]
                for a, sl in zip(acc_hi, sls):
                    out_v[s, p, sl] = a
                for a, sl in zip(acc_lo, sls):
                    out_v[s, half + p, sl] = a

        def finish(tok, s):
            pltpu.sync_copy(out_v.at[s], out_hbm.at[tok - tok0])

        _sc_pipeline(tw, tok0 + _sc_worker() * tw, nchunk, loads, gather, compute, finish)

    return run(idx_flat, w16, x13, table)


ACT_TM = 256


def _peer_act_body(after_ref, part_ref, gate_ref, fold_ref, spread_ref, own_ref, w_ref):
    del after_ref
    tm = part_ref.shape[0]
    own = own_ref[...]
    hi, lo = _split_bf16(part_ref[...].reshape(tm * SC_VROWS, LANES))
    sums = (_dot(hi, fold_ref[...]) + _dot(lo, fold_ref[...])).reshape(tm, SC_VROWS, LANES)
    act = jnp.sum(sums * own, axis=1)
    w = gate_ref[...] * (0.5 * act * (1.0 + _erf(act * (2.0 ** -0.5))))
    hi, lo = _split_bf16((w[:, None, :] * own).reshape(tm * SC_VROWS, LANES))
    w_ref[...] = (_dot(hi, spread_ref[...]) + _dot(lo, spread_ref[...])).reshape(tm, SC_VROWS, LANES)


def _peer_act(after, part3, gate_tok, blk0):
    n_tok = part3.shape[0]
    tm = ACT_TM
    lane = jnp.arange(LANES)
    fold = (lane[:, None] // SC_LANES == lane[None, :] % SC_VPR).astype(BF16)
    spread = fold.T
    own = (lane[None, :] // SC_VPR == jnp.arange(SC_VROWS)[:, None]).astype(F32)
    const = lambda a: pl.BlockSpec(a.shape, lambda i: (0,) * a.ndim)
    return pl.pallas_call(
        _peer_act_body,
        grid=(n_tok // tm,),
        in_specs=[
            pl.BlockSpec(memory_space=pl.ANY),
            pl.BlockSpec((tm, SC_VROWS, LANES), lambda i: (i, 0, 0)),
            pl.BlockSpec((tm, PEER_SLOTS), lambda i: (i + blk0, 0)),
            const(fold), const(spread), const(own),
        ],
        out_specs=pl.BlockSpec((tm, SC_VROWS, LANES), lambda i: (i, 0, 0)),
        out_shape=jax.ShapeDtypeStruct(part3.shape, F32),
        compiler_params=pltpu.CompilerParams(
            dimension_semantics=("parallel",), vmem_limit_bytes=VMEM_LIMIT),
        name="peer_act",
    )(after, part3, gate_tok, fold, spread, own)


def _table_pack_body(down_ref, up_ref, tab_ref, dn2_ref, up2_ref):
    te, d = down_ref.shape
    planes = d // LANES
    half = planes // 2

    def bits(ref, p):
        return lax.bitcast_convert_type(ref[:, p * LANES:(p + 1) * LANES].astype(BF16).astype(F32), jnp.int32)

    def pack(hi, lo):
        return hi | lax.shift_right_logical(lo, 16)

    for p in range(planes):
        tab_ref[pl.ds(p, te, stride=planes), :] = pack(bits(down_ref, p), bits(up_ref, p))
    for p in range(half):
        dn2_ref[pl.ds(p, te, stride=half), :] = pack(bits(down_ref, p), bits(down_ref, half + p))
        up2_ref[pl.ds(p, te, stride=half), :] = pack(bits(up_ref, p), bits(up_ref, half + p))


def _table_pack(down, up, te=512):
    n_exp, d = down.shape
    rows = d // LANES
    return pl.pallas_call(
        _table_pack_body,
        grid=(n_exp // te,),
        in_specs=[pl.BlockSpec((te, d), lambda i: (i, 0)), pl.BlockSpec((te, d), lambda i: (i, 0))],
        out_specs=[pl.BlockSpec((te * rows, LANES), lambda i: (i, 0)),
                   pl.BlockSpec((te * rows // 2, LANES), lambda i: (i, 0)),
                   pl.BlockSpec((te * rows // 2, LANES), lambda i: (i, 0))],
        out_shape=[jax.ShapeDtypeStruct((n_exp * rows, LANES), jnp.int32),
                   jax.ShapeDtypeStruct((n_exp * rows // 2, LANES), jnp.int32),
                   jax.ShapeDtypeStruct((n_exp * rows // 2, LANES), jnp.int32)],
        compiler_params=pltpu.CompilerParams(
            dimension_semantics=("parallel",), vmem_limit_bytes=VMEM_LIMIT),
        name="table_pack",
    )(down, up)


def _pad_lanes(a):
    return jnp.pad(a, ((0, 0), (0, LANES - a.shape[1])))


def kernel(x, norm_mix_w, w_in, a_q_norm_w, a_k_norm_w, a_rel_bias, b_conv_w, b_a_log, b_dt_bias, b_norm_w, b_merge, w_proj_a, w_proj_b, w_out, norm_ffn_w, peer_w_query, peer_keys_1, peer_keys_2, peer_down, peer_up):
    b, s, d = x.shape
    t = b * s
    depth = w_in.shape[0]
    for l in range(depth):
        x2 = x.reshape(t, d)

        wi = w_in[l]
        o_qkvb = 3 * A_WIDTH
        o_a = o_qkvb + 3 * B_WIDTH
        o_beta = o_a + B_HEADS
        o_gate = o_beta + B_HEADS
        o_merge = o_gate + B_WIDTH
        w_main = jnp.concatenate(
            [wi[:, o_qkvb:o_a], wi[:, o_gate:o_merge], wi[:, o_merge:], wi[:, :o_qkvb]], axis=1).astype(BF16)
        w_small = _pad_lanes(wi[:, o_a:o_gate]).astype(BF16)
        z_attn = Z_MERGE + 2 * d

        bias_tab = _attn_bias_table(a_rel_bias[l])
        seg = jnp.kron(jnp.eye(A_HEADS, dtype=F32), jnp.full((A_HEAD_DIM, A_HEAD_DIM), 1.0 / A_HEAD_DIM, F32))
        head_params = jnp.zeros((SUBLANES, LANES), F32)
        head_params = head_params.at[0, :B_HEADS].set(b_a_log[l]).at[1, :B_HEADS].set(b_dt_bias[l])
        head_params = head_params.at[2, :B_HEADS].set(1.0)

        def mixer_and_route(b0, nb):
            tg = nb * s
            z, zs = _in_proj(x2, norm_mix_w[l][None, :], w_main, w_small, min(IN_PROJ_TM, tg), IN_PROJ_TN,
                             b0 * s, tg)
            z3 = z.reshape(nb, s, z.shape[1])
            o_att = _band_attention(
                z3, bias_tab, jnp.tile(a_q_norm_w[l], A_HEADS)[None, :], jnp.tile(a_k_norm_w[l], A_HEADS)[None, :],
                seg.astype(BF16), z_attn // A_WIDTH)
            o_gdn = _gated_deltanet(z3, zs.reshape(nb, s, LANES), b_conv_w[l], head_params, b_norm_w[l][None, :])
            return _merge_route(
                o_att.reshape(tg, A_WIDTH), o_gdn.reshape(tg, B_WIDTH), z, x2, b0 * s, b_merge[l][None, :],
                w_proj_a[l].astype(BF16), w_proj_b[l].astype(BF16), w_out[l].astype(BF16),
                norm_ffn_w[l][None, :], peer_w_query[l].astype(BF16), peer_keys_1[l].astype(BF16),
                peer_keys_2[l].astype(BF16))

        planes = d // LANES
        n_exp = peer_down.shape[1]
        table, down2, up2 = _table_pack(peer_down[l], peer_up[l])
        table = table.reshape(n_exp, planes, LANES)
        down2 = down2.reshape(n_exp, planes // 2, LANES)
        up2 = up2.reshape(n_exp, planes // 2, LANES)

        b_sc = b * PEER_SC_SHARE_NUM // PEER_SC_SHARE_DEN
        n_sc = b_sc * s
        _, x1p_s, xnp_s, idx_s, _, gate_tok_s = mixer_and_route(0, b_sc)
        idx_s = idx_s.reshape(-1)
        part = _sc_down(idx_s, xnp_s.reshape(n_sc, planes, LANES), down2, 0, n_sc)

        x1, _, xnp, idx, gate_t, _ = mixer_and_route(b_sc, b - b_sc)
        nblk = (t - n_sc) // PEER_TP
        nblk_a = nblk * PEER_TC_FIRST_NUM // PEER_TC_FIRST_DEN
        idx3 = idx.reshape(nblk, PEER_TP, PEER_SLOTS)
        out = _peer(idx3, xnp, gate_t, x1, table, 0, nblk_a, n_sc // PEER_TP, t)
        w16 = _peer_act(out, part, gate_tok_s, 0)
        out = _peer(idx3, xnp, gate_t, x1, table, nblk_a, nblk - nblk_a, n_sc // PEER_TP, t, prev=out, after=w16)
        sc_out = _sc_up(idx_s, w16, x1p_s.reshape(n_sc, planes, LANES), up2, 0, n_sc)
        out = lax.dynamic_update_slice(out, sc_out.reshape(n_sc, d), (0, 0))
        x = out.reshape(b, s, d)
    return x
```

```python
import functools

import jax
import jax.numpy as jnp
from jax import lax
from jax.experimental import pallas as pl
from jax.experimental.pallas import tpu as pltpu
from jax.experimental.pallas import tpu_sc as plsc

F32 = jnp.float32
BF16 = jnp.bfloat16

EPS = 1e-6
NEG = -1e30

CHUNK = 64
A_HEADS = 8
A_HEAD_DIM = 64
A_WIDTH = A_HEADS * A_HEAD_DIM
A_LEFT_CHUNKS = 8
REL_CLIP = 128
B_HEADS = 8
B_HEAD_DIM = 128
B_WIDTH = B_HEADS * B_HEAD_DIM
CONV_WIDTH = 4
PEER_HEADS = 8
PEER_HALF = 128
PEER_NKEYS = 128
PEER_TOPK = 16
PEER_SLOTS = PEER_HEADS * PEER_TOPK

LANES = 128
SUBLANES = 8
VMEM_LIMIT = 56 * 1024 * 1024

Z_QKV_B = 0
Z_GATE_B = 3 * B_WIDTH
Z_MERGE = Z_GATE_B + B_WIDTH
Z_SMALL_A = 0
Z_SMALL_BETA = B_HEADS


def _dot(a, b):
    return jnp.dot(a, b, preferred_element_type=F32)


def _dot_nt(a, b):
    return lax.dot_general(a, b, (((1,), (1,)), ((), ())), preferred_element_type=F32)


def _dot_tn(a, b):
    return lax.dot_general(a, b, (((0,), (0,)), ((), ())), preferred_element_type=F32)


def _split_bf16(a):
    hi = a.astype(BF16)
    lo = (a - hi.astype(F32)).astype(BF16)
    return hi, lo


def _sigmoid(x):
    return 1.0 / (1.0 + jnp.exp(-x))


IN_PROJ_TM = 1024
IN_PROJ_TN = 1920


def _inproj_body(x_ref, nw_ref, w_ref, ws_ref, z_ref, zs_ref, h_scr):
    @pl.when(pl.program_id(1) == 0)
    def _():
        x = x_ref[...]
        ms = jnp.mean(x * x, axis=-1, keepdims=True)
        hb = (x * lax.rsqrt(ms + EPS) * nw_ref[...]).astype(BF16)
        h_scr[...] = hb
        zs_ref[...] = _dot(hb, ws_ref[...])

    z_ref[...] = _dot(h_scr[...], w_ref[...]).astype(z_ref.dtype)


def _in_proj(x2, norm_w, w_main, w_small, tm, tn, row0, t):
    d = x2.shape[1]
    n = w_main.shape[1]
    blk0 = row0 // tm
    return pl.pallas_call(
        _inproj_body,
        grid=(t // tm, n // tn),
        in_specs=[
            pl.BlockSpec((tm, d), lambda i, j: (i + blk0, 0)),
            pl.BlockSpec((1, d), lambda i, j: (0, 0)),
            pl.BlockSpec((d, tn), lambda i, j: (0, j)),
            pl.BlockSpec((d, LANES), lambda i, j: (0, 0)),
        ],
        out_specs=[
            pl.BlockSpec((tm, tn), lambda i, j: (i, j)),
            pl.BlockSpec((tm, LANES), lambda i, j: (i, 0)),
        ],
        out_shape=[
            jax.ShapeDtypeStruct((t, n), BF16),
            jax.ShapeDtypeStruct((t, LANES), F32),
        ],
        scratch_shapes=[pltpu.VMEM((tm, d), BF16)],
        compiler_params=pltpu.CompilerParams(
            dimension_semantics=("parallel", "arbitrary"), vmem_limit_bytes=VMEM_LIMIT),
        name="in_proj",
    )(x2, norm_w, w_main, w_small)


ATT_TQ = 256
ATT_WIN = ATT_TQ + A_LEFT_CHUNKS * CHUNK


def _attn_body(q_ref, k0_ref, k1_ref, k2_ref, v0_ref, v1_ref, v2_ref, bias_ref, qw_ref, kw_ref, seg_ref,
               o_ref):
    t = pl.program_id(1)
    seg = seg_ref[...]

    def head_rms(a, w):
        hi, lo = _split_bf16(a * a)
        ms = _dot(hi, seg) + _dot(lo, seg)
        return a * lax.rsqrt(ms + EPS) * w

    q = head_rms(q_ref[0].astype(F32), qw_ref[...]) * (A_HEAD_DIM ** -0.5)
    k = jnp.concatenate([k0_ref[0], k1_ref[0], k2_ref[0]], axis=0).astype(F32)
    k = head_rms(k, kw_ref[...]).astype(BF16)
    v = jnp.concatenate([v0_ref[0], v1_ref[0], v2_ref[0]], axis=0)

    kpos = t * ATT_TQ - A_LEFT_CHUNKS * CHUNK + lax.broadcasted_iota(jnp.int32, (1, ATT_WIN), 1)
    valid = kpos >= 0
    lane = lax.broadcasted_iota(jnp.int32, (1, LANES), 1)
    first = lane < A_HEAD_DIM

    for pair in range(A_HEADS // 2):
        cols = slice(pair * LANES, (pair + 1) * LANES)
        qp = q[:, cols]
        kp = k[:, cols]
        vp = v[:, cols]
        outs = []
        for half in range(2):
            h = 2 * pair + half
            qm = jnp.where(first if half == 0 else jnp.logical_not(first), qp, 0.0).astype(BF16)
            s = _dot_nt(qm, kp) + bias_ref[h]
            s = jnp.where(valid, s, NEG)
            mx = jnp.max(s, axis=-1, keepdims=True)
            p = jnp.exp(s - mx)
            den = jnp.sum(p, axis=-1, keepdims=True)
            outs.append(_dot(p.astype(BF16), vp) / den)
        o_ref[0, :, cols] = jnp.where(first, outs[0], outs[1]).astype(o_ref.dtype)


def _band_attention(z3, bias_tab, qw, kw, seg, col_q):
    b, s, _ = z3.shape
    nt = s // ATT_TQ

    def kv_spec(back, col):
        return pl.BlockSpec((1, ATT_TQ, A_WIDTH), lambda bi, ti: (bi, jnp.maximum(ti - back, 0), col))

    return pl.pallas_call(
        _attn_body,
        grid=(b, nt),
        in_specs=[
            pl.BlockSpec((1, ATT_TQ, A_WIDTH), lambda bi, ti: (bi, ti, col_q)),
            kv_spec(2, col_q + 1), kv_spec(1, col_q + 1), kv_spec(0, col_q + 1),
            kv_spec(2, col_q + 2), kv_spec(1, col_q + 2), kv_spec(0, col_q + 2),
            pl.BlockSpec((A_HEADS, ATT_TQ, ATT_WIN), lambda bi, ti: (0, 0, 0)),
            pl.BlockSpec((1, A_WIDTH), lambda bi, ti: (0, 0)),
            pl.BlockSpec((1, A_WIDTH), lambda bi, ti: (0, 0)),
            pl.BlockSpec((A_WIDTH, A_WIDTH), lambda bi, ti: (0, 0)),
        ],
        out_specs=pl.BlockSpec((1, ATT_TQ, A_WIDTH), lambda bi, ti: (bi, ti, 0)),
        out_shape=jax.ShapeDtypeStruct((b, s, A_WIDTH), BF16),
        compiler_params=pltpu.CompilerParams(
            dimension_semantics=("parallel", "parallel"), vmem_limit_bytes=VMEM_LIMIT),
        name="band_attn",
    )(z3, z3, z3, z3, z3, z3, z3, bias_tab, qw, kw, seg)


def _attn_bias_table(rel_bias):
    rb = rel_bias.astype(F32)
    heads = rb.shape[0]
    lo = A_LEFT_CHUNKS * CHUNK - (ATT_WIN - 1)
    hi = A_LEFT_CHUNKS * CHUNK + ATT_TQ - 1
    strip = jnp.concatenate([
        jnp.broadcast_to(rb[:, :1], (heads, -REL_CLIP - lo)), rb,
        jnp.broadcast_to(rb[:, -1:], (heads, hi - REL_CLIP))], axis=1)
    rev = strip[:, ::-1]
    n = rev.shape[1]
    flat = jnp.broadcast_to(rev[:, None, :], (heads, ATT_TQ, n)).reshape(heads, ATT_TQ * n)
    bias = flat[:, ATT_TQ - 1:ATT_TQ - 1 + ATT_TQ * (n - 1)].reshape(heads, ATT_TQ, n - 1)[:, :, :ATT_WIN]
    qc = jnp.arange(ATT_TQ)[:, None] // CHUNK
    kc = jnp.arange(ATT_WIN)[None, :] // CHUNK
    in_band = (kc >= qc) & (kc <= qc + A_LEFT_CHUNKS)
    return jnp.where(in_band[None], bias, NEG)


GDN_TAIL = SUBLANES


def _gdn_body(qkv_ref, zs_ref, gate_ref, convw_ref, hp_ref, nw_ref, o_ref, xbuf, s_scr):
    c = CHUNK
    hd = B_HEAD_DIM

    @pl.when(pl.program_id(1) == 0)
    def _():
        xbuf[0:GDN_TAIL, :] = jnp.zeros((GDN_TAIL, 3 * B_WIDTH), F32)
        s_scr[...] = jnp.zeros_like(s_scr)

    xbuf[GDN_TAIL:GDN_TAIL + c, :] = qkv_ref[0].astype(F32)
    y = None
    for j in range(CONV_WIDTH):
        tap = convw_ref[j:j + 1, :] * xbuf[pl.ds(GDN_TAIL - (CONV_WIDTH - 1) + j, c), :]
        y = tap if y is None else y + tap
    xbuf[0:GDN_TAIL, :] = xbuf[c:c + GDN_TAIL, :]
    y = y * _sigmoid(y)

    zs = zs_ref[0]
    a_neg = -(jnp.exp(hp_ref[0:1, :]) * hp_ref[2:3, :])
    xs = zs + hp_ref[1:2, :]
    g_all = a_neg * (jnp.maximum(xs, 0.0) + jnp.log(1.0 + jnp.exp(-jnp.abs(xs))))
    beta_all = _sigmoid(zs)

    row = lax.broadcasted_iota(jnp.int32, (c, c), 0)
    col = lax.broadcasted_iota(jnp.int32, (c, c), 1)
    tril = row >= col
    tril_strict = row > col
    eye = (row == col).astype(F32)
    lower = tril.astype(BF16)
    upper = (row <= col).astype(BF16)

    gh, gl = _split_bf16(g_all)
    gc_all = _dot(lower, gh) + _dot(lower, gl)
    gth, gtl = _split_bf16(g_all.T)
    gc_t = _dot(gth, upper) + _dot(gtl, upper)

    heads = range(B_HEADS)
    q, k, v, beta, gc, gc_row, g_last = [], [], [], [], [], [], []
    for h in heads:
        qh = y[:, h * hd:(h + 1) * hd]
        kh = y[:, B_WIDTH + h * hd:B_WIDTH + (h + 1) * hd]
        q.append(qh * lax.rsqrt(jnp.sum(qh * qh, axis=-1, keepdims=True) + EPS) * (hd ** -0.5))
        k.append(kh * lax.rsqrt(jnp.sum(kh * kh, axis=-1, keepdims=True) + EPS))
        v.append(y[:, 2 * B_WIDTH + h * hd:2 * B_WIDTH + (h + 1) * hd])
        beta.append(beta_all[:, Z_SMALL_BETA + h:Z_SMALL_BETA + h + 1])
        gc.append(gc_all[:, Z_SMALL_A + h:Z_SMALL_A + h + 1])
        gc_row.append(gc_t[Z_SMALL_A + h:Z_SMALL_A + h + 1, :])
        g_last.append(gc_row[h][:, c - 1:c])
    decay = [jnp.exp(jnp.where(tril, gc[h] - gc_row[h], NEG)) for h in heads]
    e_gc = [jnp.exp(gc[h]) for h in heads]
    kb = [k[h] * beta[h] for h in heads]
    kf = [k[h].astype(BF16) for h in heads]
    a_pow = [jnp.where(tril_strict, _dot_nt(kb[h].astype(BF16), kf[h]) * decay[h], 0.0) for h in heads]
    attn = [(_dot_nt(q[h].astype(BF16), kf[h]) * decay[h]).astype(BF16) for h in heads]

    t_mat = [eye - a_pow[h] for h in heads]
    for _ in range(5):
        ab = [a_pow[h].astype(BF16) for h in heads]
        a_pow = [_dot(ab[h], ab[h]) for h in heads]
        t_mat = [t_mat[h] + _dot(t_mat[h].astype(BF16), a_pow[h].astype(BF16)) for h in heads]
    tb = [t_mat[h].astype(BF16) for h in heads]
    u = [_dot(tb[h], (v[h] * beta[h]).astype(BF16)) for h in heads]
    w = [_dot(tb[h], (kb[h] * e_gc[h]).astype(BF16)).astype(BF16) for h in heads]

    state = [s_scr[h] for h in heads]
    sb = [state[h].astype(BF16) for h in heads]
    vnb = [(u[h] - _dot(w[h], sb[h])).astype(BF16) for h in heads]
    o = [_dot((q[h] * e_gc[h]).astype(BF16), sb[h]) + _dot(attn[h], vnb[h]) for h in heads]
    k_dec = [(k[h] * jnp.exp(g_last[h] - gc[h])).astype(BF16) for h in heads]
    for h in heads:
        s_scr[h] = state[h] * jnp.exp(g_last[h]) + _dot_tn(k_dec[h], vnb[h])
    for h in heads:
        cols = slice(h * hd, (h + 1) * hd)
        gate = gate_ref[0, :, cols].astype(F32)
        on = o[h] * lax.rsqrt(jnp.mean(o[h] * o[h], axis=-1, keepdims=True) + EPS) * nw_ref[...]
        o_ref[0, :, cols] = (on * (gate * _sigmoid(gate))).astype(o_ref.dtype)


def _gated_deltanet(z3, zs3, conv_w, head_params, norm_w):
    b, s, _ = z3.shape
    n = s // CHUNK
    return pl.pallas_call(
        _gdn_body,
        grid=(b, n),
        in_specs=[
            pl.BlockSpec((1, CHUNK, 3 * B_WIDTH), lambda bi, ni: (bi, ni, Z_QKV_B // (3 * B_WIDTH))),
            pl.BlockSpec((1, CHUNK, LANES), lambda bi, ni: (bi, ni, 0)),
            pl.BlockSpec((1, CHUNK, B_WIDTH), lambda bi, ni: (bi, ni, Z_GATE_B // B_WIDTH)),
            pl.BlockSpec((CONV_WIDTH, 3 * B_WIDTH), lambda bi, ni: (0, 0)),
            pl.BlockSpec((SUBLANES, LANES), lambda bi, ni: (0, 0)),
            pl.BlockSpec((1, B_HEAD_DIM), lambda bi, ni: (0, 0)),
        ],
        out_specs=pl.BlockSpec((1, CHUNK, B_WIDTH), lambda bi, ni: (bi, ni, 0)),
        out_shape=jax.ShapeDtypeStruct((b, s, B_WIDTH), BF16),
        scratch_shapes=[
            pltpu.VMEM((GDN_TAIL + CHUNK, 3 * B_WIDTH), F32),
            pltpu.VMEM((B_HEADS, B_HEAD_DIM, B_HEAD_DIM), F32),
        ],
        compiler_params=pltpu.CompilerParams(
            dimension_semantics=("parallel", "arbitrary"), vmem_limit_bytes=VMEM_LIMIT),
        name="gdn",
    )(z3, zs3, z3, conv_w, head_params, norm_w)


MERGE_TM = 256


def _top_k_rows(s, k, payload=None):
    n = s.shape[0]
    rows = lax.broadcasted_iota(jnp.int32, s.shape, 0)
    vals, picks = [], []
    for _ in range(k):
        m = jnp.max(s, axis=0, keepdims=True)
        i = jnp.min(jnp.where(s == m, rows, n), axis=0, keepdims=True)
        hit = rows == i
        vals.append(m)
        picks.append(i if payload is None else jnp.max(jnp.where(hit, payload, -1), axis=0, keepdims=True))
        s = jnp.where(hit, -jnp.inf, s)
    return jnp.concatenate(vals, axis=0), jnp.concatenate(picks, axis=0)


def _merge_body(oa_ref, ob_ref, mr_ref, x_ref, bm_ref, pa_ref, pb_ref, wo_ref, nw_ref, wq_ref, k1_ref, k2_ref,
                x1_ref, x1p_ref, xnp_ref, idx_ref, gate_ref, gate_tok_ref):
    tm, d = x_ref.shape
    planes = d // LANES

    def store_planes(ref, a):
        for p in range(planes):
            ref[pl.ds(p, tm, stride=planes), :] = a[:, p * LANES:(p + 1) * LANES]

    mr = mr_ref[...].astype(F32) + bm_ref[...]
    mixed = (_sigmoid(mr[:, :d]) * _dot(oa_ref[...], pa_ref[...])
             + _sigmoid(mr[:, d:]) * _dot(ob_ref[...], pb_ref[...]))
    x1 = x_ref[...] + _dot(mixed.astype(BF16), wo_ref[...])
    x1_ref[...] = x1
    store_planes(x1p_ref, x1)
    xn = x1 * lax.rsqrt(jnp.mean(x1 * x1, axis=-1, keepdims=True) + EPS) * nw_ref[...]
    store_planes(xnp_ref, xn)
    q = _dot(xn.astype(BF16), wq_ref[...]).astype(BF16)

    idx_rows, gate_rows = [], []
    for h in range(PEER_HEADS):
        q1 = q[:, (2 * h) * PEER_HALF:(2 * h + 1) * PEER_HALF]
        q2 = q[:, (2 * h + 1) * PEER_HALF:(2 * h + 2) * PEER_HALF]
        s1 = _dot_nt(k1_ref[h], q1)
        s2 = _dot_nt(k2_ref[h], q2)
        v1, i1 = _top_k_rows(s1, PEER_TOPK)
        v2, i2 = _top_k_rows(s2, PEER_TOPK)
        nb = [PEER_TOPK // (a + 1) for a in range(PEER_TOPK)]
        pad = -sum(nb) % SUBLANES
        tm = v1.shape[1]
        cand = jnp.concatenate([v1[a:a + 1] + v2[:nb[a]] for a in range(PEER_TOPK)]
                               + [jnp.full((pad, tm), -jnp.inf, F32)], axis=0)
        cand_idx = jnp.concatenate([i1[a:a + 1] * PEER_NKEYS + i2[:nb[a]] for a in range(PEER_TOPK)]
                                   + [jnp.zeros((pad, tm), jnp.int32)], axis=0)
        top_s, expert = _top_k_rows(cand, PEER_TOPK, payload=cand_idx)
        e = jnp.exp(top_s - top_s[0:1])
        gate_rows.append(e / jnp.sum(e, axis=0, keepdims=True))
        idx_rows.append(expert)
    idx_ref[...] = jnp.concatenate(idx_rows, axis=0).T
    gates = jnp.concatenate(gate_rows, axis=0)
    gate_ref[...] = gates
    gate_tok_ref[...] = gates.T


def _merge_route(o_a, o_b, z2, x2, row0, b_merge, pa, pb, wo, nw, wq, k1, k2):
    t, d = o_a.shape[0], x2.shape[1]
    tm = MERGE_TM
    xblk0 = row0 // tm
    full = lambda a: pl.BlockSpec(a.shape, lambda i: (0,) * a.ndim)
    return pl.pallas_call(
        _merge_body,
        grid=(t // tm,),
        in_specs=[
            pl.BlockSpec((tm, A_WIDTH), lambda i: (i, 0)),
            pl.BlockSpec((tm, B_WIDTH), lambda i: (i, 0)),
            pl.BlockSpec((tm, 2 * d), lambda i: (i, Z_MERGE // (2 * d))),
            pl.BlockSpec((tm, d), lambda i: (i + xblk0, 0)),
            full(b_merge), full(pa), full(pb), full(wo), full(nw), full(wq), full(k1), full(k2),
        ],
        out_specs=[
            pl.BlockSpec((tm, d), lambda i: (i, 0)),
            pl.BlockSpec((tm * d // LANES, LANES), lambda i: (i, 0)),
            pl.BlockSpec((tm * d // LANES, LANES), lambda i: (i, 0)),
            pl.BlockSpec((tm, PEER_SLOTS), lambda i: (i, 0)),
            pl.BlockSpec((PEER_SLOTS, tm), lambda i: (0, i)),
            pl.BlockSpec((tm, PEER_SLOTS), lambda i: (i, 0)),
        ],
        out_shape=[
            jax.ShapeDtypeStruct((t, d), F32),
            jax.ShapeDtypeStruct((t * d // LANES, LANES), F32),
            jax.ShapeDtypeStruct((t * d // LANES, LANES), F32),
            jax.ShapeDtypeStruct((t, PEER_SLOTS), jnp.int32),
            jax.ShapeDtypeStruct((PEER_SLOTS, t), F32),
            jax.ShapeDtypeStruct((t, PEER_SLOTS), F32),
        ],
        compiler_params=pltpu.CompilerParams(
            dimension_semantics=("parallel",), vmem_limit_bytes=VMEM_LIMIT),
        name="merge_route",
    )(o_a, o_b, z2, x2, b_merge, pa, pb, wo, nw, wq, k1, k2)


PEER_TP = 128
PEER_NBUF = 8
PEER_LOOK = PEER_NBUF - 1
PEER_SC_SHARE_NUM, PEER_SC_SHARE_DEN = 18, 32
PEER_TC_FIRST_NUM, PEER_TC_FIRST_DEN = 13, 32
PEER_SC_TAIL_BLOCKS = 6


def _erf(x):
    return lax.erf(x)


PACK_HI = -65536


def _packed_down(bitcast, word):
    return bitcast(word & PACK_HI, F32)


def _packed_up(bitcast, word):
    return bitcast(word << 16, F32)


def _peer_body(idx_ref, idx_next_ref, xn_ref, gate_ref, x1_ref, tab_ref, o_ref, *scratch):
    bufs, sem = scratch[:PEER_NBUF], scratch[PEER_NBUF]
    planes = x1_ref.shape[1] // LANES
    step = pl.program_id(0)

    def issue(ref, tok, slot):
        for j in range(PEER_SLOTS):
            pltpu.make_async_copy(
                tab_ref.at[ref[0, tok, j]], bufs[slot].at[:, j, :], sem.at[slot]).start(priority=j % 2)

    def wait(slot):
        pltpu.make_async_copy(bufs[slot], bufs[slot], sem.at[slot]).wait()

    @pl.when(step == 0)
    def _():
        for s in range(PEER_LOOK):
            issue(idx_ref, s, s)

    lane = lax.broadcasted_iota(jnp.int32, (PEER_SLOTS, PEER_TP), 1)

    def compute(tok, slot):
        buf = bufs[slot]
        acc = None
        for s in range(planes):
            term = _packed_down(lax.bitcast_convert_type, buf[s]) * xn_ref[pl.ds(tok * planes + s, 1), :]
            acc = term if acc is None else acc + term
        act = jnp.sum(acc, axis=-1, keepdims=True)
        gate = jnp.sum(jnp.where(lane == tok, gate_ref[...], 0.0), axis=-1, keepdims=True)
        w = gate * (0.5 * act * (1.0 + _erf(act * (2.0 ** -0.5))))
        out = jnp.concatenate(
            [jnp.sum(_packed_up(lax.bitcast_convert_type, buf[s]) * w, axis=0, keepdims=True)
             for s in range(planes)], axis=1)
        o_ref[pl.ds(tok, 1), :] = x1_ref[pl.ds(tok, 1), :] + out

    def group(g, carry):
        for u in range(PEER_NBUF):
            tok = g * PEER_NBUF + u
            issue(idx_ref, tok + PEER_LOOK, (u + PEER_LOOK) % PEER_NBUF)
            wait(u)
            compute(tok, u)
        return carry

    n_groups = PEER_TP // PEER_NBUF
    lax.fori_loop(0, n_groups - 1, group, 0)

    for u in range(PEER_NBUF):
        tok = (n_groups - 1) * PEER_NBUF + u
        nxt = tok + PEER_LOOK
        if nxt < PEER_TP:
            issue(idx_ref, nxt, (u + PEER_LOOK) % PEER_NBUF)
        else:
            issue(idx_next_ref, nxt - PEER_TP, (u + PEER_LOOK) % PEER_NBUF)
        wait(u)
        compute(tok, u)

    @pl.when(step == pl.num_programs(0) - 1)
    def _():
        for s in range(PEER_LOOK):
            wait(s)


def _peer_alias_body(prev_ref, after_ref, *rest):
    del prev_ref, after_ref
    _peer_body(*rest)


def _peer(idx3, xnp, gate_t, x1, table, blk0, nb, out_blk0, t_out, prev=None, after=None):
    d = x1.shape[1]
    tp = PEER_TP
    planes = d // LANES
    in_specs = [
        pl.BlockSpec((1, tp, PEER_SLOTS), lambda i: (i + blk0, 0, 0), memory_space=pltpu.SMEM),
        pl.BlockSpec((1, tp, PEER_SLOTS), lambda i: (jnp.minimum(i + 1, nb - 1) + blk0, 0, 0),
                     memory_space=pltpu.SMEM),
        pl.BlockSpec((tp * planes, LANES), lambda i: (i + blk0, 0)),
        pl.BlockSpec((PEER_SLOTS, tp), lambda i: (0, i + blk0)),
        pl.BlockSpec((tp, d), lambda i: (i + blk0, 0)),
        pl.BlockSpec(memory_space=pl.ANY),
    ]
    args = (idx3, idx3, xnp, gate_t, x1, table)
    body, aliases = _peer_body, {}
    if prev is not None:
        body, aliases = _peer_alias_body, {0: 0}
        in_specs = [pl.BlockSpec(memory_space=pl.ANY), pl.BlockSpec(memory_space=pl.ANY)] + in_specs
        args = (prev, after) + args
    return pl.pallas_call(
        body,
        grid=(nb,),
        in_specs=in_specs,
        out_specs=pl.BlockSpec((tp, d), lambda i: (i + blk0 + out_blk0, 0)),
        out_shape=jax.ShapeDtypeStruct((t_out, d), F32),
        scratch_shapes=[pltpu.VMEM((planes, PEER_SLOTS, LANES), jnp.int32) for _ in range(PEER_NBUF)]
        + [pltpu.SemaphoreType.DMA((PEER_NBUF,))],
        input_output_aliases=aliases,
        compiler_params=pltpu.CompilerParams(
            dimension_semantics=("arbitrary",), vmem_limit_bytes=VMEM_LIMIT),
        name="peer",
    )(*args)


SC_CORES = 2
SC_SUBCORES = 16
SC_LANES = 16
SC_WORKERS = SC_CORES * SC_SUBCORES
SC_ROWS = 64


SC_VPR = LANES // SC_LANES
SC_VROWS = PEER_SLOTS // SC_VPR


def _sc_vec(j):
    return (j // SC_VPR, pl.ds((j % SC_VPR) * SC_LANES, SC_LANES))


def _sc_mesh():
    return plsc.VectorSubcoreMesh(core_axis_name="c", subcore_axis_name="s")


def _sc_worker():
    return lax.axis_index("s") * SC_CORES + lax.axis_index("c")


def _sc_pipeline(tw, base, nchunk, loads, gather, compute, finish):
    assert tw % 2 == 0 and nchunk % 2 == 0
    last = base + tw - 1

    def token(tok, s):
        nxt = jnp.minimum(tok + 1, last)
        for c in loads(nxt, 1 - s):
            c.start()
        for kc in range(nchunk):
            if kc + 1 < nchunk:
                gather(s, kc + 1).start()
            else:
                for c in loads(nxt, 1 - s):
                    c.wait()
                gather(1 - s, 0).start()
            gather(s, kc).wait()
            compute(s, kc)
        finish(tok, s)

    for c in loads(base, 0):
        c.start()
    for c in loads(base, 0):
        c.wait()
    gather(0, 0).start()

    def pair(i, carry):
        token(base + 2 * i, 0)
        token(base + 2 * i + 1, 1)
        return carry

    lax.fori_loop(0, tw // 2, pair, 0)
    gather(0, 0).wait()


def _sc_down(idx_flat, xn3, table, tok0, n_tok):
    planes = xn3.shape[1]
    half = planes // 2
    tw = n_tok // SC_WORKERS
    nchunk = PEER_SLOTS // SC_ROWS
    per_plane = LANES // SC_LANES

    @functools.partial(
        pl.kernel, mesh=_sc_mesh(),
        out_type=jax.ShapeDtypeStruct((n_tok, SC_VROWS, LANES), F32),
        scratch_types=[
            pltpu.VMEM((2, PEER_SLOTS), jnp.int32),
            pltpu.VMEM((2, planes, LANES), F32),
            pltpu.VMEM((2, SC_ROWS, half, LANES), jnp.int32),
            pltpu.VMEM((SC_VROWS, LANES), F32),
            pltpu.VMEM((half, SC_VROWS, LANES), F32),
            pltpu.SemaphoreType.DMA((2,)),
            pltpu.SemaphoreType.DMA((2,)),
        ],
        compiler_params=pltpu.CompilerParams(needs_layout_passes=False),
        name="sc_down",
    )
    def run(idx_hbm, xn_hbm, tab_hbm, out_hbm, idx_v, x_v, buf, act_v, part_v, sem_g, sem_l):
        def loads(tok, s):
            return [pltpu.make_async_copy(idx_hbm.at[pl.ds(tok * PEER_SLOTS, PEER_SLOTS)], idx_v.at[s], sem_l.at[s]),
                    pltpu.make_async_copy(xn_hbm.at[tok], x_v.at[s], sem_l.at[s])]

        def gather(s, kc):
            return pltpu.make_async_copy(
                tab_hbm.at[idx_v.at[s, pl.ds(kc * SC_ROWS, SC_ROWS)]], buf.at[kc % 2], sem_g.at[kc % 2])

        def compute(s, kc):
            @plsc.parallel_loop(0, half)
            def _(p):
                sls = [pl.ds(q * SC_LANES, SC_LANES) for q in range(per_plane)]
                x_hi = [x_v[s, p, sl] for sl in sls]
                x_lo = [x_v[s, half + p, sl] for sl in sls]

                def load_row(j):
                    return [buf[kc % 2, j, p, sl] for sl in sls]

                words = load_row(0)
                for j in range(SC_ROWS):
                    ahead = load_row(j + 1) if j + 1 < SC_ROWS else None
                    terms = ([_packed_down(plsc.bitcast, w) * x for w, x in zip(words, x_hi)]
                             + [_packed_up(plsc.bitcast, w) * x for w, x in zip(words, x_lo)])
                    while len(terms) > 1:
                        terms = [terms[i] + terms[i + 1] for i in range(0, len(terms), 2)]
                    part_v[(p,) + _sc_vec(kc * SC_ROWS + j)] = terms[0]
                    words = ahead

        def finish(tok, s):
            @plsc.parallel_loop(0, SC_VROWS)
            def _(r):
                for u in range(SC_VPR):
                    sl = pl.ds(u * SC_LANES, SC_LANES)
                    acc = part_v[0, r, sl]
                    for p in range(1, half):
                        acc = acc + part_v[p, r, sl]
                    act_v[r, sl] = acc
            pltpu.sync_copy(act_v, out_hbm.at[tok - tok0])

        _sc_pipeline(tw, tok0 + _sc_worker() * tw, nchunk, loads, gather, compute, finish)

    return run(idx_flat, xn3, table)


def _sc_up(idx_flat, w16, x13, table, tok0, n_tok):
    planes = x13.shape[1]
    half = planes // 2
    tw = n_tok // SC_WORKERS
    nchunk = PEER_SLOTS // SC_ROWS
    per_plane = LANES // SC_LANES

    @functools.partial(
        pl.kernel, mesh=_sc_mesh(),
        out_type=jax.ShapeDtypeStruct((n_tok, planes, LANES), F32),
        scratch_types=[
            pltpu.VMEM((2, PEER_SLOTS), jnp.int32),
            pltpu.VMEM((2, SC_VROWS, LANES), F32),
            pltpu.VMEM((2, planes, LANES), F32),
            pltpu.VMEM((2, SC_ROWS, half, LANES), jnp.int32),
            pltpu.SemaphoreType.DMA((2,)),
            pltpu.SemaphoreType.DMA((2,)),
        ],
        compiler_params=pltpu.CompilerParams(needs_layout_passes=False),
        name="sc_up",
    )
    def run(idx_hbm, w_hbm, x1_hbm, tab_hbm, out_hbm, idx_v, w_v, out_v, buf, sem_g, sem_l):
        def loads(tok, s):
            return [pltpu.make_async_copy(idx_hbm.at[pl.ds(tok * PEER_SLOTS, PEER_SLOTS)], idx_v.at[s], sem_l.at[s]),
                    pltpu.make_async_copy(w_hbm.at[tok - tok0], w_v.at[s], sem_l.at[s]),
                    pltpu.make_async_copy(x1_hbm.at[tok], out_v.at[s], sem_l.at[s])]

        def gather(s, kc):
            return pltpu.make_async_copy(
                tab_hbm.at[idx_v.at[s, pl.ds(kc * SC_ROWS, SC_ROWS)]], buf.at[kc % 2], sem_g.at[kc % 2])

        def compute(s, kc):
            @plsc.parallel_loop(0, half)
            def _(p):
                sls = [pl.ds(q * SC_LANES, SC_LANES) for q in range(per_plane)]
                acc_hi = [out_v[s, p, sl] for sl in sls]
                acc_lo = [out_v[s, half + p, sl] for sl in sls]
                for j in range(SC_ROWS):
                    wj = w_v[(s,) + _sc_vec(kc * SC_ROWS + j)]
                    words = [buf[kc % 2, j, p, sl] for sl in sls]
                    acc_hi = [a + wj * _packed_down(plsc.bitcast, w) for a, w in zip(acc_hi, words)]
                    acc_lo = [a + wj * _packed_up(plsc.bitcast, w) for a, w in zip(acc_lo, words)]
                for a, sl in zip(acc_hi, sls):
                    out_v[s, p, sl] = a
                for a, sl in zip(acc_lo, sls):
                    out_v[s, half + p, sl] = a

        def finish(tok, s):
            pltpu.sync_copy(out_v.at[s], out_hbm.at[tok - tok0])

        _sc_pipeline(tw, tok0 + _sc_worker() * tw, nchunk, loads, gather, compute, finish)

    return run(idx_flat, w16, x13, table)


ACT_TM = 256


def _peer_act_body(after_ref, part_ref, gate_ref, fold_ref, spread_ref, own_ref, w_ref):
    del after_ref
    tm = part_ref.shape[0]
    own = own_ref[...]
    hi, lo = _split_bf16(part_ref[...].reshape(tm * SC_VROWS, LANES))
    sums = (_dot(hi, fold_ref[...]) + _dot(lo, fold_ref[...])).reshape(tm, SC_VROWS, LANES)
    act = jnp.sum(sums * own, axis=1)
    w = gate_ref[...] * (0.5 * act * (1.0 + _erf(act * (2.0 ** -0.5))))
    hi, lo = _split_bf16((w[:, None, :] * own).reshape(tm * SC_VROWS, LANES))
    w_ref[...] = (_dot(hi, spread_ref[...]) + _dot(lo, spread_ref[...])).reshape(tm, SC_VROWS, LANES)


def _peer_act(after, part3, gate_tok, blk0):
    n_tok = part3.shape[0]
    tm = ACT_TM
    lane = jnp.arange(LANES)
    fold = (lane[:, None] // SC_LANES == lane[None, :] % SC_VPR).astype(BF16)
    spread = fold.T
    own = (lane[None, :] // SC_VPR == jnp.arange(SC_VROWS)[:, None]).astype(F32)
    const = lambda a: pl.BlockSpec(a.shape, lambda i: (0,) * a.ndim)
    return pl.pallas_call(
        _peer_act_body,
        grid=(n_tok // tm,),
        in_specs=[
            pl.BlockSpec(memory_space=pl.ANY),
            pl.BlockSpec((tm, SC_VROWS, LANES), lambda i: (i, 0, 0)),
            pl.BlockSpec((tm, PEER_SLOTS), lambda i: (i + blk0, 0)),
            const(fold), const(spread), const(own),
        ],
        out_specs=pl.BlockSpec((tm, SC_VROWS, LANES), lambda i: (i, 0, 0)),
        out_shape=jax.ShapeDtypeStruct(part3.shape, F32),
        compiler_params=pltpu.CompilerParams(
            dimension_semantics=("parallel",), vmem_limit_bytes=VMEM_LIMIT),
        name="peer_act",
    )(after, part3, gate_tok, fold, spread, own)


def _table_pack_body(down_ref, up_ref, tab_ref, dn2_ref, up2_ref):
    te, d = down_ref.shape
    planes = d // LANES
    half = planes // 2

    def bits(ref, p):
        return lax.bitcast_convert_type(ref[:, p * LANES:(p + 1) * LANES].astype(BF16).astype(F32), jnp.int32)

    def pack(hi, lo):
        return hi | lax.shift_right_logical(lo, 16)

    for p in range(planes):
        tab_ref[pl.ds(p, te, stride=planes), :] = pack(bits(down_ref, p), bits(up_ref, p))
    for p in range(half):
        dn2_ref[pl.ds(p, te, stride=half), :] = pack(bits(down_ref, p), bits(down_ref, half + p))
        up2_ref[pl.ds(p, te, stride=half), :] = pack(bits(up_ref, p), bits(up_ref, half + p))


def _table_pack(down, up, te=512):
    n_exp, d = down.shape
    rows = d // LANES
    return pl.pallas_call(
        _table_pack_body,
        grid=(n_exp // te,),
        in_specs=[pl.BlockSpec((te, d), lambda i: (i, 0)), pl.BlockSpec((te, d), lambda i: (i, 0))],
        out_specs=[pl.BlockSpec((te * rows, LANES), lambda i: (i, 0)),
                   pl.BlockSpec((te * rows // 2, LANES), lambda i: (i, 0)),
                   pl.BlockSpec((te * rows // 2, LANES), lambda i: (i, 0))],
        out_shape=[jax.ShapeDtypeStruct((n_exp * rows, LANES), jnp.int32),
                   jax.ShapeDtypeStruct((n_exp * rows // 2, LANES), jnp.int32),
                   jax.ShapeDtypeStruct((n_exp * rows // 2, LANES), jnp.int32)],
        compiler_params=pltpu.CompilerParams(
            dimension_semantics=("parallel",), vmem_limit_bytes=VMEM_LIMIT),
        name="table_pack",
    )(down, up)


def _pad_lanes(a):
    return jnp.pad(a, ((0, 0), (0, LANES - a.shape[1])))


def kernel(x, norm_mix_w, w_in, a_q_norm_w, a_k_norm_w, a_rel_bias, b_conv_w, b_a_log, b_dt_bias, b_norm_w, b_merge, w_proj_a, w_proj_b, w_out, norm_ffn_w, peer_w_query, peer_keys_1, peer_keys_2, peer_down, peer_up):
    b, s, d = x.shape
    t = b * s
    depth = w_in.shape[0]
    for l in range(depth):
        x2 = x.reshape(t, d)

        wi = w_in[l]
        o_qkvb = 3 * A_WIDTH
        o_a = o_qkvb + 3 * B_WIDTH
        o_beta = o_a + B_HEADS
        o_gate = o_beta + B_HEADS
        o_merge = o_gate + B_WIDTH
        w_main = jnp.concatenate(
            [wi[:, o_qkvb:o_a], wi[:, o_gate:o_merge], wi[:, o_merge:], wi[:, :o_qkvb]], axis=1).astype(BF16)
        w_small = _pad_lanes(wi[:, o_a:o_gate]).astype(BF16)
        z_attn = Z_MERGE + 2 * d

        bias_tab = _attn_bias_table(a_rel_bias[l])
        seg = jnp.kron(jnp.eye(A_HEADS, dtype=F32), jnp.full((A_HEAD_DIM, A_HEAD_DIM), 1.0 / A_HEAD_DIM, F32))
        head_params = jnp.zeros((SUBLANES, LANES), F32)
        head_params = head_params.at[0, :B_HEADS].set(b_a_log[l]).at[1, :B_HEADS].set(b_dt_bias[l])
        head_params = head_params.at[2, :B_HEADS].set(1.0)

        def mixer_and_route(b0, nb):
            tg = nb * s
            z, zs = _in_proj(x2, norm_mix_w[l][None, :], w_main, w_small, min(IN_PROJ_TM, tg), IN_PROJ_TN,
                             b0 * s, tg)
            z3 = z.reshape(nb, s, z.shape[1])
            o_att = _band_attention(
                z3, bias_tab, jnp.tile(a_q_norm_w[l], A_HEADS)[None, :], jnp.tile(a_k_norm_w[l], A_HEADS)[None, :],
                seg.astype(BF16), z_attn // A_WIDTH)
            o_gdn = _gated_deltanet(z3, zs.reshape(nb, s, LANES), b_conv_w[l], head_params, b_norm_w[l][None, :])
            return _merge_route(
                o_att.reshape(tg, A_WIDTH), o_gdn.reshape(tg, B_WIDTH), z, x2, b0 * s, b_merge[l][None, :],
                w_proj_a[l].astype(BF16), w_proj_b[l].astype(BF16), w_out[l].astype(BF16),
                norm_ffn_w[l][None, :], peer_w_query[l].astype(BF16), peer_keys_1[l].astype(BF16),
                peer_keys_2[l].astype(BF16))

        planes = d // LANES
        n_exp = peer_down.shape[1]
        table, down2, up2 = _table_pack(peer_down[l], peer_up[l])
        table = table.reshape(n_exp, planes, LANES)
        down2 = down2.reshape(n_exp, planes // 2, LANES)
        up2 = up2.reshape(n_exp, planes // 2, LANES)

        b_sc = b * PEER_SC_SHARE_NUM // PEER_SC_SHARE_DEN
        n_first = b_sc * s
        tail = min(PEER_SC_TAIL_BLOCKS, n_first // PEER_TP // 2)
        n_sc = n_first - tail * PEER_TP
        x1_s, x1p_s, xnp_s, idx_s, gate_t_s, gate_tok_s = mixer_and_route(0, b_sc)
        idx_s_flat = idx_s.reshape(-1)
        part = _sc_down(idx_s_flat, xnp_s.reshape(n_first, planes, LANES), down2, 0, n_sc)

        x1, _, xnp, idx, gate_t, _ = mixer_and_route(b_sc, b - b_sc)
        nblk = (t - n_first) // PEER_TP
        nblk_a = nblk * PEER_TC_FIRST_NUM // PEER_TC_FIRST_DEN
        idx3 = idx.reshape(nblk, PEER_TP, PEER_SLOTS)
        out = _peer(idx3, xnp, gate_t, x1, table, 0, nblk_a, n_first // PEER_TP, t)
        w16 = _peer_act(out, part, gate_tok_s, 0)
        out = _peer(idx3, xnp, gate_t, x1, table, nblk_a, nblk - nblk_a, n_first // PEER_TP, t, prev=out, after=w16)
        out = _peer(idx_s.reshape(n_first // PEER_TP, PEER_TP, PEER_SLOTS), xnp_s, gate_t_s, x1_s, table,
                    n_sc // PEER_TP, tail, 0, t, prev=out, after=w16)
        sc_out = _sc_up(idx_s_flat, w16, x1p_s.reshape(n_first, planes, LANES), up2, 0, n_sc)
        out = lax.dynamic_update_slice(out, sc_out.reshape(n_sc, d), (0, 0))
        x = out.reshape(b, s, d)
    return x
```

```python
import functools

import jax
import jax.numpy as jnp
from jax import lax
from jax.experimental import pallas as pl
from jax.experimental.pallas import tpu as pltpu
from jax.experimental.pallas import tpu_sc as plsc

F32 = jnp.float32
BF16 = jnp.bfloat16

EPS = 1e-6
NEG = -1e30

CHUNK = 64
A_HEADS = 8
A_HEAD_DIM = 64
A_WIDTH = A_HEADS * A_HEAD_DIM
A_LEFT_CHUNKS = 8
REL_CLIP = 128
B_HEADS = 8
B_HEAD_DIM = 128
B_WIDTH = B_HEADS * B_HEAD_DIM
CONV_WIDTH = 4
PEER_HEADS = 8
PEER_HALF = 128
PEER_NKEYS = 128
PEER_TOPK = 16
PEER_SLOTS = PEER_HEADS * PEER_TOPK

LANES = 128
SUBLANES = 8
VMEM_LIMIT = 56 * 1024 * 1024

Z_QKV_B = 0
Z_GATE_B = 3 * B_WIDTH
Z_MERGE = Z_GATE_B + B_WIDTH
Z_SMALL_A = 0
Z_SMALL_BETA = B_HEADS


def _dot(a, b):
    return jnp.dot(a, b, preferred_element_type=F32)


def _dot_nt(a, b):
    return lax.dot_general(a, b, (((1,), (1,)), ((), ())), preferred_element_type=F32)


def _dot_tn(a, b):
    return lax.dot_general(a, b, (((0,), (0,)), ((), ())), preferred_element_type=F32)


def _split_bf16(a):
    hi = a.astype(BF16)
    lo = (a - hi.astype(F32)).astype(BF16)
    return hi, lo


def _sigmoid(x):
    return 1.0 / (1.0 + jnp.exp(-x))


IN_PROJ_TM = 1024
IN_PROJ_TN = 1920


def _inproj_body(x_ref, nw_ref, w_ref, ws_ref, z_ref, zs_ref, h_scr):
    @pl.when(pl.program_id(1) == 0)
    def _():
        x = x_ref[...]
        ms = jnp.mean(x * x, axis=-1, keepdims=True)
        hb = (x * lax.rsqrt(ms + EPS) * nw_ref[...]).astype(BF16)
        h_scr[...] = hb
        zs_ref[...] = _dot(hb, ws_ref[...])

    z_ref[...] = _dot(h_scr[...], w_ref[...]).astype(z_ref.dtype)


def _in_proj(x2, norm_w, w_main, w_small, tm, tn, row0, t):
    d = x2.shape[1]
    n = w_main.shape[1]
    blk0 = row0 // tm
    return pl.pallas_call(
        _inproj_body,
        grid=(t // tm, n // tn),
        in_specs=[
            pl.BlockSpec((tm, d), lambda i, j: (i + blk0, 0)),
            pl.BlockSpec((1, d), lambda i, j: (0, 0)),
            pl.BlockSpec((d, tn), lambda i, j: (0, j)),
            pl.BlockSpec((d, LANES), lambda i, j: (0, 0)),
        ],
        out_specs=[
            pl.BlockSpec((tm, tn), lambda i, j: (i, j)),
            pl.BlockSpec((tm, LANES), lambda i, j: (i, 0)),
        ],
        out_shape=[
            jax.ShapeDtypeStruct((t, n), BF16),
            jax.ShapeDtypeStruct((t, LANES), F32),
        ],
        scratch_shapes=[pltpu.VMEM((tm, d), BF16)],
        compiler_params=pltpu.CompilerParams(
            dimension_semantics=("parallel", "arbitrary"), vmem_limit_bytes=VMEM_LIMIT),
        name="in_proj",
    )(x2, norm_w, w_main, w_small)


ATT_TQ = 256
ATT_WIN = ATT_TQ + A_LEFT_CHUNKS * CHUNK


def _attn_body(q_ref, k0_ref, k1_ref, k2_ref, v0_ref, v1_ref, v2_ref, bias_ref, qw_ref, kw_ref, seg_ref,
               o_ref):
    t = pl.program_id(1)
    seg = seg_ref[...]

    def head_rms(a, w):
        hi, lo = _split_bf16(a * a)
        ms = _dot(hi, seg) + _dot(lo, seg)
        return a * lax.rsqrt(ms + EPS) * w

    q = head_rms(q_ref[0].astype(F32), qw_ref[...]) * (A_HEAD_DIM ** -0.5)
    k = jnp.concatenate([k0_ref[0], k1_ref[0], k2_ref[0]], axis=0).astype(F32)
    k = head_rms(k, kw_ref[...]).astype(BF16)
    v = jnp.concatenate([v0_ref[0], v1_ref[0], v2_ref[0]], axis=0)

    kpos = t * ATT_TQ - A_LEFT_CHUNKS * CHUNK + lax.broadcasted_iota(jnp.int32, (1, ATT_WIN), 1)
    valid = kpos >= 0
    lane = lax.broadcasted_iota(jnp.int32, (1, LANES), 1)
    first = lane < A_HEAD_DIM

    for pair in range(A_HEADS // 2):
        cols = slice(pair * LANES, (pair + 1) * LANES)
        qp = q[:, cols]
        kp = k[:, cols]
        vp = v[:, cols]
        outs = []
        for half in range(2):
            h = 2 * pair + half
            qm = jnp.where(first if half == 0 else jnp.logical_not(first), qp, 0.0).astype(BF16)
            s = _dot_nt(qm, kp) + bias_ref[h]
            s = jnp.where(valid, s, NEG)
            mx = jnp.max(s, axis=-1, keepdims=True)
            p = jnp.exp(s - mx)
            den = jnp.sum(p, axis=-1, keepdims=True)
            outs.append(_dot(p.astype(BF16), vp) / den)
        o_ref[0, :, cols] = jnp.where(first, outs[0], outs[1]).astype(o_ref.dtype)


def _band_attention(z3, bias_tab, qw, kw, seg, col_q):
    b, s, _ = z3.shape
    nt = s // ATT_TQ

    def kv_spec(back, col):
        return pl.BlockSpec((1, ATT_TQ, A_WIDTH), lambda bi, ti: (bi, jnp.maximum(ti - back, 0), col))

    return pl.pallas_call(
        _attn_body,
        grid=(b, nt),
        in_specs=[
            pl.BlockSpec((1, ATT_TQ, A_WIDTH), lambda bi, ti: (bi, ti, col_q)),
            kv_spec(2, col_q + 1), kv_spec(1, col_q + 1), kv_spec(0, col_q + 1),
            kv_spec(2, col_q + 2), kv_spec(1, col_q + 2), kv_spec(0, col_q + 2),
            pl.BlockSpec((A_HEADS, ATT_TQ, ATT_WIN), lambda bi, ti: (0, 0, 0)),
            pl.BlockSpec((1, A_WIDTH), lambda bi, ti: (0, 0)),
            pl.BlockSpec((1, A_WIDTH), lambda bi, ti: (0, 0)),
            pl.BlockSpec((A_WIDTH, A_WIDTH), lambda bi, ti: (0, 0)),
        ],
        out_specs=pl.BlockSpec((1, ATT_TQ, A_WIDTH), lambda bi, ti: (bi, ti, 0)),
        out_shape=jax.ShapeDtypeStruct((b, s, A_WIDTH), BF16),
        compiler_params=pltpu.CompilerParams(
            dimension_semantics=("parallel", "parallel"), vmem_limit_bytes=VMEM_LIMIT),
        name="band_attn",
    )(z3, z3, z3, z3, z3, z3, z3, bias_tab, qw, kw, seg)


def _attn_bias_table(rel_bias):
    rb = rel_bias.astype(F32)
    heads = rb.shape[0]
    lo = A_LEFT_CHUNKS * CHUNK - (ATT_WIN - 1)
    hi = A_LEFT_CHUNKS * CHUNK + ATT_TQ - 1
    strip = jnp.concatenate([
        jnp.broadcast_to(rb[:, :1], (heads, -REL_CLIP - lo)), rb,
        jnp.broadcast_to(rb[:, -1:], (heads, hi - REL_CLIP))], axis=1)
    rev = strip[:, ::-1]
    n = rev.shape[1]
    flat = jnp.broadcast_to(rev[:, None, :], (heads, ATT_TQ, n)).reshape(heads, ATT_TQ * n)
    bias = flat[:, ATT_TQ - 1:ATT_TQ - 1 + ATT_TQ * (n - 1)].reshape(heads, ATT_TQ, n - 1)[:, :, :ATT_WIN]
    qc = jnp.arange(ATT_TQ)[:, None] // CHUNK
    kc = jnp.arange(ATT_WIN)[None, :] // CHUNK
    in_band = (kc >= qc) & (kc <= qc + A_LEFT_CHUNKS)
    return jnp.where(in_band[None], bias, NEG)


GDN_TAIL = SUBLANES


def _gdn_body(qkv_ref, zs_ref, gate_ref, convw_ref, hp_ref, nw_ref, o_ref, xbuf, s_scr):
    c = CHUNK
    hd = B_HEAD_DIM

    @pl.when(pl.program_id(1) == 0)
    def _():
        xbuf[0:GDN_TAIL, :] = jnp.zeros((GDN_TAIL, 3 * B_WIDTH), F32)
        s_scr[...] = jnp.zeros_like(s_scr)

    xbuf[GDN_TAIL:GDN_TAIL + c, :] = qkv_ref[0].astype(F32)
    y = None
    for j in range(CONV_WIDTH):
        tap = convw_ref[j:j + 1, :] * xbuf[pl.ds(GDN_TAIL - (CONV_WIDTH - 1) + j, c), :]
        y = tap if y is None else y + tap
    xbuf[0:GDN_TAIL, :] = xbuf[c:c + GDN_TAIL, :]
    y = y * _sigmoid(y)

    zs = zs_ref[0]
    a_neg = -(jnp.exp(hp_ref[0:1, :]) * hp_ref[2:3, :])
    xs = zs + hp_ref[1:2, :]
    g_all = a_neg * (jnp.maximum(xs, 0.0) + jnp.log(1.0 + jnp.exp(-jnp.abs(xs))))
    beta_all = _sigmoid(zs)

    row = lax.broadcasted_iota(jnp.int32, (c, c), 0)
    col = lax.broadcasted_iota(jnp.int32, (c, c), 1)
    tril = row >= col
    tril_strict = row > col
    eye = (row == col).astype(F32)
    lower = tril.astype(BF16)
    upper = (row <= col).astype(BF16)

    gh, gl = _split_bf16(g_all)
    gc_all = _dot(lower, gh) + _dot(lower, gl)
    gth, gtl = _split_bf16(g_all.T)
    gc_t = _dot(gth, upper) + _dot(gtl, upper)

    heads = range(B_HEADS)
    q, k, v, beta, gc, gc_row, g_last = [], [], [], [], [], [], []
    for h in heads:
        qh = y[:, h * hd:(h + 1) * hd]
        kh = y[:, B_WIDTH + h * hd:B_WIDTH + (h + 1) * hd]
        q.append(qh * lax.rsqrt(jnp.sum(qh * qh, axis=-1, keepdims=True) + EPS) * (hd ** -0.5))
        k.append(kh * lax.rsqrt(jnp.sum(kh * kh, axis=-1, keepdims=True) + EPS))
        v.append(y[:, 2 * B_WIDTH + h * hd:2 * B_WIDTH + (h + 1) * hd])
        beta.append(beta_all[:, Z_SMALL_BETA + h:Z_SMALL_BETA + h + 1])
        gc.append(gc_all[:, Z_SMALL_A + h:Z_SMALL_A + h + 1])
        gc_row.append(gc_t[Z_SMALL_A + h:Z_SMALL_A + h + 1, :])
        g_last.append(gc_row[h][:, c - 1:c])
    decay = [jnp.exp(jnp.where(tril, gc[h] - gc_row[h], NEG)) for h in heads]
    e_gc = [jnp.exp(gc[h]) for h in heads]
    kb = [k[h] * beta[h] for h in heads]
    kf = [k[h].astype(BF16) for h in heads]
    a_pow = [jnp.where(tril_strict, _dot_nt(kb[h].astype(BF16), kf[h]) * decay[h], 0.0) for h in heads]
    attn = [(_dot_nt(q[h].astype(BF16), kf[h]) * decay[h]).astype(BF16) for h in heads]

    t_mat = [eye - a_pow[h] for h in heads]
    for _ in range(5):
        ab = [a_pow[h].astype(BF16) for h in heads]
        a_pow = [_dot(ab[h], ab[h]) for h in heads]
        t_mat = [t_mat[h] + _dot(t_mat[h].astype(BF16), a_pow[h].astype(BF16)) for h in heads]
    tb = [t_mat[h].astype(BF16) for h in heads]
    u = [_dot(tb[h], (v[h] * beta[h]).astype(BF16)) for h in heads]
    w = [_dot(tb[h], (kb[h] * e_gc[h]).astype(BF16)).astype(BF16) for h in heads]

    state = [s_scr[h] for h in heads]
    sb = [state[h].astype(BF16) for h in heads]
    vnb = [(u[h] - _dot(w[h], sb[h])).astype(BF16) for h in heads]
    o = [_dot((q[h] * e_gc[h]).astype(BF16), sb[h]) + _dot(attn[h], vnb[h]) for h in heads]
    k_dec = [(k[h] * jnp.exp(g_last[h] - gc[h])).astype(BF16) for h in heads]
    for h in heads:
        s_scr[h] = state[h] * jnp.exp(g_last[h]) + _dot_tn(k_dec[h], vnb[h])
    for h in heads:
        cols = slice(h * hd, (h + 1) * hd)
        gate = gate_ref[0, :, cols].astype(F32)
        on = o[h] * lax.rsqrt(jnp.mean(o[h] * o[h], axis=-1, keepdims=True) + EPS) * nw_ref[...]
        o_ref[0, :, cols] = (on * (gate * _sigmoid(gate))).astype(o_ref.dtype)


def _gated_deltanet(z3, zs3, conv_w, head_params, norm_w):
    b, s, _ = z3.shape
    n = s // CHUNK
    return pl.pallas_call(
        _gdn_body,
        grid=(b, n),
        in_specs=[
            pl.BlockSpec((1, CHUNK, 3 * B_WIDTH), lambda bi, ni: (bi, ni, Z_QKV_B // (3 * B_WIDTH))),
            pl.BlockSpec((1, CHUNK, LANES), lambda bi, ni: (bi, ni, 0)),
            pl.BlockSpec((1, CHUNK, B_WIDTH), lambda bi, ni: (bi, ni, Z_GATE_B // B_WIDTH)),
            pl.BlockSpec((CONV_WIDTH, 3 * B_WIDTH), lambda bi, ni: (0, 0)),
            pl.BlockSpec((SUBLANES, LANES), lambda bi, ni: (0, 0)),
            pl.BlockSpec((1, B_HEAD_DIM), lambda bi, ni: (0, 0)),
        ],
        out_specs=pl.BlockSpec((1, CHUNK, B_WIDTH), lambda bi, ni: (bi, ni, 0)),
        out_shape=jax.ShapeDtypeStruct((b, s, B_WIDTH), BF16),
        scratch_shapes=[
            pltpu.VMEM((GDN_TAIL + CHUNK, 3 * B_WIDTH), F32),
            pltpu.VMEM((B_HEADS, B_HEAD_DIM, B_HEAD_DIM), F32),
        ],
        compiler_params=pltpu.CompilerParams(
            dimension_semantics=("parallel", "arbitrary"), vmem_limit_bytes=VMEM_LIMIT),
        name="gdn",
    )(z3, zs3, z3, conv_w, head_params, norm_w)


MERGE_TM = 256


def _top_k_rows(s, k, payload=None):
    n = s.shape[0]
    rows = lax.broadcasted_iota(jnp.int32, s.shape, 0)
    vals, picks = [], []
    for _ in range(k):
        m = jnp.max(s, axis=0, keepdims=True)
        i = jnp.min(jnp.where(s == m, rows, n), axis=0, keepdims=True)
        hit = rows == i
        vals.append(m)
        picks.append(i if payload is None else jnp.max(jnp.where(hit, payload, -1), axis=0, keepdims=True))
        s = jnp.where(hit, -jnp.inf, s)
    return jnp.concatenate(vals, axis=0), jnp.concatenate(picks, axis=0)


def _merge_body(oa_ref, ob_ref, mr_ref, x_ref, bm_ref, pa_ref, pb_ref, wo_ref, nw_ref, wq_ref, k1_ref, k2_ref,
                x1_ref, x1p_ref, xnp_ref, idx_ref, gate_ref, gate_tok_ref):
    tm, d = x_ref.shape
    planes = d // LANES

    def store_planes(ref, a):
        for p in range(planes):
            ref[pl.ds(p, tm, stride=planes), :] = a[:, p * LANES:(p + 1) * LANES]

    mr = mr_ref[...].astype(F32) + bm_ref[...]
    mixed = (_sigmoid(mr[:, :d]) * _dot(oa_ref[...], pa_ref[...])
             + _sigmoid(mr[:, d:]) * _dot(ob_ref[...], pb_ref[...]))
    x1 = x_ref[...] + _dot(mixed.astype(BF16), wo_ref[...])
    x1_ref[...] = x1
    store_planes(x1p_ref, x1)
    xn = x1 * lax.rsqrt(jnp.mean(x1 * x1, axis=-1, keepdims=True) + EPS) * nw_ref[...]
    store_planes(xnp_ref, xn)
    q = _dot(xn.astype(BF16), wq_ref[...]).astype(BF16)

    idx_rows, gate_rows = [], []
    for h in range(PEER_HEADS):
        q1 = q[:, (2 * h) * PEER_HALF:(2 * h + 1) * PEER_HALF]
        q2 = q[:, (2 * h + 1) * PEER_HALF:(2 * h + 2) * PEER_HALF]
        s1 = _dot_nt(k1_ref[h], q1)
        s2 = _dot_nt(k2_ref[h], q2)
        v1, i1 = _top_k_rows(s1, PEER_TOPK)
        v2, i2 = _top_k_rows(s2, PEER_TOPK)
        nb = [PEER_TOPK // (a + 1) for a in range(PEER_TOPK)]
        pad = -sum(nb) % SUBLANES
        tm = v1.shape[1]
        cand = jnp.concatenate([v1[a:a + 1] + v2[:nb[a]] for a in range(PEER_TOPK)]
                               + [jnp.full((pad, tm), -jnp.inf, F32)], axis=0)
        cand_idx = jnp.concatenate([i1[a:a + 1] * PEER_NKEYS + i2[:nb[a]] for a in range(PEER_TOPK)]
                                   + [jnp.zeros((pad, tm), jnp.int32)], axis=0)
        top_s, expert = _top_k_rows(cand, PEER_TOPK, payload=cand_idx)
        e = jnp.exp(top_s - top_s[0:1])
        gate_rows.append(e / jnp.sum(e, axis=0, keepdims=True))
        idx_rows.append(expert)
    idx_ref[...] = jnp.concatenate(idx_rows, axis=0).T
    gates = jnp.concatenate(gate_rows, axis=0)
    gate_ref[...] = gates
    gate_tok_ref[...] = gates.T


def _merge_route(o_a, o_b, z2, x2, row0, b_merge, pa, pb, wo, nw, wq, k1, k2):
    t, d = o_a.shape[0], x2.shape[1]
    tm = MERGE_TM
    xblk0 = row0 // tm
    full = lambda a: pl.BlockSpec(a.shape, lambda i: (0,) * a.ndim)
    return pl.pallas_call(
        _merge_body,
        grid=(t // tm,),
        in_specs=[
            pl.BlockSpec((tm, A_WIDTH), lambda i: (i, 0)),
            pl.BlockSpec((tm, B_WIDTH), lambda i: (i, 0)),
            pl.BlockSpec((tm, 2 * d), lambda i: (i, Z_MERGE // (2 * d))),
            pl.BlockSpec((tm, d), lambda i: (i + xblk0, 0)),
            full(b_merge), full(pa), full(pb), full(wo), full(nw), full(wq), full(k1), full(k2),
        ],
        out_specs=[
            pl.BlockSpec((tm, d), lambda i: (i, 0)),
            pl.BlockSpec((tm * d // LANES, LANES), lambda i: (i, 0)),
            pl.BlockSpec((tm * d // LANES, LANES), lambda i: (i, 0)),
            pl.BlockSpec((tm, PEER_SLOTS), lambda i: (i, 0)),
            pl.BlockSpec((PEER_SLOTS, tm), lambda i: (0, i)),
            pl.BlockSpec((tm, PEER_SLOTS), lambda i: (i, 0)),
        ],
        out_shape=[
            jax.ShapeDtypeStruct((t, d), F32),
            jax.ShapeDtypeStruct((t * d // LANES, LANES), F32),
            jax.ShapeDtypeStruct((t * d // LANES, LANES), F32),
            jax.ShapeDtypeStruct((t, PEER_SLOTS), jnp.int32),
            jax.ShapeDtypeStruct((PEER_SLOTS, t), F32),
            jax.ShapeDtypeStruct((t, PEER_SLOTS), F32),
        ],
        compiler_params=pltpu.CompilerParams(
            dimension_semantics=("parallel",), vmem_limit_bytes=VMEM_LIMIT),
        name="merge_route",
    )(o_a, o_b, z2, x2, b_merge, pa, pb, wo, nw, wq, k1, k2)


PEER_TP = 128
PEER_NBUF = 8
PEER_LOOK = PEER_NBUF - 1
PEER_SC_SHARE_NUM, PEER_SC_SHARE_DEN = 18, 32
PEER_TC_FIRST_NUM, PEER_TC_FIRST_DEN = 13, 32
PEER_SC_TAIL_BLOCKS = 6


def _erf(x):
    return lax.erf(x)


PACK_HI = -65536


def _packed_down(bitcast, word):
    return bitcast(word & PACK_HI, F32)


def _packed_up(bitcast, word):
    return bitcast(word << 16, F32)


def _peer_body(idx_ref, idx_next_ref, xn_ref, gate_ref, x1_ref, tab_ref, o_ref, *scratch):
    bufs, sem = scratch[:PEER_NBUF], scratch[PEER_NBUF]
    planes = x1_ref.shape[1] // LANES
    step = pl.program_id(0)

    def issue(ref, tok, slot):
        for j in range(PEER_SLOTS):
            pltpu.make_async_copy(
                tab_ref.at[ref[0, tok, j]], bufs[slot].at[:, j, :], sem.at[slot]).start(priority=j % 2)

    def wait(slot):
        pltpu.make_async_copy(bufs[slot], bufs[slot], sem.at[slot]).wait()

    @pl.when(step == 0)
    def _():
        for s in range(PEER_LOOK):
            issue(idx_ref, s, s)

    lane = lax.broadcasted_iota(jnp.int32, (PEER_SLOTS, PEER_TP), 1)

    def compute(tok, slot):
        buf = bufs[slot]
        acc = None
        for s in range(planes):
            term = _packed_down(lax.bitcast_convert_type, buf[s]) * xn_ref[pl.ds(tok * planes + s, 1), :]
            acc = term if acc is None else acc + term
        act = jnp.sum(acc, axis=-1, keepdims=True)
        gate = jnp.sum(jnp.where(lane == tok, gate_ref[...], 0.0), axis=-1, keepdims=True)
        w = gate * (0.5 * act * (1.0 + _erf(act * (2.0 ** -0.5))))
        out = jnp.concatenate(
            [jnp.sum(_packed_up(lax.bitcast_convert_type, buf[s]) * w, axis=0, keepdims=True)
             for s in range(planes)], axis=1)
        o_ref[pl.ds(tok, 1), :] = x1_ref[pl.ds(tok, 1), :] + out

    def group(g, carry):
        for u in range(PEER_NBUF):
            tok = g * PEER_NBUF + u
            issue(idx_ref, tok + PEER_LOOK, (u + PEER_LOOK) % PEER_NBUF)
            wait(u)
            compute(tok, u)
        return carry

    n_groups = PEER_TP // PEER_NBUF
    lax.fori_loop(0, n_groups - 1, group, 0)

    for u in range(PEER_NBUF):
        tok = (n_groups - 1) * PEER_NBUF + u
        nxt = tok + PEER_LOOK
        if nxt < PEER_TP:
            issue(idx_ref, nxt, (u + PEER_LOOK) % PEER_NBUF)
        else:
            issue(idx_next_ref, nxt - PEER_TP, (u + PEER_LOOK) % PEER_NBUF)
        wait(u)
        compute(tok, u)

    @pl.when(step == pl.num_programs(0) - 1)
    def _():
        for s in range(PEER_LOOK):
            wait(s)


def _peer_alias_body(prev_ref, after_ref, *rest):
    del prev_ref, after_ref
    _peer_body(*rest)


def _peer(idx3, xnp, gate_t, x1, table, blk0, nb, out_blk0, t_out, prev=None, after=None):
    d = x1.shape[1]
    tp = PEER_TP
    planes = d // LANES
    in_specs = [
        pl.BlockSpec((1, tp, PEER_SLOTS), lambda i: (i + blk0, 0, 0), memory_space=pltpu.SMEM),
        pl.BlockSpec((1, tp, PEER_SLOTS), lambda i: (jnp.minimum(i + 1, nb - 1) + blk0, 0, 0),
                     memory_space=pltpu.SMEM),
        pl.BlockSpec((tp * planes, LANES), lambda i: (i + blk0, 0)),
        pl.BlockSpec((PEER_SLOTS, tp), lambda i: (0, i + blk0)),
        pl.BlockSpec((tp, d), lambda i: (i + blk0, 0)),
        pl.BlockSpec(memory_space=pl.ANY),
    ]
    args = (idx3, idx3, xnp, gate_t, x1, table)
    body, aliases = _peer_body, {}
    if prev is not None:
        body, aliases = _peer_alias_body, {0: 0}
        in_specs = [pl.BlockSpec(memory_space=pl.ANY), pl.BlockSpec(memory_space=pl.ANY)] + in_specs
        args = (prev, after) + args
    return pl.pallas_call(
        body,
        grid=(nb,),
        in_specs=in_specs,
        out_specs=pl.BlockSpec((tp, d), lambda i: (i + blk0 + out_blk0, 0)),
        out_shape=jax.ShapeDtypeStruct((t_out, d), F32),
        scratch_shapes=[pltpu.VMEM((planes, PEER_SLOTS, LANES), jnp.int32) for _ in range(PEER_NBUF)]
        + [pltpu.SemaphoreType.DMA((PEER_NBUF,))],
        input_output_aliases=aliases,
        compiler_params=pltpu.CompilerParams(
            dimension_semantics=("arbitrary",), vmem_limit_bytes=VMEM_LIMIT),
        name="peer",
    )(*args)


SC_CORES = 2
SC_SUBCORES = 16
SC_LANES = 16
SC_WORKERS = SC_CORES * SC_SUBCORES
SC_ROWS = 64


SC_VPR = LANES // SC_LANES
SC_VROWS = PEER_SLOTS // SC_VPR


def _sc_vec(j):
    return (j // SC_VPR, pl.ds((j % SC_VPR) * SC_LANES, SC_LANES))


def _sc_mesh():
    return plsc.VectorSubcoreMesh(core_axis_name="c", subcore_axis_name="s")


def _sc_worker():
    return lax.axis_index("s") * SC_CORES + lax.axis_index("c")


def _sc_pipeline(tw, base, nchunk, loads, gather, compute, finish):
    assert tw % 2 == 0 and nchunk % 2 == 0
    last = base + tw - 1

    def token(tok, s):
        nxt = jnp.minimum(tok + 1, last)
        for c in loads(nxt, 1 - s):
            c.start()
        for kc in range(nchunk):
            if kc + 1 < nchunk:
                gather(s, kc + 1).start()
            else:
                for c in loads(nxt, 1 - s):
                    c.wait()
                gather(1 - s, 0).start()
            gather(s, kc).wait()
            compute(s, kc)
        finish(tok, s)

    for c in loads(base, 0):
        c.start()
    for c in loads(base, 0):
        c.wait()
    gather(0, 0).start()

    def pair(i, carry):
        token(base + 2 * i, 0)
        token(base + 2 * i + 1, 1)
        return carry

    lax.fori_loop(0, tw // 2, pair, 0)
    gather(0, 0).wait()


def _sc_down(idx_flat, xn3, table, tok0, n_tok):
    planes = xn3.shape[1]
    half = planes // 2
    tw = n_tok // SC_WORKERS
    nchunk = PEER_SLOTS // SC_ROWS
    per_plane = LANES // SC_LANES

    @functools.partial(
        pl.kernel, mesh=_sc_mesh(),
        out_type=jax.ShapeDtypeStruct((n_tok, SC_VROWS, LANES), F32),
        scratch_types=[
            pltpu.VMEM((2, PEER_SLOTS), jnp.int32),
            pltpu.VMEM((2, planes, LANES), F32),
            pltpu.VMEM((2, SC_ROWS, half, LANES), jnp.int32),
            pltpu.VMEM((2, SC_VROWS, LANES), F32),
            pltpu.VMEM((half, SC_VROWS, LANES), F32),
            pltpu.SemaphoreType.DMA((2,)),
            pltpu.SemaphoreType.DMA((2,)),
            pltpu.SemaphoreType.DMA((2,)),
        ],
        compiler_params=pltpu.CompilerParams(needs_layout_passes=False),
        name="sc_down",
    )
    def run(idx_hbm, xn_hbm, tab_hbm, out_hbm, idx_v, x_v, buf, act_v, part_v, sem_g, sem_l, sem_o):
        def loads(tok, s):
            return [pltpu.make_async_copy(idx_hbm.at[pl.ds(tok * PEER_SLOTS, PEER_SLOTS)], idx_v.at[s], sem_l.at[s]),
                    pltpu.make_async_copy(xn_hbm.at[tok], x_v.at[s], sem_l.at[s])]

        def gather(s, kc):
            return pltpu.make_async_copy(
                tab_hbm.at[idx_v.at[s, pl.ds(kc * SC_ROWS, SC_ROWS)]], buf.at[kc % 2], sem_g.at[kc % 2])

        def compute(s, kc):
            @plsc.parallel_loop(0, half)
            def _(p):
                sls = [pl.ds(q * SC_LANES, SC_LANES) for q in range(per_plane)]
                x_hi = [x_v[s, p, sl] for sl in sls]
                x_lo = [x_v[s, half + p, sl] for sl in sls]

                def load_row(j):
                    return [buf[kc % 2, j, p, sl] for sl in sls]

                words = load_row(0)
                for j in range(SC_ROWS):
                    ahead = load_row(j + 1) if j + 1 < SC_ROWS else None
                    terms = ([_packed_down(plsc.bitcast, w) * x for w, x in zip(words, x_hi)]
                             + [_packed_up(plsc.bitcast, w) * x for w, x in zip(words, x_lo)])
                    while len(terms) > 1:
                        terms = [terms[i] + terms[i + 1] for i in range(0, len(terms), 2)]
                    part_v[(p,) + _sc_vec(kc * SC_ROWS + j)] = terms[0]
                    words = ahead

        def finish(tok, s):
            @plsc.parallel_loop(0, SC_VROWS)
            def _(r):
                for u in range(SC_VPR):
                    sl = pl.ds(u * SC_LANES, SC_LANES)
                    acc = part_v[0, r, sl]
                    for p in range(1, half):
                        acc = acc + part_v[p, r, sl]
                    act_v[s, r, sl] = acc
            store(tok, s).start()

            @pl.when(tok > base)
            def _():
                store(tok - 1, 1 - s).wait()

        def store(tok, s):
            return pltpu.make_async_copy(act_v.at[s], out_hbm.at[tok - tok0], sem_o.at[s])

        base = tok0 + _sc_worker() * tw
        _sc_pipeline(tw, base, nchunk, loads, gather, compute, finish)
        store(base + tw - 1, 1).wait()

    return run(idx_flat, xn3, table)


def _sc_up(idx_flat, w16, x13, table, tok0, n_tok):
    planes = x13.shape[1]
    half = planes // 2
    tw = n_tok // SC_WORKERS
    nchunk = PEER_SLOTS // SC_ROWS
    per_plane = LANES // SC_LANES

    @functools.partial(
        pl.kernel, mesh=_sc_mesh(),
        out_type=jax.ShapeDtypeStruct((n_tok, planes, LANES), F32),
        scratch_types=[
            pltpu.VMEM((2, PEER_SLOTS), jnp.int32),
            pltpu.VMEM((2, SC_VROWS, LANES), F32),
            pltpu.VMEM((2, planes, LANES), F32),
            pltpu.VMEM((2, SC_ROWS, half, LANES), jnp.int32),
            pltpu.SemaphoreType.DMA((2,)),
            pltpu.SemaphoreType.DMA((2,)),
        ],
        compiler_params=pltpu.CompilerParams(needs_layout_passes=False),
        name="sc_up",
    )
    def run(idx_hbm, w_hbm, x1_hbm, tab_hbm, out_hbm, idx_v, w_v, out_v, buf, sem_g, sem_l):
        def loads(tok, s):
            return [pltpu.make_async_copy(idx_hbm.at[pl.ds(tok * PEER_SLOTS, PEER_SLOTS)], idx_v.at[s], sem_l.at[s]),
                    pltpu.make_async_copy(w_hbm.at[tok - tok0], w_v.at[s], sem_l.at[s]),
                    pltpu.make_async_copy(x1_hbm.at[tok], out_v.at[s], sem_l.at[s])]

        def gather(s, kc):
            return pltpu.make_async_copy(
                tab_hbm.at[idx_v.at[s, pl.ds(kc * SC_ROWS, SC_ROWS)]], buf.at[kc % 2], sem_g.at[kc % 2])

        def compute(s, kc):
            @plsc.parallel_loop(0, half)
            def _(p):
                sls = [pl.ds(q * SC_LANES, SC_LANES) for q in range(per_plane)]
                acc_hi = [out_v[s, p, sl] for sl in sls]
                acc_lo = [out_v[s, half + p, sl] for sl in sls]
                for j in range(SC_ROWS):
                    wj = w_v[(s,) + _sc_vec(kc * SC_ROWS + j)]
                    words = [buf[kc % 2, j, p, sl] for sl in sls]
                    acc_hi = [a + wj * _packed_down(plsc.bitcast, w) for a, w in zip(acc_hi, words)]
                    acc_lo = [a + wj * _packed_up(plsc.bitcast, w) for a, w in zip(acc_lo, words)]
                for a, sl in zip(acc_hi, sls):
                    out_v[s, p, sl] = a
                for a, sl in zip(acc_lo, sls):
                    out_v[s, half + p, sl] = a

        def finish(tok, s):
            pltpu.sync_copy(out_v.at[s], out_hbm.at[tok - tok0])

        _sc_pipeline(tw, tok0 + _sc_worker() * tw, nchunk, loads, gather, compute, finish)

    return run(idx_flat, w16, x13, table)


ACT_TM = 256


def _peer_act_body(after_ref, part_ref, gate_ref, fold_ref, spread_ref, own_ref, w_ref):
    del after_ref
    tm = part_ref.shape[0]
    own = own_ref[...]
    hi, lo = _split_bf16(part_ref[...].reshape(tm * SC_VROWS, LANES))
    sums = (_dot(hi, fold_ref[...]) + _dot(lo, fold_ref[...])).reshape(tm, SC_VROWS, LANES)
    act = jnp.sum(sums * own, axis=1)
    w = gate_ref[...] * (0.5 * act * (1.0 + _erf(act * (2.0 ** -0.5))))
    hi, lo = _split_bf16((w[:, None, :] * own).reshape(tm * SC_VROWS, LANES))
    w_ref[...] = (_dot(hi, spread_ref[...]) + _dot(lo, spread_ref[...])).reshape(tm, SC_VROWS, LANES)


def _peer_act(after, part3, gate_tok, blk0):
    n_tok = part3.shape[0]
    tm = ACT_TM
    lane = jnp.arange(LANES)
    fold = (lane[:, None] // SC_LANES == lane[None, :] % SC_VPR).astype(BF16)
    spread = fold.T
    own = (lane[None, :] // SC_VPR == jnp.arange(SC_VROWS)[:, None]).astype(F32)
    const = lambda a: pl.BlockSpec(a.shape, lambda i: (0,) * a.ndim)
    return pl.pallas_call(
        _peer_act_body,
        grid=(n_tok // tm,),
        in_specs=[
            pl.BlockSpec(memory_space=pl.ANY),
            pl.BlockSpec((tm, SC_VROWS, LANES), lambda i: (i, 0, 0)),
            pl.BlockSpec((tm, PEER_SLOTS), lambda i: (i + blk0, 0)),
            const(fold), const(spread), const(own),
        ],
        out_specs=pl.BlockSpec((tm, SC_VROWS, LANES), lambda i: (i, 0, 0)),
        out_shape=jax.ShapeDtypeStruct(part3.shape, F32),
        compiler_params=pltpu.CompilerParams(
            dimension_semantics=("parallel",), vmem_limit_bytes=VMEM_LIMIT),
        name="peer_act",
    )(after, part3, gate_tok, fold, spread, own)


def _table_pack_body(down_ref, up_ref, tab_ref, dn2_ref, up2_ref):
    te, d = down_ref.shape
    planes = d // LANES
    half = planes // 2

    def bits(ref, p):
        return lax.bitcast_convert_type(ref[:, p * LANES:(p + 1) * LANES].astype(BF16).astype(F32), jnp.int32)

    def pack(hi, lo):
        return hi | lax.shift_right_logical(lo, 16)

    for p in range(planes):
        tab_ref[pl.ds(p, te, stride=planes), :] = pack(bits(down_ref, p), bits(up_ref, p))
    for p in range(half):
        dn2_ref[pl.ds(p, te, stride=half), :] = pack(bits(down_ref, p), bits(down_ref, half + p))
        up2_ref[pl.ds(p, te, stride=half), :] = pack(bits(up_ref, p), bits(up_ref, half + p))


def _table_pack(down, up, te=512):
    n_exp, d = down.shape
    rows = d // LANES
    return pl.pallas_call(
        _table_pack_body,
        grid=(n_exp // te,),
        in_specs=[pl.BlockSpec((te, d), lambda i: (i, 0)), pl.BlockSpec((te, d), lambda i: (i, 0))],
        out_specs=[pl.BlockSpec((te * rows, LANES), lambda i: (i, 0)),
                   pl.BlockSpec((te * rows // 2, LANES), lambda i: (i, 0)),
                   pl.BlockSpec((te * rows // 2, LANES), lambda i: (i, 0))],
        out_shape=[jax.ShapeDtypeStruct((n_exp * rows, LANES), jnp.int32),
                   jax.ShapeDtypeStruct((n_exp * rows // 2, LANES), jnp.int32),
                   jax.ShapeDtypeStruct((n_exp * rows // 2, LANES), jnp.int32)],
        compiler_params=pltpu.CompilerParams(
            dimension_semantics=("parallel",), vmem_limit_bytes=VMEM_LIMIT),
        name="table_pack",
    )(down, up)


def _pad_lanes(a):
    return jnp.pad(a, ((0, 0), (0, LANES - a.shape[1])))


def kernel(x, norm_mix_w, w_in, a_q_norm_w, a_k_norm_w, a_rel_bias, b_conv_w, b_a_log, b_dt_bias, b_norm_w, b_merge, w_proj_a, w_proj_b, w_out, norm_ffn_w, peer_w_query, peer_keys_1, peer_keys_2, peer_down, peer_up):
    b, s, d = x.shape
    t = b * s
    depth = w_in.shape[0]
    for l in range(depth):
        x2 = x.reshape(t, d)

        wi = w_in[l]
        o_qkvb = 3 * A_WIDTH
        o_a = o_qkvb + 3 * B_WIDTH
        o_beta = o_a + B_HEADS
        o_gate = o_beta + B_HEADS
        o_merge = o_gate + B_WIDTH
        w_main = jnp.concatenate(
            [wi[:, o_qkvb:o_a], wi[:, o_gate:o_merge], wi[:, o_merge:], wi[:, :o_qkvb]], axis=1).astype(BF16)
        w_small = _pad_lanes(wi[:, o_a:o_gate]).astype(BF16)
        z_attn = Z_MERGE + 2 * d

        bias_tab = _attn_bias_table(a_rel_bias[l])
        seg = jnp.kron(jnp.eye(A_HEADS, dtype=F32), jnp.full((A_HEAD_DIM, A_HEAD_DIM), 1.0 / A_HEAD_DIM, F32))
        head_params = jnp.zeros((SUBLANES, LANES), F32)
        head_params = head_params.at[0, :B_HEADS].set(b_a_log[l]).at[1, :B_HEADS].set(b_dt_bias[l])
        head_params = head_params.at[2, :B_HEADS].set(1.0)

        def mixer_and_route(b0, nb):
            tg = nb * s
            z, zs = _in_proj(x2, norm_mix_w[l][None, :], w_main, w_small, min(IN_PROJ_TM, tg), IN_PROJ_TN,
                             b0 * s, tg)
            z3 = z.reshape(nb, s, z.shape[1])
            o_att = _band_attention(
                z3, bias_tab, jnp.tile(a_q_norm_w[l], A_HEADS)[None, :], jnp.tile(a_k_norm_w[l], A_HEADS)[None, :],
                seg.astype(BF16), z_attn // A_WIDTH)
            o_gdn = _gated_deltanet(z3, zs.reshape(nb, s, LANES), b_conv_w[l], head_params, b_norm_w[l][None, :])
            return _merge_route(
                o_att.reshape(tg, A_WIDTH), o_gdn.reshape(tg, B_WIDTH), z, x2, b0 * s, b_merge[l][None, :],
                w_proj_a[l].astype(BF16), w_proj_b[l].astype(BF16), w_out[l].astype(BF16),
                norm_ffn_w[l][None, :], peer_w_query[l].astype(BF16), peer_keys_1[l].astype(BF16),
                peer_keys_2[l].astype(BF16))

        planes = d // LANES
        n_exp = peer_down.shape[1]
        table, down2, up2 = _table_pack(peer_down[l], peer_up[l])
        table = table.reshape(n_exp, planes, LANES)
        down2 = down2.reshape(n_exp, planes // 2, LANES)
        up2 = up2.reshape(n_exp, planes // 2, LANES)

        b_sc = b * PEER_SC_SHARE_NUM // PEER_SC_SHARE_DEN
        n_first = b_sc * s
        tail = min(PEER_SC_TAIL_BLOCKS, n_first // PEER_TP // 2)
        n_sc = n_first - tail * PEER_TP
        x1_s, x1p_s, xnp_s, idx_s, gate_t_s, gate_tok_s = mixer_and_route(0, b_sc)
        idx_s_flat = idx_s.reshape(-1)
        part = _sc_down(idx_s_flat, xnp_s.reshape(n_first, planes, LANES), down2, 0, n_sc)

        x1, _, xnp, idx, gate_t, _ = mixer_and_route(b_sc, b - b_sc)
        nblk = (t - n_first) // PEER_TP
        nblk_a = nblk * PEER_TC_FIRST_NUM // PEER_TC_FIRST_DEN
        idx3 = idx.reshape(nblk, PEER_TP, PEER_SLOTS)
        out = _peer(idx3, xnp, gate_t, x1, table, 0, nblk_a, n_first // PEER_TP, t)
        w16 = _peer_act(out, part, gate_tok_s, 0)
        out = _peer(idx3, xnp, gate_t, x1, table, nblk_a, nblk - nblk_a, n_first // PEER_TP, t, prev=out, after=w16)
        out = _peer(idx_s.reshape(n_first // PEER_TP, PEER_TP, PEER_SLOTS), xnp_s, gate_t_s, x1_s, table,
                    n_sc // PEER_TP, tail, 0, t, prev=out, after=w16)
        sc_out = _sc_up(idx_s_flat, w16, x1p_s.reshape(n_first, planes, LANES), up2, 0, n_sc)
        out = lax.dynamic_update_slice(out, sc_out.reshape(n_sc, d), (0, 0))
        x = out.reshape(b, s, d)
    return x
```

```python
import functools

import jax
import jax.numpy as jnp
from jax import lax
from jax.experimental import pallas as pl
from jax.experimental.pallas import tpu as pltpu
from jax.experimental.pallas import tpu_sc as plsc

F32 = jnp.float32
BF16 = jnp.bfloat16

EPS = 1e-6
NEG = -1e30

CHUNK = 64
A_HEADS = 8
A_HEAD_DIM = 64
A_WIDTH = A_HEADS * A_HEAD_DIM
A_LEFT_CHUNKS = 8
REL_CLIP = 128
B_HEADS = 8
B_HEAD_DIM = 128
B_WIDTH = B_HEADS * B_HEAD_DIM
CONV_WIDTH = 4
PEER_HEADS = 8
PEER_HALF = 128
PEER_NKEYS = 128
PEER_TOPK = 16
PEER_SLOTS = PEER_HEADS * PEER_TOPK

LANES = 128
SUBLANES = 8
VMEM_LIMIT = 56 * 1024 * 1024

Z_QKV_B = 0
Z_GATE_B = 3 * B_WIDTH
Z_MERGE = Z_GATE_B + B_WIDTH
Z_SMALL_A = 0
Z_SMALL_BETA = B_HEADS


def _dot(a, b):
    return jnp.dot(a, b, preferred_element_type=F32)


def _dot_nt(a, b):
    return lax.dot_general(a, b, (((1,), (1,)), ((), ())), preferred_element_type=F32)


def _dot_tn(a, b):
    return lax.dot_general(a, b, (((0,), (0,)), ((), ())), preferred_element_type=F32)


def _split_bf16(a):
    hi = a.astype(BF16)
    lo = (a - hi.astype(F32)).astype(BF16)
    return hi, lo


def _sigmoid(x):
    return 1.0 / (1.0 + jnp.exp(-x))


IN_PROJ_TM = 1024
IN_PROJ_TN = 3840


def _inproj_body(x_ref, nw_ref, w_ref, ws_ref, z_ref, zs_ref, h_scr):
    @pl.when(pl.program_id(1) == 0)
    def _():
        x = x_ref[...]
        ms = jnp.mean(x * x, axis=-1, keepdims=True)
        hb = (x * lax.rsqrt(ms + EPS) * nw_ref[...]).astype(BF16)
        h_scr[...] = hb
        zs_ref[...] = _dot(hb, ws_ref[...])

    z_ref[...] = _dot(h_scr[...], w_ref[...]).astype(z_ref.dtype)


def _in_proj(x2, norm_w, w_main, w_small, tm, tn, row0, t):
    d = x2.shape[1]
    n = w_main.shape[1]
    blk0 = row0 // tm
    return pl.pallas_call(
        _inproj_body,
        grid=(t // tm, n // tn),
        in_specs=[
            pl.BlockSpec((tm, d), lambda i, j: (i + blk0, 0)),
            pl.BlockSpec((1, d), lambda i, j: (0, 0)),
            pl.BlockSpec((d, tn), lambda i, j: (0, j)),
            pl.BlockSpec((d, LANES), lambda i, j: (0, 0)),
        ],
        out_specs=[
            pl.BlockSpec((tm, tn), lambda i, j: (i, j)),
            pl.BlockSpec((tm, LANES), lambda i, j: (i, 0)),
        ],
        out_shape=[
            jax.ShapeDtypeStruct((t, n), BF16),
            jax.ShapeDtypeStruct((t, LANES), F32),
        ],
        scratch_shapes=[pltpu.VMEM((tm, d), BF16)],
        compiler_params=pltpu.CompilerParams(
            dimension_semantics=("parallel", "arbitrary"), vmem_limit_bytes=VMEM_LIMIT),
        name="in_proj",
    )(x2, norm_w, w_main, w_small)


ATT_TQ = 256
ATT_WIN = ATT_TQ + A_LEFT_CHUNKS * CHUNK


def _attn_body(q_ref, k0_ref, k1_ref, k2_ref, v0_ref, v1_ref, v2_ref, bias_ref, qw_ref, kw_ref, seg_ref,
               o_ref):
    t = pl.program_id(1)
    seg = seg_ref[...]

    def head_rms(a, w):
        hi, lo = _split_bf16(a * a)
        ms = _dot(hi, seg) + _dot(lo, seg)
        return a * lax.rsqrt(ms + EPS) * w

    q = head_rms(q_ref[0].astype(F32), qw_ref[...]) * (A_HEAD_DIM ** -0.5)
    k = jnp.concatenate([k0_ref[0], k1_ref[0], k2_ref[0]], axis=0).astype(F32)
    k = head_rms(k, kw_ref[...]).astype(BF16)
    v = jnp.concatenate([v0_ref[0], v1_ref[0], v2_ref[0]], axis=0)

    kpos = t * ATT_TQ - A_LEFT_CHUNKS * CHUNK + lax.broadcasted_iota(jnp.int32, (1, ATT_WIN), 1)
    valid = kpos >= 0
    lane = lax.broadcasted_iota(jnp.int32, (1, LANES), 1)
    first = lane < A_HEAD_DIM

    for pair in range(A_HEADS // 2):
        cols = slice(pair * LANES, (pair + 1) * LANES)
        qp = q[:, cols]
        kp = k[:, cols]
        vp = v[:, cols]
        outs = []
        for half in range(2):
            h = 2 * pair + half
            qm = jnp.where(first if half == 0 else jnp.logical_not(first), qp, 0.0).astype(BF16)
            s = _dot_nt(qm, kp) + bias_ref[h]
            s = jnp.where(valid, s, NEG)
            mx = jnp.max(s, axis=-1, keepdims=True)
            p = jnp.exp(s - mx)
            den = jnp.sum(p, axis=-1, keepdims=True)
            outs.append(_dot(p.astype(BF16), vp) / den)
        o_ref[0, :, cols] = jnp.where(first, outs[0], outs[1]).astype(o_ref.dtype)


def _band_attention(z3, bias_tab, qw, kw, seg, col_q):
    b, s, _ = z3.shape
    nt = s // ATT_TQ

    def kv_spec(back, col):
        return pl.BlockSpec((1, ATT_TQ, A_WIDTH), lambda bi, ti: (bi, jnp.maximum(ti - back, 0), col))

    return pl.pallas_call(
        _attn_body,
        grid=(b, nt),
        in_specs=[
            pl.BlockSpec((1, ATT_TQ, A_WIDTH), lambda bi, ti: (bi, ti, col_q)),
            kv_spec(2, col_q + 1), kv_spec(1, col_q + 1), kv_spec(0, col_q + 1),
            kv_spec(2, col_q + 2), kv_spec(1, col_q + 2), kv_spec(0, col_q + 2),
            pl.BlockSpec((A_HEADS, ATT_TQ, ATT_WIN), lambda bi, ti: (0, 0, 0)),
            pl.BlockSpec((1, A_WIDTH), lambda bi, ti: (0, 0)),
            pl.BlockSpec((1, A_WIDTH), lambda bi, ti: (0, 0)),
            pl.BlockSpec((A_WIDTH, A_WIDTH), lambda bi, ti: (0, 0)),
        ],
        out_specs=pl.BlockSpec((1, ATT_TQ, A_WIDTH), lambda bi, ti: (bi, ti, 0)),
        out_shape=jax.ShapeDtypeStruct((b, s, A_WIDTH), BF16),
        compiler_params=pltpu.CompilerParams(
            dimension_semantics=("parallel", "parallel"), vmem_limit_bytes=VMEM_LIMIT),
        name="band_attn",
    )(z3, z3, z3, z3, z3, z3, z3, bias_tab, qw, kw, seg)


def _attn_bias_table(rel_bias):
    rb = rel_bias.astype(F32)
    heads = rb.shape[0]
    lo = A_LEFT_CHUNKS * CHUNK - (ATT_WIN - 1)
    hi = A_LEFT_CHUNKS * CHUNK + ATT_TQ - 1
    strip = jnp.concatenate([
        jnp.broadcast_to(rb[:, :1], (heads, -REL_CLIP - lo)), rb,
        jnp.broadcast_to(rb[:, -1:], (heads, hi - REL_CLIP))], axis=1)
    rev = strip[:, ::-1]
    n = rev.shape[1]
    flat = jnp.broadcast_to(rev[:, None, :], (heads, ATT_TQ, n)).reshape(heads, ATT_TQ * n)
    bias = flat[:, ATT_TQ - 1:ATT_TQ - 1 + ATT_TQ * (n - 1)].reshape(heads, ATT_TQ, n - 1)[:, :, :ATT_WIN]
    qc = jnp.arange(ATT_TQ)[:, None] // CHUNK
    kc = jnp.arange(ATT_WIN)[None, :] // CHUNK
    in_band = (kc >= qc) & (kc <= qc + A_LEFT_CHUNKS)
    return jnp.where(in_band[None], bias, NEG)


GDN_TAIL = SUBLANES


def _gdn_body(qkv_ref, zs_ref, gate_ref, convw_ref, hp_ref, nw_ref, o_ref, xbuf, s_scr):
    c = CHUNK
    hd = B_HEAD_DIM

    @pl.when(pl.program_id(1) == 0)
    def _():
        xbuf[0:GDN_TAIL, :] = jnp.zeros((GDN_TAIL, 3 * B_WIDTH), F32)
        s_scr[...] = jnp.zeros_like(s_scr)

    xbuf[GDN_TAIL:GDN_TAIL + c, :] = qkv_ref[0].astype(F32)
    y = None
    for j in range(CONV_WIDTH):
        tap = convw_ref[j:j + 1, :] * xbuf[pl.ds(GDN_TAIL - (CONV_WIDTH - 1) + j, c), :]
        y = tap if y is None else y + tap
    xbuf[0:GDN_TAIL, :] = xbuf[c:c + GDN_TAIL, :]
    y = y * _sigmoid(y)

    zs = zs_ref[0]
    a_neg = -(jnp.exp(hp_ref[0:1, :]) * hp_ref[2:3, :])
    xs = zs + hp_ref[1:2, :]
    g_all = a_neg * (jnp.maximum(xs, 0.0) + jnp.log(1.0 + jnp.exp(-jnp.abs(xs))))
    beta_all = _sigmoid(zs)

    row = lax.broadcasted_iota(jnp.int32, (c, c), 0)
    col = lax.broadcasted_iota(jnp.int32, (c, c), 1)
    tril = row >= col
    tril_strict = row > col
    eye = (row == col).astype(F32)
    lower = tril.astype(BF16)
    upper = (row <= col).astype(BF16)

    gh, gl = _split_bf16(g_all)
    gc_all = _dot(lower, gh) + _dot(lower, gl)
    gth, gtl = _split_bf16(g_all.T)
    gc_t = _dot(gth, upper) + _dot(gtl, upper)

    heads = range(B_HEADS)
    q, k, v, beta, gc, gc_row, g_last = [], [], [], [], [], [], []
    for h in heads:
        qh = y[:, h * hd:(h + 1) * hd]
        kh = y[:, B_WIDTH + h * hd:B_WIDTH + (h + 1) * hd]
        q.append(qh * lax.rsqrt(jnp.sum(qh * qh, axis=-1, keepdims=True) + EPS) * (hd ** -0.5))
        k.append(kh * lax.rsqrt(jnp.sum(kh * kh, axis=-1, keepdims=True) + EPS))
        v.append(y[:, 2 * B_WIDTH + h * hd:2 * B_WIDTH + (h + 1) * hd])
        beta.append(beta_all[:, Z_SMALL_BETA + h:Z_SMALL_BETA + h + 1])
        gc.append(gc_all[:, Z_SMALL_A + h:Z_SMALL_A + h + 1])
        gc_row.append(gc_t[Z_SMALL_A + h:Z_SMALL_A + h + 1, :])
        g_last.append(gc_row[h][:, c - 1:c])
    decay = [jnp.exp(jnp.where(tril, gc[h] - gc_row[h], NEG)) for h in heads]
    e_gc = [jnp.exp(gc[h]) for h in heads]
    kb = [k[h] * beta[h] for h in heads]
    kf = [k[h].astype(BF16) for h in heads]
    a_pow = [jnp.where(tril_strict, _dot_nt(kb[h].astype(BF16), kf[h]) * decay[h], 0.0) for h in heads]
    attn = [(_dot_nt(q[h].astype(BF16), kf[h]) * decay[h]).astype(BF16) for h in heads]

    t_mat = [eye - a_pow[h] for h in heads]
    for _ in range(5):
        ab = [a_pow[h].astype(BF16) for h in heads]
        a_pow = [_dot(ab[h], ab[h]) for h in heads]
        t_mat = [t_mat[h] + _dot(t_mat[h].astype(BF16), a_pow[h].astype(BF16)) for h in heads]
    tb = [t_mat[h].astype(BF16) for h in heads]
    u = [_dot(tb[h], (v[h] * beta[h]).astype(BF16)) for h in heads]
    w = [_dot(tb[h], (kb[h] * e_gc[h]).astype(BF16)).astype(BF16) for h in heads]

    state = [s_scr[h] for h in heads]
    sb = [state[h].astype(BF16) for h in heads]
    vnb = [(u[h] - _dot(w[h], sb[h])).astype(BF16) for h in heads]
    o = [_dot((q[h] * e_gc[h]).astype(BF16), sb[h]) + _dot(attn[h], vnb[h]) for h in heads]
    k_dec = [(k[h] * jnp.exp(g_last[h] - gc[h])).astype(BF16) for h in heads]
    for h in heads:
        s_scr[h] = state[h] * jnp.exp(g_last[h]) + _dot_tn(k_dec[h], vnb[h])
    for h in heads:
        cols = slice(h * hd, (h + 1) * hd)
        gate = gate_ref[0, :, cols].astype(F32)
        on = o[h] * lax.rsqrt(jnp.mean(o[h] * o[h], axis=-1, keepdims=True) + EPS) * nw_ref[...]
        o_ref[0, :, cols] = (on * (gate * _sigmoid(gate))).astype(o_ref.dtype)


def _gated_deltanet(z3, zs3, conv_w, head_params, norm_w):
    b, s, _ = z3.shape
    n = s // CHUNK
    return pl.pallas_call(
        _gdn_body,
        grid=(b, n),
        in_specs=[
            pl.BlockSpec((1, CHUNK, 3 * B_WIDTH), lambda bi, ni: (bi, ni, Z_QKV_B // (3 * B_WIDTH))),
            pl.BlockSpec((1, CHUNK, LANES), lambda bi, ni: (bi, ni, 0)),
            pl.BlockSpec((1, CHUNK, B_WIDTH), lambda bi, ni: (bi, ni, Z_GATE_B // B_WIDTH)),
            pl.BlockSpec((CONV_WIDTH, 3 * B_WIDTH), lambda bi, ni: (0, 0)),
            pl.BlockSpec((SUBLANES, LANES), lambda bi, ni: (0, 0)),
            pl.BlockSpec((1, B_HEAD_DIM), lambda bi, ni: (0, 0)),
        ],
        out_specs=pl.BlockSpec((1, CHUNK, B_WIDTH), lambda bi, ni: (bi, ni, 0)),
        out_shape=jax.ShapeDtypeStruct((b, s, B_WIDTH), BF16),
        scratch_shapes=[
            pltpu.VMEM((GDN_TAIL + CHUNK, 3 * B_WIDTH), F32),
            pltpu.VMEM((B_HEADS, B_HEAD_DIM, B_HEAD_DIM), F32),
        ],
        compiler_params=pltpu.CompilerParams(
            dimension_semantics=("parallel", "arbitrary"), vmem_limit_bytes=VMEM_LIMIT),
        name="gdn",
    )(z3, zs3, z3, conv_w, head_params, norm_w)


MERGE_TM = 256


def _top_k_rows(s, k, payload=None):
    n = s.shape[0]
    rows = lax.broadcasted_iota(jnp.int32, s.shape, 0)
    vals, picks = [], []
    for _ in range(k):
        m = jnp.max(s, axis=0, keepdims=True)
        i = jnp.min(jnp.where(s == m, rows, n), axis=0, keepdims=True)
        hit = rows == i
        vals.append(m)
        picks.append(i if payload is None else jnp.max(jnp.where(hit, payload, -1), axis=0, keepdims=True))
        s = jnp.where(hit, -jnp.inf, s)
    return jnp.concatenate(vals, axis=0), jnp.concatenate(picks, axis=0)


def _merge_body(oa_ref, ob_ref, mr_ref, x_ref, bm_ref, pa_ref, pb_ref, wo_ref, nw_ref, wq_ref, k1_ref, k2_ref,
                x1_ref, x1p_ref, xnp_ref, idx_ref, gate_ref, gate_tok_ref):
    tm, d = x_ref.shape
    planes = d // LANES

    def store_planes(ref, a):
        for p in range(planes):
            ref[pl.ds(p, tm, stride=planes), :] = a[:, p * LANES:(p + 1) * LANES]

    mr = mr_ref[...].astype(F32) + bm_ref[...]
    mixed = (_sigmoid(mr[:, :d]) * _dot(oa_ref[...], pa_ref[...])
             + _sigmoid(mr[:, d:]) * _dot(ob_ref[...], pb_ref[...]))
    x1 = x_ref[...] + _dot(mixed.astype(BF16), wo_ref[...])
    x1_ref[...] = x1
    store_planes(x1p_ref, x1)
    xn = x1 * lax.rsqrt(jnp.mean(x1 * x1, axis=-1, keepdims=True) + EPS) * nw_ref[...]
    store_planes(xnp_ref, xn)
    q = _dot(xn.astype(BF16), wq_ref[...]).astype(BF16)

    idx_rows, gate_rows = [], []
    for h in range(PEER_HEADS):
        q1 = q[:, (2 * h) * PEER_HALF:(2 * h + 1) * PEER_HALF]
        q2 = q[:, (2 * h + 1) * PEER_HALF:(2 * h + 2) * PEER_HALF]
        s1 = _dot_nt(k1_ref[h], q1)
        s2 = _dot_nt(k2_ref[h], q2)
        v1, i1 = _top_k_rows(s1, PEER_TOPK)
        v2, i2 = _top_k_rows(s2, PEER_TOPK)
        nb = [PEER_TOPK // (a + 1) for a in range(PEER_TOPK)]
        pad = -sum(nb) % SUBLANES
        tm = v1.shape[1]
        cand = jnp.concatenate([v1[a:a + 1] + v2[:nb[a]] for a in range(PEER_TOPK)]
                               + [jnp.full((pad, tm), -jnp.inf, F32)], axis=0)
        cand_idx = jnp.concatenate([i1[a:a + 1] * PEER_NKEYS + i2[:nb[a]] for a in range(PEER_TOPK)]
                                   + [jnp.zeros((pad, tm), jnp.int32)], axis=0)
        top_s, expert = _top_k_rows(cand, PEER_TOPK, payload=cand_idx)
        e = jnp.exp(top_s - top_s[0:1])
        gate_rows.append(e / jnp.sum(e, axis=0, keepdims=True))
        idx_rows.append(expert)
    idx_ref[...] = jnp.concatenate(idx_rows, axis=0).T
    gates = jnp.concatenate(gate_rows, axis=0)
    gate_ref[...] = gates
    gate_tok_ref[...] = gates.T


def _merge_route(o_a, o_b, z2, x2, row0, b_merge, pa, pb, wo, nw, wq, k1, k2):
    t, d = o_a.shape[0], x2.shape[1]
    tm = MERGE_TM
    xblk0 = row0 // tm
    full = lambda a: pl.BlockSpec(a.shape, lambda i: (0,) * a.ndim)
    return pl.pallas_call(
        _merge_body,
        grid=(t // tm,),
        in_specs=[
            pl.BlockSpec((tm, A_WIDTH), lambda i: (i, 0)),
            pl.BlockSpec((tm, B_WIDTH), lambda i: (i, 0)),
            pl.BlockSpec((tm, 2 * d), lambda i: (i, Z_MERGE // (2 * d))),
            pl.BlockSpec((tm, d), lambda i: (i + xblk0, 0)),
            full(b_merge), full(pa), full(pb), full(wo), full(nw), full(wq), full(k1), full(k2),
        ],
        out_specs=[
            pl.BlockSpec((tm, d), lambda i: (i, 0)),
            pl.BlockSpec((tm * d // LANES, LANES), lambda i: (i, 0)),
            pl.BlockSpec((tm * d // LANES, LANES), lambda i: (i, 0)),
            pl.BlockSpec((tm, PEER_SLOTS), lambda i: (i, 0)),
            pl.BlockSpec((PEER_SLOTS, tm), lambda i: (0, i)),
            pl.BlockSpec((tm, PEER_SLOTS), lambda i: (i, 0)),
        ],
        out_shape=[
            jax.ShapeDtypeStruct((t, d), F32),
            jax.ShapeDtypeStruct((t * d // LANES, LANES), F32),
            jax.ShapeDtypeStruct((t * d // LANES, LANES), F32),
            jax.ShapeDtypeStruct((t, PEER_SLOTS), jnp.int32),
            jax.ShapeDtypeStruct((PEER_SLOTS, t), F32),
            jax.ShapeDtypeStruct((t, PEER_SLOTS), F32),
        ],
        compiler_params=pltpu.CompilerParams(
            dimension_semantics=("parallel",), vmem_limit_bytes=VMEM_LIMIT),
        name="merge_route",
    )(o_a, o_b, z2, x2, b_merge, pa, pb, wo, nw, wq, k1, k2)


PEER_TP = 128
PEER_NBUF = 8
PEER_LOOK = PEER_NBUF - 1
PEER_SC_SHARE_NUM, PEER_SC_SHARE_DEN = 18, 32
PEER_TC_FIRST_NUM, PEER_TC_FIRST_DEN = 13, 32
PEER_SC_TAIL_BLOCKS = 6


def _erf(x):
    return lax.erf(x)


PACK_HI = -65536


def _packed_down(bitcast, word):
    return bitcast(word & PACK_HI, F32)


def _packed_up(bitcast, word):
    return bitcast(word << 16, F32)


def _peer_body(idx_ref, idx_next_ref, xn_ref, gate_ref, x1_ref, tab_ref, o_ref, *scratch):
    bufs, sem = scratch[:PEER_NBUF], scratch[PEER_NBUF]
    planes = x1_ref.shape[1] // LANES
    step = pl.program_id(0)

    def issue(ref, tok, slot):
        for j in range(PEER_SLOTS):
            pltpu.make_async_copy(
                tab_ref.at[ref[0, tok, j]], bufs[slot].at[:, j, :], sem.at[slot]).start(priority=j % 2)

    def wait(slot):
        pltpu.make_async_copy(bufs[slot], bufs[slot], sem.at[slot]).wait()

    @pl.when(step == 0)
    def _():
        for s in range(PEER_LOOK):
            issue(idx_ref, s, s)

    lane = lax.broadcasted_iota(jnp.int32, (PEER_SLOTS, PEER_TP), 1)

    def compute(tok, slot):
        buf = bufs[slot]
        acc = None
        for s in range(planes):
            term = _packed_down(lax.bitcast_convert_type, buf[s]) * xn_ref[pl.ds(tok * planes + s, 1), :]
            acc = term if acc is None else acc + term
        act = jnp.sum(acc, axis=-1, keepdims=True)
        gate = jnp.sum(jnp.where(lane == tok, gate_ref[...], 0.0), axis=-1, keepdims=True)
        w = gate * (0.5 * act * (1.0 + _erf(act * (2.0 ** -0.5))))
        out = jnp.concatenate(
            [jnp.sum(_packed_up(lax.bitcast_convert_type, buf[s]) * w, axis=0, keepdims=True)
             for s in range(planes)], axis=1)
        o_ref[pl.ds(tok, 1), :] = x1_ref[pl.ds(tok, 1), :] + out

    def group(g, carry):
        for u in range(PEER_NBUF):
            tok = g * PEER_NBUF + u
            issue(idx_ref, tok + PEER_LOOK, (u + PEER_LOOK) % PEER_NBUF)
            wait(u)
            compute(tok, u)
        return carry

    n_groups = PEER_TP // PEER_NBUF
    lax.fori_loop(0, n_groups - 1, group, 0)

    for u in range(PEER_NBUF):
        tok = (n_groups - 1) * PEER_NBUF + u
        nxt = tok + PEER_LOOK
        if nxt < PEER_TP:
            issue(idx_ref, nxt, (u + PEER_LOOK) % PEER_NBUF)
        else:
            issue(idx_next_ref, nxt - PEER_TP, (u + PEER_LOOK) % PEER_NBUF)
        wait(u)
        compute(tok, u)

    @pl.when(step == pl.num_programs(0) - 1)
    def _():
        for s in range(PEER_LOOK):
            wait(s)


def _peer_alias_body(prev_ref, after_ref, *rest):
    del prev_ref, after_ref
    _peer_body(*rest)


def _peer(idx3, xnp, gate_t, x1, table, blk0, nb, out_blk0, t_out, prev=None, after=None):
    d = x1.shape[1]
    tp = PEER_TP
    planes = d // LANES
    in_specs = [
        pl.BlockSpec((1, tp, PEER_SLOTS), lambda i: (i + blk0, 0, 0), memory_space=pltpu.SMEM),
        pl.BlockSpec((1, tp, PEER_SLOTS), lambda i: (jnp.minimum(i + 1, nb - 1) + blk0, 0, 0),
                     memory_space=pltpu.SMEM),
        pl.BlockSpec((tp * planes, LANES), lambda i: (i + blk0, 0)),
        pl.BlockSpec((PEER_SLOTS, tp), lambda i: (0, i + blk0)),
        pl.BlockSpec((tp, d), lambda i: (i + blk0, 0)),
        pl.BlockSpec(memory_space=pl.ANY),
    ]
    args = (idx3, idx3, xnp, gate_t, x1, table)
    body, aliases = _peer_body, {}
    if prev is not None:
        body, aliases = _peer_alias_body, {0: 0}
        in_specs = [pl.BlockSpec(memory_space=pl.ANY), pl.BlockSpec(memory_space=pl.ANY)] + in_specs
        args = (prev, after) + args
    return pl.pallas_call(
        body,
        grid=(nb,),
        in_specs=in_specs,
        out_specs=pl.BlockSpec((tp, d), lambda i: (i + blk0 + out_blk0, 0)),
        out_shape=jax.ShapeDtypeStruct((t_out, d), F32),
        scratch_shapes=[pltpu.VMEM((planes, PEER_SLOTS, LANES), jnp.int32) for _ in range(PEER_NBUF)]
        + [pltpu.SemaphoreType.DMA((PEER_NBUF,))],
        input_output_aliases=aliases,
        compiler_params=pltpu.CompilerParams(
            dimension_semantics=("arbitrary",), vmem_limit_bytes=VMEM_LIMIT),
        name="peer",
    )(*args)


SC_CORES = 2
SC_SUBCORES = 16
SC_LANES = 16
SC_WORKERS = SC_CORES * SC_SUBCORES
SC_ROWS = 64


SC_VPR = LANES // SC_LANES
SC_VROWS = PEER_SLOTS // SC_VPR


def _sc_vec(j):
    return (j // SC_VPR, pl.ds((j % SC_VPR) * SC_LANES, SC_LANES))


def _sc_mesh():
    return plsc.VectorSubcoreMesh(core_axis_name="c", subcore_axis_name="s")


def _sc_worker():
    return lax.axis_index("s") * SC_CORES + lax.axis_index("c")


def _sc_pipeline(tw, base, nchunk, loads, gather, compute, finish):
    assert tw % 2 == 0 and nchunk % 2 == 0
    last = base + tw - 1

    def token(tok, s):
        nxt = jnp.minimum(tok + 1, last)
        for c in loads(nxt, 1 - s):
            c.start()
        for kc in range(nchunk):
            if kc + 1 < nchunk:
                gather(s, kc + 1).start()
            else:
                for c in loads(nxt, 1 - s):
                    c.wait()
                gather(1 - s, 0).start()
            gather(s, kc).wait()
            compute(s, kc)
        finish(tok, s)

    for c in loads(base, 0):
        c.start()
    for c in loads(base, 0):
        c.wait()
    gather(0, 0).start()

    def pair(i, carry):
        token(base + 2 * i, 0)
        token(base + 2 * i + 1, 1)
        return carry

    lax.fori_loop(0, tw // 2, pair, 0)
    gather(0, 0).wait()


def _sc_down(idx_flat, xn3, table, tok0, n_tok):
    planes = xn3.shape[1]
    half = planes // 2
    tw = n_tok // SC_WORKERS
    nchunk = PEER_SLOTS // SC_ROWS
    per_plane = LANES // SC_LANES

    @functools.partial(
        pl.kernel, mesh=_sc_mesh(),
        out_type=jax.ShapeDtypeStruct((n_tok, SC_VROWS, LANES), F32),
        scratch_types=[
            pltpu.VMEM((2, PEER_SLOTS), jnp.int32),
            pltpu.VMEM((2, planes, LANES), F32),
            pltpu.VMEM((2, SC_ROWS, half, LANES), jnp.int32),
            pltpu.VMEM((SC_VROWS, LANES), F32),
            pltpu.VMEM((half, SC_VROWS, LANES), F32),
            pltpu.SemaphoreType.DMA((2,)),
            pltpu.SemaphoreType.DMA((2,)),
        ],
        compiler_params=pltpu.CompilerParams(needs_layout_passes=False),
        name="sc_down",
    )
    def run(idx_hbm, xn_hbm, tab_hbm, out_hbm, idx_v, x_v, buf, act_v, part_v, sem_g, sem_l):
        def loads(tok, s):
            return [pltpu.make_async_copy(idx_hbm.at[pl.ds(tok * PEER_SLOTS, PEER_SLOTS)], idx_v.at[s], sem_l.at[s]),
                    pltpu.make_async_copy(xn_hbm.at[tok], x_v.at[s], sem_l.at[s])]

        def gather(s, kc):
            return pltpu.make_async_copy(
                tab_hbm.at[idx_v.at[s, pl.ds(kc * SC_ROWS, SC_ROWS)]], buf.at[kc % 2], sem_g.at[kc % 2])

        def compute(s, kc):
            @plsc.parallel_loop(0, half)
            def _(p):
                sls = [pl.ds(q * SC_LANES, SC_LANES) for q in range(per_plane)]
                x_hi = [x_v[s, p, sl] for sl in sls]
                x_lo = [x_v[s, half + p, sl] for sl in sls]

                def load_row(j):
                    return [buf[kc % 2, j, p, sl] for sl in sls]

                words = load_row(0)
                for j in range(SC_ROWS):
                    ahead = load_row(j + 1) if j + 1 < SC_ROWS else None
                    terms = ([_packed_down(plsc.bitcast, w) * x for w, x in zip(words, x_hi)]
                             + [_packed_up(plsc.bitcast, w) * x for w, x in zip(words, x_lo)])
                    while len(terms) > 1:
                        terms = [terms[i] + terms[i + 1] for i in range(0, len(terms), 2)]
                    part_v[(p,) + _sc_vec(kc * SC_ROWS + j)] = terms[0]
                    words = ahead

        def finish(tok, s):
            @plsc.parallel_loop(0, SC_VROWS)
            def _(r):
                for u in range(SC_VPR):
                    sl = pl.ds(u * SC_LANES, SC_LANES)
                    acc = part_v[0, r, sl]
                    for p in range(1, half):
                        acc = acc + part_v[p, r, sl]
                    act_v[r, sl] = acc
            pltpu.sync_copy(act_v, out_hbm.at[tok - tok0])

        _sc_pipeline(tw, tok0 + _sc_worker() * tw, nchunk, loads, gather, compute, finish)

    return run(idx_flat, xn3, table)


def _sc_up(idx_flat, w16, x13, table, tok0, n_tok):
    planes = x13.shape[1]
    half = planes // 2
    tw = n_tok // SC_WORKERS
    nchunk = PEER_SLOTS // SC_ROWS
    per_plane = LANES // SC_LANES

    @functools.partial(
        pl.kernel, mesh=_sc_mesh(),
        out_type=jax.ShapeDtypeStruct((n_tok, planes, LANES), F32),
        scratch_types=[
            pltpu.VMEM((2, PEER_SLOTS), jnp.int32),
            pltpu.VMEM((2, SC_VROWS, LANES), F32),
            pltpu.VMEM((2, planes, LANES), F32),
            pltpu.VMEM((2, SC_ROWS, half, LANES), jnp.int32),
            pltpu.SemaphoreType.DMA((2,)),
            pltpu.SemaphoreType.DMA((2,)),
        ],
        compiler_params=pltpu.CompilerParams(needs_layout_passes=False),
        name="sc_up",
    )
    def run(idx_hbm, w_hbm, x1_hbm, tab_hbm, out_hbm, idx_v, w_v, out_v, buf, sem_g, sem_l):
        def loads(tok, s):
            return [pltpu.make_async_copy(idx_hbm.at[pl.ds(tok * PEER_SLOTS, PEER_SLOTS)], idx_v.at[s], sem_l.at[s]),
                    pltpu.make_async_copy(w_hbm.at[tok - tok0], w_v.at[s], sem_l.at[s]),
                    pltpu.make_async_copy(x1_hbm.at[tok], out_v.at[s], sem_l.at[s])]

        def gather(s, kc):
            return pltpu.make_async_copy(
                tab_hbm.at[idx_v.at[s, pl.ds(kc * SC_ROWS, SC_ROWS)]], buf.at[kc % 2], sem_g.at[kc % 2])

        def compute(s, kc):
            @plsc.parallel_loop(0, half)
            def _(p):
                sls = [pl.ds(q * SC_LANES, SC_LANES) for q in range(per_plane)]
                acc_hi = [out_v[s, p, sl] for sl in sls]
                acc_lo = [out_v[s, half + p, sl] for sl in sls]
                for j in range(SC_ROWS):
                    wj = w_v[(s,) + _sc_vec(kc * SC_ROWS + j)]
                    words = [buf[kc % 2, j, p, sl] for sl in sls]
                    acc_hi = [a + wj * _packed_down(plsc.bitcast, w) for a, w in zip(acc_hi, words)]
                    acc_lo = [a + wj * _packed_up(plsc.bitcast, w) for a, w in zip(acc_lo, words)]
                for a, sl in zip(acc_hi, sls):
                    out_v[s, p, sl] = a
                for a, sl in zip(acc_lo, sls):
                    out_v[s, half + p, sl] = a

        def finish(tok, s):
            pltpu.sync_copy(out_v.at[s], out_hbm.at[tok - tok0])

        _sc_pipeline(tw, tok0 + _sc_worker() * tw, nchunk, loads, gather, compute, finish)

    return run(idx_flat, w16, x13, table)


ACT_TM = 256


def _peer_act_body(after_ref, part_ref, gate_ref, fold_ref, spread_ref, own_ref, w_ref):
    del after_ref
    tm = part_ref.shape[0]
    own = own_ref[...]
    hi, lo = _split_bf16(part_ref[...].reshape(tm * SC_VROWS, LANES))
    sums = (_dot(hi, fold_ref[...]) + _dot(lo, fold_ref[...])).reshape(tm, SC_VROWS, LANES)
    act = jnp.sum(sums * own, axis=1)
    w = gate_ref[...] * (0.5 * act * (1.0 + _erf(act * (2.0 ** -0.5))))
    hi, lo = _split_bf16((w[:, None, :] * own).reshape(tm * SC_VROWS, LANES))
    w_ref[...] = (_dot(hi, spread_ref[...]) + _dot(lo, spread_ref[...])).reshape(tm, SC_VROWS, LANES)


def _peer_act(after, part3, gate_tok, blk0):
    n_tok = part3.shape[0]
    tm = ACT_TM
    lane = jnp.arange(LANES)
    fold = (lane[:, None] // SC_LANES == lane[None, :] % SC_VPR).astype(BF16)
    spread = fold.T
    own = (lane[None, :] // SC_VPR == jnp.arange(SC_VROWS)[:, None]).astype(F32)
    const = lambda a: pl.BlockSpec(a.shape, lambda i: (0,) * a.ndim)
    return pl.pallas_call(
        _peer_act_body,
        grid=(n_tok // tm,),
        in_specs=[
            pl.BlockSpec(memory_space=pl.ANY),
            pl.BlockSpec((tm, SC_VROWS, LANES), lambda i: (i, 0, 0)),
            pl.BlockSpec((tm, PEER_SLOTS), lambda i: (i + blk0, 0)),
            const(fold), const(spread), const(own),
        ],
        out_specs=pl.BlockSpec((tm, SC_VROWS, LANES), lambda i: (i, 0, 0)),
        out_shape=jax.ShapeDtypeStruct(part3.shape, F32),
        compiler_params=pltpu.CompilerParams(
            dimension_semantics=("parallel",), vmem_limit_bytes=VMEM_LIMIT),
        name="peer_act",
    )(after, part3, gate_tok, fold, spread, own)


def _table_pack_body(down_ref, up_ref, tab_ref, dn2_ref, up2_ref):
    te, d = down_ref.shape
    planes = d // LANES
    half = planes // 2

    def bits(ref, p):
        return lax.bitcast_convert_type(ref[:, p * LANES:(p + 1) * LANES].astype(BF16).astype(F32), jnp.int32)

    def pack(hi, lo):
        return hi | lax.shift_right_logical(lo, 16)

    for p in range(planes):
        tab_ref[pl.ds(p, te, stride=planes), :] = pack(bits(down_ref, p), bits(up_ref, p))
    for p in range(half):
        dn2_ref[pl.ds(p, te, stride=half), :] = pack(bits(down_ref, p), bits(down_ref, half + p))
        up2_ref[pl.ds(p, te, stride=half), :] = pack(bits(up_ref, p), bits(up_ref, half + p))


def _table_pack(down, up, te=512):
    n_exp, d = down.shape
    rows = d // LANES
    return pl.pallas_call(
        _table_pack_body,
        grid=(n_exp // te,),
        in_specs=[pl.BlockSpec((te, d), lambda i: (i, 0)), pl.BlockSpec((te, d), lambda i: (i, 0))],
        out_specs=[pl.BlockSpec((te * rows, LANES), lambda i: (i, 0)),
                   pl.BlockSpec((te * rows // 2, LANES), lambda i: (i, 0)),
                   pl.BlockSpec((te * rows // 2, LANES), lambda i: (i, 0))],
        out_shape=[jax.ShapeDtypeStruct((n_exp * rows, LANES), jnp.int32),
                   jax.ShapeDtypeStruct((n_exp * rows // 2, LANES), jnp.int32),
                   jax.ShapeDtypeStruct((n_exp * rows // 2, LANES), jnp.int32)],
        compiler_params=pltpu.CompilerParams(
            dimension_semantics=("parallel",), vmem_limit_bytes=VMEM_LIMIT),
        name="table_pack",
    )(down, up)


def _pad_lanes(a):
    return jnp.pad(a, ((0, 0), (0, LANES - a.shape[1])))


def kernel(x, norm_mix_w, w_in, a_q_norm_w, a_k_norm_w, a_rel_bias, b_conv_w, b_a_log, b_dt_bias, b_norm_w, b_merge, w_proj_a, w_proj_b, w_out, norm_ffn_w, peer_w_query, peer_keys_1, peer_keys_2, peer_down, peer_up):
    b, s, d = x.shape
    t = b * s
    depth = w_in.shape[0]
    for l in range(depth):
        x2 = x.reshape(t, d)

        wi = w_in[l]
        o_qkvb = 3 * A_WIDTH
        o_a = o_qkvb + 3 * B_WIDTH
        o_beta = o_a + B_HEADS
        o_gate = o_beta + B_HEADS
        o_merge = o_gate + B_WIDTH
        w_main = jnp.concatenate(
            [wi[:, o_qkvb:o_a], wi[:, o_gate:o_merge], wi[:, o_merge:], wi[:, :o_qkvb]], axis=1).astype(BF16)
        w_small = _pad_lanes(wi[:, o_a:o_gate]).astype(BF16)
        z_attn = Z_MERGE + 2 * d

        bias_tab = _attn_bias_table(a_rel_bias[l])
        seg = jnp.kron(jnp.eye(A_HEADS, dtype=F32), jnp.full((A_HEAD_DIM, A_HEAD_DIM), 1.0 / A_HEAD_DIM, F32))
        head_params = jnp.zeros((SUBLANES, LANES), F32)
        head_params = head_params.at[0, :B_HEADS].set(b_a_log[l]).at[1, :B_HEADS].set(b_dt_bias[l])
        head_params = head_params.at[2, :B_HEADS].set(1.0)

        def mixer_and_route(b0, nb):
            tg = nb * s
            z, zs = _in_proj(x2, norm_mix_w[l][None, :], w_main, w_small, min(IN_PROJ_TM, tg), IN_PROJ_TN,
                             b0 * s, tg)
            z3 = z.reshape(nb, s, z.shape[1])
            o_att = _band_attention(
                z3, bias_tab, jnp.tile(a_q_norm_w[l], A_HEADS)[None, :], jnp.tile(a_k_norm_w[l], A_HEADS)[None, :],
                seg.astype(BF16), z_attn // A_WIDTH)
            o_gdn = _gated_deltanet(z3, zs.reshape(nb, s, LANES), b_conv_w[l], head_params, b_norm_w[l][None, :])
            return _merge_route(
                o_att.reshape(tg, A_WIDTH), o_gdn.reshape(tg, B_WIDTH), z, x2, b0 * s, b_merge[l][None, :],
                w_proj_a[l].astype(BF16), w_proj_b[l].astype(BF16), w_out[l].astype(BF16),
                norm_ffn_w[l][None, :], peer_w_query[l].astype(BF16), peer_keys_1[l].astype(BF16),
                peer_keys_2[l].astype(BF16))

        planes = d // LANES
        n_exp = peer_down.shape[1]
        table, down2, up2 = _table_pack(peer_down[l], peer_up[l])
        table = table.reshape(n_exp, planes, LANES)
        down2 = down2.reshape(n_exp, planes // 2, LANES)
        up2 = up2.reshape(n_exp, planes // 2, LANES)

        b_sc = b * PEER_SC_SHARE_NUM // PEER_SC_SHARE_DEN
        n_first = b_sc * s
        tail = min(PEER_SC_TAIL_BLOCKS, n_first // PEER_TP // 2)
        n_sc = n_first - tail * PEER_TP
        x1_s, x1p_s, xnp_s, idx_s, gate_t_s, gate_tok_s = mixer_and_route(0, b_sc)
        idx_s_flat = idx_s.reshape(-1)
        part = _sc_down(idx_s_flat, xnp_s.reshape(n_first, planes, LANES), down2, 0, n_sc)

        x1, _, xnp, idx, gate_t, _ = mixer_and_route(b_sc, b - b_sc)
        nblk = (t - n_first) // PEER_TP
        nblk_a = nblk * PEER_TC_FIRST_NUM // PEER_TC_FIRST_DEN
        idx3 = idx.reshape(nblk, PEER_TP, PEER_SLOTS)
        out = _peer(idx3, xnp, gate_t, x1, table, 0, nblk_a, n_first // PEER_TP, t)
        w16 = _peer_act(out, part, gate_tok_s, 0)
        out = _peer(idx3, xnp, gate_t, x1, table, nblk_a, nblk - nblk_a, n_first // PEER_TP, t, prev=out, after=w16)
        out = _peer(idx_s.reshape(n_first // PEER_TP, PEER_TP, PEER_SLOTS), xnp_s, gate_t_s, x1_s, table,
                    n_sc // PEER_TP, tail, 0, t, prev=out, after=w16)
        sc_out = _sc_up(idx_s_flat, w16, x1p_s.reshape(n_first, planes, LANES), up2, 0, n_sc)
        out = lax.dynamic_update_slice(out, sc_out.reshape(n_sc, d), (0, 0))
        x = out.reshape(b, s, d)
    return x
```

```python
import functools

import jax
import jax.numpy as jnp
from jax import lax
from jax.experimental import pallas as pl
from jax.experimental.pallas import tpu as pltpu
from jax.experimental.pallas import tpu_sc as plsc

F32 = jnp.float32
BF16 = jnp.bfloat16

EPS = 1e-6
NEG = -1e30

CHUNK = 64
A_HEADS = 8
A_HEAD_DIM = 64
A_WIDTH = A_HEADS * A_HEAD_DIM
A_LEFT_CHUNKS = 8
REL_CLIP = 128
B_HEADS = 8
B_HEAD_DIM = 128
B_WIDTH = B_HEADS * B_HEAD_DIM
CONV_WIDTH = 4
PEER_HEADS = 8
PEER_HALF = 128
PEER_NKEYS = 128
PEER_TOPK = 16
PEER_SLOTS = PEER_HEADS * PEER_TOPK

LANES = 128
SUBLANES = 8
VMEM_LIMIT = 56 * 1024 * 1024

Z_QKV_B = 0
Z_GATE_B = 3 * B_WIDTH
Z_MERGE = Z_GATE_B + B_WIDTH
Z_SMALL_A = 0
Z_SMALL_BETA = B_HEADS


def _dot(a, b):
    return jnp.dot(a, b, preferred_element_type=F32)


def _dot_nt(a, b):
    return lax.dot_general(a, b, (((1,), (1,)), ((), ())), preferred_element_type=F32)


def _dot_tn(a, b):
    return lax.dot_general(a, b, (((0,), (0,)), ((), ())), preferred_element_type=F32)


def _split_bf16(a):
    hi = a.astype(BF16)
    lo = (a - hi.astype(F32)).astype(BF16)
    return hi, lo


def _sigmoid(x):
    return 1.0 / (1.0 + jnp.exp(-x))


IN_PROJ_TM = 1024
IN_PROJ_TN = 3840


def _inproj_body(x_ref, nw_ref, w_ref, ws_ref, z_ref, zs_ref, h_scr):
    @pl.when(pl.program_id(1) == 0)
    def _():
        x = x_ref[...]
        ms = jnp.mean(x * x, axis=-1, keepdims=True)
        hb = (x * lax.rsqrt(ms + EPS) * nw_ref[...]).astype(BF16)
        h_scr[...] = hb
        zs_ref[...] = _dot(hb, ws_ref[...])

    z_ref[...] = _dot(h_scr[...], w_ref[...]).astype(z_ref.dtype)


def _in_proj(x2, norm_w, w_main, w_small, tm, tn, row0, t):
    d = x2.shape[1]
    n = w_main.shape[1]
    blk0 = row0 // tm
    return pl.pallas_call(
        _inproj_body,
        grid=(t // tm, n // tn),
        in_specs=[
            pl.BlockSpec((tm, d), lambda i, j: (i + blk0, 0)),
            pl.BlockSpec((1, d), lambda i, j: (0, 0)),
            pl.BlockSpec((d, tn), lambda i, j: (0, j)),
            pl.BlockSpec((d, LANES), lambda i, j: (0, 0)),
        ],
        out_specs=[
            pl.BlockSpec((tm, tn), lambda i, j: (i, j)),
            pl.BlockSpec((tm, LANES), lambda i, j: (i, 0)),
        ],
        out_shape=[
            jax.ShapeDtypeStruct((t, n), BF16),
            jax.ShapeDtypeStruct((t, LANES), F32),
        ],
        scratch_shapes=[pltpu.VMEM((tm, d), BF16)],
        compiler_params=pltpu.CompilerParams(
            dimension_semantics=("parallel", "arbitrary"), vmem_limit_bytes=VMEM_LIMIT),
        name="in_proj",
    )(x2, norm_w, w_main, w_small)


ATT_TQ = 256
ATT_WIN = ATT_TQ + A_LEFT_CHUNKS * CHUNK


def _attn_body(q_ref, k0_ref, k1_ref, k2_ref, v0_ref, v1_ref, v2_ref, bias_ref, qw_ref, kw_ref, seg_ref,
               o_ref):
    t = pl.program_id(1)
    seg = seg_ref[...]

    def head_rms(a, w):
        hi, lo = _split_bf16(a * a)
        ms = _dot(hi, seg) + _dot(lo, seg)
        return a * lax.rsqrt(ms + EPS) * w

    q = head_rms(q_ref[0].astype(F32), qw_ref[...]) * (A_HEAD_DIM ** -0.5)
    k = jnp.concatenate([k0_ref[0], k1_ref[0], k2_ref[0]], axis=0).astype(F32)
    k = head_rms(k, kw_ref[...]).astype(BF16)
    v = jnp.concatenate([v0_ref[0], v1_ref[0], v2_ref[0]], axis=0)

    kpos = t * ATT_TQ - A_LEFT_CHUNKS * CHUNK + lax.broadcasted_iota(jnp.int32, (1, ATT_WIN), 1)
    valid = kpos >= 0
    lane = lax.broadcasted_iota(jnp.int32, (1, LANES), 1)
    first = lane < A_HEAD_DIM

    for pair in range(A_HEADS // 2):
        cols = slice(pair * LANES, (pair + 1) * LANES)
        qp = q[:, cols]
        kp = k[:, cols]
        vp = v[:, cols]
        outs = []
        for half in range(2):
            h = 2 * pair + half
            qm = jnp.where(first if half == 0 else jnp.logical_not(first), qp, 0.0).astype(BF16)
            s = _dot_nt(qm, kp) + bias_ref[h]
            s = jnp.where(valid, s, NEG)
            mx = jnp.max(s, axis=-1, keepdims=True)
            p = jnp.exp(s - mx)
            den = jnp.sum(p, axis=-1, keepdims=True)
            outs.append(_dot(p.astype(BF16), vp) / den)
        o_ref[0, :, cols] = jnp.where(first, outs[0], outs[1]).astype(o_ref.dtype)


def _band_attention(z3, bias_tab, qw, kw, seg, col_q):
    b, s, _ = z3.shape
    nt = s // ATT_TQ

    def kv_spec(back, col):
        return pl.BlockSpec((1, ATT_TQ, A_WIDTH), lambda bi, ti: (bi, jnp.maximum(ti - back, 0), col))

    return pl.pallas_call(
        _attn_body,
        grid=(b, nt),
        in_specs=[
            pl.BlockSpec((1, ATT_TQ, A_WIDTH), lambda bi, ti: (bi, ti, col_q)),
            kv_spec(2, col_q + 1), kv_spec(1, col_q + 1), kv_spec(0, col_q + 1),
            kv_spec(2, col_q + 2), kv_spec(1, col_q + 2), kv_spec(0, col_q + 2),
            pl.BlockSpec((A_HEADS, ATT_TQ, ATT_WIN), lambda bi, ti: (0, 0, 0)),
            pl.BlockSpec((1, A_WIDTH), lambda bi, ti: (0, 0)),
            pl.BlockSpec((1, A_WIDTH), lambda bi, ti: (0, 0)),
            pl.BlockSpec((A_WIDTH, A_WIDTH), lambda bi, ti: (0, 0)),
        ],
        out_specs=pl.BlockSpec((1, ATT_TQ, A_WIDTH), lambda bi, ti: (bi, ti, 0)),
        out_shape=jax.ShapeDtypeStruct((b, s, A_WIDTH), BF16),
        compiler_params=pltpu.CompilerParams(
            dimension_semantics=("parallel", "parallel"), vmem_limit_bytes=VMEM_LIMIT),
        name="band_attn",
    )(z3, z3, z3, z3, z3, z3, z3, bias_tab, qw, kw, seg)


def _attn_bias_table(rel_bias):
    rb = rel_bias.astype(F32)
    heads = rb.shape[0]
    lo = A_LEFT_CHUNKS * CHUNK - (ATT_WIN - 1)
    hi = A_LEFT_CHUNKS * CHUNK + ATT_TQ - 1
    strip = jnp.concatenate([
        jnp.broadcast_to(rb[:, :1], (heads, -REL_CLIP - lo)), rb,
        jnp.broadcast_to(rb[:, -1:], (heads, hi - REL_CLIP))], axis=1)
    rev = strip[:, ::-1]
    n = rev.shape[1]
    flat = jnp.broadcast_to(rev[:, None, :], (heads, ATT_TQ, n)).reshape(heads, ATT_TQ * n)
    bias = flat[:, ATT_TQ - 1:ATT_TQ - 1 + ATT_TQ * (n - 1)].reshape(heads, ATT_TQ, n - 1)[:, :, :ATT_WIN]
    qc = jnp.arange(ATT_TQ)[:, None] // CHUNK
    kc = jnp.arange(ATT_WIN)[None, :] // CHUNK
    in_band = (kc >= qc) & (kc <= qc + A_LEFT_CHUNKS)
    return jnp.where(in_band[None], bias, NEG)


GDN_TAIL = SUBLANES


def _gdn_body(qkv_ref, zs_ref, gate_ref, convw_ref, hp_ref, nw_ref, o_ref, xbuf, s_scr):
    c = CHUNK
    hd = B_HEAD_DIM

    @pl.when(pl.program_id(1) == 0)
    def _():
        xbuf[0:GDN_TAIL, :] = jnp.zeros((GDN_TAIL, 3 * B_WIDTH), F32)
        s_scr[...] = jnp.zeros_like(s_scr)

    xbuf[GDN_TAIL:GDN_TAIL + c, :] = qkv_ref[0].astype(F32)
    y = None
    for j in range(CONV_WIDTH):
        tap = convw_ref[j:j + 1, :] * xbuf[pl.ds(GDN_TAIL - (CONV_WIDTH - 1) + j, c), :]
        y = tap if y is None else y + tap
    xbuf[0:GDN_TAIL, :] = xbuf[c:c + GDN_TAIL, :]
    y = y * _sigmoid(y)

    zs = zs_ref[0]
    a_neg = -(jnp.exp(hp_ref[0:1, :]) * hp_ref[2:3, :])
    xs = zs + hp_ref[1:2, :]
    g_all = a_neg * (jnp.maximum(xs, 0.0) + jnp.log(1.0 + jnp.exp(-jnp.abs(xs))))
    beta_all = _sigmoid(zs)

    row = lax.broadcasted_iota(jnp.int32, (c, c), 0)
    col = lax.broadcasted_iota(jnp.int32, (c, c), 1)
    tril = row >= col
    tril_strict = row > col
    eye = (row == col).astype(F32)
    lower = tril.astype(BF16)
    upper = (row <= col).astype(BF16)

    gh, gl = _split_bf16(g_all)
    gc_all = _dot(lower, gh) + _dot(lower, gl)
    gth, gtl = _split_bf16(g_all.T)
    gc_t = _dot(gth, upper) + _dot(gtl, upper)

    heads = range(B_HEADS)
    q, k, v, beta, gc, gc_row, g_last = [], [], [], [], [], [], []
    for h in heads:
        qh = y[:, h * hd:(h + 1) * hd]
        kh = y[:, B_WIDTH + h * hd:B_WIDTH + (h + 1) * hd]
        q.append(qh * lax.rsqrt(jnp.sum(qh * qh, axis=-1, keepdims=True) + EPS) * (hd ** -0.5))
        k.append(kh * lax.rsqrt(jnp.sum(kh * kh, axis=-1, keepdims=True) + EPS))
        v.append(y[:, 2 * B_WIDTH + h * hd:2 * B_WIDTH + (h + 1) * hd])
        beta.append(beta_all[:, Z_SMALL_BETA + h:Z_SMALL_BETA + h + 1])
        gc.append(gc_all[:, Z_SMALL_A + h:Z_SMALL_A + h + 1])
        gc_row.append(gc_t[Z_SMALL_A + h:Z_SMALL_A + h + 1, :])
        g_last.append(gc_row[h][:, c - 1:c])
    decay = [jnp.exp(jnp.where(tril, gc[h] - gc_row[h], NEG)) for h in heads]
    e_gc = [jnp.exp(gc[h]) for h in heads]
    kb = [k[h] * beta[h] for h in heads]
    kf = [k[h].astype(BF16) for h in heads]
    a_pow = [jnp.where(tril_strict, _dot_nt(kb[h].astype(BF16), kf[h]) * decay[h], 0.0) for h in heads]
    attn = [(_dot_nt(q[h].astype(BF16), kf[h]) * decay[h]).astype(BF16) for h in heads]

    t_mat = [eye - a_pow[h] for h in heads]
    for _ in range(5):
        ab = [a_pow[h].astype(BF16) for h in heads]
        a_pow = [_dot(ab[h], ab[h]) for h in heads]
        t_mat = [t_mat[h] + _dot(t_mat[h].astype(BF16), a_pow[h].astype(BF16)) for h in heads]
    tb = [t_mat[h].astype(BF16) for h in heads]
    u = [_dot(tb[h], (v[h] * beta[h]).astype(BF16)) for h in heads]
    w = [_dot(tb[h], (kb[h] * e_gc[h]).astype(BF16)).astype(BF16) for h in heads]

    state = [s_scr[h] for h in heads]
    sb = [state[h].astype(BF16) for h in heads]
    vnb = [(u[h] - _dot(w[h], sb[h])).astype(BF16) for h in heads]
    o = [_dot((q[h] * e_gc[h]).astype(BF16), sb[h]) + _dot(attn[h], vnb[h]) for h in heads]
    k_dec = [(k[h] * jnp.exp(g_last[h] - gc[h])).astype(BF16) for h in heads]
    for h in heads:
        s_scr[h] = state[h] * jnp.exp(g_last[h]) + _dot_tn(k_dec[h], vnb[h])
    for h in heads:
        cols = slice(h * hd, (h + 1) * hd)
        gate = gate_ref[0, :, cols].astype(F32)
        on = o[h] * lax.rsqrt(jnp.mean(o[h] * o[h], axis=-1, keepdims=True) + EPS) * nw_ref[...]
        o_ref[0, :, cols] = (on * (gate * _sigmoid(gate))).astype(o_ref.dtype)


def _gated_deltanet(z3, zs3, conv_w, head_params, norm_w):
    b, s, _ = z3.shape
    n = s // CHUNK
    return pl.pallas_call(
        _gdn_body,
        grid=(b, n),
        in_specs=[
            pl.BlockSpec((1, CHUNK, 3 * B_WIDTH), lambda bi, ni: (bi, ni, Z_QKV_B // (3 * B_WIDTH))),
            pl.BlockSpec((1, CHUNK, LANES), lambda bi, ni: (bi, ni, 0)),
            pl.BlockSpec((1, CHUNK, B_WIDTH), lambda bi, ni: (bi, ni, Z_GATE_B // B_WIDTH)),
            pl.BlockSpec((CONV_WIDTH, 3 * B_WIDTH), lambda bi, ni: (0, 0)),
            pl.BlockSpec((SUBLANES, LANES), lambda bi, ni: (0, 0)),
            pl.BlockSpec((1, B_HEAD_DIM), lambda bi, ni: (0, 0)),
        ],
        out_specs=pl.BlockSpec((1, CHUNK, B_WIDTH), lambda bi, ni: (bi, ni, 0)),
        out_shape=jax.ShapeDtypeStruct((b, s, B_WIDTH), BF16),
        scratch_shapes=[
            pltpu.VMEM((GDN_TAIL + CHUNK, 3 * B_WIDTH), F32),
            pltpu.VMEM((B_HEADS, B_HEAD_DIM, B_HEAD_DIM), F32),
        ],
        compiler_params=pltpu.CompilerParams(
            dimension_semantics=("parallel", "arbitrary"), vmem_limit_bytes=VMEM_LIMIT),
        name="gdn",
    )(z3, zs3, z3, conv_w, head_params, norm_w)


MERGE_TM = 256


def _top_k_rows(s, k, payload=None):
    n = s.shape[0]
    rows = lax.broadcasted_iota(jnp.int32, s.shape, 0)
    vals, picks = [], []
    for _ in range(k):
        m = jnp.max(s, axis=0, keepdims=True)
        i = jnp.min(jnp.where(s == m, rows, n), axis=0, keepdims=True)
        hit = rows == i
        vals.append(m)
        picks.append(i if payload is None else jnp.max(jnp.where(hit, payload, -1), axis=0, keepdims=True))
        s = jnp.where(hit, -jnp.inf, s)
    return jnp.concatenate(vals, axis=0), jnp.concatenate(picks, axis=0)


def _merge_body(oa_ref, ob_ref, mr_ref, x_ref, bm_ref, pa_ref, pb_ref, wo_ref, nw_ref, wq_ref, k1_ref, k2_ref,
                x1_ref, x1p_ref, xnp_ref, idx_ref, gate_ref, gate_tok_ref):
    tm, d = x_ref.shape
    planes = d // LANES

    def store_planes(ref, a):
        for p in range(planes):
            ref[pl.ds(p, tm, stride=planes), :] = a[:, p * LANES:(p + 1) * LANES]

    mr = mr_ref[...].astype(F32) + bm_ref[...]
    mixed = (_sigmoid(mr[:, :d]) * _dot(oa_ref[...], pa_ref[...])
             + _sigmoid(mr[:, d:]) * _dot(ob_ref[...], pb_ref[...]))
    x1 = x_ref[...] + _dot(mixed.astype(BF16), wo_ref[...])
    x1_ref[...] = x1
    store_planes(x1p_ref, x1)
    xn = x1 * lax.rsqrt(jnp.mean(x1 * x1, axis=-1, keepdims=True) + EPS) * nw_ref[...]
    store_planes(xnp_ref, xn)
    q = _dot(xn.astype(BF16), wq_ref[...]).astype(BF16)

    idx_rows, gate_rows = [], []
    for h in range(PEER_HEADS):
        q1 = q[:, (2 * h) * PEER_HALF:(2 * h + 1) * PEER_HALF]
        q2 = q[:, (2 * h + 1) * PEER_HALF:(2 * h + 2) * PEER_HALF]
        s1 = _dot_nt(k1_ref[h], q1)
        s2 = _dot_nt(k2_ref[h], q2)
        v1, i1 = _top_k_rows(s1, PEER_TOPK)
        v2, i2 = _top_k_rows(s2, PEER_TOPK)
        nb = [PEER_TOPK // (a + 1) for a in range(PEER_TOPK)]
        pad = -sum(nb) % SUBLANES
        tm = v1.shape[1]
        cand = jnp.concatenate([v1[a:a + 1] + v2[:nb[a]] for a in range(PEER_TOPK)]
                               + [jnp.full((pad, tm), -jnp.inf, F32)], axis=0)
        cand_idx = jnp.concatenate([i1[a:a + 1] * PEER_NKEYS + i2[:nb[a]] for a in range(PEER_TOPK)]
                                   + [jnp.zeros((pad, tm), jnp.int32)], axis=0)
        top_s, expert = _top_k_rows(cand, PEER_TOPK, payload=cand_idx)
        e = jnp.exp(top_s - top_s[0:1])
        gate_rows.append(e / jnp.sum(e, axis=0, keepdims=True))
        idx_rows.append(expert)
    idx_ref[...] = jnp.concatenate(idx_rows, axis=0).T
    gates = jnp.concatenate(gate_rows, axis=0)
    gate_ref[...] = gates
    gate_tok_ref[...] = gates.T


def _merge_route(o_a, o_b, z2, x2, row0, b_merge, pa, pb, wo, nw, wq, k1, k2):
    t, d = o_a.shape[0], x2.shape[1]
    tm = MERGE_TM
    xblk0 = row0 // tm
    full = lambda a: pl.BlockSpec(a.shape, lambda i: (0,) * a.ndim)
    return pl.pallas_call(
        _merge_body,
        grid=(t // tm,),
        in_specs=[
            pl.BlockSpec((tm, A_WIDTH), lambda i: (i, 0)),
            pl.BlockSpec((tm, B_WIDTH), lambda i: (i, 0)),
            pl.BlockSpec((tm, 2 * d), lambda i: (i, Z_MERGE // (2 * d))),
            pl.BlockSpec((tm, d), lambda i: (i + xblk0, 0)),
            full(b_merge), full(pa), full(pb), full(wo), full(nw), full(wq), full(k1), full(k2),
        ],
        out_specs=[
            pl.BlockSpec((tm, d), lambda i: (i, 0)),
            pl.BlockSpec((tm * d // LANES, LANES), lambda i: (i, 0)),
            pl.BlockSpec((tm * d // LANES, LANES), lambda i: (i, 0)),
            pl.BlockSpec((tm, PEER_SLOTS), lambda i: (i, 0)),
            pl.BlockSpec((PEER_SLOTS, tm), lambda i: (0, i)),
            pl.BlockSpec((tm, PEER_SLOTS), lambda i: (i, 0)),
        ],
        out_shape=[
            jax.ShapeDtypeStruct((t, d), F32),
            jax.ShapeDtypeStruct((t * d // LANES, LANES), F32),
            jax.ShapeDtypeStruct((t * d // LANES, LANES), F32),
            jax.ShapeDtypeStruct((t, PEER_SLOTS), jnp.int32),
            jax.ShapeDtypeStruct((PEER_SLOTS, t), F32),
            jax.ShapeDtypeStruct((t, PEER_SLOTS), F32),
        ],
        compiler_params=pltpu.CompilerParams(
            dimension_semantics=("parallel",), vmem_limit_bytes=VMEM_LIMIT),
        name="merge_route",
    )(o_a, o_b, z2, x2, b_merge, pa, pb, wo, nw, wq, k1, k2)


PEER_TP = 128
PEER_NBUF = 8
PEER_LOOK = PEER_NBUF - 1
PEER_SC_SHARE_NUM, PEER_SC_SHARE_DEN = 18, 32
PEER_TC_FIRST_NUM, PEER_TC_FIRST_DEN = 13, 32
PEER_SC_TAIL_BLOCKS = 14


def _erf(x):
    return lax.erf(x)


PACK_HI = -65536


def _packed_down(bitcast, word):
    return bitcast(word & PACK_HI, F32)


def _packed_up(bitcast, word):
    return bitcast(word << 16, F32)


def _peer_body(idx_ref, idx_next_ref, xn_ref, gate_ref, x1_ref, tab_ref, o_ref, *scratch):
    bufs, sem = scratch[:PEER_NBUF], scratch[PEER_NBUF]
    planes = x1_ref.shape[1] // LANES
    step = pl.program_id(0)

    def issue(ref, tok, slot):
        for j in range(PEER_SLOTS):
            pltpu.make_async_copy(
                tab_ref.at[ref[0, tok, j]], bufs[slot].at[:, j, :], sem.at[slot]).start(priority=j % 2)

    def wait(slot):
        pltpu.make_async_copy(bufs[slot], bufs[slot], sem.at[slot]).wait()

    @pl.when(step == 0)
    def _():
        for s in range(PEER_LOOK):
            issue(idx_ref, s, s)

    lane = lax.broadcasted_iota(jnp.int32, (PEER_SLOTS, PEER_TP), 1)

    def compute(tok, slot):
        buf = bufs[slot]
        acc = None
        for s in range(planes):
            term = _packed_down(lax.bitcast_convert_type, buf[s]) * xn_ref[pl.ds(tok * planes + s, 1), :]
            acc = term if acc is None else acc + term
        act = jnp.sum(acc, axis=-1, keepdims=True)
        gate = jnp.sum(jnp.where(lane == tok, gate_ref[...], 0.0), axis=-1, keepdims=True)
        w = gate * (0.5 * act * (1.0 + _erf(act * (2.0 ** -0.5))))
        out = jnp.concatenate(
            [jnp.sum(_packed_up(lax.bitcast_convert_type, buf[s]) * w, axis=0, keepdims=True)
             for s in range(planes)], axis=1)
        o_ref[pl.ds(tok, 1), :] = x1_ref[pl.ds(tok, 1), :] + out

    def group(g, carry):
        for u in range(PEER_NBUF):
            tok = g * PEER_NBUF + u
            wait(u)
            compute(tok, u)
            issue(idx_ref, tok + PEER_LOOK, (u + PEER_LOOK) % PEER_NBUF)
        return carry

    n_groups = PEER_TP // PEER_NBUF
    lax.fori_loop(0, n_groups - 1, group, 0)

    for u in range(PEER_NBUF):
        tok = (n_groups - 1) * PEER_NBUF + u
        nxt = tok + PEER_LOOK
        wait(u)
        compute(tok, u)
        if nxt < PEER_TP:
            issue(idx_ref, nxt, (u + PEER_LOOK) % PEER_NBUF)
        else:
            issue(idx_next_ref, nxt - PEER_TP, (u + PEER_LOOK) % PEER_NBUF)

    @pl.when(step == pl.num_programs(0) - 1)
    def _():
        for s in range(PEER_LOOK):
            wait(s)


def _peer_alias_body(prev_ref, after_ref, *rest):
    del prev_ref, after_ref
    _peer_body(*rest)


def _peer(idx3, xnp, gate_t, x1, table, blk0, nb, out_blk0, t_out, prev=None, after=None):
    d = x1.shape[1]
    tp = PEER_TP
    planes = d // LANES
    in_specs = [
        pl.BlockSpec((1, tp, PEER_SLOTS), lambda i: (i + blk0, 0, 0), memory_space=pltpu.SMEM),
        pl.BlockSpec((1, tp, PEER_SLOTS), lambda i: (jnp.minimum(i + 1, nb - 1) + blk0, 0, 0),
                     memory_space=pltpu.SMEM),
        pl.BlockSpec((tp * planes, LANES), lambda i: (i + blk0, 0)),
        pl.BlockSpec((PEER_SLOTS, tp), lambda i: (0, i + blk0)),
        pl.BlockSpec((tp, d), lambda i: (i + blk0, 0)),
        pl.BlockSpec(memory_space=pl.ANY),
    ]
    args = (idx3, idx3, xnp, gate_t, x1, table)
    body, aliases = _peer_body, {}
    if prev is not None:
        body, aliases = _peer_alias_body, {0: 0}
        in_specs = [pl.BlockSpec(memory_space=pl.ANY), pl.BlockSpec(memory_space=pl.ANY)] + in_specs
        args = (prev, after) + args
    return pl.pallas_call(
        body,
        grid=(nb,),
        in_specs=in_specs,
        out_specs=pl.BlockSpec((tp, d), lambda i: (i + blk0 + out_blk0, 0)),
        out_shape=jax.ShapeDtypeStruct((t_out, d), F32),
        scratch_shapes=[pltpu.VMEM((planes, PEER_SLOTS, LANES), jnp.int32) for _ in range(PEER_NBUF)]
        + [pltpu.SemaphoreType.DMA((PEER_NBUF,))],
        input_output_aliases=aliases,
        compiler_params=pltpu.CompilerParams(
            dimension_semantics=("arbitrary",), vmem_limit_bytes=VMEM_LIMIT),
        name="peer",
    )(*args)


SC_CORES = 2
SC_SUBCORES = 16
SC_LANES = 16
SC_WORKERS = SC_CORES * SC_SUBCORES
SC_ROWS = 64


SC_VPR = LANES // SC_LANES
SC_VROWS = PEER_SLOTS // SC_VPR


def _sc_vec(j):
    return (j // SC_VPR, pl.ds((j % SC_VPR) * SC_LANES, SC_LANES))


def _sc_mesh():
    return plsc.VectorSubcoreMesh(core_axis_name="c", subcore_axis_name="s")


def _sc_worker():
    return lax.axis_index("s") * SC_CORES + lax.axis_index("c")


def _sc_pipeline(tw, base, nchunk, loads, gather, compute, finish):
    assert tw % 2 == 0 and nchunk % 2 == 0
    last = base + tw - 1

    def token(tok, s):
        nxt = jnp.minimum(tok + 1, last)
        for c in loads(nxt, 1 - s):
            c.start()
        for kc in range(nchunk):
            if kc + 1 < nchunk:
                gather(s, kc + 1).start()
            else:
                for c in loads(nxt, 1 - s):
                    c.wait()
                gather(1 - s, 0).start()
            gather(s, kc).wait()
            compute(s, kc)
        finish(tok, s)

    for c in loads(base, 0):
        c.start()
    for c in loads(base, 0):
        c.wait()
    gather(0, 0).start()

    def pair(i, carry):
        token(base + 2 * i, 0)
        token(base + 2 * i + 1, 1)
        return carry

    lax.fori_loop(0, tw // 2, pair, 0)
    gather(0, 0).wait()


def _sc_down(idx_flat, xn3, table, tok0, n_tok):
    planes = xn3.shape[1]
    half = planes // 2
    tw = n_tok // SC_WORKERS
    nchunk = PEER_SLOTS // SC_ROWS
    per_plane = LANES // SC_LANES

    @functools.partial(
        pl.kernel, mesh=_sc_mesh(),
        out_type=jax.ShapeDtypeStruct((n_tok, SC_VROWS, LANES), F32),
        scratch_types=[
            pltpu.VMEM((2, PEER_SLOTS), jnp.int32),
            pltpu.VMEM((2, planes, LANES), F32),
            pltpu.VMEM((2, SC_ROWS, half, LANES), jnp.int32),
            pltpu.VMEM((SC_VROWS, LANES), F32),
            pltpu.VMEM((half, SC_VROWS, LANES), F32),
            pltpu.SemaphoreType.DMA((2,)),
            pltpu.SemaphoreType.DMA((2,)),
        ],
        compiler_params=pltpu.CompilerParams(needs_layout_passes=False),
        name="sc_down",
    )
    def run(idx_hbm, xn_hbm, tab_hbm, out_hbm, idx_v, x_v, buf, act_v, part_v, sem_g, sem_l):
        def loads(tok, s):
            return [pltpu.make_async_copy(idx_hbm.at[pl.ds(tok * PEER_SLOTS, PEER_SLOTS)], idx_v.at[s], sem_l.at[s]),
                    pltpu.make_async_copy(xn_hbm.at[tok], x_v.at[s], sem_l.at[s])]

        def gather(s, kc):
            return pltpu.make_async_copy(
                tab_hbm.at[idx_v.at[s, pl.ds(kc * SC_ROWS, SC_ROWS)]], buf.at[kc % 2], sem_g.at[kc % 2])

        def compute(s, kc):
            @plsc.parallel_loop(0, half)
            def _(p):
                sls = [pl.ds(q * SC_LANES, SC_LANES) for q in range(per_plane)]
                x_hi = [x_v[s, p, sl] for sl in sls]
                x_lo = [x_v[s, half + p, sl] for sl in sls]

                def load_row(j):
                    return [buf[kc % 2, j, p, sl] for sl in sls]

                words = load_row(0)
                for j in range(SC_ROWS):
                    ahead = load_row(j + 1) if j + 1 < SC_ROWS else None
                    terms = ([_packed_down(plsc.bitcast, w) * x for w, x in zip(words, x_hi)]
                             + [_packed_up(plsc.bitcast, w) * x for w, x in zip(words, x_lo)])
                    while len(terms) > 1:
                        terms = [terms[i] + terms[i + 1] for i in range(0, len(terms), 2)]
                    part_v[(p,) + _sc_vec(kc * SC_ROWS + j)] = terms[0]
                    words = ahead

        def finish(tok, s):
            @plsc.parallel_loop(0, SC_VROWS)
            def _(r):
                for u in range(SC_VPR):
                    sl = pl.ds(u * SC_LANES, SC_LANES)
                    acc = part_v[0, r, sl]
                    for p in range(1, half):
                        acc = acc + part_v[p, r, sl]
                    act_v[r, sl] = acc
            pltpu.sync_copy(act_v, out_hbm.at[tok - tok0])

        _sc_pipeline(tw, tok0 + _sc_worker() * tw, nchunk, loads, gather, compute, finish)

    return run(idx_flat, xn3, table)


def _sc_up(idx_flat, w16, x13, table, tok0, n_tok):
    planes = x13.shape[1]
    half = planes // 2
    tw = n_tok // SC_WORKERS
    nchunk = PEER_SLOTS // SC_ROWS
    per_plane = LANES // SC_LANES

    @functools.partial(
        pl.kernel, mesh=_sc_mesh(),
        out_type=jax.ShapeDtypeStruct((n_tok, planes, LANES), F32),
        scratch_types=[
            pltpu.VMEM((2, PEER_SLOTS), jnp.int32),
            pltpu.VMEM((2, SC_VROWS, LANES), F32),
            pltpu.VMEM((2, planes, LANES), F32),
            pltpu.VMEM((2, SC_ROWS, half, LANES), jnp.int32),
            pltpu.SemaphoreType.DMA((2,)),
            pltpu.SemaphoreType.DMA((2,)),
        ],
        compiler_params=pltpu.CompilerParams(needs_layout_passes=False),
        name="sc_up",
    )
    def run(idx_hbm, w_hbm, x1_hbm, tab_hbm, out_hbm, idx_v, w_v, out_v, buf, sem_g, sem_l):
        def loads(tok, s):
            return [pltpu.make_async_copy(idx_hbm.at[pl.ds(tok * PEER_SLOTS, PEER_SLOTS)], idx_v.at[s], sem_l.at[s]),
                    pltpu.make_async_copy(w_hbm.at[tok - tok0], w_v.at[s], sem_l.at[s]),
                    pltpu.make_async_copy(x1_hbm.at[tok], out_v.at[s], sem_l.at[s])]

        def gather(s, kc):
            return pltpu.make_async_copy(
                tab_hbm.at[idx_v.at[s, pl.ds(kc * SC_ROWS, SC_ROWS)]], buf.at[kc % 2], sem_g.at[kc % 2])

        def compute(s, kc):
            @plsc.parallel_loop(0, half)
            def _(p):
                sls = [pl.ds(q * SC_LANES, SC_LANES) for q in range(per_plane)]
                acc_hi = [out_v[s, p, sl] for sl in sls]
                acc_lo = [out_v[s, half + p, sl] for sl in sls]
                for j in range(SC_ROWS):
                    wj = w_v[(s,) + _sc_vec(kc * SC_ROWS + j)]
                    words = [buf[kc % 2, j, p, sl] for sl in sls]
                    acc_hi = [a + wj * _packed_down(plsc.bitcast, w) for a, w in zip(acc_hi, words)]
                    acc_lo = [a + wj * _packed_up(plsc.bitcast, w) for a, w in zip(acc_lo, words)]
                for a, sl in zip(acc_hi, sls):
                    out_v[s, p, sl] = a
                for a, sl in zip(acc_lo, sls):
                    out_v[s, half + p, sl] = a

        def finish(tok, s):
            pltpu.sync_copy(out_v.at[s], out_hbm.at[tok - tok0])

        _sc_pipeline(tw, tok0 + _sc_worker() * tw, nchunk, loads, gather, compute, finish)

    return run(idx_flat, w16, x13, table)


ACT_TM = 256


def _peer_act_body(after_ref, part_ref, gate_ref, fold_ref, spread_ref, own_ref, w_ref):
    del after_ref
    tm = part_ref.shape[0]
    own = own_ref[...]
    hi, lo = _split_bf16(part_ref[...].reshape(tm * SC_VROWS, LANES))
    sums = (_dot(hi, fold_ref[...]) + _dot(lo, fold_ref[...])).reshape(tm, SC_VROWS, LANES)
    act = jnp.sum(sums * own, axis=1)
    w = gate_ref[...] * (0.5 * act * (1.0 + _erf(act * (2.0 ** -0.5))))
    hi, lo = _split_bf16((w[:, None, :] * own).reshape(tm * SC_VROWS, LANES))
    w_ref[...] = (_dot(hi, spread_ref[...]) + _dot(lo, spread_ref[...])).reshape(tm, SC_VROWS, LANES)


def _peer_act(after, part3, gate_tok, blk0):
    n_tok = part3.shape[0]
    tm = ACT_TM
    lane = jnp.arange(LANES)
    fold = (lane[:, None] // SC_LANES == lane[None, :] % SC_VPR).astype(BF16)
    spread = fold.T
    own = (lane[None, :] // SC_VPR == jnp.arange(SC_VROWS)[:, None]).astype(F32)
    const = lambda a: pl.BlockSpec(a.shape, lambda i: (0,) * a.ndim)
    return pl.pallas_call(
        _peer_act_body,
        grid=(n_tok // tm,),
        in_specs=[
            pl.BlockSpec(memory_space=pl.ANY),
            pl.BlockSpec((tm, SC_VROWS, LANES), lambda i: (i, 0, 0)),
            pl.BlockSpec((tm, PEER_SLOTS), lambda i: (i + blk0, 0)),
            const(fold), const(spread), const(own),
        ],
        out_specs=pl.BlockSpec((tm, SC_VROWS, LANES), lambda i: (i, 0, 0)),
        out_shape=jax.ShapeDtypeStruct(part3.shape, F32),
        compiler_params=pltpu.CompilerParams(
            dimension_semantics=("parallel",), vmem_limit_bytes=VMEM_LIMIT),
        name="peer_act",
    )(after, part3, gate_tok, fold, spread, own)


def _table_pack_body(down_ref, up_ref, tab_ref, dn2_ref, up2_ref):
    te, d = down_ref.shape
    planes = d // LANES
    half = planes // 2

    def bits(ref, p):
        return lax.bitcast_convert_type(ref[:, p * LANES:(p + 1) * LANES].astype(BF16).astype(F32), jnp.int32)

    def pack(hi, lo):
        return hi | lax.shift_right_logical(lo, 16)

    for p in range(planes):
        tab_ref[pl.ds(p, te, stride=planes), :] = pack(bits(down_ref, p), bits(up_ref, p))
    for p in range(half):
        dn2_ref[pl.ds(p, te, stride=half), :] = pack(bits(down_ref, p), bits(down_ref, half + p))
        up2_ref[pl.ds(p, te, stride=half), :] = pack(bits(up_ref, p), bits(up_ref, half + p))


def _table_pack(down, up, te=512):
    n_exp, d = down.shape
    rows = d // LANES
    return pl.pallas_call(
        _table_pack_body,
        grid=(n_exp // te,),
        in_specs=[pl.BlockSpec((te, d), lambda i: (i, 0)), pl.BlockSpec((te, d), lambda i: (i, 0))],
        out_specs=[pl.BlockSpec((te * rows, LANES), lambda i: (i, 0)),
                   pl.BlockSpec((te * rows // 2, LANES), lambda i: (i, 0)),
                   pl.BlockSpec((te * rows // 2, LANES), lambda i: (i, 0))],
        out_shape=[jax.ShapeDtypeStruct((n_exp * rows, LANES), jnp.int32),
                   jax.ShapeDtypeStruct((n_exp * rows // 2, LANES), jnp.int32),
                   jax.ShapeDtypeStruct((n_exp * rows // 2, LANES), jnp.int32)],
        compiler_params=pltpu.CompilerParams(
            dimension_semantics=("parallel",), vmem_limit_bytes=VMEM_LIMIT),
        name="table_pack",
    )(down, up)


def _pad_lanes(a):
    return jnp.pad(a, ((0, 0), (0, LANES - a.shape[1])))


def kernel(x, norm_mix_w, w_in, a_q_norm_w, a_k_norm_w, a_rel_bias, b_conv_w, b_a_log, b_dt_bias, b_norm_w, b_merge, w_proj_a, w_proj_b, w_out, norm_ffn_w, peer_w_query, peer_keys_1, peer_keys_2, peer_down, peer_up):
    b, s, d = x.shape
    t = b * s
    depth = w_in.shape[0]
    for l in range(depth):
        x2 = x.reshape(t, d)

        wi = w_in[l]
        o_qkvb = 3 * A_WIDTH
        o_a = o_qkvb + 3 * B_WIDTH
        o_beta = o_a + B_HEADS
        o_gate = o_beta + B_HEADS
        o_merge = o_gate + B_WIDTH
        w_main = jnp.concatenate(
            [wi[:, o_qkvb:o_a], wi[:, o_gate:o_merge], wi[:, o_merge:], wi[:, :o_qkvb]], axis=1).astype(BF16)
        w_small = _pad_lanes(wi[:, o_a:o_gate]).astype(BF16)
        z_attn = Z_MERGE + 2 * d

        bias_tab = _attn_bias_table(a_rel_bias[l])
        seg = jnp.kron(jnp.eye(A_HEADS, dtype=F32), jnp.full((A_HEAD_DIM, A_HEAD_DIM), 1.0 / A_HEAD_DIM, F32))
        head_params = jnp.zeros((SUBLANES, LANES), F32)
        head_params = head_params.at[0, :B_HEADS].set(b_a_log[l]).at[1, :B_HEADS].set(b_dt_bias[l])
        head_params = head_params.at[2, :B_HEADS].set(1.0)

        def mixer_and_route(b0, nb):
            tg = nb * s
            z, zs = _in_proj(x2, norm_mix_w[l][None, :], w_main, w_small, min(IN_PROJ_TM, tg), IN_PROJ_TN,
                             b0 * s, tg)
            z3 = z.reshape(nb, s, z.shape[1])
            o_att = _band_attention(
                z3, bias_tab, jnp.tile(a_q_norm_w[l], A_HEADS)[None, :], jnp.tile(a_k_norm_w[l], A_HEADS)[None, :],
                seg.astype(BF16), z_attn // A_WIDTH)
            o_gdn = _gated_deltanet(z3, zs.reshape(nb, s, LANES), b_conv_w[l], head_params, b_norm_w[l][None, :])
            return _merge_route(
                o_att.reshape(tg, A_WIDTH), o_gdn.reshape(tg, B_WIDTH), z, x2, b0 * s, b_merge[l][None, :],
                w_proj_a[l].astype(BF16), w_proj_b[l].astype(BF16), w_out[l].astype(BF16),
                norm_ffn_w[l][None, :], peer_w_query[l].astype(BF16), peer_keys_1[l].astype(BF16),
                peer_keys_2[l].astype(BF16))

        planes = d // LANES
        n_exp = peer_down.shape[1]
        table, down2, up2 = _table_pack(peer_down[l], peer_up[l])
        table = table.reshape(n_exp, planes, LANES)
        down2 = down2.reshape(n_exp, planes // 2, LANES)
        up2 = up2.reshape(n_exp, planes // 2, LANES)

        b_sc = b * PEER_SC_SHARE_NUM // PEER_SC_SHARE_DEN
        n_first = b_sc * s
        tail = min(PEER_SC_TAIL_BLOCKS, n_first // PEER_TP // 2)
        n_sc = n_first - tail * PEER_TP
        x1_s, x1p_s, xnp_s, idx_s, gate_t_s, gate_tok_s = mixer_and_route(0, b_sc)
        idx_s_flat = idx_s.reshape(-1)
        part = _sc_down(idx_s_flat, xnp_s.reshape(n_first, planes, LANES), down2, 0, n_sc)

        x1, _, xnp, idx, gate_t, _ = mixer_and_route(b_sc, b - b_sc)
        nblk = (t - n_first) // PEER_TP
        nblk_a = nblk * PEER_TC_FIRST_NUM // PEER_TC_FIRST_DEN
        idx3 = idx.reshape(nblk, PEER_TP, PEER_SLOTS)
        out = _peer(idx3, xnp, gate_t, x1, table, 0, nblk_a, n_first // PEER_TP, t)
        w16 = _peer_act(out, part, gate_tok_s, 0)
        out = _peer(idx3, xnp, gate_t, x1, table, nblk_a, nblk - nblk_a, n_first // PEER_TP, t, prev=out, after=w16)
        out = _peer(idx_s.reshape(n_first // PEER_TP, PEER_TP, PEER_SLOTS), xnp_s, gate_t_s, x1_s, table,
                    n_sc // PEER_TP, tail, 0, t, prev=out, after=w16)
        sc_out = _sc_up(idx_s_flat, w16, x1p_s.reshape(n_first, planes, LANES), up2, 0, n_sc)
        out = lax.dynamic_update_slice(out, sc_out.reshape(n_sc, d), (0, 0))
        x = out.reshape(b, s, d)
    return x
```
